```python
import jax
import jax.numpy as jnp
from jax import lax
import numpy as np

D_MODEL = 1024
BATCH = 4
SEQ = 8192
DEPTH = 1

PLE_DIM = 256
D_MIX = D_MODEL
RET_WIDTH = D_MIX // 2
RET_HEADS = 8
RET_HEAD_DIM = RET_WIDTH // RET_HEADS
ML_WIDTH = D_MIX - RET_WIDTH
ML_HEADS = 4
ML_HEAD_DIM = ML_WIDTH // ML_HEADS
N_PROJ = 4 * RET_WIDTH + 4 * ML_WIDTH + 2 * ML_HEADS
CHUNK = 128
CONV_W = 4
ROPE_BASE = 10000.0
N_GROUPS = 4
EXPERTS_PER_GROUP = 8
N_EXPERTS = N_GROUPS * EXPERTS_PER_GROUP
TOP_K = 2
D_EXPERT = D_MODEL // 2
MOE_BLOCK = 128
EPS = 1e-6

kernel_name = 'hymba_style_retention_mlstm_hmoe_ple'


def rms_norm(x, g):
    xf = x.astype(jnp.float32)
    y = xf * lax.rsqrt(jnp.mean(xf * xf, axis=-1, keepdims=True) + EPS)
    return (y * g.astype(jnp.float32)).astype(x.dtype)


def rotary(x, positions):
    half = x.shape[-1] // 2
    freqs = ROPE_BASE ** (-jnp.arange(half, dtype=jnp.float32) / half)
    ang = positions.astype(jnp.float32)[..., None] * freqs
    cos = jnp.cos(ang)[:, :, None, :]
    sin = jnp.sin(ang)[:, :, None, :]
    x1, x2 = x[..., :half], x[..., half:]
    return jnp.concatenate([x1 * cos - x2 * sin, x1 * sin + x2 * cos], axis=-1)


def _to_chunks(t):
    B, S, H, d = t.shape
    return t.reshape(B, S // CHUNK, CHUNK, H, d).transpose(0, 3, 1, 2, 4)


def _from_chunks(t):
    B, H, N, C, d = t.shape
    return t.transpose(0, 2, 3, 1, 4).reshape(B, N * C, H, d)


def retention(q, k, v, positions):
    B, S, H, dh = q.shape
    q = rotary(q, positions)
    k = rotary(k, positions) * (dh ** -0.5)
    qc, kc, vc = _to_chunks(q), _to_chunks(k), _to_chunks(v)
    log_gamma = jnp.log1p(-(2.0 ** (-5.0 - jnp.arange(H, dtype=jnp.float32))))
    idx = jnp.arange(CHUNK, dtype=jnp.float32)
    rel = idx[:, None] - idx[None, :]
    causal = rel >= 0
    decay = jnp.where(causal, jnp.exp(log_gamma[:, None, None] * jnp.where(causal, rel, 0.0)), 0.0)
    scores = jnp.einsum('bhncd,bhnmd->bhncm', qc, kc) * decay[:, None]
    intra = jnp.einsum('bhncm,bhnmd->bhncd', scores, vc)
    w_state = jnp.exp(log_gamma[:, None] * (CHUNK - 1 - idx))
    chunk_kv = jnp.einsum('bhncd,bhnce->bhnde', kc * w_state[:, None, :, None], vc)
    chunk_decay = jnp.exp(log_gamma * CHUNK)[:, None, None]

    def step(R, s):
        return chunk_decay * R + s, R

    R0 = jnp.zeros((B, H, dh, dh), jnp.float32)
    _, R_prev = lax.scan(step, R0, chunk_kv.transpose(2, 0, 1, 3, 4))
    R_prev = R_prev.transpose(1, 2, 0, 3, 4)
    w_query = jnp.exp(log_gamma[:, None] * (idx + 1.0))
    inter = jnp.einsum('bhncd,bhnde->bhnce', qc, R_prev) * w_query[:, None, :, None]
    return _from_chunks(intra + inter)


def mlstm(q, k, v, i_pre, f_pre):
    B, S, H, d = q.shape
    N = S // CHUNK
    k = k * (d ** -0.5)
    qc, kc, vc = _to_chunks(q), _to_chunks(k), _to_chunks(v)
    ig = i_pre.reshape(B, N, CHUNK, H).transpose(0, 3, 1, 2)
    logf = jax.nn.log_sigmoid(f_pre).reshape(B, N, CHUNK, H).transpose(0, 3, 1, 2)
    b = jnp.cumsum(logf, axis=-1)
    bL = b[..., -1]
    a = bL[..., None] - b + ig

    def step(carry, inp):
        S_, n_, m_ = carry
        k_, v_, a_, bL_ = inp
        m_new = jnp.maximum(bL_ + m_, jnp.max(a_, axis=-1))
        w = jnp.exp(a_ - m_new[..., None])
        dec = jnp.exp(bL_ + m_ - m_new)
        S_new = dec[..., None, None] * S_ + jnp.einsum('bhcd,bhce->bhde', k_ * w[..., None], v_)
        n_new = dec[..., None] * n_ + jnp.einsum('bhc,bhcd->bhd', w, k_)
        return (S_new, n_new, m_new), (S_, n_, m_)

    init = (jnp.zeros((B, H, d, d), jnp.float32), jnp.zeros((B, H, d), jnp.float32),
            jnp.zeros((B, H), jnp.float32))
    xs = (kc.transpose(2, 0, 1, 3, 4), vc.transpose(2, 0, 1, 3, 4),
          a.transpose(2, 0, 1, 3), bL.transpose(2, 0, 1))
    _, (S_prev, n_prev, m_prev) = lax.scan(step, init, xs)
    S_prev = S_prev.transpose(1, 2, 0, 3, 4)
    n_prev = n_prev.transpose(1, 2, 0, 3)
    m_prev = m_prev.transpose(1, 2, 0)

    causal = jnp.arange(CHUNK)[:, None] >= jnp.arange(CHUNK)[None, :]
    Dlog = jnp.where(causal, b[..., :, None] - b[..., None, :] + ig[..., None, :], -jnp.inf)
    inter_log = b + m_prev[..., None]
    m_row = jnp.maximum(jnp.max(Dlog, axis=-1), inter_log)
    P = jnp.exp(Dlog - m_row[..., None]) * jnp.einsum('bhncd,bhnmd->bhncm', qc, kc)
    w_inter = jnp.exp(inter_log - m_row)
    num = (jnp.einsum('bhncm,bhnmd->bhncd', P, vc)
           + w_inter[..., None] * jnp.einsum('bhncd,bhnde->bhnce', qc, S_prev))
    den = jnp.sum(P, axis=-1) + w_inter * jnp.einsum('bhncd,bhnd->bhnc', qc, n_prev)
    h = num / jnp.maximum(jnp.abs(den), jnp.exp(-m_row))[..., None]
    return _from_chunks(h)


def causal_conv(x, w, bias):
    C = x.shape[-1]
    y = lax.conv_general_dilated(x, w[:, None, :], window_strides=(1,),
                                 padding=((CONV_W - 1, 0),),
                                 dimension_numbers=('NWC', 'WIO', 'NWC'),
                                 feature_group_count=C)
    return y + bias


def mixer(xn, positions, w_in, conv_w, conv_b, b_igate, b_fgate, ret_gn, ml_gn, w_out):
    B, S, _ = xn.shape
    proj = xn @ w_in
    sizes = [RET_WIDTH] * 4 + [ML_WIDTH] * 4 + [ML_HEADS] * 2
    offs = np.cumsum(sizes)[:-1].tolist()
    rq, rk, rv, rg, mq, mk, mv, mo, mi, mf = jnp.split(proj, offs, axis=-1)
    f32 = jnp.float32
    rsh = (B, S, RET_HEADS, RET_HEAD_DIM)
    r = retention(rq.reshape(rsh).astype(f32), rk.reshape(rsh).astype(f32),
                  rv.reshape(rsh).astype(f32), positions)
    r = rms_norm(r, ret_gn.reshape(RET_HEADS, RET_HEAD_DIM)).reshape(B, S, RET_WIDTH)
    r = jax.nn.silu(rg.astype(f32)) * r
    qk = jax.nn.silu(causal_conv(jnp.concatenate([mq, mk], axis=-1), conv_w, conv_b))
    mq2, mk2 = jnp.split(qk, 2, axis=-1)
    msh = (B, S, ML_HEADS, ML_HEAD_DIM)
    i_pre = (mi + b_igate).astype(f32)
    f_pre = (mf + b_fgate).astype(f32)
    hm = mlstm(mq2.reshape(msh).astype(f32), mk2.reshape(msh).astype(f32),
               mv.reshape(msh).astype(f32), i_pre, f_pre)
    hm = jax.nn.sigmoid(mo.astype(f32)).reshape(msh) * hm
    hm = rms_norm(hm, ml_gn.reshape(ML_HEADS, ML_HEAD_DIM)).reshape(B, S, ML_WIDTH)
    y = jnp.concatenate([r, hm], axis=-1).astype(xn.dtype)
    return y @ w_out


def hier_moe(xn, w_group, b_group, w_router, b_router, w1, w3, w2):
    B, S, D = xn.shape
    T = B * S
    xf = xn.reshape(T, D)
    g_logits = (xf @ w_group).astype(jnp.float32) + b_group.astype(jnp.float32)
    g_prob = jax.nn.softmax(g_logits, axis=-1)
    g_sel = jnp.argmax(g_logits, axis=-1)
    p_g = jnp.take_along_axis(g_prob, g_sel[:, None], axis=-1)
    e_logits = ((xf @ w_router).astype(jnp.float32) + b_router.astype(jnp.float32)
                ).reshape(T, N_GROUPS, EXPERTS_PER_GROUP)
    e_in = jnp.take_along_axis(e_logits, g_sel[:, None, None], axis=1)[:, 0]
    top_p, top_i = lax.top_k(jax.nn.softmax(e_in, axis=-1), TOP_K)
    gates = p_g * top_p / jnp.sum(top_p, axis=-1, keepdims=True)
    expert_id = (g_sel[:, None] * EXPERTS_PER_GROUP + top_i).astype(jnp.int32)
    TK = T * TOP_K
    flat_e = expert_id.reshape(-1)
    order = jnp.argsort(flat_e).astype(jnp.int32)
    sorted_e = flat_e[order]
    counts = jnp.bincount(flat_e, length=N_EXPERTS).astype(jnp.int32)
    padded = (counts + MOE_BLOCK - 1) // MOE_BLOCK * MOE_BLOCK
    start = jnp.cumsum(counts) - counts
    pad_end = jnp.cumsum(padded)
    pad_start = pad_end - padded
    dest = pad_start[sorted_e] + jnp.arange(TK, dtype=jnp.int32) - start[sorted_e]
    n_blocks = TK // MOE_BLOCK + N_EXPERTS
    P = n_blocks * MOE_BLOCK
    row_token = jnp.full((P,), T, jnp.int32).at[dest].set(order // TOP_K)
    x_pad = jnp.concatenate([xf, jnp.zeros((1, D), xf.dtype)], axis=0)
    x_buf = x_pad[row_token].reshape(n_blocks, MOE_BLOCK, D)
    block_start = jnp.arange(n_blocks, dtype=jnp.int32) * MOE_BLOCK
    block_expert = jnp.clip(jnp.searchsorted(pad_end, block_start, side='right'),
                            0, N_EXPERTS - 1).astype(jnp.int32)

    def expert_block(args):
        xb, e = args
        hmid = jax.nn.silu(xb @ w1[e]) * (xb @ w3[e])
        return hmid @ w2[e]

    y_buf = lax.map(expert_block, (x_buf, block_expert)).reshape(P, D)
    slot_pos = jnp.zeros((TK,), jnp.int32).at[order].set(dest)
    y = y_buf[slot_pos].reshape(T, TOP_K, D)
    out = jnp.einsum('tk,tkd->td', gates.astype(y.dtype), y)
    return out.reshape(B, S, D)


def setup_inputs(seed: int = 0) -> dict:
    key = jax.random.key(seed)
    ks = jax.random.split(key, 26)
    f32 = jnp.float32
    nrm = lambda k, shape, scale: jax.random.normal(k, shape, f32) * scale
    L, D = DEPTH, D_MODEL
    x = jax.random.normal(ks[0], (BATCH, SEQ, D), f32)
    p = jax.random.normal(ks[1], (DEPTH, BATCH, SEQ, PLE_DIM), f32)
    positions = jnp.broadcast_to(jnp.arange(SEQ, dtype=jnp.int32)[None, :], (BATCH, SEQ))
    return {
        'x': x,
        'p': p,
        'positions': positions,
        'attn_norm': 1.0 + nrm(ks[2], (L, D), 0.02),
        'w_in': nrm(ks[3], (L, D, N_PROJ), D ** -0.5),
        'conv_w': nrm(ks[4], (L, CONV_W, 2 * ML_WIDTH), CONV_W ** -0.5),
        'conv_b': nrm(ks[5], (L, 2 * ML_WIDTH), 0.02),
        'b_igate': nrm(ks[6], (L, ML_HEADS), 0.1),
        'b_fgate': jnp.linspace(3.0, 6.0, ML_HEADS, dtype=f32)[None, :] + nrm(ks[7], (L, ML_HEADS), 0.02),
        'ret_gn': 1.0 + nrm(ks[8], (L, RET_WIDTH), 0.02),
        'ml_gn': 1.0 + nrm(ks[9], (L, ML_WIDTH), 0.02),
        'w_out': nrm(ks[10], (L, D_MIX, D), D_MIX ** -0.5),
        'moe_norm': 1.0 + nrm(ks[11], (L, D), 0.02),
        'w_group': nrm(ks[12], (L, D, N_GROUPS), D ** -0.5),
        'b_group': nrm(ks[13], (L, N_GROUPS), 0.01),
        'w_router': nrm(ks[14], (L, D, N_EXPERTS), D ** -0.5),
        'b_router': nrm(ks[15], (L, N_EXPERTS), 0.01),
        'w1': nrm(ks[16], (L, N_EXPERTS, D, D_EXPERT), D ** -0.5),
        'w3': nrm(ks[17], (L, N_EXPERTS, D, D_EXPERT), D ** -0.5),
        'w2': nrm(ks[18], (L, N_EXPERTS, D_EXPERT, D), D_EXPERT ** -0.5),
        'w_ple_up': nrm(ks[19], (L, PLE_DIM, D), PLE_DIM ** -0.5),
        'ple_norm': 1.0 + nrm(ks[20], (L, D), 0.02),
        'ple_gate_norm': 1.0 + nrm(ks[21], (L, D), 0.02),
        'w_ple_gate': nrm(ks[22], (L, D, D), D ** -0.5),
        'final_norm': 1.0 + nrm(ks[23], (D,), 0.02),
    }


def reference(x, p, positions, attn_norm, w_in, conv_w, conv_b, b_igate, b_fgate,
              ret_gn, ml_gn, w_out, moe_norm, w_group, b_group, w_router, b_router,
              w1, w3, w2, w_ple_up, ple_norm, ple_gate_norm, w_ple_gate, final_norm):
    h = x
    for l in range(DEPTH):
        xn = rms_norm(h, attn_norm[l])
        h = h + mixer(xn, positions, w_in[l], conv_w[l], conv_b[l], b_igate[l], b_fgate[l],
                      ret_gn[l], ml_gn[l], w_out[l])
        xn = rms_norm(h, moe_norm[l])
        h = h + hier_moe(xn, w_group[l], b_group[l], w_router[l], b_router[l],
                         w1[l], w3[l], w2[l])
        e = rms_norm(p[l] @ w_ple_up[l], ple_norm[l])
        gate = jax.nn.sigmoid(rms_norm(h, ple_gate_norm[l]) @ w_ple_gate[l])
        h = h + gate * e
    return rms_norm(h, final_norm)
```

```python
import functools

import jax
import jax.numpy as jnp
from jax import lax
from jax.experimental import pallas as pl
from jax.experimental.pallas import tpu as pltpu

F32 = jnp.float32
BF16 = jnp.bfloat16

RET_HEADS = 8
ML_HEADS = 4
CHUNK = 128
CONV_W = 4
ROPE_BASE = 10000.0
N_GROUPS = 4
EXPERTS_PER_GROUP = 8
N_EXPERTS = N_GROUPS * EXPERTS_PER_GROUP
EPS = 1e-6

LANES = 128
VMEM_LIMIT = 56 * 1024 * 1024
EXPERT_BLOCK = 256
GROUP_LANE0 = N_EXPERTS


def _rms(x, g):
    return x * lax.rsqrt(jnp.mean(x * x, axis=-1, keepdims=True) + EPS) * g


def _dot(a, b):
    return jnp.dot(a, b, preferred_element_type=F32)


def _dot_nt(a, b):
    return lax.dot_general(a, b, (((1,), (1,)), ((), ())), preferred_element_type=F32)


def _split3(x):
    hi = x.astype(BF16)
    r1 = x - hi.astype(F32)
    mid = r1.astype(BF16)
    lo = (r1 - mid.astype(F32)).astype(BF16)
    return hi, mid, lo


def _inproj_kernel(x_ref, g_ref, wm_ref, wg_ref, gb_ref, proj_ref, gate_ref, *, col_chunk):
    xn = _rms(x_ref[...], g_ref[...]).astype(BF16)
    n_main = wm_ref.shape[1]
    for c in range(0, n_main, col_chunk):
        proj_ref[:, c:c + col_chunk] = _dot(xn, wm_ref[:, c:c + col_chunk])
    gate_ref[...] = _dot(xn, wg_ref[...]) + gb_ref[...]


def _inproj(x2, g, w_main, w_gate, gate_bias, tm=256):
    T, D = x2.shape
    n_main = w_main.shape[1]
    const = lambda i: (0, 0)
    return pl.pallas_call(
        functools.partial(_inproj_kernel, col_chunk=512),
        grid=(T // tm,),
        in_specs=[
            pl.BlockSpec((tm, D), lambda i: (i, 0)),
            pl.BlockSpec((1, D), const),
            pl.BlockSpec((D, n_main), const),
            pl.BlockSpec((D, LANES), const),
            pl.BlockSpec((1, LANES), const),
        ],
        out_specs=[
            pl.BlockSpec((tm, n_main), lambda i: (i, 0)),
            pl.BlockSpec((tm, LANES), lambda i: (i, 0)),
        ],
        out_shape=[
            jax.ShapeDtypeStruct((T, n_main), F32),
            jax.ShapeDtypeStruct((T, LANES), F32),
        ],
        compiler_params=pltpu.CompilerParams(
            dimension_semantics=("arbitrary",), vmem_limit_bytes=VMEM_LIMIT),
        name="inproj",
    )(x2, g, w_main, w_gate, gate_bias)


def _mixer_kernel(proj_ref, gate_ref, pos_ref, freq_ref, sign_ref, decay_ref, ws_ref, wq_ref,
                  cd_ref, convw_ref, convb_ref, retgn_ref, mlgn_ref,
                  y_ref, r_state, s_state, n_state, m_state, conv_prev,
                  *, ret_w, ml_w):
    C = proj_ref.shape[0]
    n_pairs = ret_w // LANES
    ml_heads = ml_w // LANES
    ret_dh = LANES // 2
    o_rq, o_rk, o_rv, o_rg = 0, ret_w, 2 * ret_w, 3 * ret_w
    o_mq = 4 * ret_w
    o_mv = o_mq + 2 * ml_w
    o_mo = o_mv + ml_w

    @pl.when(pl.program_id(1) == 0)
    def _():
        r_state[...] = jnp.zeros_like(r_state)
        s_state[...] = jnp.zeros_like(s_state)
        n_state[...] = jnp.zeros_like(n_state)
        m_state[...] = jnp.zeros_like(m_state)
        conv_prev[...] = jnp.zeros_like(conv_prev)

    lane = lax.broadcasted_iota(jnp.int32, (C, LANES), 1)
    row = lax.broadcasted_iota(jnp.int32, (C, LANES), 0)
    lo = lane < ret_dh
    first_half = (lane % ret_dh) < (ret_dh // 2)
    causal = row >= lane
    blockdiag = (row < ret_dh) == lo

    ang = pos_ref[...].astype(F32) * freq_ref[...]
    cos_t = jnp.cos(ang)
    sin_t = jnp.sin(ang) * sign_ref[...]

    def rot(t):
        swapped = jnp.where(first_half, pltpu.roll(t, LANES - ret_dh // 2, 1),
                            pltpu.roll(t, ret_dh // 2, 1))
        return t * cos_t + swapped * sin_t

    for p in range(n_pairs):
        sl = lambda o: slice(o + p * LANES, o + (p + 1) * LANES)
        q = rot(proj_ref[:, sl(o_rq)])
        k = rot(proj_ref[:, sl(o_rk)]) * (ret_dh ** -0.5)
        v = proj_ref[:, sl(o_rv)]
        kb = k.astype(BF16)
        q_a = jnp.where(lo, q, 0.0).astype(BF16)
        q_b = jnp.where(lo, 0.0, q).astype(BF16)
        s_a = _dot_nt(q_a, kb) * decay_ref[2 * p]
        s_b = _dot_nt(q_b, kb) * decay_ref[2 * p + 1]
        v_a = jnp.where(lo, v, 0.0).astype(BF16)
        v_b = jnp.where(lo, 0.0, v).astype(BF16)
        r_prev = r_state[p]
        o = (_dot(s_a.astype(BF16), v_a) + _dot(s_b.astype(BF16), v_b)
             + _dot(q.astype(BF16), r_prev.astype(BF16)) * wq_ref[p])
        kw_t = (k * ws_ref[p]).T.astype(BF16)
        kv = _dot(kw_t, v.astype(BF16))
        r_state[p] = cd_ref[p] * r_prev + jnp.where(blockdiag, kv, 0.0)
        sq = o * o
        ms_a = jnp.sum(jnp.where(lo, sq, 0.0), axis=-1, keepdims=True)
        ms_b = jnp.sum(jnp.where(lo, 0.0, sq), axis=-1, keepdims=True)
        ms = jnp.where(lo, ms_a, ms_b) * (1.0 / ret_dh)
        r = o * lax.rsqrt(ms + EPS) * retgn_ref[:, p * LANES:(p + 1) * LANES]
        g = proj_ref[:, sl(o_rg)]
        y_ref[:, p * LANES:(p + 1) * LANES] = (g * jax.nn.sigmoid(g) * r).astype(y_ref.dtype)

    xqk = proj_ref[:, o_mq:o_mq + 2 * ml_w]
    prev = conv_prev[...]
    conv_prev[...] = xqk
    rowq = lax.broadcasted_iota(jnp.int32, xqk.shape, 0)
    acc = xqk * convw_ref[CONV_W - 1:CONV_W, :] + convb_ref[...]
    for s in range(1, CONV_W):
        shifted = jnp.where(rowq < s, pltpu.roll(prev, s, 0), pltpu.roll(xqk, s, 0))
        acc = acc + shifted * convw_ref[CONV_W - 1 - s:CONV_W - s, :]
    qk = acc * jax.nn.sigmoid(acc)

    G = gate_ref[...]
    logf = jax.nn.log_sigmoid(G)
    tri = (row >= lane).astype(BF16)
    l_hi, l_mid, l_lo = _split3(logf)
    bcum = _dot(tri, l_hi) + _dot(tri, l_mid) + _dot(tri, l_lo)
    GT = G.T
    bT = bcum.T

    for h in range(ml_heads):
        hs = slice(h * LANES, (h + 1) * LANES)
        q = qk[:, hs]
        k = qk[:, ml_w + h * LANES: ml_w + (h + 1) * LANES] * (LANES ** -0.5)
        v = proj_ref[:, o_mv + h * LANES:o_mv + (h + 1) * LANES]
        ig_col = G[:, h:h + 1]
        ig_row = GT[h:h + 1, :]
        b_col = bcum[:, ml_heads + h:ml_heads + h + 1]
        b_row = bT[ml_heads + h:ml_heads + h + 1, :]
        bL = b_col[C - 1:C, :]
        m_prev = m_state[h][:, 0:1]
        S_prev = s_state[h]
        n_prev = n_state[h]
        qb, kb, vb = q.astype(BF16), k.astype(BF16), v.astype(BF16)

        dlog = jnp.where(causal, b_col - b_row + ig_row, -jnp.inf)
        inter_log = b_col + m_prev
        m_row = jnp.maximum(jnp.max(dlog, axis=-1, keepdims=True), inter_log)
        P = jnp.exp(dlog - m_row) * _dot_nt(qb, kb)
        w_inter = jnp.exp(inter_log - m_row)
        num = _dot(P.astype(BF16), vb) + w_inter * _dot(qb, S_prev.astype(BF16))
        den = (jnp.sum(P, axis=-1, keepdims=True)
               + w_inter * jnp.sum(q * n_prev, axis=-1, keepdims=True))
        hh = num / jnp.maximum(jnp.abs(den), jnp.exp(-m_row))

        a_col = bL - b_col + ig_col
        m_new = jnp.maximum(bL + m_prev, jnp.max(a_col, axis=0, keepdims=True))
        w_col = jnp.exp(a_col - m_new)
        dec = jnp.exp(bL + m_prev - m_new)
        kw = k * w_col
        s_state[h] = dec * S_prev + _dot(kw.T.astype(BF16), vb)
        n_state[h] = dec * n_prev + jnp.sum(kw, axis=0, keepdims=True)
        m_state[h] = jnp.broadcast_to(m_new, (1, LANES))

        og = proj_ref[:, o_mo + h * LANES:o_mo + (h + 1) * LANES]
        hm = jax.nn.sigmoid(og) * hh
        hm = _rms(hm, mlgn_ref[:, hs])
        y_ref[:, ret_w + h * LANES:ret_w + (h + 1) * LANES] = hm.astype(y_ref.dtype)


def _ret_tables(C):
    H = RET_HEADS
    dh = LANES // 2
    log_gamma = jnp.log1p(-(2.0 ** (-5.0 - jnp.arange(H, dtype=F32))))
    idx = jnp.arange(C, dtype=F32)
    rel = idx[:, None] - idx[None, :]
    causal = rel >= 0
    decay = jnp.where(causal, jnp.exp(log_gamma[:, None, None] * jnp.where(causal, rel, 0.0)), 0.0)
    w_state = jnp.exp(log_gamma[:, None] * (C - 1 - idx))
    w_query = jnp.exp(log_gamma[:, None] * (idx + 1.0))
    chunk_decay = jnp.exp(log_gamma * C)
    pair = lambda t: jnp.repeat(t.reshape(H // 2, 2, C).transpose(0, 2, 1), dh, axis=2)
    cd = jnp.repeat(chunk_decay.reshape(H // 2, 2), dh, axis=1)
    cd = jnp.broadcast_to(cd[:, :, None], (H // 2, LANES, LANES))
    half = dh // 2
    freqs = ROPE_BASE ** (-jnp.arange(half, dtype=F32) / half)
    freq_row = jnp.tile(freqs, LANES // half)[None, :]
    sign_row = jnp.tile(jnp.concatenate([-jnp.ones((half,), F32), jnp.ones((half,), F32)]),
                        LANES // dh)[None, :]
    return freq_row, sign_row, decay, pair(w_state), pair(w_query), cd


def _mixer(proj, gates, positions, conv_w, conv_b, ret_gn, ml_gn, B, S, ret_w, ml_w):
    C = CHUNK
    N = S // C
    n_main = proj.shape[1]
    proj3 = proj.reshape(B, S, n_main)
    gates3 = gates.reshape(B, S, LANES)
    pos3 = positions.reshape(B, S, 1)
    freq_row, sign_row, decay, ws, wq, cd = _ret_tables(C)
    n_pairs = ret_w // LANES
    ml_heads = ml_w // LANES
    c2 = lambda b, n: (0, 0)
    c3 = lambda b, n: (0, 0, 0)
    y = pl.pallas_call(
        functools.partial(_mixer_kernel, ret_w=ret_w, ml_w=ml_w),
        grid=(B, N),
        in_specs=[
            pl.BlockSpec((None, C, n_main), lambda b, n: (b, n, 0)),
            pl.BlockSpec((None, C, LANES), lambda b, n: (b, n, 0)),
            pl.BlockSpec((None, C, 1), lambda b, n: (b, n, 0)),
            pl.BlockSpec((1, LANES), c2),
            pl.BlockSpec((1, LANES), c2),
            pl.BlockSpec((RET_HEADS, C, C), c3),
            pl.BlockSpec((n_pairs, C, LANES), c3),
            pl.BlockSpec((n_pairs, C, LANES), c3),
            pl.BlockSpec((n_pairs, LANES, LANES), c3),
            pl.BlockSpec((CONV_W, 2 * ml_w), c2),
            pl.BlockSpec((1, 2 * ml_w), c2),
            pl.BlockSpec((1, ret_w), c2),
            pl.BlockSpec((1, ml_w), c2),
        ],
        out_specs=pl.BlockSpec((None, C, ret_w + ml_w), lambda b, n: (b, n, 0)),
        out_shape=jax.ShapeDtypeStruct((B, S, ret_w + ml_w), BF16),
        scratch_shapes=[
            pltpu.VMEM((n_pairs, LANES, LANES), F32),
            pltpu.VMEM((ml_heads, LANES, LANES), F32),
            pltpu.VMEM((ml_heads, 1, LANES), F32),
            pltpu.VMEM((ml_heads, 1, LANES), F32),
            pltpu.VMEM((C, 2 * ml_w), F32),
        ],
        compiler_params=pltpu.CompilerParams(
            dimension_semantics=("arbitrary", "arbitrary"), vmem_limit_bytes=VMEM_LIMIT),
        name="mixer",
    )(proj3, gates3, pos3, freq_row, sign_row, decay, ws, wq, cd,
      conv_w, conv_b.reshape(1, -1), ret_gn.reshape(1, -1), ml_gn.reshape(1, -1))
    return y.reshape(B * S, ret_w + ml_w)


def _router_kernel(y_ref, x_ref, wo_ref, g_ref, wr_hi_ref, wr_lo_ref, br_ref,
                   h_ref, xn_ref, ri_ref, rf_ref, cnt_ref, run_cnt):
    tm = y_ref.shape[0]

    @pl.when(pl.program_id(0) == 0)
    def _():
        run_cnt[...] = jnp.zeros_like(run_cnt)

    h = x_ref[...] + _dot(y_ref[...], wo_ref[...])
    h_ref[...] = h
    xn = _rms(h, g_ref[...])
    xn_ref[...] = xn
    x_hi = xn.astype(BF16)
    x_lo = (xn - x_hi.astype(F32)).astype(BF16)
    logits = (_dot(x_hi, wr_hi_ref[...]) + _dot(x_lo, wr_hi_ref[...])
              + _dot(x_hi, wr_lo_ref[...]) + br_ref[...])

    lane = lax.broadcasted_iota(jnp.int32, (tm, LANES), 1)
    big = jnp.int32(LANES)
    neg = -jnp.inf
    is_g = (lane >= GROUP_LANE0) & (lane < GROUP_LANE0 + N_GROUPS)
    gl = jnp.where(is_g, logits, neg)
    gmax = jnp.max(gl, axis=-1, keepdims=True)
    gsum = jnp.sum(jnp.where(is_g, jnp.exp(gl - gmax), 0.0), axis=-1, keepdims=True)
    p_g = 1.0 / gsum
    g_sel = jnp.min(jnp.where(is_g & (gl == gmax), lane - GROUP_LANE0, big), axis=-1, keepdims=True)
    in_grp = (lane < N_EXPERTS) & ((lane // EXPERTS_PER_GROUP) == g_sel)
    el = jnp.where(in_grp, logits, neg)
    emax = jnp.max(el, axis=-1, keepdims=True)
    eexp = jnp.where(in_grp, jnp.exp(el - emax), 0.0)
    prob = eexp / jnp.sum(eexp, axis=-1, keepdims=True)
    pm1 = jnp.where(in_grp, prob, -1.0)
    p1 = jnp.max(pm1, axis=-1, keepdims=True)
    i1 = jnp.min(jnp.where(pm1 == p1, lane, big), axis=-1, keepdims=True)
    pm2 = jnp.where(lane == i1, -1.0, pm1)
    p2 = jnp.max(pm2, axis=-1, keepdims=True)
    i2 = jnp.min(jnp.where(pm2 == p2, lane, big), axis=-1, keepdims=True)
    denom = p1 + p2
    g1 = p_g * p1 / denom
    g2 = p_g * p2 / denom

    sel1 = lane == i1
    sel2 = lane == i2
    onehot = (sel1 | sel2).astype(BF16)
    r_i = lax.broadcasted_iota(jnp.int32, (tm, tm), 0)
    c_i = lax.broadcasted_iota(jnp.int32, (tm, tm), 1)
    tri = (r_i > c_i).astype(BF16)
    prefix = _dot(tri, onehot) + run_cnt[...]
    rank1 = jnp.sum(jnp.where(sel1, prefix, 0.0), axis=-1, keepdims=True).astype(jnp.int32)
    rank2 = jnp.sum(jnp.where(sel2, prefix, 0.0), axis=-1, keepdims=True).astype(jnp.int32)
    new_cnt = run_cnt[...] + jnp.sum(onehot.astype(F32), axis=0, keepdims=True)
    run_cnt[...] = new_cnt
    cnt_ref[...] = new_cnt.astype(jnp.int32)

    ri_ref[...] = jnp.where(lane == 0, i1, jnp.where(lane == 1, i2,
                            jnp.where(lane == 2, rank1, jnp.where(lane == 3, rank2, 0))))
    rf_ref[...] = jnp.where(lane == 0, g1, jnp.where(lane == 1, g2, 0.0))


def _router(y, x2, w_out, g, wr_hi, wr_lo, br, tm=256):
    T, D = x2.shape
    const = lambda i: (0, 0)
    tile = lambda i: (i, 0)
    return pl.pallas_call(
        _router_kernel,
        grid=(T // tm,),
        in_specs=[
            pl.BlockSpec((tm, y.shape[1]), tile),
            pl.BlockSpec((tm, D), tile),
            pl.BlockSpec(w_out.shape, const),
            pl.BlockSpec((1, D), const),
            pl.BlockSpec((D, LANES), const),
            pl.BlockSpec((D, LANES), const),
            pl.BlockSpec((1, LANES), const),
        ],
        out_specs=[
            pl.BlockSpec((tm, D), tile),
            pl.BlockSpec((tm, D), tile),
            pl.BlockSpec((tm, LANES), tile),
            pl.BlockSpec((tm, LANES), tile),
            pl.BlockSpec((1, LANES), const),
        ],
        out_shape=[
            jax.ShapeDtypeStruct((T, D), F32),
            jax.ShapeDtypeStruct((T, D), F32),
            jax.ShapeDtypeStruct((T, LANES), jnp.int32),
            jax.ShapeDtypeStruct((T, LANES), F32),
            jax.ShapeDtypeStruct((1, LANES), jnp.int32),
        ],
        scratch_shapes=[pltpu.VMEM((1, LANES), F32)],
        compiler_params=pltpu.CompilerParams(
            dimension_semantics=("arbitrary",), vmem_limit_bytes=VMEM_LIMIT),
        name="router",
    )(y, x2, w_out, g, wr_hi, wr_lo, br)


def _dispatch_kernel(dest_ref, xn_ref, buf_in_ref, buf_ref, sem):
    del buf_in_ref
    tm = xn_ref.shape[0]
    base = pl.program_id(0) * tm

    def copy(t, k):
        d = dest_ref[2 * (base + t) + k]
        return pltpu.make_async_copy(xn_ref.at[pl.ds(t, 1)], buf_ref.at[pl.ds(d, 1)], sem)

    def issue(t, c):
        copy(t, 0).start()
        copy(t, 1).start()
        return c

    def drain(t, c):
        copy(t, 0).wait()
        copy(t, 1).wait()
        return c

    lax.fori_loop(0, tm, issue, 0)
    lax.fori_loop(0, tm, drain, 0)


def _dispatch(dest, xn, n_rows, tm=512):
    T, D = xn.shape
    buf0 = jnp.zeros((n_rows, D), xn.dtype)
    return pl.pallas_call(
        _dispatch_kernel,
        grid_spec=pltpu.PrefetchScalarGridSpec(
            num_scalar_prefetch=1,
            grid=(T // tm,),
            in_specs=[
                pl.BlockSpec((tm, D), lambda i, d: (i, 0)),
                pl.BlockSpec(memory_space=pl.ANY),
            ],
            out_specs=pl.BlockSpec(memory_space=pl.ANY),
            scratch_shapes=[pltpu.SemaphoreType.DMA],
        ),
        out_shape=jax.ShapeDtypeStruct((n_rows, D), xn.dtype),
        input_output_aliases={2: 0},
        compiler_params=pltpu.CompilerParams(
            dimension_semantics=("arbitrary",), vmem_limit_bytes=VMEM_LIMIT),
        name="dispatch",
    )(dest, xn, buf0)


def _expert_kernel(be_ref, nu_ref, x_ref, w13_ref, w2_ref, y_ref):
    j = pl.program_id(0)
    d_exp = w2_ref.shape[0]

    @pl.when(j < nu_ref[0])
    def _():
        ab = _dot(x_ref[...].astype(BF16), w13_ref[...])
        a = ab[:, :d_exp]
        hmid = a * jax.nn.sigmoid(a) * ab[:, d_exp:]
        y_ref[...] = _dot(hmid.astype(BF16), w2_ref[...])

    @pl.when(j >= nu_ref[0])
    def _():
        y_ref[...] = jnp.zeros_like(y_ref)


def _experts(block_expert, n_used, x_buf, w13, w2):
    P, D = x_buf.shape
    blk = EXPERT_BLOCK
    d_exp = w2.shape[1]
    clamp = lambda j, nu: jnp.minimum(j, nu[0] - 1)
    return pl.pallas_call(
        _expert_kernel,
        grid_spec=pltpu.PrefetchScalarGridSpec(
            num_scalar_prefetch=2,
            grid=(P // blk,),
            in_specs=[
                pl.BlockSpec((blk, D), lambda j, be, nu: (clamp(j, nu), 0)),
                pl.BlockSpec((None, D, 2 * d_exp), lambda j, be, nu: (be[clamp(j, nu)], 0, 0)),
                pl.BlockSpec((None, d_exp, D), lambda j, be, nu: (be[clamp(j, nu)], 0, 0)),
            ],
            out_specs=pl.BlockSpec((blk, D), lambda j, be, nu: (j, 0)),
        ),
        out_shape=jax.ShapeDtypeStruct((P, D), F32),
        compiler_params=pltpu.CompilerParams(
            dimension_semantics=("arbitrary",), vmem_limit_bytes=VMEM_LIMIT),
        name="experts",
    )(block_expert, n_used, x_buf, w13, w2)


def _final_kernel(dest_ref, h_ref, rf_ref, p_ref, wup_ref, gple_ref, ggate_ref, wgate_ref,
                  gfin_ref, ybuf_ref, out_ref, y0, y1, sem):
    tm = h_ref.shape[0]
    base = pl.program_id(0) * tm

    def copy(t, k):
        d = dest_ref[2 * (base + t) + k]
        dst = y0 if k == 0 else y1
        return pltpu.make_async_copy(ybuf_ref.at[pl.ds(d, 1)], dst.at[pl.ds(t, 1)], sem)

    def issue(t, c):
        copy(t, 0).start()
        copy(t, 1).start()
        return c

    def drain(t, c):
        copy(t, 0).wait()
        copy(t, 1).wait()
        return c

    lax.fori_loop(0, tm, issue, 0)
    e = _rms(_dot(p_ref[...].astype(BF16), wup_ref[...]), gple_ref[...])
    lax.fori_loop(0, tm, drain, 0)

    rf = rf_ref[...]
    h = h_ref[...] + rf[:, 0:1] * y0[...] + rf[:, 1:2] * y1[...]
    gate = jax.nn.sigmoid(_dot(_rms(h, ggate_ref[...]).astype(BF16), wgate_ref[...]))
    h = h + gate * e
    out_ref[...] = _rms(h, gfin_ref[...])


def _final(dest, h, rf, p2, w_up, g_ple, g_gate, w_gate, g_fin, y_buf, tm=256):
    T, D = h.shape
    const = lambda i, d: (0, 0)
    tile = lambda i, d: (i, 0)
    return pl.pallas_call(
        _final_kernel,
        grid_spec=pltpu.PrefetchScalarGridSpec(
            num_scalar_prefetch=1,
            grid=(T // tm,),
            in_specs=[
                pl.BlockSpec((tm, D), tile),
                pl.BlockSpec((tm, LANES), tile),
                pl.BlockSpec((tm, p2.shape[1]), tile),
                pl.BlockSpec(w_up.shape, const),
                pl.BlockSpec((1, D), const),
                pl.BlockSpec((1, D), const),
                pl.BlockSpec(w_gate.shape, const),
                pl.BlockSpec((1, D), const),
                pl.BlockSpec(memory_space=pl.ANY),
            ],
            out_specs=pl.BlockSpec((tm, D), tile),
            scratch_shapes=[
                pltpu.VMEM((tm, D), F32),
                pltpu.VMEM((tm, D), F32),
                pltpu.SemaphoreType.DMA,
            ],
        ),
        out_shape=jax.ShapeDtypeStruct((T, D), F32),
        compiler_params=pltpu.CompilerParams(
            dimension_semantics=("arbitrary",), vmem_limit_bytes=VMEM_LIMIT),
        name="final",
    )(dest, h, rf, p2, w_up, g_ple, g_gate, w_gate, g_fin, y_buf)


def _layer(h2, p2, positions, B, S, attn_norm, w_in, conv_w, conv_b, b_igate, b_fgate, ret_gn,
           ml_gn, w_out, moe_norm, w_group, b_group, w_router, b_router, w1, w3, w2, w_ple_up,
           ple_norm, ple_gate_norm, w_ple_gate, out_norm):
    T, D = h2.shape
    ret_w = ret_gn.shape[0]
    ml_w = ml_gn.shape[0]
    n_main = 4 * ret_w + 4 * ml_w
    row = lambda v: v.reshape(1, -1).astype(F32)
    pad_lanes = lambda a: jnp.pad(a, ((0, 0), (0, LANES - a.shape[1])))

    w_main = w_in[:, :n_main].astype(BF16)
    w_gate = pad_lanes(w_in[:, n_main:]).astype(BF16)
    gate_bias = pad_lanes(jnp.concatenate([b_igate, b_fgate])[None, :].astype(F32))
    proj, gates = _inproj(h2, row(attn_norm), w_main, w_gate, gate_bias)

    y = _mixer(proj, gates, positions, conv_w, conv_b, ret_gn, ml_gn, B, S, ret_w, ml_w)

    wr = pad_lanes(jnp.concatenate([w_router, w_group], axis=1))
    wr_hi = wr.astype(BF16)
    wr_lo = (wr - wr_hi.astype(F32)).astype(BF16)
    br = pad_lanes(jnp.concatenate([b_router, b_group])[None, :].astype(F32))
    h_mid, xn, ri, rf, counts = _router(y, h2, w_out.astype(BF16), row(moe_norm), wr_hi, wr_lo, br)

    blk = EXPERT_BLOCK
    counts = counts[0, :N_EXPERTS]
    padded = (counts + blk - 1) // blk * blk
    pad_end = jnp.cumsum(padded)
    pad_start = pad_end - padded
    n_blocks = (2 * T) // blk + N_EXPERTS
    block_start = jnp.arange(n_blocks, dtype=jnp.int32) * blk
    block_expert = jnp.clip(jnp.searchsorted(pad_end, block_start, side='right'),
                            0, N_EXPERTS - 1).astype(jnp.int32)
    n_used = (pad_end[-1:] // blk).astype(jnp.int32)
    dest = (pad_start[ri[:, 0:2]] + ri[:, 2:4]).astype(jnp.int32).reshape(-1)

    x_buf = _dispatch(dest, xn, n_blocks * blk)
    w13 = jnp.concatenate([w1, w3], axis=2).astype(BF16)
    y_buf = _experts(block_expert, n_used, x_buf, w13, w2.astype(BF16))

    return _final(dest, h_mid, rf, p2, w_ple_up.astype(BF16), row(ple_norm), row(ple_gate_norm),
                  w_ple_gate.astype(BF16), row(out_norm), y_buf)


def kernel(x, p, positions, attn_norm, w_in, conv_w, conv_b, b_igate, b_fgate, ret_gn, ml_gn,
           w_out, moe_norm, w_group, b_group, w_router, b_router, w1, w3, w2, w_ple_up, ple_norm,
           ple_gate_norm, w_ple_gate, final_norm):
    B, S, D = x.shape
    depth = p.shape[0]
    assert depth == 1, "the final RMSNorm is fused into the layer's last kernel"
    out = _layer(x.reshape(B * S, D), p[0].reshape(B * S, -1), positions, B, S,
                 attn_norm[0], w_in[0], conv_w[0], conv_b[0], b_igate[0], b_fgate[0], ret_gn[0],
                 ml_gn[0], w_out[0], moe_norm[0], w_group[0], b_group[0], w_router[0],
                 b_router[0], w1[0], w3[0], w2[0], w_ple_up[0], ple_norm[0], ple_gate_norm[0],
                 w_ple_gate[0], final_norm)
    return out.reshape(B, S, D)
```

```python
import functools

import jax
import jax.numpy as jnp
from jax import lax
from jax.experimental import pallas as pl
from jax.experimental.pallas import tpu as pltpu

F32 = jnp.float32
BF16 = jnp.bfloat16

RET_HEADS = 8
ML_HEADS = 4
CHUNK = 128
CONV_W = 4
ROPE_BASE = 10000.0
N_GROUPS = 4
EXPERTS_PER_GROUP = 8
N_EXPERTS = N_GROUPS * EXPERTS_PER_GROUP
EPS = 1e-6

LANES = 128
VMEM_LIMIT = 56 * 1024 * 1024
EXPERT_BLOCK = 256
GROUP_LANE0 = N_EXPERTS
ROUTE_TILE = 256
ROUTE_ROWS = 8


def _rms(x, g):
    return x * lax.rsqrt(jnp.mean(x * x, axis=-1, keepdims=True) + EPS) * g


def _dot(a, b):
    return jnp.dot(a, b, preferred_element_type=F32)


def _dot_nt(a, b):
    return lax.dot_general(a, b, (((1,), (1,)), ((), ())), preferred_element_type=F32)


def _split3(x):
    hi = x.astype(BF16)
    r1 = x - hi.astype(F32)
    mid = r1.astype(BF16)
    lo = (r1 - mid.astype(F32)).astype(BF16)
    return hi, mid, lo


def _inproj_kernel(x_ref, g_ref, wm_ref, wg_ref, gb_ref, proj_ref, gate_ref, *, col_chunk):
    xn = _rms(x_ref[...], g_ref[...]).astype(BF16)
    n_main = wm_ref.shape[1]
    for c in range(0, n_main, col_chunk):
        proj_ref[:, c:c + col_chunk] = _dot(xn, wm_ref[:, c:c + col_chunk])
    gate_ref[...] = _dot(xn, wg_ref[...]) + gb_ref[...]


def _inproj(x2, g, w_main, w_gate, gate_bias, tm=256):
    T, D = x2.shape
    n_main = w_main.shape[1]
    const = lambda i: (0, 0)
    return pl.pallas_call(
        functools.partial(_inproj_kernel, col_chunk=512),
        grid=(T // tm,),
        in_specs=[
            pl.BlockSpec((tm, D), lambda i: (i, 0)),
            pl.BlockSpec((1, D), const),
            pl.BlockSpec((D, n_main), const),
            pl.BlockSpec((D, LANES), const),
            pl.BlockSpec((1, LANES), const),
        ],
        out_specs=[
            pl.BlockSpec((tm, n_main), lambda i: (i, 0)),
            pl.BlockSpec((tm, LANES), lambda i: (i, 0)),
        ],
        out_shape=[
            jax.ShapeDtypeStruct((T, n_main), F32),
            jax.ShapeDtypeStruct((T, LANES), F32),
        ],
        compiler_params=pltpu.CompilerParams(
            dimension_semantics=("arbitrary",), vmem_limit_bytes=VMEM_LIMIT),
        name="inproj",
    )(x2, g, w_main, w_gate, gate_bias)


def _mixer_kernel(proj_ref, gate_ref, pos_ref, freq_ref, sign_ref, decay_ref, ws_ref, wq_ref,
                  cd_ref, convw_ref, convb_ref, retgn_ref, mlgn_ref,
                  y_ref, r_state, s_state, n_state, m_state, conv_prev,
                  *, ret_w, ml_w):
    C = proj_ref.shape[0]
    n_pairs = ret_w // LANES
    ml_heads = ml_w // LANES
    ret_dh = LANES // 2
    o_rq, o_rk, o_rv, o_rg = 0, ret_w, 2 * ret_w, 3 * ret_w
    o_mq = 4 * ret_w
    o_mv = o_mq + 2 * ml_w
    o_mo = o_mv + ml_w

    @pl.when(pl.program_id(1) == 0)
    def _():
        r_state[...] = jnp.zeros_like(r_state)
        s_state[...] = jnp.zeros_like(s_state)
        n_state[...] = jnp.zeros_like(n_state)
        m_state[...] = jnp.zeros_like(m_state)
        conv_prev[...] = jnp.zeros_like(conv_prev)

    lane = lax.broadcasted_iota(jnp.int32, (C, LANES), 1)
    row = lax.broadcasted_iota(jnp.int32, (C, LANES), 0)
    lo = lane < ret_dh
    first_half = (lane % ret_dh) < (ret_dh // 2)
    causal = row >= lane
    blockdiag = (row < ret_dh) == lo

    ang = pos_ref[...].astype(F32) * freq_ref[...]
    cos_t = jnp.cos(ang)
    sin_t = jnp.sin(ang) * sign_ref[...]

    def rot(t):
        swapped = jnp.where(first_half, pltpu.roll(t, LANES - ret_dh // 2, 1),
                            pltpu.roll(t, ret_dh // 2, 1))
        return t * cos_t + swapped * sin_t

    for p in range(n_pairs):
        sl = lambda o: slice(o + p * LANES, o + (p + 1) * LANES)
        q = rot(proj_ref[:, sl(o_rq)])
        k = rot(proj_ref[:, sl(o_rk)]) * (ret_dh ** -0.5)
        v = proj_ref[:, sl(o_rv)]
        kb = k.astype(BF16)
        q_a = jnp.where(lo, q, 0.0).astype(BF16)
        q_b = jnp.where(lo, 0.0, q).astype(BF16)
        s_a = _dot_nt(q_a, kb) * decay_ref[2 * p]
        s_b = _dot_nt(q_b, kb) * decay_ref[2 * p + 1]
        v_a = jnp.where(lo, v, 0.0).astype(BF16)
        v_b = jnp.where(lo, 0.0, v).astype(BF16)
        r_prev = r_state[p]
        o = (_dot(s_a.astype(BF16), v_a) + _dot(s_b.astype(BF16), v_b)
             + _dot(q.astype(BF16), r_prev.astype(BF16)) * wq_ref[p])
        kw_t = (k * ws_ref[p]).T.astype(BF16)
        kv = _dot(kw_t, v.astype(BF16))
        r_state[p] = cd_ref[p] * r_prev + jnp.where(blockdiag, kv, 0.0)
        sq = o * o
        ms_a = jnp.sum(jnp.where(lo, sq, 0.0), axis=-1, keepdims=True)
        ms_b = jnp.sum(jnp.where(lo, 0.0, sq), axis=-1, keepdims=True)
        ms = jnp.where(lo, ms_a, ms_b) * (1.0 / ret_dh)
        r = o * lax.rsqrt(ms + EPS) * retgn_ref[:, p * LANES:(p + 1) * LANES]
        g = proj_ref[:, sl(o_rg)]
        y_ref[:, p * LANES:(p + 1) * LANES] = (g * jax.nn.sigmoid(g) * r).astype(y_ref.dtype)

    xqk = proj_ref[:, o_mq:o_mq + 2 * ml_w]
    prev = conv_prev[...]
    conv_prev[...] = xqk
    rowq = lax.broadcasted_iota(jnp.int32, xqk.shape, 0)
    acc = xqk * convw_ref[CONV_W - 1:CONV_W, :] + convb_ref[...]
    for s in range(1, CONV_W):
        shifted = jnp.where(rowq < s, pltpu.roll(prev, s, 0), pltpu.roll(xqk, s, 0))
        acc = acc + shifted * convw_ref[CONV_W - 1 - s:CONV_W - s, :]
    qk = acc * jax.nn.sigmoid(acc)

    G = gate_ref[...]
    logf = jax.nn.log_sigmoid(G)
    tri = (row >= lane).astype(BF16)
    l_hi, l_mid, l_lo = _split3(logf)
    bcum = _dot(tri, l_hi) + _dot(tri, l_mid) + _dot(tri, l_lo)
    GT = G.T
    bT = bcum.T

    for h in range(ml_heads):
        hs = slice(h * LANES, (h + 1) * LANES)
        q = qk[:, hs]
        k = qk[:, ml_w + h * LANES: ml_w + (h + 1) * LANES] * (LANES ** -0.5)
        v = proj_ref[:, o_mv + h * LANES:o_mv + (h + 1) * LANES]
        ig_col = G[:, h:h + 1]
        ig_row = GT[h:h + 1, :]
        b_col = bcum[:, ml_heads + h:ml_heads + h + 1]
        b_row = bT[ml_heads + h:ml_heads + h + 1, :]
        bL = b_col[C - 1:C, :]
        m_prev = m_state[h][:, 0:1]
        S_prev = s_state[h]
        n_prev = n_state[h]
        qb, kb, vb = q.astype(BF16), k.astype(BF16), v.astype(BF16)

        dlog = jnp.where(causal, b_col - b_row + ig_row, -jnp.inf)
        inter_log = b_col + m_prev
        m_row = jnp.maximum(jnp.max(dlog, axis=-1, keepdims=True), inter_log)
        P = jnp.exp(dlog - m_row) * _dot_nt(qb, kb)
        w_inter = jnp.exp(inter_log - m_row)
        num = _dot(P.astype(BF16), vb) + w_inter * _dot(qb, S_prev.astype(BF16))
        den = (jnp.sum(P, axis=-1, keepdims=True)
               + w_inter * jnp.sum(q * n_prev, axis=-1, keepdims=True))
        hh = num / jnp.maximum(jnp.abs(den), jnp.exp(-m_row))

        a_col = bL - b_col + ig_col
        m_new = jnp.maximum(bL + m_prev, jnp.max(a_col, axis=0, keepdims=True))
        w_col = jnp.exp(a_col - m_new)
        dec = jnp.exp(bL + m_prev - m_new)
        kw = k * w_col
        s_state[h] = dec * S_prev + _dot(kw.T.astype(BF16), vb)
        n_state[h] = dec * n_prev + jnp.sum(kw, axis=0, keepdims=True)
        m_state[h] = jnp.broadcast_to(m_new, (1, LANES))

        og = proj_ref[:, o_mo + h * LANES:o_mo + (h + 1) * LANES]
        hm = jax.nn.sigmoid(og) * hh
        hm = _rms(hm, mlgn_ref[:, hs])
        y_ref[:, ret_w + h * LANES:ret_w + (h + 1) * LANES] = hm.astype(y_ref.dtype)


def _ret_tables(C):
    H = RET_HEADS
    dh = LANES // 2
    log_gamma = jnp.log1p(-(2.0 ** (-5.0 - jnp.arange(H, dtype=F32))))
    idx = jnp.arange(C, dtype=F32)
    rel = idx[:, None] - idx[None, :]
    causal = rel >= 0
    decay = jnp.where(causal, jnp.exp(log_gamma[:, None, None] * jnp.where(causal, rel, 0.0)), 0.0)
    w_state = jnp.exp(log_gamma[:, None] * (C - 1 - idx))
    w_query = jnp.exp(log_gamma[:, None] * (idx + 1.0))
    chunk_decay = jnp.exp(log_gamma * C)
    pair = lambda t: jnp.repeat(t.reshape(H // 2, 2, C).transpose(0, 2, 1), dh, axis=2)
    cd = jnp.repeat(chunk_decay.reshape(H // 2, 2), dh, axis=1)
    cd = jnp.broadcast_to(cd[:, :, None], (H // 2, LANES, LANES))
    half = dh // 2
    freqs = ROPE_BASE ** (-jnp.arange(half, dtype=F32) / half)
    freq_row = jnp.tile(freqs, LANES // half)[None, :]
    sign_row = jnp.tile(jnp.concatenate([-jnp.ones((half,), F32), jnp.ones((half,), F32)]),
                        LANES // dh)[None, :]
    return freq_row, sign_row, decay, pair(w_state), pair(w_query), cd


def _mixer(proj, gates, positions, conv_w, conv_b, ret_gn, ml_gn, B, S, ret_w, ml_w):
    C = CHUNK
    N = S // C
    n_main = proj.shape[1]
    proj3 = proj.reshape(B, S, n_main)
    gates3 = gates.reshape(B, S, LANES)
    pos3 = positions.reshape(B, S, 1)
    freq_row, sign_row, decay, ws, wq, cd = _ret_tables(C)
    n_pairs = ret_w // LANES
    ml_heads = ml_w // LANES
    c2 = lambda b, n: (0, 0)
    c3 = lambda b, n: (0, 0, 0)
    y = pl.pallas_call(
        functools.partial(_mixer_kernel, ret_w=ret_w, ml_w=ml_w),
        grid=(B, N),
        in_specs=[
            pl.BlockSpec((None, C, n_main), lambda b, n: (b, n, 0)),
            pl.BlockSpec((None, C, LANES), lambda b, n: (b, n, 0)),
            pl.BlockSpec((None, C, 1), lambda b, n: (b, n, 0)),
            pl.BlockSpec((1, LANES), c2),
            pl.BlockSpec((1, LANES), c2),
            pl.BlockSpec((RET_HEADS, C, C), c3),
            pl.BlockSpec((n_pairs, C, LANES), c3),
            pl.BlockSpec((n_pairs, C, LANES), c3),
            pl.BlockSpec((n_pairs, LANES, LANES), c3),
            pl.BlockSpec((CONV_W, 2 * ml_w), c2),
            pl.BlockSpec((1, 2 * ml_w), c2),
            pl.BlockSpec((1, ret_w), c2),
            pl.BlockSpec((1, ml_w), c2),
        ],
        out_specs=pl.BlockSpec((None, C, ret_w + ml_w), lambda b, n: (b, n, 0)),
        out_shape=jax.ShapeDtypeStruct((B, S, ret_w + ml_w), BF16),
        scratch_shapes=[
            pltpu.VMEM((n_pairs, LANES, LANES), F32),
            pltpu.VMEM((ml_heads, LANES, LANES), F32),
            pltpu.VMEM((ml_heads, 1, LANES), F32),
            pltpu.VMEM((ml_heads, 1, LANES), F32),
            pltpu.VMEM((C, 2 * ml_w), F32),
        ],
        compiler_params=pltpu.CompilerParams(
            dimension_semantics=("arbitrary", "arbitrary"), vmem_limit_bytes=VMEM_LIMIT),
        name="mixer",
    )(proj3, gates3, pos3, freq_row, sign_row, decay, ws, wq, cd,
      conv_w, conv_b.reshape(1, -1), ret_gn.reshape(1, -1), ml_gn.reshape(1, -1))
    return y.reshape(B * S, ret_w + ml_w)


def _router_kernel(y_ref, x_ref, wo_ref, g_ref, wr_hi_ref, wr_lo_ref, br_ref,
                   h_ref, xn_ref, ri_ref, rf_ref, cnt_ref, run_cnt):
    tm = y_ref.shape[0]

    @pl.when(pl.program_id(0) == 0)
    def _():
        run_cnt[...] = jnp.zeros_like(run_cnt)

    h = x_ref[...] + _dot(y_ref[...], wo_ref[...])
    h_ref[...] = h
    xn = _rms(h, g_ref[...])
    xn_ref[...] = xn
    x_hi = xn.astype(BF16)
    x_lo = (xn - x_hi.astype(F32)).astype(BF16)
    logits = (_dot(x_hi, wr_hi_ref[...]) + _dot(x_lo, wr_hi_ref[...])
              + _dot(x_hi, wr_lo_ref[...]) + br_ref[...])

    lane = lax.broadcasted_iota(jnp.int32, (tm, LANES), 1)
    big = jnp.int32(LANES)
    neg = -jnp.inf
    is_g = (lane >= GROUP_LANE0) & (lane < GROUP_LANE0 + N_GROUPS)
    gl = jnp.where(is_g, logits, neg)
    gmax = jnp.max(gl, axis=-1, keepdims=True)
    gsum = jnp.sum(jnp.where(is_g, jnp.exp(gl - gmax), 0.0), axis=-1, keepdims=True)
    p_g = 1.0 / gsum
    g_sel = jnp.min(jnp.where(is_g & (gl == gmax), lane - GROUP_LANE0, big), axis=-1, keepdims=True)
    in_grp = (lane < N_EXPERTS) & ((lane // EXPERTS_PER_GROUP) == g_sel)
    el = jnp.where(in_grp, logits, neg)
    emax = jnp.max(el, axis=-1, keepdims=True)
    eexp = jnp.where(in_grp, jnp.exp(el - emax), 0.0)
    prob = eexp / jnp.sum(eexp, axis=-1, keepdims=True)
    pm1 = jnp.where(in_grp, prob, -1.0)
    p1 = jnp.max(pm1, axis=-1, keepdims=True)
    i1 = jnp.min(jnp.where(pm1 == p1, lane, big), axis=-1, keepdims=True)
    pm2 = jnp.where(lane == i1, -1.0, pm1)
    p2 = jnp.max(pm2, axis=-1, keepdims=True)
    i2 = jnp.min(jnp.where(pm2 == p2, lane, big), axis=-1, keepdims=True)
    denom = p1 + p2
    g1 = p_g * p1 / denom
    g2 = p_g * p2 / denom

    sel1 = lane == i1
    sel2 = lane == i2
    onehot = (sel1 | sel2).astype(BF16)
    r_i = lax.broadcasted_iota(jnp.int32, (tm, tm), 0)
    c_i = lax.broadcasted_iota(jnp.int32, (tm, tm), 1)
    tri = (r_i > c_i).astype(BF16)
    prefix = _dot(tri, onehot) + run_cnt[...]
    rank1 = jnp.sum(jnp.where(sel1, prefix, 0.0), axis=-1, keepdims=True).astype(jnp.int32)
    rank2 = jnp.sum(jnp.where(sel2, prefix, 0.0), axis=-1, keepdims=True).astype(jnp.int32)
    new_cnt = run_cnt[...] + jnp.sum(onehot.astype(F32), axis=0, keepdims=True)
    run_cnt[...] = new_cnt
    cnt_ref[...] = new_cnt.astype(jnp.int32)

    ri = jnp.where(lane == 0, i1, jnp.where(lane == 1, i2,
                   jnp.where(lane == 2, rank1, jnp.where(lane == 3, rank2, 0))))
    ri_ref[...] = ri.astype(F32).T[0:ROUTE_ROWS, :].astype(jnp.int32)
    rf_ref[...] = jnp.where(lane == 0, g1, jnp.where(lane == 1, g2, 0.0))


def _router(y, x2, w_out, g, wr_hi, wr_lo, br, tm=ROUTE_TILE):
    T, D = x2.shape
    const = lambda i: (0, 0)
    tile = lambda i: (i, 0)
    return pl.pallas_call(
        _router_kernel,
        grid=(T // tm,),
        in_specs=[
            pl.BlockSpec((tm, y.shape[1]), tile),
            pl.BlockSpec((tm, D), tile),
            pl.BlockSpec(w_out.shape, const),
            pl.BlockSpec((1, D), const),
            pl.BlockSpec((D, LANES), const),
            pl.BlockSpec((D, LANES), const),
            pl.BlockSpec((1, LANES), const),
        ],
        out_specs=[
            pl.BlockSpec((tm, D), tile),
            pl.BlockSpec((tm, D), tile),
            pl.BlockSpec((None, ROUTE_ROWS, tm), lambda i: (i, 0, 0)),
            pl.BlockSpec((tm, LANES), tile),
            pl.BlockSpec((1, LANES), const),
        ],
        out_shape=[
            jax.ShapeDtypeStruct((T, D), F32),
            jax.ShapeDtypeStruct((T, D), F32),
            jax.ShapeDtypeStruct((T // tm, ROUTE_ROWS, tm), jnp.int32),
            jax.ShapeDtypeStruct((T, LANES), F32),
            jax.ShapeDtypeStruct((1, LANES), jnp.int32),
        ],
        scratch_shapes=[pltpu.VMEM((1, LANES), F32)],
        compiler_params=pltpu.CompilerParams(
            dimension_semantics=("arbitrary",), vmem_limit_bytes=VMEM_LIMIT),
        name="router",
    )(y, x2, w_out, g, wr_hi, wr_lo, br)


def _for_each_pad_block(pend_ref, nu_ref, n_blocks, fn):
    blk = EXPERT_BLOCK
    for e in range(N_EXPERTS):
        prev_end = 0 if e == 0 else pend_ref[e - 1]

        @pl.when(pend_ref[e] > prev_end)
        def _():
            fn(pl.multiple_of(pend_ref[e] - blk, blk))

    def tail(j, c):
        fn(pl.multiple_of(j * blk, blk))
        return c

    lax.fori_loop(nu_ref[0], n_blocks, tail, 0)


def _dispatch_kernel(dest_ref, pend_ref, nu_ref, xn_ref, buf_ref, zeros, sem, zsem):
    tm = xn_ref.shape[0]
    blk = EXPERT_BLOCK
    i = pl.program_id(0)

    @pl.when(i == 0)
    def _():
        zeros[...] = jnp.zeros_like(zeros)
        zcopy = lambda row: pltpu.make_async_copy(zeros, buf_ref.at[pl.ds(row, blk)], zsem)
        n_blocks = buf_ref.shape[0] // blk
        _for_each_pad_block(pend_ref, nu_ref, n_blocks, lambda row: zcopy(row).start())
        _for_each_pad_block(pend_ref, nu_ref, n_blocks, lambda row: zcopy(row).wait())

    subs = tm // ROUTE_TILE
    for sub in range(subs):
        slot0 = (i * subs + sub) * (2 * ROUTE_TILE)

        def issue(t, c, sub=sub, slot0=slot0):
            for k in range(2):
                d = dest_ref[slot0 + k * ROUTE_TILE + t]
                pltpu.make_async_copy(xn_ref.at[pl.ds(sub * ROUTE_TILE + t, 1)],
                                      buf_ref.at[pl.ds(d, 1)], sem).start()
            return c

        lax.fori_loop(0, ROUTE_TILE, issue, 0, unroll=8)
    for k in range(2):
        pltpu.make_async_copy(xn_ref, buf_ref.at[pl.ds(0, tm)], sem).wait()


def _dispatch(dest, pad_end, n_used, xn, n_rows, tm=2 * ROUTE_TILE):
    T, D = xn.shape
    return pl.pallas_call(
        _dispatch_kernel,
        grid_spec=pltpu.PrefetchScalarGridSpec(
            num_scalar_prefetch=3,
            grid=(T // tm,),
            in_specs=[pl.BlockSpec((tm, D), lambda i, *_: (i, 0))],
            out_specs=pl.BlockSpec(memory_space=pl.ANY),
            scratch_shapes=[
                pltpu.VMEM((EXPERT_BLOCK, D), xn.dtype),
                pltpu.SemaphoreType.DMA,
                pltpu.SemaphoreType.DMA,
            ],
        ),
        out_shape=jax.ShapeDtypeStruct((n_rows, D), xn.dtype),
        compiler_params=pltpu.CompilerParams(
            dimension_semantics=("arbitrary",), vmem_limit_bytes=VMEM_LIMIT),
        name="dispatch",
    )(dest, pad_end, n_used, xn)


def _expert_kernel(be_ref, nu_ref, x_ref, w1_ref, w3_ref, w2_ref, y_ref, w1b, w3b, w2b):
    j = pl.program_id(0)
    used = j < nu_ref[0]
    prev_expert = be_ref[jnp.maximum(j, 1) - 1]

    @pl.when(used & ((j == 0) | (be_ref[j] != prev_expert)))
    def _():
        w1b[...] = w1_ref[...].astype(BF16)
        w3b[...] = w3_ref[...].astype(BF16)
        w2b[...] = w2_ref[...].astype(BF16)

    @pl.when(used)
    def _():
        x = x_ref[...].astype(BF16)
        a = _dot(x, w1b[...])
        hmid = a * jax.nn.sigmoid(a) * _dot(x, w3b[...])
        y_ref[...] = _dot(hmid.astype(BF16), w2b[...])

    @pl.when(jnp.logical_not(used))
    def _():
        y_ref[...] = jnp.zeros_like(y_ref)


def _experts(block_expert, n_used, x_buf, w1, w3, w2):
    P, D = x_buf.shape
    blk = EXPERT_BLOCK
    d_exp = w2.shape[1]
    clamp = lambda j, nu: jnp.minimum(j, nu[0] - 1)
    w_idx = lambda j, be, nu: (be[clamp(j, nu)], 0, 0)
    return pl.pallas_call(
        _expert_kernel,
        grid_spec=pltpu.PrefetchScalarGridSpec(
            num_scalar_prefetch=2,
            grid=(P // blk,),
            in_specs=[
                pl.BlockSpec((blk, D), lambda j, be, nu: (clamp(j, nu), 0)),
                pl.BlockSpec((None, D, d_exp), w_idx),
                pl.BlockSpec((None, D, d_exp), w_idx),
                pl.BlockSpec((None, d_exp, D), w_idx),
            ],
            out_specs=pl.BlockSpec((blk, D), lambda j, be, nu: (j, 0)),
            scratch_shapes=[
                pltpu.VMEM((D, d_exp), BF16),
                pltpu.VMEM((D, d_exp), BF16),
                pltpu.VMEM((d_exp, D), BF16),
            ],
        ),
        out_shape=jax.ShapeDtypeStruct((P, D), F32),
        compiler_params=pltpu.CompilerParams(
            dimension_semantics=("arbitrary",), vmem_limit_bytes=VMEM_LIMIT),
        name="experts",
    )(block_expert, n_used, x_buf, w1, w3, w2)


def _final_kernel(dest_ref, h_ref, rf_ref, p_ref, wup_ref, gple_ref, ggate_ref, wgate_ref,
                  gfin_ref, ybuf_ref, out_ref, y0, y1, sem):
    tm = h_ref.shape[0]
    subs = tm // ROUTE_TILE
    ys = (y0, y1)
    for sub in range(subs):
        slot0 = (pl.program_id(0) * subs + sub) * (2 * ROUTE_TILE)

        def issue(t, c, sub=sub, slot0=slot0):
            for k in range(2):
                d = dest_ref[slot0 + k * ROUTE_TILE + t]
                pltpu.make_async_copy(ybuf_ref.at[pl.ds(d, 1)],
                                      ys[k].at[pl.ds(sub * ROUTE_TILE + t, 1)], sem).start()
            return c

        lax.fori_loop(0, ROUTE_TILE, issue, 0, unroll=8)
    e = _rms(_dot(p_ref[...].astype(BF16), wup_ref[...]), gple_ref[...])
    for k in range(2):
        pltpu.make_async_copy(ybuf_ref.at[pl.ds(0, tm)], ys[k], sem).wait()

    rf = rf_ref[...]
    h = h_ref[...] + rf[:, 0:1] * y0[...] + rf[:, 1:2] * y1[...]
    gate = jax.nn.sigmoid(_dot(_rms(h, ggate_ref[...]).astype(BF16), wgate_ref[...]))
    h = h + gate * e
    out_ref[...] = _rms(h, gfin_ref[...])


def _final(dest, h, rf, p2, w_up, g_ple, g_gate, w_gate, g_fin, y_buf, tm=256):
    T, D = h.shape
    const = lambda i, d: (0, 0)
    tile = lambda i, d: (i, 0)
    return pl.pallas_call(
        _final_kernel,
        grid_spec=pltpu.PrefetchScalarGridSpec(
            num_scalar_prefetch=1,
            grid=(T // tm,),
            in_specs=[
                pl.BlockSpec((tm, D), tile),
                pl.BlockSpec((tm, LANES), tile),
                pl.BlockSpec((tm, p2.shape[1]), tile),
                pl.BlockSpec(w_up.shape, const),
                pl.BlockSpec((1, D), const),
                pl.BlockSpec((1, D), const),
                pl.BlockSpec(w_gate.shape, const),
                pl.BlockSpec((1, D), const),
                pl.BlockSpec(memory_space=pl.ANY),
            ],
            out_specs=pl.BlockSpec((tm, D), tile),
            scratch_shapes=[
                pltpu.VMEM((tm, D), F32),
                pltpu.VMEM((tm, D), F32),
                pltpu.SemaphoreType.DMA,
            ],
        ),
        out_shape=jax.ShapeDtypeStruct((T, D), F32),
        compiler_params=pltpu.CompilerParams(
            dimension_semantics=("arbitrary",), vmem_limit_bytes=VMEM_LIMIT),
        name="final",
    )(dest, h, rf, p2, w_up, g_ple, g_gate, w_gate, g_fin, y_buf)


def _layer(h2, p2, positions, B, S, attn_norm, w_in, conv_w, conv_b, b_igate, b_fgate, ret_gn,
           ml_gn, w_out, moe_norm, w_group, b_group, w_router, b_router, w1, w3, w2, w_ple_up,
           ple_norm, ple_gate_norm, w_ple_gate, out_norm):
    T, D = h2.shape
    ret_w = ret_gn.shape[0]
    ml_w = ml_gn.shape[0]
    n_main = 4 * ret_w + 4 * ml_w
    row = lambda v: v.reshape(1, -1).astype(F32)
    pad_lanes = lambda a: jnp.pad(a, ((0, 0), (0, LANES - a.shape[1])))

    w_main = w_in[:, :n_main].astype(BF16)
    w_gate = pad_lanes(w_in[:, n_main:]).astype(BF16)
    gate_bias = pad_lanes(jnp.concatenate([b_igate, b_fgate])[None, :].astype(F32))
    proj, gates = _inproj(h2, row(attn_norm), w_main, w_gate, gate_bias)

    y = _mixer(proj, gates, positions, conv_w, conv_b, ret_gn, ml_gn, B, S, ret_w, ml_w)

    wr = pad_lanes(jnp.concatenate([w_router, w_group], axis=1))
    wr_hi = wr.astype(BF16)
    wr_lo = (wr - wr_hi.astype(F32)).astype(BF16)
    br = pad_lanes(jnp.concatenate([b_router, b_group])[None, :].astype(F32))
    h_mid, xn, ri, rf, counts = _router(y, h2, w_out.astype(BF16), row(moe_norm), wr_hi, wr_lo, br)

    blk = EXPERT_BLOCK
    counts = counts[0, :N_EXPERTS]
    padded = (counts + blk - 1) // blk * blk
    pad_end = jnp.cumsum(padded)
    pad_start = pad_end - padded
    n_blocks = (2 * T) // blk + N_EXPERTS
    block_start = jnp.arange(n_blocks, dtype=jnp.int32) * blk
    block_expert = jnp.minimum(jnp.sum(pad_end[None, :] <= block_start[:, None], axis=1),
                               N_EXPERTS - 1).astype(jnp.int32)
    n_used = (pad_end[-1:] // blk).astype(jnp.int32)
    dest = (pad_start[ri[:, 0:2, :]] + ri[:, 2:4, :]).astype(jnp.int32).reshape(-1)

    x_buf = _dispatch(dest, pad_end.astype(jnp.int32), n_used, xn, n_blocks * blk)
    y_buf = _experts(block_expert, n_used, x_buf, w1, w3, w2)

    return _final(dest, h_mid, rf, p2, w_ple_up.astype(BF16), row(ple_norm), row(ple_gate_norm),
                  w_ple_gate.astype(BF16), row(out_norm), y_buf)


def kernel(x, p, positions, attn_norm, w_in, conv_w, conv_b, b_igate, b_fgate, ret_gn, ml_gn,
           w_out, moe_norm, w_group, b_group, w_router, b_router, w1, w3, w2, w_ple_up, ple_norm,
           ple_gate_norm, w_ple_gate, final_norm):
    B, S, D = x.shape
    depth = p.shape[0]
    assert depth == 1, "the final RMSNorm is fused into the layer's last kernel"
    out = _layer(x.reshape(B * S, D), p[0].reshape(B * S, -1), positions, B, S,
                 attn_norm[0], w_in[0], conv_w[0], conv_b[0], b_igate[0], b_fgate[0], ret_gn[0],
                 ml_gn[0], w_out[0], moe_norm[0], w_group[0], b_group[0], w_router[0],
                 b_router[0], w1[0], w3[0], w2[0], w_ple_up[0], ple_norm[0], ple_gate_norm[0],
                 w_ple_gate[0], final_norm)
    return out.reshape(B, S, D)
```

```python
import functools

import jax
import jax.numpy as jnp
from jax import lax
from jax.experimental import pallas as pl
from jax.experimental.pallas import tpu as pltpu

F32 = jnp.float32
BF16 = jnp.bfloat16

RET_HEADS = 8
ML_HEADS = 4
CHUNK = 128
CONV_W = 4
ROPE_BASE = 10000.0
N_GROUPS = 4
EXPERTS_PER_GROUP = 8
N_EXPERTS = N_GROUPS * EXPERTS_PER_GROUP
EPS = 1e-6

LANES = 128
VMEM_LIMIT = 56 * 1024 * 1024
EXPERT_BLOCK = 512
GROUP_ROW0 = N_EXPERTS
ROUTER_ROWS = 64
ROUTE_TILE = 512
ROUTE_ROWS = 8


def _rms(x, g):
    return x * lax.rsqrt(jnp.mean(x * x, axis=-1, keepdims=True) + EPS) * g


def _dot(a, b):
    return jnp.dot(a, b, preferred_element_type=F32)


def _dot_nt(a, b):
    return lax.dot_general(a, b, (((1,), (1,)), ((), ())), preferred_element_type=F32)


def _split3(x):
    hi = x.astype(BF16)
    r1 = x - hi.astype(F32)
    mid = r1.astype(BF16)
    lo = (r1 - mid.astype(F32)).astype(BF16)
    return hi, mid, lo


def _inproj_kernel(x_ref, g_ref, wm_ref, wg_ref, gb_ref, proj_ref, gate_ref, *, col_chunk):
    xn = _rms(x_ref[...], g_ref[...]).astype(BF16)
    n_main = wm_ref.shape[1]
    for c in range(0, n_main, col_chunk):
        proj_ref[:, c:c + col_chunk] = _dot(xn, wm_ref[:, c:c + col_chunk])
    gate_ref[...] = _dot(xn, wg_ref[...]) + gb_ref[...]


def _inproj(x2, g, w_main, w_gate, gate_bias, tm=256):
    T, D = x2.shape
    n_main = w_main.shape[1]
    const = lambda i: (0, 0)
    return pl.pallas_call(
        functools.partial(_inproj_kernel, col_chunk=512),
        grid=(T // tm,),
        in_specs=[
            pl.BlockSpec((tm, D), lambda i: (i, 0)),
            pl.BlockSpec((1, D), const),
            pl.BlockSpec((D, n_main), const),
            pl.BlockSpec((D, LANES), const),
            pl.BlockSpec((1, LANES), const),
        ],
        out_specs=[
            pl.BlockSpec((tm, n_main), lambda i: (i, 0)),
            pl.BlockSpec((tm, LANES), lambda i: (i, 0)),
        ],
        out_shape=[
            jax.ShapeDtypeStruct((T, n_main), F32),
            jax.ShapeDtypeStruct((T, LANES), F32),
        ],
        compiler_params=pltpu.CompilerParams(
            dimension_semantics=("arbitrary",), vmem_limit_bytes=VMEM_LIMIT),
        name="inproj",
    )(x2, g, w_main, w_gate, gate_bias)


def _mixer_kernel(proj_ref, gate_ref, pos_ref, freq_ref, sign_ref, decay_ref, ws_ref, wq_ref,
                  cd_ref, convw_ref, convb_ref, retgn_ref, mlgn_ref,
                  y_ref, r_state, s_state, n_state, m_state, conv_prev,
                  *, ret_w, ml_w):
    C = proj_ref.shape[0]
    n_pairs = ret_w // LANES
    ml_heads = ml_w // LANES
    ret_dh = LANES // 2
    o_rq, o_rk, o_rv, o_rg = 0, ret_w, 2 * ret_w, 3 * ret_w
    o_mq = 4 * ret_w
    o_mv = o_mq + 2 * ml_w
    o_mo = o_mv + ml_w

    @pl.when(pl.program_id(1) == 0)
    def _():
        r_state[...] = jnp.zeros_like(r_state)
        s_state[...] = jnp.zeros_like(s_state)
        n_state[...] = jnp.zeros_like(n_state)
        m_state[...] = jnp.zeros_like(m_state)
        conv_prev[...] = jnp.zeros_like(conv_prev)

    lane = lax.broadcasted_iota(jnp.int32, (C, LANES), 1)
    row = lax.broadcasted_iota(jnp.int32, (C, LANES), 0)
    lo = lane < ret_dh
    first_half = (lane % ret_dh) < (ret_dh // 2)
    causal = row >= lane
    blockdiag = (row < ret_dh) == lo

    ang = pos_ref[...].astype(F32) * freq_ref[...]
    cos_t = jnp.cos(ang)
    sin_t = jnp.sin(ang) * sign_ref[...]

    def rot(t):
        swapped = jnp.where(first_half, pltpu.roll(t, LANES - ret_dh // 2, 1),
                            pltpu.roll(t, ret_dh // 2, 1))
        return t * cos_t + swapped * sin_t

    for p in range(n_pairs):
        sl = lambda o: slice(o + p * LANES, o + (p + 1) * LANES)
        q = rot(proj_ref[:, sl(o_rq)])
        k = rot(proj_ref[:, sl(o_rk)]) * (ret_dh ** -0.5)
        v = proj_ref[:, sl(o_rv)]
        kb = k.astype(BF16)
        q_a = jnp.where(lo, q, 0.0).astype(BF16)
        q_b = jnp.where(lo, 0.0, q).astype(BF16)
        s_a = _dot_nt(q_a, kb) * decay_ref[2 * p]
        s_b = _dot_nt(q_b, kb) * decay_ref[2 * p + 1]
        v_a = jnp.where(lo, v, 0.0).astype(BF16)
        v_b = jnp.where(lo, 0.0, v).astype(BF16)
        r_prev = r_state[p]
        o = (_dot(s_a.astype(BF16), v_a) + _dot(s_b.astype(BF16), v_b)
             + _dot(q.astype(BF16), r_prev.astype(BF16)) * wq_ref[p])
        kw_t = (k * ws_ref[p]).T.astype(BF16)
        kv = _dot(kw_t, v.astype(BF16))
        r_state[p] = cd_ref[p] * r_prev + jnp.where(blockdiag, kv, 0.0)
        sq = o * o
        ms_a = jnp.sum(jnp.where(lo, sq, 0.0), axis=-1, keepdims=True)
        ms_b = jnp.sum(jnp.where(lo, 0.0, sq), axis=-1, keepdims=True)
        ms = jnp.where(lo, ms_a, ms_b) * (1.0 / ret_dh)
        r = o * lax.rsqrt(ms + EPS) * retgn_ref[:, p * LANES:(p + 1) * LANES]
        g = proj_ref[:, sl(o_rg)]
        y_ref[:, p * LANES:(p + 1) * LANES] = (g * jax.nn.sigmoid(g) * r).astype(y_ref.dtype)

    xqk = proj_ref[:, o_mq:o_mq + 2 * ml_w]
    prev = conv_prev[...]
    conv_prev[...] = xqk
    rowq = lax.broadcasted_iota(jnp.int32, xqk.shape, 0)
    acc = xqk * convw_ref[CONV_W - 1:CONV_W, :] + convb_ref[...]
    for s in range(1, CONV_W):
        shifted = jnp.where(rowq < s, pltpu.roll(prev, s, 0), pltpu.roll(xqk, s, 0))
        acc = acc + shifted * convw_ref[CONV_W - 1 - s:CONV_W - s, :]
    qk = acc * jax.nn.sigmoid(acc)

    G = gate_ref[...]
    logf = jax.nn.log_sigmoid(G)
    tri = (row >= lane).astype(BF16)
    l_hi, l_mid, l_lo = _split3(logf)
    bcum = _dot(tri, l_hi) + _dot(tri, l_mid) + _dot(tri, l_lo)
    GT = G.T
    bT = bcum.T

    for h in range(ml_heads):
        hs = slice(h * LANES, (h + 1) * LANES)
        q = qk[:, hs]
        k = qk[:, ml_w + h * LANES: ml_w + (h + 1) * LANES] * (LANES ** -0.5)
        v = proj_ref[:, o_mv + h * LANES:o_mv + (h + 1) * LANES]
        ig_col = G[:, h:h + 1]
        ig_row = GT[h:h + 1, :]
        b_col = bcum[:, ml_heads + h:ml_heads + h + 1]
        b_row = bT[ml_heads + h:ml_heads + h + 1, :]
        bL = b_col[C - 1:C, :]
        m_prev = m_state[h][:, 0:1]
        S_prev = s_state[h]
        n_prev = n_state[h]
        qb, kb, vb = q.astype(BF16), k.astype(BF16), v.astype(BF16)

        dlog = jnp.where(causal, b_col - b_row + ig_row, -jnp.inf)
        inter_log = b_col + m_prev
        m_row = jnp.maximum(jnp.max(dlog, axis=-1, keepdims=True), inter_log)
        P = jnp.exp(dlog - m_row) * _dot_nt(qb, kb)
        w_inter = jnp.exp(inter_log - m_row)
        num = _dot(P.astype(BF16), vb) + w_inter * _dot(qb, S_prev.astype(BF16))
        den = (jnp.sum(P, axis=-1, keepdims=True)
               + w_inter * jnp.sum(q * n_prev, axis=-1, keepdims=True))
        hh = num / jnp.maximum(jnp.abs(den), jnp.exp(-m_row))

        a_col = bL - b_col + ig_col
        m_new = jnp.maximum(bL + m_prev, jnp.max(a_col, axis=0, keepdims=True))
        w_col = jnp.exp(a_col - m_new)
        dec = jnp.exp(bL + m_prev - m_new)
        kw = k * w_col
        s_state[h] = dec * S_prev + _dot(kw.T.astype(BF16), vb)
        n_state[h] = dec * n_prev + jnp.sum(kw, axis=0, keepdims=True)
        m_state[h] = jnp.broadcast_to(m_new, (1, LANES))

        og = proj_ref[:, o_mo + h * LANES:o_mo + (h + 1) * LANES]
        hm = jax.nn.sigmoid(og) * hh
        hm = _rms(hm, mlgn_ref[:, hs])
        y_ref[:, ret_w + h * LANES:ret_w + (h + 1) * LANES] = hm.astype(y_ref.dtype)


def _ret_tables(C):
    H = RET_HEADS
    dh = LANES // 2
    log_gamma = jnp.log1p(-(2.0 ** (-5.0 - jnp.arange(H, dtype=F32))))
    idx = jnp.arange(C, dtype=F32)
    rel = idx[:, None] - idx[None, :]
    causal = rel >= 0
    decay = jnp.where(causal, jnp.exp(log_gamma[:, None, None] * jnp.where(causal, rel, 0.0)), 0.0)
    w_state = jnp.exp(log_gamma[:, None] * (C - 1 - idx))
    w_query = jnp.exp(log_gamma[:, None] * (idx + 1.0))
    chunk_decay = jnp.exp(log_gamma * C)
    pair = lambda t: jnp.repeat(t.reshape(H // 2, 2, C).transpose(0, 2, 1), dh, axis=2)
    cd = jnp.repeat(chunk_decay.reshape(H // 2, 2), dh, axis=1)
    cd = jnp.broadcast_to(cd[:, :, None], (H // 2, LANES, LANES))
    half = dh // 2
    freqs = ROPE_BASE ** (-jnp.arange(half, dtype=F32) / half)
    freq_row = jnp.tile(freqs, LANES // half)[None, :]
    sign_row = jnp.tile(jnp.concatenate([-jnp.ones((half,), F32), jnp.ones((half,), F32)]),
                        LANES // dh)[None, :]
    return freq_row, sign_row, decay, pair(w_state), pair(w_query), cd


def _mixer(proj, gates, positions, conv_w, conv_b, ret_gn, ml_gn, B, S, ret_w, ml_w):
    C = CHUNK
    N = S // C
    n_main = proj.shape[1]
    proj3 = proj.reshape(B, S, n_main)
    gates3 = gates.reshape(B, S, LANES)
    pos3 = positions.reshape(B, S, 1)
    freq_row, sign_row, decay, ws, wq, cd = _ret_tables(C)
    n_pairs = ret_w // LANES
    ml_heads = ml_w // LANES
    c2 = lambda b, n: (0, 0)
    c3 = lambda b, n: (0, 0, 0)
    y = pl.pallas_call(
        functools.partial(_mixer_kernel, ret_w=ret_w, ml_w=ml_w),
        grid=(B, N),
        in_specs=[
            pl.BlockSpec((None, C, n_main), lambda b, n: (b, n, 0)),
            pl.BlockSpec((None, C, LANES), lambda b, n: (b, n, 0)),
            pl.BlockSpec((None, C, 1), lambda b, n: (b, n, 0)),
            pl.BlockSpec((1, LANES), c2),
            pl.BlockSpec((1, LANES), c2),
            pl.BlockSpec((RET_HEADS, C, C), c3),
            pl.BlockSpec((n_pairs, C, LANES), c3),
            pl.BlockSpec((n_pairs, C, LANES), c3),
            pl.BlockSpec((n_pairs, LANES, LANES), c3),
            pl.BlockSpec((CONV_W, 2 * ml_w), c2),
            pl.BlockSpec((1, 2 * ml_w), c2),
            pl.BlockSpec((1, ret_w), c2),
            pl.BlockSpec((1, ml_w), c2),
        ],
        out_specs=pl.BlockSpec((None, C, ret_w + ml_w), lambda b, n: (b, n, 0)),
        out_shape=jax.ShapeDtypeStruct((B, S, ret_w + ml_w), BF16),
        scratch_shapes=[
            pltpu.VMEM((n_pairs, LANES, LANES), F32),
            pltpu.VMEM((ml_heads, LANES, LANES), F32),
            pltpu.VMEM((ml_heads, 1, LANES), F32),
            pltpu.VMEM((ml_heads, 1, LANES), F32),
            pltpu.VMEM((C, 2 * ml_w), F32),
        ],
        compiler_params=pltpu.CompilerParams(
            dimension_semantics=("arbitrary", "arbitrary"), vmem_limit_bytes=VMEM_LIMIT),
        name="mixer",
    )(proj3, gates3, pos3, freq_row, sign_row, decay, ws, wq, cd,
      conv_w, conv_b.reshape(1, -1), ret_gn.reshape(1, -1), ml_gn.reshape(1, -1))
    return y.reshape(B * S, ret_w + ml_w)


def _router_kernel(y_ref, x_ref, wo_ref, g_ref, wr_hi_ref, wr_lo_ref, br_ref,
                   h_ref, xn_ref, ri_ref, rf_ref, cnt_ref, run_cnt):
    tm = y_ref.shape[0]

    @pl.when(pl.program_id(0) == 0)
    def _():
        run_cnt[...] = jnp.zeros_like(run_cnt)

    h = x_ref[...] + _dot(y_ref[...], wo_ref[...])
    h_ref[...] = h
    xn = _rms(h, g_ref[...])
    xn_ref[...] = xn
    x_hi = xn.astype(BF16)
    x_lo = (xn - x_hi.astype(F32)).astype(BF16)
    logits = (_dot_nt(wr_hi_ref[...], x_hi) + _dot_nt(wr_hi_ref[...], x_lo)
              + _dot_nt(wr_lo_ref[...], x_hi) + br_ref[...])
    big = jnp.int32(LANES)
    neg = -jnp.inf
    gl = logits[GROUP_ROW0:GROUP_ROW0 + 8]
    grow = lax.broadcasted_iota(jnp.int32, gl.shape, 0)
    is_g = grow < N_GROUPS
    gl = jnp.where(is_g, gl, neg)
    gmax = jnp.max(gl, axis=0, keepdims=True)
    gsum = jnp.sum(jnp.where(is_g, jnp.exp(gl - gmax), 0.0), axis=0, keepdims=True)
    p_g = 1.0 / gsum
    g_sel = jnp.min(jnp.where(is_g & (gl == gmax), grow, big), axis=0, keepdims=True)
    el = logits[0:N_EXPERTS]
    erow = lax.broadcasted_iota(jnp.int32, el.shape, 0)
    in_grp = (erow // EXPERTS_PER_GROUP) == g_sel
    el = jnp.where(in_grp, el, neg)
    emax = jnp.max(el, axis=0, keepdims=True)
    eexp = jnp.where(in_grp, jnp.exp(el - emax), 0.0)
    prob = eexp / jnp.sum(eexp, axis=0, keepdims=True)
    pm1 = jnp.where(in_grp, prob, -1.0)
    p1 = jnp.max(pm1, axis=0, keepdims=True)
    i1 = jnp.min(jnp.where(pm1 == p1, erow, big), axis=0, keepdims=True)
    pm2 = jnp.where(erow == i1, -1.0, pm1)
    p2 = jnp.max(pm2, axis=0, keepdims=True)
    i2 = jnp.min(jnp.where(pm2 == p2, erow, big), axis=0, keepdims=True)
    denom = p1 + p2
    g1 = p_g * p1 / denom
    g2 = p_g * p2 / denom

    sel1 = erow == i1
    sel2 = erow == i2
    onehot = (sel1 | sel2).astype(BF16)
    r_i = lax.broadcasted_iota(jnp.int32, (tm, tm), 0)
    c_i = lax.broadcasted_iota(jnp.int32, (tm, tm), 1)
    tri = (r_i < c_i).astype(BF16)
    prefix = _dot(onehot, tri) + run_cnt[:, 0:1]
    rank1 = jnp.sum(jnp.where(sel1, prefix, 0.0), axis=0, keepdims=True).astype(jnp.int32)
    rank2 = jnp.sum(jnp.where(sel2, prefix, 0.0), axis=0, keepdims=True).astype(jnp.int32)
    new_cnt = run_cnt[...] + jnp.sum(onehot.astype(F32), axis=1, keepdims=True)
    run_cnt[...] = new_cnt
    cnt_ref[...] = new_cnt.astype(jnp.int32)

    rrow = lax.broadcasted_iota(jnp.int32, (ROUTE_ROWS, tm), 0)
    ri_ref[...] = jnp.where(rrow == 0, i1, jnp.where(rrow == 1, i2,
                            jnp.where(rrow == 2, rank1, jnp.where(rrow == 3, rank2, 0))))
    lrow = lax.broadcasted_iota(jnp.int32, (LANES, tm), 0)
    rf_ref[...] = jnp.where(lrow == 0, g1, jnp.where(lrow == 1, g2, 0.0)).T


def _router(y, x2, w_out, g, wr_hi, wr_lo, br, tm=ROUTE_TILE):
    T, D = x2.shape
    const = lambda i: (0, 0)
    tile = lambda i: (i, 0)
    return pl.pallas_call(
        _router_kernel,
        grid=(T // tm,),
        in_specs=[
            pl.BlockSpec((tm, y.shape[1]), tile),
            pl.BlockSpec((tm, D), tile),
            pl.BlockSpec(w_out.shape, const),
            pl.BlockSpec((1, D), const),
            pl.BlockSpec((ROUTER_ROWS, D), const),
            pl.BlockSpec((ROUTER_ROWS, D), const),
            pl.BlockSpec((ROUTER_ROWS, 1), const),
        ],
        out_specs=[
            pl.BlockSpec((tm, D), tile),
            pl.BlockSpec((tm, D), tile),
            pl.BlockSpec((None, ROUTE_ROWS, tm), lambda i: (i, 0, 0)),
            pl.BlockSpec((tm, LANES), tile),
            pl.BlockSpec((N_EXPERTS, LANES), const),
        ],
        out_shape=[
            jax.ShapeDtypeStruct((T, D), F32),
            jax.ShapeDtypeStruct((T, D), F32),
            jax.ShapeDtypeStruct((T // tm, ROUTE_ROWS, tm), jnp.int32),
            jax.ShapeDtypeStruct((T, LANES), F32),
            jax.ShapeDtypeStruct((N_EXPERTS, LANES), jnp.int32),
        ],
        scratch_shapes=[pltpu.VMEM((N_EXPERTS, LANES), F32)],
        compiler_params=pltpu.CompilerParams(
            dimension_semantics=("arbitrary",), vmem_limit_bytes=VMEM_LIMIT),
        name="router",
    )(y, x2, w_out, g, wr_hi, wr_lo, br)


def _slot_base(tok0):
    return (tok0 // ROUTE_TILE) * (2 * ROUTE_TILE) + tok0 % ROUTE_TILE


def _route_spans(tm):
    assert tm % ROUTE_TILE == 0 or ROUTE_TILE % tm == 0
    span = min(tm, ROUTE_TILE)
    return [(t0, span) for t0 in range(0, tm, span)]


def _for_each_pad_block(pend_ref, nu_ref, n_blocks, fn):
    blk = EXPERT_BLOCK
    for e in range(N_EXPERTS):
        prev_end = 0 if e == 0 else pend_ref[e - 1]

        @pl.when(pend_ref[e] > prev_end)
        def _():
            fn(pl.multiple_of(pend_ref[e] - blk, blk))

    def tail(j, c):
        fn(pl.multiple_of(j * blk, blk))
        return c

    lax.fori_loop(nu_ref[0], n_blocks, tail, 0)


def _dispatch_kernel(dest_ref, pend_ref, nu_ref, xn_ref, buf_ref, zeros, sem, zsem):
    tm = xn_ref.shape[0]
    blk = EXPERT_BLOCK
    i = pl.program_id(0)

    @pl.when(i == 0)
    def _():
        zeros[...] = jnp.zeros_like(zeros)
        zcopy = lambda row: pltpu.make_async_copy(zeros, buf_ref.at[pl.ds(row, blk)], zsem)
        n_blocks = buf_ref.shape[0] // blk
        _for_each_pad_block(pend_ref, nu_ref, n_blocks, lambda row: zcopy(row).start())
        _for_each_pad_block(pend_ref, nu_ref, n_blocks, lambda row: zcopy(row).wait())

    for t0, span in _route_spans(tm):
        slot0 = _slot_base(i * tm + t0)

        def issue(t, c, t0=t0, slot0=slot0):
            for k in range(2):
                d = dest_ref[slot0 + k * ROUTE_TILE + t]
                pltpu.make_async_copy(xn_ref.at[pl.ds(t0 + t, 1)],
                                      buf_ref.at[pl.ds(d, 1)], sem).start()
            return c

        lax.fori_loop(0, span, issue, 0, unroll=8)
    for k in range(2):
        pltpu.make_async_copy(xn_ref, buf_ref.at[pl.ds(0, tm)], sem).wait()


def _dispatch(dest, pad_end, n_used, xn, n_rows, tm=ROUTE_TILE):
    T, D = xn.shape
    return pl.pallas_call(
        _dispatch_kernel,
        grid_spec=pltpu.PrefetchScalarGridSpec(
            num_scalar_prefetch=3,
            grid=(T // tm,),
            in_specs=[pl.BlockSpec((tm, D), lambda i, *_: (i, 0))],
            out_specs=pl.BlockSpec(memory_space=pl.ANY),
            scratch_shapes=[
                pltpu.VMEM((EXPERT_BLOCK, D), xn.dtype),
                pltpu.SemaphoreType.DMA,
                pltpu.SemaphoreType.DMA,
            ],
        ),
        out_shape=jax.ShapeDtypeStruct((n_rows, D), xn.dtype),
        compiler_params=pltpu.CompilerParams(
            dimension_semantics=("arbitrary",), vmem_limit_bytes=VMEM_LIMIT),
        name="dispatch",
    )(dest, pad_end, n_used, xn)


def _expert_kernel(be_ref, nu_ref, x_ref, w1_ref, w3_ref, w2_ref, y_ref, w1b, w3b, w2b):
    j = pl.program_id(0)
    used = j < nu_ref[0]
    prev_expert = be_ref[jnp.maximum(j, 1) - 1]

    @pl.when(used & ((j == 0) | (be_ref[j] != prev_expert)))
    def _():
        w1b[...] = w1_ref[...].astype(BF16)
        w3b[...] = w3_ref[...].astype(BF16)
        w2b[...] = w2_ref[...].astype(BF16)

    @pl.when(used)
    def _():
        x = x_ref[...].astype(BF16)
        a = _dot(x, w1b[...])
        hmid = a * jax.nn.sigmoid(a) * _dot(x, w3b[...])
        y_ref[...] = _dot(hmid.astype(BF16), w2b[...])

    @pl.when(jnp.logical_not(used))
    def _():
        y_ref[...] = jnp.zeros_like(y_ref)


def _experts(block_expert, n_used, x_buf, w1, w3, w2):
    P, D = x_buf.shape
    blk = EXPERT_BLOCK
    d_exp = w2.shape[1]
    clamp = lambda j, nu: jnp.minimum(j, nu[0] - 1)
    w_idx = lambda j, be, nu: (be[clamp(j, nu)], 0, 0)
    return pl.pallas_call(
        _expert_kernel,
        grid_spec=pltpu.PrefetchScalarGridSpec(
            num_scalar_prefetch=2,
            grid=(P // blk,),
            in_specs=[
                pl.BlockSpec((blk, D), lambda j, be, nu: (clamp(j, nu), 0)),
                pl.BlockSpec((None, D, d_exp), w_idx),
                pl.BlockSpec((None, D, d_exp), w_idx),
                pl.BlockSpec((None, d_exp, D), w_idx),
            ],
            out_specs=pl.BlockSpec((blk, D), lambda j, be, nu: (j, 0)),
            scratch_shapes=[
                pltpu.VMEM((D, d_exp), BF16),
                pltpu.VMEM((D, d_exp), BF16),
                pltpu.VMEM((d_exp, D), BF16),
            ],
        ),
        out_shape=jax.ShapeDtypeStruct((P, D), F32),
        compiler_params=pltpu.CompilerParams(
            dimension_semantics=("arbitrary",), vmem_limit_bytes=VMEM_LIMIT),
        name="experts",
    )(block_expert, n_used, x_buf, w1, w3, w2)


def _final_kernel(dest_ref, h_ref, rf_ref, p_ref, wup_ref, gple_ref, ggate_ref, wgate_ref,
                  gfin_ref, ybuf_ref, out_ref, y0, y1, sem):
    tm = h_ref.shape[0]
    ys = (y0, y1)
    for t0, span in _route_spans(tm):
        slot0 = _slot_base(pl.program_id(0) * tm + t0)

        def issue(t, c, t0=t0, slot0=slot0):
            for k in range(2):
                d = dest_ref[slot0 + k * ROUTE_TILE + t]
                pltpu.make_async_copy(ybuf_ref.at[pl.ds(d, 1)],
                                      ys[k].at[pl.ds(t0 + t, 1)], sem).start()
            return c

        lax.fori_loop(0, span, issue, 0, unroll=8)
    e = _rms(_dot(p_ref[...].astype(BF16), wup_ref[...]), gple_ref[...])
    for k in range(2):
        pltpu.make_async_copy(ybuf_ref.at[pl.ds(0, tm)], ys[k], sem).wait()

    rf = rf_ref[...]
    h = h_ref[...] + rf[:, 0:1] * y0[...] + rf[:, 1:2] * y1[...]
    gate = jax.nn.sigmoid(_dot(_rms(h, ggate_ref[...]).astype(BF16), wgate_ref[...]))
    h = h + gate * e
    out_ref[...] = _rms(h, gfin_ref[...])


def _final(dest, h, rf, p2, w_up, g_ple, g_gate, w_gate, g_fin, y_buf, tm=256):
    T, D = h.shape
    const = lambda i, d: (0, 0)
    tile = lambda i, d: (i, 0)
    return pl.pallas_call(
        _final_kernel,
        grid_spec=pltpu.PrefetchScalarGridSpec(
            num_scalar_prefetch=1,
            grid=(T // tm,),
            in_specs=[
                pl.BlockSpec((tm, D), tile),
                pl.BlockSpec((tm, LANES), tile),
                pl.BlockSpec((tm, p2.shape[1]), tile),
                pl.BlockSpec(w_up.shape, const),
                pl.BlockSpec((1, D), const),
                pl.BlockSpec((1, D), const),
                pl.BlockSpec(w_gate.shape, const),
                pl.BlockSpec((1, D), const),
                pl.BlockSpec(memory_space=pl.ANY),
            ],
            out_specs=pl.BlockSpec((tm, D), tile),
            scratch_shapes=[
                pltpu.VMEM((tm, D), F32),
                pltpu.VMEM((tm, D), F32),
                pltpu.SemaphoreType.DMA,
            ],
        ),
        out_shape=jax.ShapeDtypeStruct((T, D), F32),
        compiler_params=pltpu.CompilerParams(
            dimension_semantics=("arbitrary",), vmem_limit_bytes=VMEM_LIMIT),
        name="final",
    )(dest, h, rf, p2, w_up, g_ple, g_gate, w_gate, g_fin, y_buf)


def _layer(h2, p2, positions, B, S, attn_norm, w_in, conv_w, conv_b, b_igate, b_fgate, ret_gn,
           ml_gn, w_out, moe_norm, w_group, b_group, w_router, b_router, w1, w3, w2, w_ple_up,
           ple_norm, ple_gate_norm, w_ple_gate, out_norm):
    T, D = h2.shape
    ret_w = ret_gn.shape[0]
    ml_w = ml_gn.shape[0]
    n_main = 4 * ret_w + 4 * ml_w
    row = lambda v: v.reshape(1, -1).astype(F32)
    pad_lanes = lambda a: jnp.pad(a, ((0, 0), (0, LANES - a.shape[1])))

    w_main = w_in[:, :n_main].astype(BF16)
    w_gate = pad_lanes(w_in[:, n_main:]).astype(BF16)
    gate_bias = pad_lanes(jnp.concatenate([b_igate, b_fgate])[None, :].astype(F32))
    proj, gates = _inproj(h2, row(attn_norm), w_main, w_gate, gate_bias)

    y = _mixer(proj, gates, positions, conv_w, conv_b, ret_gn, ml_gn, B, S, ret_w, ml_w)

    pad_rows = lambda a: jnp.pad(a, ((0, ROUTER_ROWS - a.shape[0]), (0, 0)))
    wr = pad_rows(jnp.concatenate([w_router, w_group], axis=1).T)
    wr_hi = wr.astype(BF16)
    wr_lo = (wr - wr_hi.astype(F32)).astype(BF16)
    br = pad_rows(jnp.concatenate([b_router, b_group])[:, None].astype(F32))
    h_mid, xn, ri, rf, counts = _router(y, h2, w_out.astype(BF16), row(moe_norm), wr_hi, wr_lo, br)

    blk = EXPERT_BLOCK
    counts = counts[:, 0]
    padded = (counts + blk - 1) // blk * blk
    pad_end = jnp.cumsum(padded)
    pad_start = pad_end - padded
    n_blocks = (2 * T) // blk + N_EXPERTS
    block_start = jnp.arange(n_blocks, dtype=jnp.int32) * blk
    block_expert = jnp.minimum(jnp.sum(pad_end[None, :] <= block_start[:, None], axis=1),
                               N_EXPERTS - 1).astype(jnp.int32)
    n_used = (pad_end[-1:] // blk).astype(jnp.int32)
    dest = ri[:, 2:4, :]
    for e in range(N_EXPERTS):
        dest = dest + jnp.where(ri[:, 0:2, :] == e, pad_start[e], 0)
    dest = dest.astype(jnp.int32).reshape(-1)

    x_buf = _dispatch(dest, pad_end.astype(jnp.int32), n_used, xn, n_blocks * blk)
    y_buf = _experts(block_expert, n_used, x_buf, w1, w3, w2)

    return _final(dest, h_mid, rf, p2, w_ple_up.astype(BF16), row(ple_norm), row(ple_gate_norm),
                  w_ple_gate.astype(BF16), row(out_norm), y_buf)


def kernel(x, p, positions, attn_norm, w_in, conv_w, conv_b, b_igate, b_fgate, ret_gn, ml_gn,
           w_out, moe_norm, w_group, b_group, w_router, b_router, w1, w3, w2, w_ple_up, ple_norm,
           ple_gate_norm, w_ple_gate, final_norm):
    B, S, D = x.shape
    depth = p.shape[0]
    assert depth == 1, "the final RMSNorm is fused into the layer's last kernel"
    out = _layer(x.reshape(B * S, D), p[0].reshape(B * S, -1), positions, B, S,
                 attn_norm[0], w_in[0], conv_w[0], conv_b[0], b_igate[0], b_fgate[0], ret_gn[0],
                 ml_gn[0], w_out[0], moe_norm[0], w_group[0], b_group[0], w_router[0],
                 b_router[0], w1[0], w3[0], w2[0], w_ple_up[0], ple_norm[0], ple_gate_norm[0],
                 w_ple_gate[0], final_norm)
    return out.reshape(B, S, D)
```

```python
import functools

import jax
import jax.numpy as jnp
from jax import lax
from jax.experimental import pallas as pl
from jax.experimental.pallas import tpu as pltpu

F32 = jnp.float32
BF16 = jnp.bfloat16

RET_HEADS = 8
ML_HEADS = 4
CHUNK = 128
CONV_W = 4
ROPE_BASE = 10000.0
N_GROUPS = 4
EXPERTS_PER_GROUP = 8
N_EXPERTS = N_GROUPS * EXPERTS_PER_GROUP
EPS = 1e-6

LANES = 128
SUBLANES = 8
VMEM_LIMIT = 56 * 1024 * 1024
EXPERT_BLOCK = 512
GROUP_ROW0 = N_EXPERTS
ROUTER_ROWS = 64
ROUTE_TILE = 512
ROUTE_ROWS = 8
GATE_ROWS = 2 * SUBLANES


def _rms(x, g):
    return x * lax.rsqrt(jnp.mean(x * x, axis=-1, keepdims=True) + EPS) * g


def _dot(a, b):
    return jnp.dot(a, b, preferred_element_type=F32)


def _dot_nt(a, b):
    return lax.dot_general(a, b, (((1,), (1,)), ((), ())), preferred_element_type=F32)


def _split3(x):
    hi = x.astype(BF16)
    r1 = x - hi.astype(F32)
    mid = r1.astype(BF16)
    lo = (r1 - mid.astype(F32)).astype(BF16)
    return hi, mid, lo


def _rope_kernel(pos_ref, freq_ref, cos_ref, sin_ref):
    ang = pos_ref[...].astype(F32) * freq_ref[...]
    cos_ref[...] = jnp.cos(ang)
    sin_ref[...] = jnp.sin(ang)


def _rope_tables(positions, dh):
    half = dh // 2
    per_row = LANES // half
    T = positions.size
    rows = T // per_row
    tr = min(rows, 1024)
    freqs = ROPE_BASE ** (-jnp.arange(half, dtype=F32) / half)
    pos_c = jnp.repeat(positions.reshape(rows, per_row), half, axis=1)
    tile = pl.BlockSpec((tr, LANES), lambda i: (i, 0))
    cos_c, sin_c = pl.pallas_call(
        _rope_kernel,
        grid=(rows // tr,),
        in_specs=[tile, pl.BlockSpec((1, LANES), lambda i: (0, 0))],
        out_specs=[tile, tile],
        out_shape=[jax.ShapeDtypeStruct((rows, LANES), F32)] * 2,
        name="rope",
    )(pos_c, jnp.tile(freqs, per_row)[None, :])
    expand = lambda t: jnp.tile(t.reshape(T, half), (1, per_row))
    sign = jnp.tile(jnp.concatenate([-jnp.ones((half,), F32), jnp.ones((half,), F32)]),
                    LANES // dh)[None, :]
    return expand(cos_c), expand(sin_c) * sign


def _inproj_kernel(x_ref, g_ref, wm_ref, wgt_ref, gb_ref, cos_ref, sin_ref, ws_ref, convw_ref,
                   convb_ref, rq_ref, rk_ref, rkwt_ref, rv_ref, rg_ref, mq_ref, mk_ref, mvt_ref,
                   mo_ref, gt_ref, carry, *, ret_w, ml_w):
    tm = x_ref.shape[0]
    ret_dh = LANES // 2

    @pl.when(pl.program_id(1) == 0)
    def _():
        carry[...] = jnp.zeros_like(carry)

    xn = _rms(x_ref[...], g_ref[...]).astype(BF16)
    proj = lambda o, w: _dot(xn, wm_ref[:, o:o + w])
    tiles = lambda w: [slice(t, t + LANES) for t in range(0, w, LANES)]

    lane = lax.broadcasted_iota(jnp.int32, (tm, LANES), 1)
    first_half = (lane % ret_dh) < (ret_dh // 2)
    cos_t = cos_ref[...]
    sin_t = sin_ref[...]

    def rot(t):
        swapped = jnp.where(first_half, pltpu.roll(t, LANES - ret_dh // 2, 1),
                            pltpu.roll(t, ret_dh // 2, 1))
        return t * cos_t + swapped * sin_t

    rq = proj(0, ret_w)
    for ps in tiles(ret_w):
        rq_ref[:, ps] = rot(rq[:, ps]).astype(BF16)
    rk = proj(ret_w, ret_w)
    for ps in tiles(ret_w):
        k = rot(rk[:, ps]) * (ret_dh ** -0.5)
        rk_ref[:, ps] = k.astype(BF16)
        rkwt_ref[ps, :] = (k * ws_ref[:, ps]).T.astype(BF16)
    rv_ref[...] = proj(2 * ret_w, ret_w).astype(BF16)
    g = proj(3 * ret_w, ret_w)
    rg_ref[...] = g * jax.nn.sigmoid(g)

    o_mq = 4 * ret_w
    row8 = lax.broadcasted_iota(jnp.int32, (SUBLANES, ml_w), 0)
    for idx, dst in enumerate((mq_ref, mk_ref)):
        cs = slice(idx * ml_w, (idx + 1) * ml_w)
        xq = proj(o_mq + idx * ml_w, ml_w)
        prev8 = carry[:, cs]
        carry[:, cs] = xq[tm - SUBLANES:tm]
        acc = xq * convw_ref[CONV_W - 1:CONV_W, cs] + convb_ref[:, cs]
        for s in range(1, CONV_W):
            rolled = pltpu.roll(xq, s, 0)
            head = jnp.where(row8 < s, pltpu.roll(prev8, s, 0), rolled[0:SUBLANES])
            shifted = jnp.concatenate([head, rolled[SUBLANES:]], axis=0)
            acc = acc + shifted * convw_ref[CONV_W - 1 - s:CONV_W - s, cs]
        act = acc * jax.nn.sigmoid(acc)
        if idx == 1:
            act = act * (LANES ** -0.5)
        dst[...] = act.astype(BF16)
    mv = proj(o_mq + 2 * ml_w, ml_w)
    for hs in tiles(ml_w):
        mvt_ref[hs, :] = mv[:, hs].T
    mo_ref[...] = jax.nn.sigmoid(proj(o_mq + 3 * ml_w, ml_w))
    gt_ref[...] = _dot_nt(wgt_ref[...], xn) + gb_ref[...]


def _inproj(x3, g, w_main, w_gate_t, gate_bias, cos_t, sin_t, ws, conv_w, conv_b, ret_w, ml_w,
            tm=256):
    B, S, D = x3.shape
    n_main = w_main.shape[1]
    const = lambda b, i: (0, 0)
    tok = lambda w: pl.BlockSpec((None, tm, w), lambda b, i: (b, i, 0))
    tok_t = lambda r: pl.BlockSpec((None, r, tm), lambda b, i: (b, 0, i))
    act = lambda w, dt: jax.ShapeDtypeStruct((B, S, w), dt)
    act_t = lambda r, dt: jax.ShapeDtypeStruct((B, r, S), dt)
    return pl.pallas_call(
        functools.partial(_inproj_kernel, ret_w=ret_w, ml_w=ml_w),
        grid=(B, S // tm),
        in_specs=[
            tok(D),
            pl.BlockSpec((1, D), const),
            pl.BlockSpec((D, n_main), const),
            pl.BlockSpec((GATE_ROWS, D), const),
            pl.BlockSpec((GATE_ROWS, 1), const),
            tok(LANES),
            tok(LANES),
            pl.BlockSpec((tm, ret_w), const),
            pl.BlockSpec((CONV_W, 2 * ml_w), const),
            pl.BlockSpec((1, 2 * ml_w), const),
        ],
        out_specs=[tok(ret_w), tok(ret_w), tok_t(ret_w), tok(ret_w), tok(ret_w),
                   tok(ml_w), tok(ml_w), tok_t(ml_w), tok(ml_w), tok_t(GATE_ROWS)],
        out_shape=[
            act(ret_w, BF16),
            act(ret_w, BF16),
            act_t(ret_w, BF16),
            act(ret_w, BF16),
            act(ret_w, F32),
            act(ml_w, BF16),
            act(ml_w, BF16),
            act_t(ml_w, F32),
            act(ml_w, F32),
            act_t(GATE_ROWS, F32),
        ],
        scratch_shapes=[pltpu.VMEM((SUBLANES, 2 * ml_w), F32)],
        compiler_params=pltpu.CompilerParams(
            dimension_semantics=("arbitrary", "arbitrary"), vmem_limit_bytes=VMEM_LIMIT),
        name="inproj",
    )(x3, g, w_main, w_gate_t, gate_bias, cos_t, sin_t, ws, conv_w, conv_b)


def _mixer_kernel(rq_ref, rk_ref, rkwt_ref, rv_ref, rg_ref, mq_ref, mk_ref, mvt_ref, mo_ref,
                  gt_ref, decay_ref, wq_ref, cd_ref, retgn_ref, mlgn_ref,
                  y_ref, r_state, s_state, n_state, m_state, *, ret_w, ml_w):
    C = rq_ref.shape[0]
    n_pairs = ret_w // LANES
    ml_heads = ml_w // LANES
    ret_dh = LANES // 2

    @pl.when(pl.program_id(1) == 0)
    def _():
        r_state[...] = jnp.zeros_like(r_state)
        s_state[...] = jnp.zeros_like(s_state)
        n_state[...] = jnp.zeros_like(n_state)
        m_state[...] = jnp.zeros_like(m_state)

    lane = lax.broadcasted_iota(jnp.int32, (C, LANES), 1)
    row = lax.broadcasted_iota(jnp.int32, (C, LANES), 0)
    assert C == LANES
    lo = lane < ret_dh
    blockdiag = (row < ret_dh) == lo
    lo_b = jnp.where(lo, 1.0, 0.0).astype(BF16)
    hi_b = jnp.where(lo, 0.0, 1.0).astype(BF16)
    pairs = [slice(p * LANES, (p + 1) * LANES) for p in range(n_pairs)]
    heads = [slice(h * LANES, (h + 1) * LANES) for h in range(ml_heads)]


    row8 = lax.broadcasted_iota(jnp.int32, (SUBLANES, C), 0)
    lane8 = lax.broadcasted_iota(jnp.int32, (SUBLANES, C), 1)
    live = row8 < ml_heads
    ig = jnp.where(live, gt_ref[0:SUBLANES, :], 0.0)
    f_pre = jnp.where(live, gt_ref[SUBLANES:2 * SUBLANES, :], 30.0)
    logf = jax.nn.log_sigmoid(f_pre)
    triu = (row <= lane).astype(BF16)
    l_hi, l_mid, l_lo = _split3(logf)
    b = _dot(l_hi, triu) + _dot(l_mid, triu) + _dot(l_lo, triu)
    beta = ig - b
    cm = beta
    shift = 1
    while shift < C:
        cm = jnp.maximum(cm, jnp.where(lane8 >= shift, pltpu.roll(cm, shift, 1), -jnp.inf))
        shift *= 2
    m_prev = m_state[...]
    mx = jnp.maximum(cm, m_prev)
    mx_last = jnp.broadcast_to(mx[:, C - 1:C], (SUBLANES, C))
    w_inter = jnp.exp(m_prev - mx)
    e_negm = jnp.exp(-(b + mx))
    w_state = jnp.exp(beta - mx_last)
    dec = jnp.exp(m_prev - mx_last)
    m_state[...] = jnp.where(live, jnp.broadcast_to(b[:, C - 1:C], (SUBLANES, C)) + mx_last, 0.0)
    w_state_b = w_state.astype(BF16)
    beta_t = jnp.concatenate([beta, jnp.zeros((LANES - SUBLANES, C), F32)], axis=0).T
    before = row <= lane

    hrow = lambda t, h: t[h:h + 1, :]

    rq = [rq_ref[:, ps] for ps in pairs]
    rk = [rk_ref[:, ps] for ps in pairs]
    rv = [rv_ref[:, ps] for ps in pairs]
    r_prev = [r_state[p] for p in range(n_pairs)]
    s_a = [_dot_nt(rq[p] * lo_b, rk[p]) for p in range(n_pairs)]
    s_b = [_dot_nt(rq[p] * hi_b, rk[p]) for p in range(n_pairs)]
    r_read = [_dot(rq[p], r_prev[p].astype(BF16)) for p in range(n_pairs)]
    r_new = [_dot(rkwt_ref[pairs[p], :], rv[p]) for p in range(n_pairs)]
    mq = [mq_ref[:, hs] for hs in heads]
    mk = [mk_ref[:, hs] for hs in heads]
    mv_t = [mvt_ref[hs, :] for hs in heads]
    st_prev = [s_state[h] for h in range(ml_heads)]
    n_prev = [n_state[h] for h in range(ml_heads)]
    a_t = [_dot_nt(mk[h], mq[h]) for h in range(ml_heads)]
    s_read = [_dot_nt(st_prev[h].astype(BF16), mq[h]) for h in range(ml_heads)]
    qn = [_dot_nt(n_prev[h].astype(BF16), mq[h])[0:1, :] for h in range(ml_heads)]
    s_new = [_dot((mv_t[h] * hrow(w_state, h)).astype(BF16), mk[h]) for h in range(ml_heads)]
    n_new = [_dot(w_state_b, mk[h])[h:h + 1, :] for h in range(ml_heads)]
    for p in range(n_pairs):
        r_state[p] = cd_ref[p] * r_prev[p] + jnp.where(blockdiag, r_new[p], 0.0)
    for h in range(ml_heads):
        s_state[h] = hrow(dec, h) * st_prev[h] + s_new[h]
        n_state[h] = hrow(dec, h) * n_prev[h] + n_new[h]

    s_a = [(s_a[p] * decay_ref[2 * p]).astype(BF16) for p in range(n_pairs)]
    s_b = [(s_b[p] * decay_ref[2 * p + 1]).astype(BF16) for p in range(n_pairs)]
    p_t = [jnp.exp(jnp.where(before, beta_t[:, h:h + 1] - hrow(mx, h), -jnp.inf)) * a_t[h]
           for h in range(ml_heads)]

    o = [_dot(s_a[p], rv[p] * lo_b) + _dot(s_b[p], rv[p] * hi_b) + r_read[p] * wq_ref[p]
         for p in range(n_pairs)]
    num_t = [_dot(mv_t[h].astype(BF16), p_t[h].astype(BF16)) + hrow(w_inter, h) * s_read[h]
             for h in range(ml_heads)]

    for p in range(n_pairs):
        sq = o[p] * o[p]
        ms_a = jnp.sum(jnp.where(lo, sq, 0.0), axis=-1, keepdims=True)
        ms_b = jnp.sum(jnp.where(lo, 0.0, sq), axis=-1, keepdims=True)
        ms = jnp.where(lo, ms_a, ms_b) * (1.0 / ret_dh)
        r = o[p] * lax.rsqrt(ms + EPS) * retgn_ref[:, pairs[p]]
        y_ref[:, pairs[p]] = (rg_ref[:, pairs[p]] * r).astype(y_ref.dtype)
    for h in range(ml_heads):
        den = jnp.sum(p_t[h], axis=0, keepdims=True) + hrow(w_inter, h) * qn[h]
        hh = (num_t[h] * (1.0 / jnp.maximum(jnp.abs(den), hrow(e_negm, h)))).T
        hm = _rms(mo_ref[:, heads[h]] * hh, mlgn_ref[:, heads[h]])
        y_ref[:, ret_w + h * LANES:ret_w + (h + 1) * LANES] = hm.astype(y_ref.dtype)


def _ret_tables(C):
    H = RET_HEADS
    dh = LANES // 2
    log_gamma = jnp.log1p(-(2.0 ** (-5.0 - jnp.arange(H, dtype=F32))))
    idx = jnp.arange(C, dtype=F32)
    rel = idx[:, None] - idx[None, :]
    causal = rel >= 0
    decay = jnp.where(causal, jnp.exp(log_gamma[:, None, None] * jnp.where(causal, rel, 0.0)), 0.0)
    w_state = jnp.exp(log_gamma[:, None] * (C - 1 - idx))
    w_query = jnp.exp(log_gamma[:, None] * (idx + 1.0))
    chunk_decay = jnp.exp(log_gamma * C)
    pair = lambda t: jnp.repeat(t.reshape(H // 2, 2, C).transpose(0, 2, 1), dh, axis=2)
    cd = jnp.repeat(chunk_decay.reshape(H // 2, 2), dh, axis=1)
    cd = jnp.broadcast_to(cd[:, :, None], (H // 2, LANES, LANES))
    ws = pair(w_state).transpose(1, 0, 2).reshape(C, (H // 2) * LANES)
    return decay, ws, pair(w_query), cd


def _mixer(ops, decay, wq, cd, ret_gn, ml_gn, B, S, ret_w, ml_w):
    C = CHUNK
    N = S // C
    n_pairs = ret_w // LANES
    ml_heads = ml_w // LANES
    c2 = lambda b, n: (0, 0)
    c3 = lambda b, n: (0, 0, 0)
    tok = lambda w: pl.BlockSpec((None, C, w), lambda b, n: (b, n, 0))
    tok_t = lambda r: pl.BlockSpec((None, r, C), lambda b, n: (b, 0, n))
    y = pl.pallas_call(
        functools.partial(_mixer_kernel, ret_w=ret_w, ml_w=ml_w),
        grid=(B, N),
        in_specs=[
            tok(ret_w), tok(ret_w), tok_t(ret_w), tok(ret_w), tok(ret_w),
            tok(ml_w), tok(ml_w), tok_t(ml_w), tok(ml_w), tok_t(GATE_ROWS),
            pl.BlockSpec((RET_HEADS, C, C), c3),
            pl.BlockSpec((n_pairs, C, LANES), c3),
            pl.BlockSpec((n_pairs, LANES, LANES), c3),
            pl.BlockSpec((1, ret_w), c2),
            pl.BlockSpec((1, ml_w), c2),
        ],
        out_specs=pl.BlockSpec((None, C, ret_w + ml_w), lambda b, n: (b, n, 0)),
        out_shape=jax.ShapeDtypeStruct((B, S, ret_w + ml_w), BF16),
        scratch_shapes=[
            pltpu.VMEM((n_pairs, LANES, LANES), F32),
            pltpu.VMEM((ml_heads, LANES, LANES), F32),
            pltpu.VMEM((ml_heads, SUBLANES, LANES), F32),
            pltpu.VMEM((SUBLANES, LANES), F32),
        ],
        compiler_params=pltpu.CompilerParams(
            dimension_semantics=("arbitrary", "arbitrary"), vmem_limit_bytes=VMEM_LIMIT),
        name="mixer",
    )(*ops, decay, wq, cd, ret_gn.reshape(1, -1), ml_gn.reshape(1, -1))
    return y.reshape(B * S, ret_w + ml_w)


def _router_kernel(y_ref, x_ref, wo_ref, g_ref, wr_hi_ref, wr_lo_ref, br_ref,
                   h_ref, xn_ref, ri_ref, rf_ref, cnt_ref, run_cnt):
    tm = y_ref.shape[0]

    @pl.when(pl.program_id(0) == 0)
    def _():
        run_cnt[...] = jnp.zeros_like(run_cnt)

    h = x_ref[...] + _dot(y_ref[...], wo_ref[...])
    h_ref[...] = h
    xn = _rms(h, g_ref[...])
    xn_ref[...] = xn
    x_hi = xn.astype(BF16)
    x_lo = (xn - x_hi.astype(F32)).astype(BF16)
    logits = (_dot_nt(wr_hi_ref[...], x_hi) + _dot_nt(wr_hi_ref[...], x_lo)
              + _dot_nt(wr_lo_ref[...], x_hi) + br_ref[...])
    big = jnp.int32(LANES)
    neg = -jnp.inf
    gl = logits[GROUP_ROW0:GROUP_ROW0 + 8]
    grow = lax.broadcasted_iota(jnp.int32, gl.shape, 0)
    is_g = grow < N_GROUPS
    gl = jnp.where(is_g, gl, neg)
    gmax = jnp.max(gl, axis=0, keepdims=True)
    gsum = jnp.sum(jnp.where(is_g, jnp.exp(gl - gmax), 0.0), axis=0, keepdims=True)
    p_g = 1.0 / gsum
    g_sel = jnp.min(jnp.where(is_g & (gl == gmax), grow, big), axis=0, keepdims=True)
    el = logits[0:N_EXPERTS]
    erow = lax.broadcasted_iota(jnp.int32, el.shape, 0)
    in_grp = (erow // EXPERTS_PER_GROUP) == g_sel
    el = jnp.where(in_grp, el, neg)
    emax = jnp.max(el, axis=0, keepdims=True)
    eexp = jnp.where(in_grp, jnp.exp(el - emax), 0.0)
    prob = eexp / jnp.sum(eexp, axis=0, keepdims=True)
    pm1 = jnp.where(in_grp, prob, -1.0)
    p1 = jnp.max(pm1, axis=0, keepdims=True)
    i1 = jnp.min(jnp.where(pm1 == p1, erow, big), axis=0, keepdims=True)
    pm2 = jnp.where(erow == i1, -1.0, pm1)
    p2 = jnp.max(pm2, axis=0, keepdims=True)
    i2 = jnp.min(jnp.where(pm2 == p2, erow, big), axis=0, keepdims=True)
    denom = p1 + p2
    g1 = p_g * p1 / denom
    g2 = p_g * p2 / denom

    sel1 = erow == i1
    sel2 = erow == i2
    onehot = (sel1 | sel2).astype(BF16)
    r_i = lax.broadcasted_iota(jnp.int32, (tm, tm), 0)
    c_i = lax.broadcasted_iota(jnp.int32, (tm, tm), 1)
    tri = (r_i < c_i).astype(BF16)
    prefix = _dot(onehot, tri) + run_cnt[:, 0:1]
    rank1 = jnp.sum(jnp.where(sel1, prefix, 0.0), axis=0, keepdims=True).astype(jnp.int32)
    rank2 = jnp.sum(jnp.where(sel2, prefix, 0.0), axis=0, keepdims=True).astype(jnp.int32)
    new_cnt = run_cnt[...] + jnp.sum(onehot.astype(F32), axis=1, keepdims=True)
    run_cnt[...] = new_cnt
    cnt_ref[...] = new_cnt.astype(jnp.int32)

    rrow = lax.broadcasted_iota(jnp.int32, (ROUTE_ROWS, tm), 0)
    ri_ref[...] = jnp.where(rrow == 0, i1, jnp.where(rrow == 1, i2,
                            jnp.where(rrow == 2, rank1, jnp.where(rrow == 3, rank2, 0))))
    lrow = lax.broadcasted_iota(jnp.int32, (LANES, tm), 0)
    rf_ref[...] = jnp.where(lrow == 0, g1, jnp.where(lrow == 1, g2, 0.0)).T


def _router(y, x2, w_out, g, wr_hi, wr_lo, br, tm=ROUTE_TILE):
    T, D = x2.shape
    const = lambda i: (0, 0)
    tile = lambda i: (i, 0)
    return pl.pallas_call(
        _router_kernel,
        grid=(T // tm,),
        in_specs=[
            pl.BlockSpec((tm, y.shape[1]), tile),
            pl.BlockSpec((tm, D), tile),
            pl.BlockSpec(w_out.shape, const),
            pl.BlockSpec((1, D), const),
            pl.BlockSpec((ROUTER_ROWS, D), const),
            pl.BlockSpec((ROUTER_ROWS, D), const),
            pl.BlockSpec((ROUTER_ROWS, 1), const),
        ],
        out_specs=[
            pl.BlockSpec((tm, D), tile),
            pl.BlockSpec((tm, D), tile),
            pl.BlockSpec((None, ROUTE_ROWS, tm), lambda i: (i, 0, 0)),
            pl.BlockSpec((tm, LANES), tile),
            pl.BlockSpec((N_EXPERTS, LANES), const),
        ],
        out_shape=[
            jax.ShapeDtypeStruct((T, D), F32),
            jax.ShapeDtypeStruct((T, D), F32),
            jax.ShapeDtypeStruct((T // tm, ROUTE_ROWS, tm), jnp.int32),
            jax.ShapeDtypeStruct((T, LANES), F32),
            jax.ShapeDtypeStruct((N_EXPERTS, LANES), jnp.int32),
        ],
        scratch_shapes=[pltpu.VMEM((N_EXPERTS, LANES), F32)],
        compiler_params=pltpu.CompilerParams(
            dimension_semantics=("arbitrary",), vmem_limit_bytes=VMEM_LIMIT),
        name="router",
    )(y, x2, w_out, g, wr_hi, wr_lo, br)


def _slot_base(tok0):
    return (tok0 // ROUTE_TILE) * (2 * ROUTE_TILE) + tok0 % ROUTE_TILE


def _route_spans(tm):
    assert tm % ROUTE_TILE == 0 or ROUTE_TILE % tm == 0
    span = min(tm, ROUTE_TILE)
    return [(t0, span) for t0 in range(0, tm, span)]


def _for_each_pad_block(pend_ref, nu_ref, n_blocks, fn):
    blk = EXPERT_BLOCK
    for e in range(N_EXPERTS):
        prev_end = 0 if e == 0 else pend_ref[e - 1]

        @pl.when(pend_ref[e] > prev_end)
        def _():
            fn(pl.multiple_of(pend_ref[e] - blk, blk))

    def tail(j, c):
        fn(pl.multiple_of(j * blk, blk))
        return c

    lax.fori_loop(nu_ref[0], n_blocks, tail, 0)


def _dispatch_kernel(dest_ref, pend_ref, nu_ref, xn_ref, buf_ref, zeros, sem, zsem):
    tm = xn_ref.shape[0]
    blk = EXPERT_BLOCK
    i = pl.program_id(0)

    @pl.when(i == 0)
    def _():
        zeros[...] = jnp.zeros_like(zeros)
        zcopy = lambda row: pltpu.make_async_copy(zeros, buf_ref.at[pl.ds(row, blk)], zsem)
        n_blocks = buf_ref.shape[0] // blk
        _for_each_pad_block(pend_ref, nu_ref, n_blocks, lambda row: zcopy(row).start())
        _for_each_pad_block(pend_ref, nu_ref, n_blocks, lambda row: zcopy(row).wait())

    for t0, span in _route_spans(tm):
        slot0 = _slot_base(i * tm + t0)

        def issue(t, c, t0=t0, slot0=slot0):
            for k in range(2):
                d = dest_ref[slot0 + k * ROUTE_TILE + t]
                pltpu.make_async_copy(xn_ref.at[pl.ds(t0 + t, 1)],
                                      buf_ref.at[pl.ds(d, 1)], sem).start()
            return c

        lax.fori_loop(0, span, issue, 0, unroll=8)
    for k in range(2):
        pltpu.make_async_copy(xn_ref, buf_ref.at[pl.ds(0, tm)], sem).wait()


def _dispatch(dest, pad_end, n_used, xn, n_rows, tm=ROUTE_TILE):
    T, D = xn.shape
    return pl.pallas_call(
        _dispatch_kernel,
        grid_spec=pltpu.PrefetchScalarGridSpec(
            num_scalar_prefetch=3,
            grid=(T // tm,),
            in_specs=[pl.BlockSpec((tm, D), lambda i, *_: (i, 0))],
            out_specs=pl.BlockSpec(memory_space=pl.ANY),
            scratch_shapes=[
                pltpu.VMEM((EXPERT_BLOCK, D), xn.dtype),
                pltpu.SemaphoreType.DMA,
                pltpu.SemaphoreType.DMA,
            ],
        ),
        out_shape=jax.ShapeDtypeStruct((n_rows, D), xn.dtype),
        compiler_params=pltpu.CompilerParams(
            dimension_semantics=("arbitrary",), vmem_limit_bytes=VMEM_LIMIT),
        name="dispatch",
    )(dest, pad_end, n_used, xn)


def _expert_kernel(be_ref, nu_ref, x_ref, w1_ref, w3_ref, w2_ref, y_ref, w1b, w3b, w2b):
    j = pl.program_id(0)
    used = j < nu_ref[0]
    prev_expert = be_ref[jnp.maximum(j, 1) - 1]

    @pl.when(used & ((j == 0) | (be_ref[j] != prev_expert)))
    def _():
        w1b[...] = w1_ref[...].astype(BF16)
        w3b[...] = w3_ref[...].astype(BF16)
        w2b[...] = w2_ref[...].astype(BF16)

    @pl.when(used)
    def _():
        x = x_ref[...].astype(BF16)
        a = _dot(x, w1b[...])
        hmid = a * jax.nn.sigmoid(a) * _dot(x, w3b[...])
        y_ref[...] = _dot(hmid.astype(BF16), w2b[...])

    @pl.when(jnp.logical_not(used))
    def _():
        y_ref[...] = jnp.zeros_like(y_ref)


def _experts(block_expert, n_used, x_buf, w1, w3, w2):
    P, D = x_buf.shape
    blk = EXPERT_BLOCK
    d_exp = w2.shape[1]
    clamp = lambda j, nu: jnp.minimum(j, nu[0] - 1)
    w_idx = lambda j, be, nu: (be[clamp(j, nu)], 0, 0)
    return pl.pallas_call(
        _expert_kernel,
        grid_spec=pltpu.PrefetchScalarGridSpec(
            num_scalar_prefetch=2,
            grid=(P // blk,),
            in_specs=[
                pl.BlockSpec((blk, D), lambda j, be, nu: (clamp(j, nu), 0)),
                pl.BlockSpec((None, D, d_exp), w_idx),
                pl.BlockSpec((None, D, d_exp), w_idx),
                pl.BlockSpec((None, d_exp, D), w_idx),
            ],
            out_specs=pl.BlockSpec((blk, D), lambda j, be, nu: (j, 0)),
            scratch_shapes=[
                pltpu.VMEM((D, d_exp), BF16),
                pltpu.VMEM((D, d_exp), BF16),
                pltpu.VMEM((d_exp, D), BF16),
            ],
        ),
        out_shape=jax.ShapeDtypeStruct((P, D), F32),
        compiler_params=pltpu.CompilerParams(
            dimension_semantics=("arbitrary",), vmem_limit_bytes=VMEM_LIMIT),
        name="experts",
    )(block_expert, n_used, x_buf, w1, w3, w2)


def _final_kernel(dest_ref, h_ref, rf_ref, p_ref, wup_ref, gple_ref, ggate_ref, wgate_ref,
                  gfin_ref, ybuf_ref, out_ref, y0, y1, sem):
    tm = h_ref.shape[0]
    ys = (y0, y1)
    for t0, span in _route_spans(tm):
        slot0 = _slot_base(pl.program_id(0) * tm + t0)

        def issue(t, c, t0=t0, slot0=slot0):
            for k in range(2):
                d = dest_ref[slot0 + k * ROUTE_TILE + t]
                pltpu.make_async_copy(ybuf_ref.at[pl.ds(d, 1)],
                                      ys[k].at[pl.ds(t0 + t, 1)], sem).start()
            return c

        lax.fori_loop(0, span, issue, 0, unroll=8)
    e = _rms(_dot(p_ref[...].astype(BF16), wup_ref[...]), gple_ref[...])
    for k in range(2):
        pltpu.make_async_copy(ybuf_ref.at[pl.ds(0, tm)], ys[k], sem).wait()

    rf = rf_ref[...]
    h = h_ref[...] + rf[:, 0:1] * y0[...] + rf[:, 1:2] * y1[...]
    gate = jax.nn.sigmoid(_dot(_rms(h, ggate_ref[...]).astype(BF16), wgate_ref[...]))
    h = h + gate * e
    out_ref[...] = _rms(h, gfin_ref[...])


def _final(dest, h, rf, p2, w_up, g_ple, g_gate, w_gate, g_fin, y_buf, tm=256):
    T, D = h.shape
    const = lambda i, d: (0, 0)
    tile = lambda i, d: (i, 0)
    return pl.pallas_call(
        _final_kernel,
        grid_spec=pltpu.PrefetchScalarGridSpec(
            num_scalar_prefetch=1,
            grid=(T // tm,),
            in_specs=[
                pl.BlockSpec((tm, D), tile),
                pl.BlockSpec((tm, LANES), tile),
                pl.BlockSpec((tm, p2.shape[1]), tile),
                pl.BlockSpec(w_up.shape, const),
                pl.BlockSpec((1, D), const),
                pl.BlockSpec((1, D), const),
                pl.BlockSpec(w_gate.shape, const),
                pl.BlockSpec((1, D), const),
                pl.BlockSpec(memory_space=pl.ANY),
            ],
            out_specs=pl.BlockSpec((tm, D), tile),
            scratch_shapes=[
                pltpu.VMEM((tm, D), F32),
                pltpu.VMEM((tm, D), F32),
                pltpu.SemaphoreType.DMA,
            ],
        ),
        out_shape=jax.ShapeDtypeStruct((T, D), F32),
        compiler_params=pltpu.CompilerParams(
            dimension_semantics=("arbitrary",), vmem_limit_bytes=VMEM_LIMIT),
        name="final",
    )(dest, h, rf, p2, w_up, g_ple, g_gate, w_gate, g_fin, y_buf)


def _layer(h2, p2, positions, B, S, attn_norm, w_in, conv_w, conv_b, b_igate, b_fgate, ret_gn,
           ml_gn, w_out, moe_norm, w_group, b_group, w_router, b_router, w1, w3, w2, w_ple_up,
           ple_norm, ple_gate_norm, w_ple_gate, out_norm):
    T, D = h2.shape
    ret_w = ret_gn.shape[0]
    ml_w = ml_gn.shape[0]
    n_main = 4 * ret_w + 4 * ml_w
    row = lambda v: v.reshape(1, -1).astype(F32)
    pad_lanes = lambda a: jnp.pad(a, ((0, 0), (0, LANES - a.shape[1])))

    ml_heads = ml_w // LANES
    w_main = w_in[:, :n_main].astype(BF16)
    gate_rows = lambda a: jnp.pad(a, ((0, SUBLANES - ml_heads), (0, 0)))
    w_i, w_f = w_in[:, n_main:n_main + ml_heads].T, w_in[:, n_main + ml_heads:].T
    w_gate_t = jnp.concatenate([gate_rows(w_i), gate_rows(w_f)]).astype(BF16)
    gate_bias = jnp.concatenate([gate_rows(b_igate[:, None]), gate_rows(b_fgate[:, None])])
    cos_t, sin_t = _rope_tables(positions, LANES // 2)
    decay, ws, wq, cd = _ret_tables(CHUNK)
    tm_in = 2 * CHUNK
    ops = _inproj(h2.reshape(B, S, D), row(attn_norm), w_main, w_gate_t, gate_bias.astype(F32),
                  cos_t.reshape(B, S, LANES), sin_t.reshape(B, S, LANES),
                  jnp.tile(ws, (tm_in // CHUNK, 1)), conv_w, conv_b.reshape(1, -1), ret_w, ml_w,
                  tm=tm_in)

    y = _mixer(ops, decay, wq, cd, ret_gn, ml_gn, B, S, ret_w, ml_w)

    pad_rows = lambda a: jnp.pad(a, ((0, ROUTER_ROWS - a.shape[0]), (0, 0)))
    wr = pad_rows(jnp.concatenate([w_router, w_group], axis=1).T)
    wr_hi = wr.astype(BF16)
    wr_lo = (wr - wr_hi.astype(F32)).astype(BF16)
    br = pad_rows(jnp.concatenate([b_router, b_group])[:, None].astype(F32))
    h_mid, xn, ri, rf, counts = _router(y, h2, w_out.astype(BF16), row(moe_norm), wr_hi, wr_lo, br)

    blk = EXPERT_BLOCK
    counts = counts[:, 0]
    padded = (counts + blk - 1) // blk * blk
    pad_end = jnp.cumsum(padded)
    pad_start = pad_end - padded
    n_blocks = (2 * T) // blk + N_EXPERTS
    block_start = jnp.arange(n_blocks, dtype=jnp.int32) * blk
    block_expert = jnp.minimum(jnp.sum(pad_end[None, :] <= block_start[:, None], axis=1),
                               N_EXPERTS - 1).astype(jnp.int32)
    n_used = (pad_end[-1:] // blk).astype(jnp.int32)
    dest = ri[:, 2:4, :]
    for e in range(N_EXPERTS):
        dest = dest + jnp.where(ri[:, 0:2, :] == e, pad_start[e], 0)
    dest = dest.astype(jnp.int32).reshape(-1)

    x_buf = _dispatch(dest, pad_end.astype(jnp.int32), n_used, xn, n_blocks * blk)
    y_buf = _experts(block_expert, n_used, x_buf, w1, w3, w2)

    return _final(dest, h_mid, rf, p2, w_ple_up.astype(BF16), row(ple_norm), row(ple_gate_norm),
                  w_ple_gate.astype(BF16), row(out_norm), y_buf)


def kernel(x, p, positions, attn_norm, w_in, conv_w, conv_b, b_igate, b_fgate, ret_gn, ml_gn,
           w_out, moe_norm, w_group, b_group, w_router, b_router, w1, w3, w2, w_ple_up, ple_norm,
           ple_gate_norm, w_ple_gate, final_norm):
    B, S, D = x.shape
    depth = p.shape[0]
    assert depth == 1, "the final RMSNorm is fused into the layer's last kernel"
    out = _layer(x.reshape(B * S, D), p[0].reshape(B * S, -1), positions, B, S,
                 attn_norm[0], w_in[0], conv_w[0], conv_b[0], b_igate[0], b_fgate[0], ret_gn[0],
                 ml_gn[0], w_out[0], moe_norm[0], w_group[0], b_group[0], w_router[0],
                 b_router[0], w1[0], w3[0], w2[0], w_ple_up[0], ple_norm[0], ple_gate_norm[0],
                 w_ple_gate[0], final_norm)
    return out.reshape(B, S, D)
```

```python
import functools

import jax
import jax.numpy as jnp
from jax import lax
from jax.experimental import pallas as pl
from jax.experimental.pallas import tpu as pltpu

F32 = jnp.float32
BF16 = jnp.bfloat16

RET_HEADS = 8
ML_HEADS = 4
CHUNK = 128
CONV_W = 4
ROPE_BASE = 10000.0
N_GROUPS = 4
EXPERTS_PER_GROUP = 8
N_EXPERTS = N_GROUPS * EXPERTS_PER_GROUP
EPS = 1e-6

LANES = 128
SUBLANES = 8
VMEM_LIMIT = 56 * 1024 * 1024
EXPERT_BLOCK = 512
GROUP_ROW0 = N_EXPERTS
ROUTER_ROWS = 64
ROUTE_TILE = 512
ROUTE_ROWS = 8
GATE_ROWS = 2 * SUBLANES


def _rms(x, g):
    return x * lax.rsqrt(jnp.mean(x * x, axis=-1, keepdims=True) + EPS) * g


def _dot(a, b):
    return jnp.dot(a, b, preferred_element_type=F32)


def _dot_nt(a, b):
    return lax.dot_general(a, b, (((1,), (1,)), ((), ())), preferred_element_type=F32)


def _split3(x):
    hi = x.astype(BF16)
    r1 = x - hi.astype(F32)
    mid = r1.astype(BF16)
    lo = (r1 - mid.astype(F32)).astype(BF16)
    return hi, mid, lo


def _rope_kernel(pos_ref, freq_ref, sign_ref, cos_ref, sin_ref, *, half):
    ang = pos_ref[...].astype(F32) * freq_ref[...]
    cos_c = jnp.cos(ang)
    sin_c = jnp.sin(ang)
    group = lax.broadcasted_iota(jnp.int32, ang.shape, 1) // half
    per_row = LANES // half
    for s in range(per_row):
        def spread(t):
            out = t
            for g in range(per_row):
                if g != s:
                    out = jnp.where(group == g, pltpu.roll(t, (half * (g - s)) % LANES, 1), out)
            return out
        cos_ref[s] = spread(cos_c)
        sin_ref[s] = spread(sin_c) * sign_ref[...]


def _rope_tables(positions, dh):
    half = dh // 2
    per_row = LANES // half
    T = positions.size
    rows = T // per_row
    tr = min(rows, 1024)
    freqs = ROPE_BASE ** (-jnp.arange(half, dtype=F32) / half)
    pos_c = jnp.repeat(positions.reshape(per_row, rows).T, half, axis=1)
    sign = jnp.tile(jnp.concatenate([-jnp.ones((half,), F32), jnp.ones((half,), F32)]),
                    LANES // dh)[None, :]
    const = pl.BlockSpec((1, LANES), lambda i: (0, 0))
    out = pl.BlockSpec((per_row, tr, LANES), lambda i: (0, i, 0))
    cos_t, sin_t = pl.pallas_call(
        functools.partial(_rope_kernel, half=half),
        grid=(rows // tr,),
        in_specs=[pl.BlockSpec((tr, LANES), lambda i: (i, 0)), const, const],
        out_specs=[out, out],
        out_shape=[jax.ShapeDtypeStruct((per_row, rows, LANES), F32)] * 2,
        name="rope",
    )(pos_c, jnp.tile(freqs, per_row)[None, :], sign)
    return cos_t.reshape(T, LANES), sin_t.reshape(T, LANES)


def _inproj_kernel(x_ref, g_ref, wm_ref, wgt_ref, gb_ref, cos_ref, sin_ref, ws_ref, convw_ref,
                   convb_ref, rq_ref, rk_ref, rkwt_ref, rv_ref, rg_ref, mq_ref, mk_ref, mvt_ref,
                   mo_ref, gt_ref, carry, *, ret_w, ml_w):
    tm = x_ref.shape[0]
    ret_dh = LANES // 2

    @pl.when(pl.program_id(1) == 0)
    def _():
        carry[...] = jnp.zeros_like(carry)

    xn = _rms(x_ref[...], g_ref[...]).astype(BF16)
    proj = lambda o, w: _dot(xn, wm_ref[:, o:o + w])
    tiles = lambda w: [slice(t, t + LANES) for t in range(0, w, LANES)]

    lane = lax.broadcasted_iota(jnp.int32, (tm, LANES), 1)
    first_half = (lane % ret_dh) < (ret_dh // 2)
    cos_t = cos_ref[...]
    sin_t = sin_ref[...]

    def rot(t):
        swapped = jnp.where(first_half, pltpu.roll(t, LANES - ret_dh // 2, 1),
                            pltpu.roll(t, ret_dh // 2, 1))
        return t * cos_t + swapped * sin_t

    rq = proj(0, ret_w)
    for ps in tiles(ret_w):
        rq_ref[:, ps] = rot(rq[:, ps]).astype(BF16)
    rk = proj(ret_w, ret_w)
    for ps in tiles(ret_w):
        k = rot(rk[:, ps]) * (ret_dh ** -0.5)
        rk_ref[:, ps] = k.astype(BF16)
        rkwt_ref[ps, :] = (k * ws_ref[:, ps]).T.astype(BF16)
    rv_ref[...] = proj(2 * ret_w, ret_w).astype(BF16)
    g = proj(3 * ret_w, ret_w)
    rg_ref[...] = g * jax.nn.sigmoid(g)

    o_mq = 4 * ret_w
    row8 = lax.broadcasted_iota(jnp.int32, (SUBLANES, ml_w), 0)
    for idx, dst in enumerate((mq_ref, mk_ref)):
        cs = slice(idx * ml_w, (idx + 1) * ml_w)
        xq = proj(o_mq + idx * ml_w, ml_w)
        prev8 = carry[:, cs]
        carry[:, cs] = xq[tm - SUBLANES:tm]
        acc = xq * convw_ref[CONV_W - 1:CONV_W, cs] + convb_ref[:, cs]
        for s in range(1, CONV_W):
            rolled = pltpu.roll(xq, s, 0)
            head = jnp.where(row8 < s, pltpu.roll(prev8, s, 0), rolled[0:SUBLANES])
            shifted = jnp.concatenate([head, rolled[SUBLANES:]], axis=0)
            acc = acc + shifted * convw_ref[CONV_W - 1 - s:CONV_W - s, cs]
        act = acc * jax.nn.sigmoid(acc)
        if idx == 1:
            act = act * (LANES ** -0.5)
        dst[...] = act.astype(BF16)
    mv = proj(o_mq + 2 * ml_w, ml_w)
    for hs in tiles(ml_w):
        mvt_ref[hs, :] = mv[:, hs].T
    mo_ref[...] = jax.nn.sigmoid(proj(o_mq + 3 * ml_w, ml_w))
    gt_ref[...] = _dot_nt(wgt_ref[...], xn) + gb_ref[...]


def _inproj(x3, g, w_main, w_gate_t, gate_bias, cos_t, sin_t, ws, conv_w, conv_b, ret_w, ml_w,
            tm=256):
    B, S, D = x3.shape
    n_main = w_main.shape[1]
    const = lambda b, i: (0, 0)
    tok = lambda w: pl.BlockSpec((None, tm, w), lambda b, i: (b, i, 0))
    tok_t = lambda r: pl.BlockSpec((None, r, tm), lambda b, i: (b, 0, i))
    act = lambda w, dt: jax.ShapeDtypeStruct((B, S, w), dt)
    act_t = lambda r, dt: jax.ShapeDtypeStruct((B, r, S), dt)
    return pl.pallas_call(
        functools.partial(_inproj_kernel, ret_w=ret_w, ml_w=ml_w),
        grid=(B, S // tm),
        in_specs=[
            tok(D),
            pl.BlockSpec((1, D), const),
            pl.BlockSpec((D, n_main), const),
            pl.BlockSpec((GATE_ROWS, D), const),
            pl.BlockSpec((GATE_ROWS, 1), const),
            tok(LANES),
            tok(LANES),
            pl.BlockSpec((tm, ret_w), const),
            pl.BlockSpec((CONV_W, 2 * ml_w), const),
            pl.BlockSpec((1, 2 * ml_w), const),
        ],
        out_specs=[tok(ret_w), tok(ret_w), tok_t(ret_w), tok(ret_w), tok(ret_w),
                   tok(ml_w), tok(ml_w), tok_t(ml_w), tok(ml_w), tok_t(GATE_ROWS)],
        out_shape=[
            act(ret_w, BF16),
            act(ret_w, BF16),
            act_t(ret_w, BF16),
            act(ret_w, BF16),
            act(ret_w, F32),
            act(ml_w, BF16),
            act(ml_w, BF16),
            act_t(ml_w, F32),
            act(ml_w, F32),
            act_t(GATE_ROWS, F32),
        ],
        scratch_shapes=[pltpu.VMEM((SUBLANES, 2 * ml_w), F32)],
        compiler_params=pltpu.CompilerParams(
            dimension_semantics=("arbitrary", "arbitrary"), vmem_limit_bytes=VMEM_LIMIT),
        name="inproj",
    )(x3, g, w_main, w_gate_t, gate_bias, cos_t, sin_t, ws, conv_w, conv_b)


def _mixer_kernel(rq_ref, rk_ref, rkwt_ref, rv_ref, rg_ref, mq_ref, mk_ref, mvt_ref, mo_ref,
                  gt_ref, decay_ref, wq_ref, cd_ref, retgn_ref, mlgn_ref,
                  y_ref, r_state, s_state, n_state, m_state, *, ret_w, ml_w):
    C = rq_ref.shape[0]
    n_pairs = ret_w // LANES
    ml_heads = ml_w // LANES
    ret_dh = LANES // 2

    @pl.when(pl.program_id(1) == 0)
    def _():
        r_state[...] = jnp.zeros_like(r_state)
        s_state[...] = jnp.zeros_like(s_state)
        n_state[...] = jnp.zeros_like(n_state)
        m_state[...] = jnp.zeros_like(m_state)

    lane = lax.broadcasted_iota(jnp.int32, (C, LANES), 1)
    row = lax.broadcasted_iota(jnp.int32, (C, LANES), 0)
    assert C == LANES
    lo = lane < ret_dh
    blockdiag = (row < ret_dh) == lo
    lo_b = jnp.where(lo, 1.0, 0.0).astype(BF16)
    hi_b = jnp.where(lo, 0.0, 1.0).astype(BF16)
    pairs = [slice(p * LANES, (p + 1) * LANES) for p in range(n_pairs)]
    heads = [slice(h * LANES, (h + 1) * LANES) for h in range(ml_heads)]


    row8 = lax.broadcasted_iota(jnp.int32, (SUBLANES, C), 0)
    lane8 = lax.broadcasted_iota(jnp.int32, (SUBLANES, C), 1)
    live = row8 < ml_heads
    ig = jnp.where(live, gt_ref[0:SUBLANES, :], 0.0)
    f_pre = jnp.where(live, gt_ref[SUBLANES:2 * SUBLANES, :], 30.0)
    logf = jax.nn.log_sigmoid(f_pre)
    triu = (row <= lane).astype(BF16)
    l_hi, l_mid, l_lo = _split3(logf)
    b = _dot(l_hi, triu) + _dot(l_mid, triu) + _dot(l_lo, triu)
    beta = ig - b
    cm = beta
    shift = 1
    while shift < C:
        cm = jnp.maximum(cm, jnp.where(lane8 >= shift, pltpu.roll(cm, shift, 1), -jnp.inf))
        shift *= 2
    m_prev = m_state[...]
    mx = jnp.maximum(cm, m_prev)
    mx_last = jnp.broadcast_to(mx[:, C - 1:C], (SUBLANES, C))
    w_inter = jnp.exp(m_prev - mx)
    e_negm = jnp.exp(-(b + mx))
    w_state = jnp.exp(beta - mx_last)
    dec = jnp.exp(m_prev - mx_last)
    m_state[...] = jnp.where(live, jnp.broadcast_to(b[:, C - 1:C], (SUBLANES, C)) + mx_last, 0.0)
    w_state_b = w_state.astype(BF16)
    beta_t = jnp.concatenate([beta, jnp.zeros((LANES - SUBLANES, C), F32)], axis=0).T
    before = row <= lane

    hrow = lambda t, h: t[h:h + 1, :]

    rq = [rq_ref[:, ps] for ps in pairs]
    rk = [rk_ref[:, ps] for ps in pairs]
    rv = [rv_ref[:, ps] for ps in pairs]
    r_prev = [r_state[p] for p in range(n_pairs)]
    s_a = [_dot_nt(rq[p] * lo_b, rk[p]) for p in range(n_pairs)]
    s_b = [_dot_nt(rq[p] * hi_b, rk[p]) for p in range(n_pairs)]
    r_read = [_dot(rq[p], r_prev[p].astype(BF16)) for p in range(n_pairs)]
    r_new = [_dot(rkwt_ref[pairs[p], :], rv[p]) for p in range(n_pairs)]
    mq = [mq_ref[:, hs] for hs in heads]
    mk = [mk_ref[:, hs] for hs in heads]
    mv_t = [mvt_ref[hs, :] for hs in heads]
    st_prev = [s_state[h] for h in range(ml_heads)]
    n_prev = [n_state[h] for h in range(ml_heads)]
    a_t = [_dot_nt(mk[h], mq[h]) for h in range(ml_heads)]
    s_read = [_dot_nt(st_prev[h].astype(BF16), mq[h]) for h in range(ml_heads)]
    qn = [_dot_nt(n_prev[h].astype(BF16), mq[h])[0:1, :] for h in range(ml_heads)]
    s_new = [_dot((mv_t[h] * hrow(w_state, h)).astype(BF16), mk[h]) for h in range(ml_heads)]
    n_new = [_dot(w_state_b, mk[h])[h:h + 1, :] for h in range(ml_heads)]
    for p in range(n_pairs):
        r_state[p] = cd_ref[p] * r_prev[p] + jnp.where(blockdiag, r_new[p], 0.0)
    for h in range(ml_heads):
        s_state[h] = hrow(dec, h) * st_prev[h] + s_new[h]
        n_state[h] = hrow(dec, h) * n_prev[h] + n_new[h]

    s_a = [(s_a[p] * decay_ref[2 * p]).astype(BF16) for p in range(n_pairs)]
    s_b = [(s_b[p] * decay_ref[2 * p + 1]).astype(BF16) for p in range(n_pairs)]
    p_t = [jnp.exp(jnp.where(before, beta_t[:, h:h + 1] - hrow(mx, h), -jnp.inf)) * a_t[h]
           for h in range(ml_heads)]

    o = [_dot(s_a[p], rv[p] * lo_b) + _dot(s_b[p], rv[p] * hi_b) + r_read[p] * wq_ref[p]
         for p in range(n_pairs)]
    num_t = [_dot(mv_t[h].astype(BF16), p_t[h].astype(BF16)) + hrow(w_inter, h) * s_read[h]
             for h in range(ml_heads)]

    for p in range(n_pairs):
        sq = o[p] * o[p]
        ms_a = jnp.sum(jnp.where(lo, sq, 0.0), axis=-1, keepdims=True)
        ms_b = jnp.sum(jnp.where(lo, 0.0, sq), axis=-1, keepdims=True)
        ms = jnp.where(lo, ms_a, ms_b) * (1.0 / ret_dh)
        r = o[p] * lax.rsqrt(ms + EPS) * retgn_ref[:, pairs[p]]
        y_ref[:, pairs[p]] = (rg_ref[:, pairs[p]] * r).astype(y_ref.dtype)
    for h in range(ml_heads):
        den = jnp.sum(p_t[h], axis=0, keepdims=True) + hrow(w_inter, h) * qn[h]
        hh = (num_t[h] * (1.0 / jnp.maximum(jnp.abs(den), hrow(e_negm, h)))).T
        hm = _rms(mo_ref[:, heads[h]] * hh, mlgn_ref[:, heads[h]])
        y_ref[:, ret_w + h * LANES:ret_w + (h + 1) * LANES] = hm.astype(y_ref.dtype)


def _ret_tables(C):
    H = RET_HEADS
    dh = LANES // 2
    log_gamma = jnp.log1p(-(2.0 ** (-5.0 - jnp.arange(H, dtype=F32))))
    idx = jnp.arange(C, dtype=F32)
    rel = idx[:, None] - idx[None, :]
    causal = rel >= 0
    decay = jnp.where(causal, jnp.exp(log_gamma[:, None, None] * jnp.where(causal, rel, 0.0)), 0.0)
    w_state = jnp.exp(log_gamma[:, None] * (C - 1 - idx))
    w_query = jnp.exp(log_gamma[:, None] * (idx + 1.0))
    chunk_decay = jnp.exp(log_gamma * C)
    pair = lambda t: jnp.repeat(t.reshape(H // 2, 2, C).transpose(0, 2, 1), dh, axis=2)
    cd = jnp.repeat(chunk_decay.reshape(H // 2, 2), dh, axis=1)
    cd = jnp.broadcast_to(cd[:, :, None], (H // 2, LANES, LANES))
    ws = pair(w_state).transpose(1, 0, 2).reshape(C, (H // 2) * LANES)
    return decay, ws, pair(w_query), cd


def _mixer(ops, decay, wq, cd, ret_gn, ml_gn, B, S, ret_w, ml_w):
    C = CHUNK
    N = S // C
    n_pairs = ret_w // LANES
    ml_heads = ml_w // LANES
    c2 = lambda b, n: (0, 0)
    c3 = lambda b, n: (0, 0, 0)
    tok = lambda w: pl.BlockSpec((None, C, w), lambda b, n: (b, n, 0))
    tok_t = lambda r: pl.BlockSpec((None, r, C), lambda b, n: (b, 0, n))
    y = pl.pallas_call(
        functools.partial(_mixer_kernel, ret_w=ret_w, ml_w=ml_w),
        grid=(B, N),
        in_specs=[
            tok(ret_w), tok(ret_w), tok_t(ret_w), tok(ret_w), tok(ret_w),
            tok(ml_w), tok(ml_w), tok_t(ml_w), tok(ml_w), tok_t(GATE_ROWS),
            pl.BlockSpec((RET_HEADS, C, C), c3),
            pl.BlockSpec((n_pairs, C, LANES), c3),
            pl.BlockSpec((n_pairs, LANES, LANES), c3),
            pl.BlockSpec((1, ret_w), c2),
            pl.BlockSpec((1, ml_w), c2),
        ],
        out_specs=pl.BlockSpec((None, C, ret_w + ml_w), lambda b, n: (b, n, 0)),
        out_shape=jax.ShapeDtypeStruct((B, S, ret_w + ml_w), BF16),
        scratch_shapes=[
            pltpu.VMEM((n_pairs, LANES, LANES), F32),
            pltpu.VMEM((ml_heads, LANES, LANES), F32),
            pltpu.VMEM((ml_heads, SUBLANES, LANES), F32),
            pltpu.VMEM((SUBLANES, LANES), F32),
        ],
        compiler_params=pltpu.CompilerParams(
            dimension_semantics=("arbitrary", "arbitrary"), vmem_limit_bytes=VMEM_LIMIT),
        name="mixer",
    )(*ops, decay, wq, cd, ret_gn.reshape(1, -1), ml_gn.reshape(1, -1))
    return y.reshape(B * S, ret_w + ml_w)


def _router_kernel(y_ref, x_ref, wo_ref, g_ref, wr_hi_ref, wr_lo_ref, br_ref,
                   h_ref, xn_ref, ri_ref, rf_ref, cnt_ref, run_cnt):
    tm = y_ref.shape[0]

    @pl.when(pl.program_id(0) == 0)
    def _():
        run_cnt[...] = jnp.zeros_like(run_cnt)

    h = x_ref[...] + _dot(y_ref[...], wo_ref[...])
    h_ref[...] = h
    xn = _rms(h, g_ref[...])
    xn_ref[...] = xn
    x_hi = xn.astype(BF16)
    x_lo = (xn - x_hi.astype(F32)).astype(BF16)
    logits = (_dot_nt(wr_hi_ref[...], x_hi) + _dot_nt(wr_hi_ref[...], x_lo)
              + _dot_nt(wr_lo_ref[...], x_hi) + br_ref[...])
    big = jnp.int32(LANES)
    neg = -jnp.inf
    gl = logits[GROUP_ROW0:GROUP_ROW0 + 8]
    grow = lax.broadcasted_iota(jnp.int32, gl.shape, 0)
    is_g = grow < N_GROUPS
    gl = jnp.where(is_g, gl, neg)
    gmax = jnp.max(gl, axis=0, keepdims=True)
    gsum = jnp.sum(jnp.where(is_g, jnp.exp(gl - gmax), 0.0), axis=0, keepdims=True)
    p_g = 1.0 / gsum
    g_sel = jnp.min(jnp.where(is_g & (gl == gmax), grow, big), axis=0, keepdims=True)
    el = logits[0:N_EXPERTS]
    erow = lax.broadcasted_iota(jnp.int32, el.shape, 0)
    in_grp = (erow // EXPERTS_PER_GROUP) == g_sel
    el = jnp.where(in_grp, el, neg)
    emax = jnp.max(el, axis=0, keepdims=True)
    eexp = jnp.where(in_grp, jnp.exp(el - emax), 0.0)
    prob = eexp / jnp.sum(eexp, axis=0, keepdims=True)
    pm1 = jnp.where(in_grp, prob, -1.0)
    p1 = jnp.max(pm1, axis=0, keepdims=True)
    i1 = jnp.min(jnp.where(pm1 == p1, erow, big), axis=0, keepdims=True)
    pm2 = jnp.where(erow == i1, -1.0, pm1)
    p2 = jnp.max(pm2, axis=0, keepdims=True)
    i2 = jnp.min(jnp.where(pm2 == p2, erow, big), axis=0, keepdims=True)
    denom = p1 + p2
    g1 = p_g * p1 / denom
    g2 = p_g * p2 / denom

    sel1 = erow == i1
    sel2 = erow == i2
    onehot = (sel1 | sel2).astype(BF16)
    r_i = lax.broadcasted_iota(jnp.int32, (tm, tm), 0)
    c_i = lax.broadcasted_iota(jnp.int32, (tm, tm), 1)
    tri = (r_i < c_i).astype(BF16)
    prefix = _dot(onehot, tri) + run_cnt[:, 0:1]
    rank1 = jnp.sum(jnp.where(sel1, prefix, 0.0), axis=0, keepdims=True).astype(jnp.int32)
    rank2 = jnp.sum(jnp.where(sel2, prefix, 0.0), axis=0, keepdims=True).astype(jnp.int32)
    new_cnt = run_cnt[...] + jnp.sum(onehot.astype(F32), axis=1, keepdims=True)
    run_cnt[...] = new_cnt
    cnt_ref[...] = new_cnt.astype(jnp.int32)

    rrow = lax.broadcasted_iota(jnp.int32, (ROUTE_ROWS, tm), 0)
    ri_ref[...] = jnp.where(rrow == 0, i1, jnp.where(rrow == 1, i2,
                            jnp.where(rrow == 2, rank1, jnp.where(rrow == 3, rank2, 0))))
    lrow = lax.broadcasted_iota(jnp.int32, (LANES, tm), 0)
    rf_ref[...] = jnp.where(lrow == 0, g1, jnp.where(lrow == 1, g2, 0.0)).T


def _router(y, x2, w_out, g, wr_hi, wr_lo, br, tm=ROUTE_TILE):
    T, D = x2.shape
    const = lambda i: (0, 0)
    tile = lambda i: (i, 0)
    return pl.pallas_call(
        _router_kernel,
        grid=(T // tm,),
        in_specs=[
            pl.BlockSpec((tm, y.shape[1]), tile),
            pl.BlockSpec((tm, D), tile),
            pl.BlockSpec(w_out.shape, const),
            pl.BlockSpec((1, D), const),
            pl.BlockSpec((ROUTER_ROWS, D), const),
            pl.BlockSpec((ROUTER_ROWS, D), const),
            pl.BlockSpec((ROUTER_ROWS, 1), const),
        ],
        out_specs=[
            pl.BlockSpec((tm, D), tile),
            pl.BlockSpec((tm, D), tile),
            pl.BlockSpec((None, ROUTE_ROWS, tm), lambda i: (i, 0, 0)),
            pl.BlockSpec((tm, LANES), tile),
            pl.BlockSpec((N_EXPERTS, LANES), const),
        ],
        out_shape=[
            jax.ShapeDtypeStruct((T, D), F32),
            jax.ShapeDtypeStruct((T, D), F32),
            jax.ShapeDtypeStruct((T // tm, ROUTE_ROWS, tm), jnp.int32),
            jax.ShapeDtypeStruct((T, LANES), F32),
            jax.ShapeDtypeStruct((N_EXPERTS, LANES), jnp.int32),
        ],
        scratch_shapes=[pltpu.VMEM((N_EXPERTS, LANES), F32)],
        compiler_params=pltpu.CompilerParams(
            dimension_semantics=("arbitrary",), vmem_limit_bytes=VMEM_LIMIT),
        name="router",
    )(y, x2, w_out, g, wr_hi, wr_lo, br)


def _slot_base(tok0):
    return (tok0 // ROUTE_TILE) * (2 * ROUTE_TILE) + tok0 % ROUTE_TILE


def _route_spans(tm):
    assert tm % ROUTE_TILE == 0 or ROUTE_TILE % tm == 0
    span = min(tm, ROUTE_TILE)
    return [(t0, span) for t0 in range(0, tm, span)]


def _for_each_pad_block(pend_ref, nu_ref, n_blocks, fn):
    blk = EXPERT_BLOCK
    for e in range(N_EXPERTS):
        prev_end = 0 if e == 0 else pend_ref[e - 1]

        @pl.when(pend_ref[e] > prev_end)
        def _():
            fn(pl.multiple_of(pend_ref[e] - blk, blk))

    def tail(j, c):
        fn(pl.multiple_of(j * blk, blk))
        return c

    lax.fori_loop(nu_ref[0], n_blocks, tail, 0)


def _dispatch_kernel(dest_ref, pend_ref, nu_ref, xn_ref, buf_ref, zeros, sem, zsem):
    tm = xn_ref.shape[0] * SUBLANES
    blk = EXPERT_BLOCK
    i = pl.program_id(0)

    @pl.when(i == 0)
    def _():
        zeros[...] = jnp.zeros_like(zeros)
        zcopy = lambda row: pltpu.make_async_copy(zeros, buf_ref.at[pl.ds(row, blk)], zsem)
        n_blocks = buf_ref.shape[0] // blk
        _for_each_pad_block(pend_ref, nu_ref, n_blocks, lambda row: zcopy(row).start())
        _for_each_pad_block(pend_ref, nu_ref, n_blocks, lambda row: zcopy(row).wait())

    for t0, span in _route_spans(tm):
        slot0 = _slot_base(i * tm + t0)

        def issue(g, c, t0=t0, slot0=slot0):
            for j in range(SUBLANES):
                for k in range(2):
                    d = dest_ref[slot0 + k * ROUTE_TILE + g * SUBLANES + j]
                    pltpu.make_async_copy(xn_ref.at[t0 // SUBLANES + g, pl.ds(j, 1)],
                                          buf_ref.at[pl.ds(d, 1)], sem).start()
            return c

        lax.fori_loop(0, span // SUBLANES, issue, 0)
    for k in range(2):
        pltpu.make_async_copy(buf_ref.at[pl.ds(tm, tm)], buf_ref.at[pl.ds(0, tm)], sem).wait()


def _dispatch(dest, pad_end, n_used, xn, n_rows, tm=ROUTE_TILE):
    T, D = xn.shape
    xn = xn.reshape(T // SUBLANES, SUBLANES, D)
    return pl.pallas_call(
        _dispatch_kernel,
        grid_spec=pltpu.PrefetchScalarGridSpec(
            num_scalar_prefetch=3,
            grid=(T // tm,),
            in_specs=[pl.BlockSpec((tm // SUBLANES, SUBLANES, D), lambda i, *_: (i, 0, 0))],
            out_specs=pl.BlockSpec(memory_space=pl.ANY),
            scratch_shapes=[
                pltpu.VMEM((EXPERT_BLOCK, D), xn.dtype),
                pltpu.SemaphoreType.DMA,
                pltpu.SemaphoreType.DMA,
            ],
        ),
        out_shape=jax.ShapeDtypeStruct((n_rows, D), xn.dtype),
        compiler_params=pltpu.CompilerParams(
            dimension_semantics=("arbitrary",), vmem_limit_bytes=VMEM_LIMIT),
        name="dispatch",
    )(dest, pad_end, n_used, xn)


def _expert_kernel(be_ref, nu_ref, x_ref, w1_ref, w3_ref, w2_ref, y_ref, w1b, w3b, w2b):
    j = pl.program_id(0)
    used = j < nu_ref[0]
    prev_expert = be_ref[jnp.maximum(j, 1) - 1]

    @pl.when(used & ((j == 0) | (be_ref[j] != prev_expert)))
    def _():
        w1b[...] = w1_ref[...].astype(BF16)
        w3b[...] = w3_ref[...].astype(BF16)
        w2b[...] = w2_ref[...].astype(BF16)

    @pl.when(used)
    def _():
        x = x_ref[...].astype(BF16)
        a = _dot(x, w1b[...])
        hmid = a * jax.nn.sigmoid(a) * _dot(x, w3b[...])
        y_ref[...] = _dot(hmid.astype(BF16), w2b[...])

    @pl.when(jnp.logical_not(used))
    def _():
        y_ref[...] = jnp.zeros_like(y_ref)


def _experts(block_expert, n_used, x_buf, w1, w3, w2):
    P, D = x_buf.shape
    blk = EXPERT_BLOCK
    d_exp = w2.shape[1]
    clamp = lambda j, nu: jnp.minimum(j, nu[0] - 1)
    w_idx = lambda j, be, nu: (be[clamp(j, nu)], 0, 0)
    return pl.pallas_call(
        _expert_kernel,
        grid_spec=pltpu.PrefetchScalarGridSpec(
            num_scalar_prefetch=2,
            grid=(P // blk,),
            in_specs=[
                pl.BlockSpec((blk, D), lambda j, be, nu: (clamp(j, nu), 0)),
                pl.BlockSpec((None, D, d_exp), w_idx),
                pl.BlockSpec((None, D, d_exp), w_idx),
                pl.BlockSpec((None, d_exp, D), w_idx),
            ],
            out_specs=pl.BlockSpec((blk, D), lambda j, be, nu: (j, 0)),
            scratch_shapes=[
                pltpu.VMEM((D, d_exp), BF16),
                pltpu.VMEM((D, d_exp), BF16),
                pltpu.VMEM((d_exp, D), BF16),
            ],
        ),
        out_shape=jax.ShapeDtypeStruct((P, D), F32),
        compiler_params=pltpu.CompilerParams(
            dimension_semantics=("arbitrary",), vmem_limit_bytes=VMEM_LIMIT),
        name="experts",
    )(block_expert, n_used, x_buf, w1, w3, w2)


def _final_kernel(dest_ref, h_ref, rf_ref, p_ref, wup_ref, gple_ref, ggate_ref, wgate_ref,
                  gfin_ref, ybuf_ref, out_ref, y0, y1, sem):
    tm = h_ref.shape[0]
    ys = (y0, y1)
    for t0, span in _route_spans(tm):
        slot0 = _slot_base(pl.program_id(0) * tm + t0)

        def issue(g, c, t0=t0, slot0=slot0):
            for j in range(SUBLANES):
                for k in range(2):
                    d = dest_ref[slot0 + k * ROUTE_TILE + g * SUBLANES + j]
                    pltpu.make_async_copy(ybuf_ref.at[pl.ds(d, 1)],
                                          ys[k].at[t0 // SUBLANES + g, pl.ds(j, 1)], sem).start()
            return c

        lax.fori_loop(0, span // SUBLANES, issue, 0)
    e = _rms(_dot(p_ref[...].astype(BF16), wup_ref[...]), gple_ref[...])
    for k in range(2):
        pltpu.make_async_copy(ybuf_ref.at[pl.ds(0, tm)], ybuf_ref.at[pl.ds(tm, tm)], sem).wait()

    rf = rf_ref[...]
    D = h_ref.shape[1]
    h = (h_ref[...] + rf[:, 0:1] * y0[...].reshape(tm, D) + rf[:, 1:2] * y1[...].reshape(tm, D))
    gate = jax.nn.sigmoid(_dot(_rms(h, ggate_ref[...]).astype(BF16), wgate_ref[...]))
    h = h + gate * e
    out_ref[...] = _rms(h, gfin_ref[...])


def _final(dest, h, rf, p2, w_up, g_ple, g_gate, w_gate, g_fin, y_buf, tm=256):
    T, D = h.shape
    const = lambda i, d: (0, 0)
    tile = lambda i, d: (i, 0)
    return pl.pallas_call(
        _final_kernel,
        grid_spec=pltpu.PrefetchScalarGridSpec(
            num_scalar_prefetch=1,
            grid=(T // tm,),
            in_specs=[
                pl.BlockSpec((tm, D), tile),
                pl.BlockSpec((tm, LANES), tile),
                pl.BlockSpec((tm, p2.shape[1]), tile),
                pl.BlockSpec(w_up.shape, const),
                pl.BlockSpec((1, D), const),
                pl.BlockSpec((1, D), const),
                pl.BlockSpec(w_gate.shape, const),
                pl.BlockSpec((1, D), const),
                pl.BlockSpec(memory_space=pl.ANY),
            ],
            out_specs=pl.BlockSpec((tm, D), tile),
            scratch_shapes=[
                pltpu.VMEM((tm // SUBLANES, SUBLANES, D), F32),
                pltpu.VMEM((tm // SUBLANES, SUBLANES, D), F32),
                pltpu.SemaphoreType.DMA,
            ],
        ),
        out_shape=jax.ShapeDtypeStruct((T, D), F32),
        compiler_params=pltpu.CompilerParams(
            dimension_semantics=("arbitrary",), vmem_limit_bytes=VMEM_LIMIT),
        name="final",
    )(dest, h, rf, p2, w_up, g_ple, g_gate, w_gate, g_fin, y_buf)


def _layer(h2, p2, positions, B, S, attn_norm, w_in, conv_w, conv_b, b_igate, b_fgate, ret_gn,
           ml_gn, w_out, moe_norm, w_group, b_group, w_router, b_router, w1, w3, w2, w_ple_up,
           ple_norm, ple_gate_norm, w_ple_gate, out_norm):
    T, D = h2.shape
    ret_w = ret_gn.shape[0]
    ml_w = ml_gn.shape[0]
    n_main = 4 * ret_w + 4 * ml_w
    row = lambda v: v.reshape(1, -1).astype(F32)
    pad_lanes = lambda a: jnp.pad(a, ((0, 0), (0, LANES - a.shape[1])))

    ml_heads = ml_w // LANES
    w_main = w_in[:, :n_main].astype(BF16)
    gate_rows = lambda a: jnp.pad(a, ((0, SUBLANES - ml_heads), (0, 0)))
    w_i, w_f = w_in[:, n_main:n_main + ml_heads].T, w_in[:, n_main + ml_heads:].T
    w_gate_t = jnp.concatenate([gate_rows(w_i), gate_rows(w_f)]).astype(BF16)
    gate_bias = jnp.concatenate([gate_rows(b_igate[:, None]), gate_rows(b_fgate[:, None])])
    cos_t, sin_t = _rope_tables(positions, LANES // 2)
    decay, ws, wq, cd = _ret_tables(CHUNK)
    tm_in = 2 * CHUNK
    ops = _inproj(h2.reshape(B, S, D), row(attn_norm), w_main, w_gate_t, gate_bias.astype(F32),
                  cos_t.reshape(B, S, LANES), sin_t.reshape(B, S, LANES),
                  jnp.tile(ws, (tm_in // CHUNK, 1)), conv_w, conv_b.reshape(1, -1), ret_w, ml_w,
                  tm=tm_in)

    y = _mixer(ops, decay, wq, cd, ret_gn, ml_gn, B, S, ret_w, ml_w)

    pad_rows = lambda a: jnp.pad(a, ((0, ROUTER_ROWS - a.shape[0]), (0, 0)))
    wr = pad_rows(jnp.concatenate([w_router, w_group], axis=1).T)
    wr_hi = wr.astype(BF16)
    wr_lo = (wr - wr_hi.astype(F32)).astype(BF16)
    br = pad_rows(jnp.concatenate([b_router, b_group])[:, None].astype(F32))
    h_mid, xn, ri, rf, counts = _router(y, h2, w_out.astype(BF16), row(moe_norm), wr_hi, wr_lo, br)

    blk = EXPERT_BLOCK
    counts = counts[:, 0]
    padded = (counts + blk - 1) // blk * blk
    pad_end = jnp.cumsum(padded)
    pad_start = pad_end - padded
    n_blocks = (2 * T) // blk + N_EXPERTS
    block_start = jnp.arange(n_blocks, dtype=jnp.int32) * blk
    block_expert = jnp.minimum(jnp.sum(pad_end[None, :] <= block_start[:, None], axis=1),
                               N_EXPERTS - 1).astype(jnp.int32)
    n_used = (pad_end[-1:] // blk).astype(jnp.int32)
    dest = ri[:, 2:4, :]
    for e in range(N_EXPERTS):
        dest = dest + jnp.where(ri[:, 0:2, :] == e, pad_start[e], 0)
    dest = dest.astype(jnp.int32).reshape(-1)

    x_buf = _dispatch(dest, pad_end.astype(jnp.int32), n_used, xn, n_blocks * blk)
    y_buf = _experts(block_expert, n_used, x_buf, w1, w3, w2)

    return _final(dest, h_mid, rf, p2, w_ple_up.astype(BF16), row(ple_norm), row(ple_gate_norm),
                  w_ple_gate.astype(BF16), row(out_norm), y_buf)


def kernel(x, p, positions, attn_norm, w_in, conv_w, conv_b, b_igate, b_fgate, ret_gn, ml_gn,
           w_out, moe_norm, w_group, b_group, w_router, b_router, w1, w3, w2, w_ple_up, ple_norm,
           ple_gate_norm, w_ple_gate, final_norm):
    B, S, D = x.shape
    depth = p.shape[0]
    assert depth == 1, "the final RMSNorm is fused into the layer's last kernel"
    out = _layer(x.reshape(B * S, D), p[0].reshape(B * S, -1), positions, B, S,
                 attn_norm[0], w_in[0], conv_w[0], conv_b[0], b_igate[0], b_fgate[0], ret_gn[0],
                 ml_gn[0], w_out[0], moe_norm[0], w_group[0], b_group[0], w_router[0],
                 b_router[0], w1[0], w3[0], w2[0], w_ple_up[0], ple_norm[0], ple_gate_norm[0],
                 w_ple_gate[0], final_norm)
    return out.reshape(B, S, D)
```

```python
import functools

import jax
import jax.numpy as jnp
from jax import lax
from jax.experimental import pallas as pl
from jax.experimental.pallas import tpu as pltpu

F32 = jnp.float32
BF16 = jnp.bfloat16

RET_HEADS = 8
ML_HEADS = 4
CHUNK = 128
CONV_W = 4
ROPE_BASE = 10000.0
N_GROUPS = 4
EXPERTS_PER_GROUP = 8
N_EXPERTS = N_GROUPS * EXPERTS_PER_GROUP
EPS = 1e-6

LANES = 128
SUBLANES = 8
VMEM_LIMIT = 56 * 1024 * 1024
EXPERT_BLOCK = 512
GROUP_ROW0 = N_EXPERTS
ROUTER_ROWS = 64
ROUTE_TILE = 512
ROUTE_ROWS = 8
GATE_ROWS = 2 * SUBLANES


def _rms(x, g):
    return x * lax.rsqrt(jnp.mean(x * x, axis=-1, keepdims=True) + EPS) * g


def _dot(a, b):
    return jnp.dot(a, b, preferred_element_type=F32)


def _dot_nt(a, b):
    return lax.dot_general(a, b, (((1,), (1,)), ((), ())), preferred_element_type=F32)


def _split3(x):
    hi = x.astype(BF16)
    r1 = x - hi.astype(F32)
    mid = r1.astype(BF16)
    lo = (r1 - mid.astype(F32)).astype(BF16)
    return hi, mid, lo


def _rope_kernel(pos_ref, freq_ref, sign_ref, cos_ref, sin_ref, *, half):
    ang = pos_ref[...].astype(F32) * freq_ref[...]
    cos_c = jnp.cos(ang)
    sin_c = jnp.sin(ang)
    group = lax.broadcasted_iota(jnp.int32, ang.shape, 1) // half
    per_row = LANES // half
    for s in range(per_row):
        def spread(t):
            out = t
            for g in range(per_row):
                if g != s:
                    out = jnp.where(group == g, pltpu.roll(t, (half * (g - s)) % LANES, 1), out)
            return out
        cos_ref[s] = spread(cos_c)
        sin_ref[s] = spread(sin_c) * sign_ref[...]


def _rope_tables(positions, dh):
    half = dh // 2
    per_row = LANES // half
    T = positions.size
    rows = T // per_row
    tr = min(rows, 1024)
    freqs = ROPE_BASE ** (-jnp.arange(half, dtype=F32) / half)
    pos_c = jnp.repeat(positions.reshape(per_row, rows).T, half, axis=1)
    sign = jnp.tile(jnp.concatenate([-jnp.ones((half,), F32), jnp.ones((half,), F32)]),
                    LANES // dh)[None, :]
    const = pl.BlockSpec((1, LANES), lambda i: (0, 0))
    out = pl.BlockSpec((per_row, tr, LANES), lambda i: (0, i, 0))
    cos_t, sin_t = pl.pallas_call(
        functools.partial(_rope_kernel, half=half),
        grid=(rows // tr,),
        in_specs=[pl.BlockSpec((tr, LANES), lambda i: (i, 0)), const, const],
        out_specs=[out, out],
        out_shape=[jax.ShapeDtypeStruct((per_row, rows, LANES), F32)] * 2,
        name="rope",
    )(pos_c, jnp.tile(freqs, per_row)[None, :], sign)
    return cos_t.reshape(T, LANES), sin_t.reshape(T, LANES)


def _inproj_kernel(x_ref, g_ref, wm_ref, wgt_ref, gb_ref, cos_ref, sin_ref, ws_ref, convw_ref,
                   convb_ref, rq_ref, rk_ref, rkwt_ref, rv_ref, rg_ref, mq_ref, mk_ref, mvt_ref,
                   mo_ref, gt_ref, carry, *, ret_w, ml_w):
    tm = x_ref.shape[0]
    ret_dh = LANES // 2

    @pl.when(pl.program_id(1) == 0)
    def _():
        carry[...] = jnp.zeros_like(carry)

    xn = _rms(x_ref[...], g_ref[...]).astype(BF16)
    proj = lambda o, w: _dot(xn, wm_ref[:, o:o + w])
    tiles = lambda w: [slice(t, t + LANES) for t in range(0, w, LANES)]

    lane = lax.broadcasted_iota(jnp.int32, (tm, LANES), 1)
    first_half = (lane % ret_dh) < (ret_dh // 2)
    cos_t = cos_ref[...]
    sin_t = sin_ref[...]

    def rot(t):
        swapped = jnp.where(first_half, pltpu.roll(t, LANES - ret_dh // 2, 1),
                            pltpu.roll(t, ret_dh // 2, 1))
        return t * cos_t + swapped * sin_t

    rq = proj(0, ret_w)
    for ps in tiles(ret_w):
        rq_ref[:, ps] = rot(rq[:, ps]).astype(BF16)
    rk = proj(ret_w, ret_w)
    for ps in tiles(ret_w):
        k = rot(rk[:, ps]) * (ret_dh ** -0.5)
        rk_ref[:, ps] = k.astype(BF16)
        rkwt_ref[ps, :] = (k * ws_ref[:, ps]).T.astype(BF16)
    rv_ref[...] = proj(2 * ret_w, ret_w).astype(BF16)
    g = proj(3 * ret_w, ret_w)
    rg_ref[...] = g * jax.nn.sigmoid(g)

    o_mq = 4 * ret_w
    row8 = lax.broadcasted_iota(jnp.int32, (SUBLANES, ml_w), 0)
    for idx, dst in enumerate((mq_ref, mk_ref)):
        cs = slice(idx * ml_w, (idx + 1) * ml_w)
        xq = proj(o_mq + idx * ml_w, ml_w)
        prev8 = carry[:, cs]
        carry[:, cs] = xq[tm - SUBLANES:tm]
        acc = xq * convw_ref[CONV_W - 1:CONV_W, cs] + convb_ref[:, cs]
        for s in range(1, CONV_W):
            rolled = pltpu.roll(xq, s, 0)
            head = jnp.where(row8 < s, pltpu.roll(prev8, s, 0), rolled[0:SUBLANES])
            shifted = jnp.concatenate([head, rolled[SUBLANES:]], axis=0)
            acc = acc + shifted * convw_ref[CONV_W - 1 - s:CONV_W - s, cs]
        act = acc * jax.nn.sigmoid(acc)
        if idx == 1:
            act = act * (LANES ** -0.5)
        dst[...] = act.astype(BF16)
    mv = proj(o_mq + 2 * ml_w, ml_w)
    for hs in tiles(ml_w):
        mvt_ref[hs, :] = mv[:, hs].T
    mo_ref[...] = jax.nn.sigmoid(proj(o_mq + 3 * ml_w, ml_w))
    gt_ref[...] = _dot_nt(wgt_ref[...], xn) + gb_ref[...]


def _inproj(x3, g, w_main, w_gate_t, gate_bias, cos_t, sin_t, ws, conv_w, conv_b, ret_w, ml_w,
            tm=256):
    B, S, D = x3.shape
    n_main = w_main.shape[1]
    const = lambda b, i: (0, 0)
    tok = lambda w: pl.BlockSpec((None, tm, w), lambda b, i: (b, i, 0))
    tok_t = lambda r: pl.BlockSpec((None, r, tm), lambda b, i: (b, 0, i))
    act = lambda w, dt: jax.ShapeDtypeStruct((B, S, w), dt)
    act_t = lambda r, dt: jax.ShapeDtypeStruct((B, r, S), dt)
    return pl.pallas_call(
        functools.partial(_inproj_kernel, ret_w=ret_w, ml_w=ml_w),
        grid=(B, S // tm),
        in_specs=[
            tok(D),
            pl.BlockSpec((1, D), const),
            pl.BlockSpec((D, n_main), const),
            pl.BlockSpec((GATE_ROWS, D), const),
            pl.BlockSpec((GATE_ROWS, 1), const),
            tok(LANES),
            tok(LANES),
            pl.BlockSpec((tm, ret_w), const),
            pl.BlockSpec((CONV_W, 2 * ml_w), const),
            pl.BlockSpec((1, 2 * ml_w), const),
        ],
        out_specs=[tok(ret_w), tok(ret_w), tok_t(ret_w), tok(ret_w), tok(ret_w),
                   tok(ml_w), tok(ml_w), tok_t(ml_w), tok(ml_w), tok_t(GATE_ROWS)],
        out_shape=[
            act(ret_w, BF16),
            act(ret_w, BF16),
            act_t(ret_w, BF16),
            act(ret_w, BF16),
            act(ret_w, F32),
            act(ml_w, BF16),
            act(ml_w, BF16),
            act_t(ml_w, F32),
            act(ml_w, F32),
            act_t(GATE_ROWS, F32),
        ],
        scratch_shapes=[pltpu.VMEM((SUBLANES, 2 * ml_w), F32)],
        compiler_params=pltpu.CompilerParams(
            dimension_semantics=("arbitrary", "arbitrary"), vmem_limit_bytes=VMEM_LIMIT),
        name="inproj",
    )(x3, g, w_main, w_gate_t, gate_bias, cos_t, sin_t, ws, conv_w, conv_b)


def _mixer_kernel(rq_ref, rk_ref, rkwt_ref, rv_ref, rg_ref, mq_ref, mk_ref, mvt_ref, mo_ref,
                  gt_ref, decay_ref, wq_ref, cd_ref, retgn_ref, mlgn_ref,
                  y_ref, r_state, s_state, n_state, m_state, *, ret_w, ml_w):
    C = rq_ref.shape[0]
    n_pairs = ret_w // LANES
    ml_heads = ml_w // LANES
    ret_dh = LANES // 2

    @pl.when(pl.program_id(1) == 0)
    def _():
        r_state[...] = jnp.zeros_like(r_state)
        s_state[...] = jnp.zeros_like(s_state)
        n_state[...] = jnp.zeros_like(n_state)
        m_state[...] = jnp.zeros_like(m_state)

    lane = lax.broadcasted_iota(jnp.int32, (C, LANES), 1)
    row = lax.broadcasted_iota(jnp.int32, (C, LANES), 0)
    assert C == LANES
    lo = lane < ret_dh
    blockdiag = (row < ret_dh) == lo
    lo_b = jnp.where(lo, 1.0, 0.0).astype(BF16)
    hi_b = jnp.where(lo, 0.0, 1.0).astype(BF16)
    pairs = [slice(p * LANES, (p + 1) * LANES) for p in range(n_pairs)]
    heads = [slice(h * LANES, (h + 1) * LANES) for h in range(ml_heads)]


    row8 = lax.broadcasted_iota(jnp.int32, (SUBLANES, C), 0)
    lane8 = lax.broadcasted_iota(jnp.int32, (SUBLANES, C), 1)
    live = row8 < ml_heads
    ig = jnp.where(live, gt_ref[0:SUBLANES, :], 0.0)
    f_pre = jnp.where(live, gt_ref[SUBLANES:2 * SUBLANES, :], 30.0)
    logf = jax.nn.log_sigmoid(f_pre)
    triu = (row <= lane).astype(BF16)
    l_hi, l_mid, l_lo = _split3(logf)
    b = _dot(l_hi, triu) + _dot(l_mid, triu) + _dot(l_lo, triu)
    beta = ig - b
    cm = beta
    shift = 1
    while shift < C:
        cm = jnp.maximum(cm, jnp.where(lane8 >= shift, pltpu.roll(cm, shift, 1), -jnp.inf))
        shift *= 2
    m_prev = m_state[...]
    mx = jnp.maximum(cm, m_prev)
    mx_last = jnp.broadcast_to(mx[:, C - 1:C], (SUBLANES, C))
    w_inter = jnp.exp(m_prev - mx)
    e_negm = jnp.exp(-(b + mx))
    w_state = jnp.exp(beta - mx_last)
    dec = jnp.exp(m_prev - mx_last)
    m_state[...] = jnp.where(live, jnp.broadcast_to(b[:, C - 1:C], (SUBLANES, C)) + mx_last, 0.0)
    w_state_b = w_state.astype(BF16)
    beta_t = jnp.concatenate([beta, jnp.zeros((LANES - SUBLANES, C), F32)], axis=0).T
    before = row <= lane

    hrow = lambda t, h: t[h:h + 1, :]

    rq = [rq_ref[:, ps] for ps in pairs]
    rk = [rk_ref[:, ps] for ps in pairs]
    rv = [rv_ref[:, ps] for ps in pairs]
    r_prev = [r_state[p] for p in range(n_pairs)]
    s_a = [_dot_nt(rq[p] * lo_b, rk[p]) for p in range(n_pairs)]
    s_b = [_dot_nt(rq[p] * hi_b, rk[p]) for p in range(n_pairs)]
    r_read = [_dot(rq[p], r_prev[p].astype(BF16)) for p in range(n_pairs)]
    r_new = [_dot(rkwt_ref[pairs[p], :], rv[p]) for p in range(n_pairs)]
    mq = [mq_ref[:, hs] for hs in heads]
    mk = [mk_ref[:, hs] for hs in heads]
    mv_t = [mvt_ref[hs, :] for hs in heads]
    st_prev = [s_state[h] for h in range(ml_heads)]
    n_prev = [n_state[h] for h in range(ml_heads)]
    a_t = [_dot_nt(mk[h], mq[h]) for h in range(ml_heads)]
    s_read = [_dot_nt(st_prev[h].astype(BF16), mq[h]) for h in range(ml_heads)]
    qn = [_dot_nt(n_prev[h].astype(BF16), mq[h])[0:1, :] for h in range(ml_heads)]
    s_new = [_dot((mv_t[h] * hrow(w_state, h)).astype(BF16), mk[h]) for h in range(ml_heads)]
    n_new = [_dot(w_state_b, mk[h])[h:h + 1, :] for h in range(ml_heads)]
    for p in range(n_pairs):
        r_state[p] = cd_ref[p] * r_prev[p] + jnp.where(blockdiag, r_new[p], 0.0)
    for h in range(ml_heads):
        s_state[h] = hrow(dec, h) * st_prev[h] + s_new[h]
        n_state[h] = hrow(dec, h) * n_prev[h] + n_new[h]

    s_a = [(s_a[p] * decay_ref[2 * p]).astype(BF16) for p in range(n_pairs)]
    s_b = [(s_b[p] * decay_ref[2 * p + 1]).astype(BF16) for p in range(n_pairs)]
    p_t = [jnp.exp(jnp.where(before, beta_t[:, h:h + 1] - hrow(mx, h), -jnp.inf)) * a_t[h]
           for h in range(ml_heads)]

    o = [_dot(s_a[p], rv[p] * lo_b) + _dot(s_b[p], rv[p] * hi_b) + r_read[p] * wq_ref[p]
         for p in range(n_pairs)]
    num_t = [_dot(mv_t[h].astype(BF16), p_t[h].astype(BF16)) + hrow(w_inter, h) * s_read[h]
             for h in range(ml_heads)]

    for p in range(n_pairs):
        sq = o[p] * o[p]
        ms_a = jnp.sum(jnp.where(lo, sq, 0.0), axis=-1, keepdims=True)
        ms_b = jnp.sum(jnp.where(lo, 0.0, sq), axis=-1, keepdims=True)
        ms = jnp.where(lo, ms_a, ms_b) * (1.0 / ret_dh)
        r = o[p] * lax.rsqrt(ms + EPS) * retgn_ref[:, pairs[p]]
        y_ref[:, pairs[p]] = (rg_ref[:, pairs[p]] * r).astype(y_ref.dtype)
    for h in range(ml_heads):
        den = jnp.sum(p_t[h], axis=0, keepdims=True) + hrow(w_inter, h) * qn[h]
        hh = (num_t[h] * (1.0 / jnp.maximum(jnp.abs(den), hrow(e_negm, h)))).T
        hm = _rms(mo_ref[:, heads[h]] * hh, mlgn_ref[:, heads[h]])
        y_ref[:, ret_w + h * LANES:ret_w + (h + 1) * LANES] = hm.astype(y_ref.dtype)


def _ret_tables(C):
    H = RET_HEADS
    dh = LANES // 2
    log_gamma = jnp.log1p(-(2.0 ** (-5.0 - jnp.arange(H, dtype=F32))))
    idx = jnp.arange(C, dtype=F32)
    rel = idx[:, None] - idx[None, :]
    causal = rel >= 0
    decay = jnp.where(causal, jnp.exp(log_gamma[:, None, None] * jnp.where(causal, rel, 0.0)), 0.0)
    w_state = jnp.exp(log_gamma[:, None] * (C - 1 - idx))
    w_query = jnp.exp(log_gamma[:, None] * (idx + 1.0))
    chunk_decay = jnp.exp(log_gamma * C)
    pair = lambda t: jnp.repeat(t.reshape(H // 2, 2, C).transpose(0, 2, 1), dh, axis=2)
    cd = jnp.repeat(chunk_decay.reshape(H // 2, 2), dh, axis=1)
    cd = jnp.broadcast_to(cd[:, :, None], (H // 2, LANES, LANES))
    ws = pair(w_state).transpose(1, 0, 2).reshape(C, (H // 2) * LANES)
    return decay, ws, pair(w_query), cd


def _mixer(ops, decay, wq, cd, ret_gn, ml_gn, B, S, ret_w, ml_w):
    C = CHUNK
    N = S // C
    n_pairs = ret_w // LANES
    ml_heads = ml_w // LANES
    c2 = lambda b, n: (0, 0)
    c3 = lambda b, n: (0, 0, 0)
    tok = lambda w: pl.BlockSpec((None, C, w), lambda b, n: (b, n, 0))
    tok_t = lambda r: pl.BlockSpec((None, r, C), lambda b, n: (b, 0, n))
    y = pl.pallas_call(
        functools.partial(_mixer_kernel, ret_w=ret_w, ml_w=ml_w),
        grid=(B, N),
        in_specs=[
            tok(ret_w), tok(ret_w), tok_t(ret_w), tok(ret_w), tok(ret_w),
            tok(ml_w), tok(ml_w), tok_t(ml_w), tok(ml_w), tok_t(GATE_ROWS),
            pl.BlockSpec((RET_HEADS, C, C), c3),
            pl.BlockSpec((n_pairs, C, LANES), c3),
            pl.BlockSpec((n_pairs, LANES, LANES), c3),
            pl.BlockSpec((1, ret_w), c2),
            pl.BlockSpec((1, ml_w), c2),
        ],
        out_specs=pl.BlockSpec((None, C, ret_w + ml_w), lambda b, n: (b, n, 0)),
        out_shape=jax.ShapeDtypeStruct((B, S, ret_w + ml_w), BF16),
        scratch_shapes=[
            pltpu.VMEM((n_pairs, LANES, LANES), F32),
            pltpu.VMEM((ml_heads, LANES, LANES), F32),
            pltpu.VMEM((ml_heads, SUBLANES, LANES), F32),
            pltpu.VMEM((SUBLANES, LANES), F32),
        ],
        compiler_params=pltpu.CompilerParams(
            dimension_semantics=("arbitrary", "arbitrary"), vmem_limit_bytes=VMEM_LIMIT),
        name="mixer",
    )(*ops, decay, wq, cd, ret_gn.reshape(1, -1), ml_gn.reshape(1, -1))
    return y.reshape(B * S, ret_w + ml_w)


def _router_kernel(y_ref, x_ref, wo_ref, g_ref, wr_hi_ref, wr_lo_ref, br_ref,
                   h_ref, xn_ref, ri_ref, rf_ref, cnt_ref, run_cnt):
    tm = y_ref.shape[0]

    @pl.when(pl.program_id(0) == 0)
    def _():
        run_cnt[...] = jnp.zeros_like(run_cnt)

    h = x_ref[...] + _dot(y_ref[...], wo_ref[...])
    h_ref[...] = h
    xn = _rms(h, g_ref[...])
    xn_ref[...] = xn
    x_hi = xn.astype(BF16)
    x_lo = (xn - x_hi.astype(F32)).astype(BF16)
    logits = (_dot_nt(wr_hi_ref[...], x_hi) + _dot_nt(wr_hi_ref[...], x_lo)
              + _dot_nt(wr_lo_ref[...], x_hi) + br_ref[...])
    big = jnp.int32(LANES)
    neg = -jnp.inf
    gl = logits[GROUP_ROW0:GROUP_ROW0 + 8]
    grow = lax.broadcasted_iota(jnp.int32, gl.shape, 0)
    is_g = grow < N_GROUPS
    gl = jnp.where(is_g, gl, neg)
    gmax = jnp.max(gl, axis=0, keepdims=True)
    gsum = jnp.sum(jnp.where(is_g, jnp.exp(gl - gmax), 0.0), axis=0, keepdims=True)
    p_g = 1.0 / gsum
    g_sel = jnp.min(jnp.where(is_g & (gl == gmax), grow, big), axis=0, keepdims=True)
    el = logits[0:N_EXPERTS]
    erow = lax.broadcasted_iota(jnp.int32, el.shape, 0)
    in_grp = (erow // EXPERTS_PER_GROUP) == g_sel
    el = jnp.where(in_grp, el, neg)
    emax = jnp.max(el, axis=0, keepdims=True)
    eexp = jnp.where(in_grp, jnp.exp(el - emax), 0.0)
    prob = eexp / jnp.sum(eexp, axis=0, keepdims=True)
    pm1 = jnp.where(in_grp, prob, -1.0)
    p1 = jnp.max(pm1, axis=0, keepdims=True)
    i1 = jnp.min(jnp.where(pm1 == p1, erow, big), axis=0, keepdims=True)
    pm2 = jnp.where(erow == i1, -1.0, pm1)
    p2 = jnp.max(pm2, axis=0, keepdims=True)
    i2 = jnp.min(jnp.where(pm2 == p2, erow, big), axis=0, keepdims=True)
    denom = p1 + p2
    g1 = p_g * p1 / denom
    g2 = p_g * p2 / denom

    sel1 = erow == i1
    sel2 = erow == i2
    onehot = (sel1 | sel2).astype(BF16)
    r_i = lax.broadcasted_iota(jnp.int32, (tm, tm), 0)
    c_i = lax.broadcasted_iota(jnp.int32, (tm, tm), 1)
    tri = (r_i < c_i).astype(BF16)
    prefix = _dot(onehot, tri) + run_cnt[:, 0:1]
    rank1 = jnp.sum(jnp.where(sel1, prefix, 0.0), axis=0, keepdims=True).astype(jnp.int32)
    rank2 = jnp.sum(jnp.where(sel2, prefix, 0.0), axis=0, keepdims=True).astype(jnp.int32)
    new_cnt = run_cnt[...] + jnp.sum(onehot.astype(F32), axis=1, keepdims=True)
    run_cnt[...] = new_cnt
    cnt_ref[...] = new_cnt.astype(jnp.int32)

    rrow = lax.broadcasted_iota(jnp.int32, (ROUTE_ROWS, tm), 0)
    ri_ref[...] = jnp.where(rrow == 0, i1, jnp.where(rrow == 1, i2,
                            jnp.where(rrow == 2, rank1, jnp.where(rrow == 3, rank2, 0))))
    lrow = lax.broadcasted_iota(jnp.int32, (LANES, tm), 0)
    rf_ref[...] = jnp.where(lrow == 0, g1, jnp.where(lrow == 1, g2, 0.0)).T


def _router(y, x2, w_out, g, wr_hi, wr_lo, br, tm=ROUTE_TILE):
    T, D = x2.shape
    const = lambda i: (0, 0)
    tile = lambda i: (i, 0)
    return pl.pallas_call(
        _router_kernel,
        grid=(T // tm,),
        in_specs=[
            pl.BlockSpec((tm, y.shape[1]), tile),
            pl.BlockSpec((tm, D), tile),
            pl.BlockSpec(w_out.shape, const),
            pl.BlockSpec((1, D), const),
            pl.BlockSpec((ROUTER_ROWS, D), const),
            pl.BlockSpec((ROUTER_ROWS, D), const),
            pl.BlockSpec((ROUTER_ROWS, 1), const),
        ],
        out_specs=[
            pl.BlockSpec((tm, D), tile),
            pl.BlockSpec((tm, D), tile),
            pl.BlockSpec((None, ROUTE_ROWS, tm), lambda i: (i, 0, 0)),
            pl.BlockSpec((tm, LANES), tile),
            pl.BlockSpec((N_EXPERTS, LANES), const),
        ],
        out_shape=[
            jax.ShapeDtypeStruct((T, D), F32),
            jax.ShapeDtypeStruct((T, D), F32),
            jax.ShapeDtypeStruct((T // tm, ROUTE_ROWS, tm), jnp.int32),
            jax.ShapeDtypeStruct((T, LANES), F32),
            jax.ShapeDtypeStruct((N_EXPERTS, LANES), jnp.int32),
        ],
        scratch_shapes=[pltpu.VMEM((N_EXPERTS, LANES), F32)],
        compiler_params=pltpu.CompilerParams(
            dimension_semantics=("arbitrary",), vmem_limit_bytes=VMEM_LIMIT),
        name="router",
    )(y, x2, w_out, g, wr_hi, wr_lo, br)


def _slot_base(tok0):
    return (tok0 // ROUTE_TILE) * (2 * ROUTE_TILE) + tok0 % ROUTE_TILE


def _route_spans(tm):
    assert tm % ROUTE_TILE == 0 or ROUTE_TILE % tm == 0
    span = min(tm, ROUTE_TILE)
    return [(t0, span) for t0 in range(0, tm, span)]


def _for_each_pad_block(pend_ref, nu_ref, n_blocks, fn):
    blk = EXPERT_BLOCK
    for e in range(N_EXPERTS):
        prev_end = 0 if e == 0 else pend_ref[e - 1]

        @pl.when(pend_ref[e] > prev_end)
        def _():
            fn(pl.multiple_of(pend_ref[e] - blk, blk))

    def tail(j, c):
        fn(pl.multiple_of(j * blk, blk))
        return c

    lax.fori_loop(nu_ref[0], n_blocks, tail, 0)


def _dispatch_kernel(dest_ref, pend_ref, nu_ref, xn_ref, buf_ref, zeros, sem, zsem):
    tm = xn_ref.shape[0] * SUBLANES
    blk = EXPERT_BLOCK
    i = pl.program_id(0)

    @pl.when(i == 0)
    def _():
        zeros[...] = jnp.zeros_like(zeros)
        zcopy = lambda row: pltpu.make_async_copy(zeros, buf_ref.at[pl.ds(row, blk)], zsem)
        n_blocks = buf_ref.shape[0] // blk
        _for_each_pad_block(pend_ref, nu_ref, n_blocks, lambda row: zcopy(row).start())
        _for_each_pad_block(pend_ref, nu_ref, n_blocks, lambda row: zcopy(row).wait())

    for t0, span in _route_spans(tm):
        slot0 = _slot_base(i * tm + t0)

        def issue(g, c, t0=t0, slot0=slot0):
            for j in range(SUBLANES):
                for k in range(2):
                    d = dest_ref[slot0 + k * ROUTE_TILE + g * SUBLANES + j]
                    pltpu.make_async_copy(xn_ref.at[t0 // SUBLANES + g, pl.ds(j, 1)],
                                          buf_ref.at[pl.ds(d, 1)], sem).start()
            return c

        lax.fori_loop(0, span // SUBLANES, issue, 0)
    for k in range(2):
        pltpu.make_async_copy(buf_ref.at[pl.ds(tm, tm)], buf_ref.at[pl.ds(0, tm)], sem).wait()


def _dispatch(dest, pad_end, n_used, xn, n_rows, tm=ROUTE_TILE):
    T, D = xn.shape
    xn = xn.reshape(T // SUBLANES, SUBLANES, D)
    return pl.pallas_call(
        _dispatch_kernel,
        grid_spec=pltpu.PrefetchScalarGridSpec(
            num_scalar_prefetch=3,
            grid=(T // tm,),
            in_specs=[pl.BlockSpec((tm // SUBLANES, SUBLANES, D), lambda i, *_: (i, 0, 0))],
            out_specs=pl.BlockSpec(memory_space=pl.ANY),
            scratch_shapes=[
                pltpu.VMEM((EXPERT_BLOCK, D), xn.dtype),
                pltpu.SemaphoreType.DMA,
                pltpu.SemaphoreType.DMA,
            ],
        ),
        out_shape=jax.ShapeDtypeStruct((n_rows, D), xn.dtype),
        compiler_params=pltpu.CompilerParams(
            dimension_semantics=("arbitrary",), vmem_limit_bytes=VMEM_LIMIT),
        name="dispatch",
    )(dest, pad_end, n_used, xn)


def _expert_kernel(be_ref, nu_ref, x_ref, w1_ref, w3_ref, w2_ref, y_ref, w1b, w3b, w2b):
    j = pl.program_id(0)
    used = j < nu_ref[0]
    prev_expert = be_ref[jnp.maximum(j, 1) - 1]

    @pl.when(used & ((j == 0) | (be_ref[j] != prev_expert)))
    def _():
        w1b[...] = w1_ref[...].astype(BF16)
        w3b[...] = w3_ref[...].astype(BF16)
        w2b[...] = w2_ref[...].astype(BF16)

    @pl.when(used)
    def _():
        x = x_ref[...].astype(BF16)
        a = _dot(x, w1b[...])
        hmid = a * jax.nn.sigmoid(a) * _dot(x, w3b[...])
        y_ref[...] = _dot(hmid.astype(BF16), w2b[...])

    @pl.when(jnp.logical_not(used))
    def _():
        y_ref[...] = jnp.zeros_like(y_ref)


def _experts(block_expert, n_used, x_buf, w1, w3, w2):
    P, D = x_buf.shape
    blk = EXPERT_BLOCK
    d_exp = w2.shape[1]
    clamp = lambda j, nu: jnp.minimum(j, nu[0] - 1)
    w_idx = lambda j, be, nu: (be[clamp(j, nu)], 0, 0)
    return pl.pallas_call(
        _expert_kernel,
        grid_spec=pltpu.PrefetchScalarGridSpec(
            num_scalar_prefetch=2,
            grid=(P // blk,),
            in_specs=[
                pl.BlockSpec((blk, D), lambda j, be, nu: (clamp(j, nu), 0)),
                pl.BlockSpec((None, D, d_exp), w_idx),
                pl.BlockSpec((None, D, d_exp), w_idx),
                pl.BlockSpec((None, d_exp, D), w_idx),
            ],
            out_specs=pl.BlockSpec((blk, D), lambda j, be, nu: (j, 0)),
            scratch_shapes=[
                pltpu.VMEM((D, d_exp), BF16),
                pltpu.VMEM((D, d_exp), BF16),
                pltpu.VMEM((d_exp, D), BF16),
            ],
        ),
        out_shape=jax.ShapeDtypeStruct((P, D), F32),
        compiler_params=pltpu.CompilerParams(
            dimension_semantics=("arbitrary",), vmem_limit_bytes=VMEM_LIMIT),
        name="experts",
    )(block_expert, n_used, x_buf, w1, w3, w2)


def _final_kernel(dest_ref, h_ref, rf_ref, p_ref, wup_ref, gple_ref, ggate_ref, wgate_ref,
                  gfin_ref, ybuf_ref, out_ref, ya0, ya1, yb0, yb1, sem_a, sem_b, *, tm):
    s = pl.program_id(0)
    D = h_ref.shape[1]
    assert ROUTE_TILE % tm == 0

    def gather(tile, bufs, sem, straight=False):
        slot0 = _slot_base(tile * tm)

        def issue(g, c):
            for j in range(SUBLANES):
                for k in range(2):
                    d = dest_ref[slot0 + k * ROUTE_TILE + g * SUBLANES + j]
                    pltpu.make_async_copy(ybuf_ref.at[pl.ds(d, 1)],
                                          bufs[k].at[g, pl.ds(j, 1)], sem).start()
            return c

        if straight:
            for g in range(tm // SUBLANES):
                issue(g, 0)
        else:
            lax.fori_loop(0, tm // SUBLANES, issue, 0)

    def wait(sem):
        for k in range(2):
            pltpu.make_async_copy(ybuf_ref.at[pl.ds(0, tm)], ybuf_ref.at[pl.ds(tm, tm)], sem).wait()

    def compute(rows, bufs):
        e = _rms(_dot(p_ref[rows, :].astype(BF16), wup_ref[...]), gple_ref[...])
        rf = rf_ref[rows, :]
        h = (h_ref[rows, :] + rf[:, 0:1] * bufs[0][...].reshape(tm, D)
             + rf[:, 1:2] * bufs[1][...].reshape(tm, D))
        gate = jax.nn.sigmoid(_dot(_rms(h, ggate_ref[...]).astype(BF16), wgate_ref[...]))
        h = h + gate * e
        out_ref[rows, :] = _rms(h, gfin_ref[...])

    last = pl.num_programs(0) - 1

    @pl.when(s == 0)
    def _():
        gather(0, (ya0, ya1), sem_a)

    wait(sem_a)
    gather(2 * s + 1, (yb0, yb1), sem_b, straight=True)
    compute(slice(0, tm), (ya0, ya1))
    wait(sem_b)
    gather(2 * jnp.minimum(s + 1, last), (ya0, ya1), sem_a, straight=True)
    compute(slice(tm, 2 * tm), (yb0, yb1))

    @pl.when(s == last)
    def _():
        wait(sem_a)


def _final(dest, h, rf, p2, w_up, g_ple, g_gate, w_gate, g_fin, y_buf, tm=256):
    T, D = h.shape
    const = lambda i, d: (0, 0)
    tile = lambda i, d: (i, 0)
    row_buf = pltpu.VMEM((tm // SUBLANES, SUBLANES, D), F32)
    return pl.pallas_call(
        functools.partial(_final_kernel, tm=tm),
        grid_spec=pltpu.PrefetchScalarGridSpec(
            num_scalar_prefetch=1,
            grid=(T // (2 * tm),),
            in_specs=[
                pl.BlockSpec((2 * tm, D), tile),
                pl.BlockSpec((2 * tm, LANES), tile),
                pl.BlockSpec((2 * tm, p2.shape[1]), tile),
                pl.BlockSpec(w_up.shape, const),
                pl.BlockSpec((1, D), const),
                pl.BlockSpec((1, D), const),
                pl.BlockSpec(w_gate.shape, const),
                pl.BlockSpec((1, D), const),
                pl.BlockSpec(memory_space=pl.ANY),
            ],
            out_specs=pl.BlockSpec((2 * tm, D), tile),
            scratch_shapes=[row_buf, row_buf, row_buf, row_buf,
                            pltpu.SemaphoreType.DMA, pltpu.SemaphoreType.DMA],
        ),
        out_shape=jax.ShapeDtypeStruct((T, D), F32),
        compiler_params=pltpu.CompilerParams(
            dimension_semantics=("arbitrary",), vmem_limit_bytes=VMEM_LIMIT),
        name="final",
    )(dest, h, rf, p2, w_up, g_ple, g_gate, w_gate, g_fin, y_buf)


def _layer(h2, p2, positions, B, S, attn_norm, w_in, conv_w, conv_b, b_igate, b_fgate, ret_gn,
           ml_gn, w_out, moe_norm, w_group, b_group, w_router, b_router, w1, w3, w2, w_ple_up,
           ple_norm, ple_gate_norm, w_ple_gate, out_norm):
    T, D = h2.shape
    ret_w = ret_gn.shape[0]
    ml_w = ml_gn.shape[0]
    n_main = 4 * ret_w + 4 * ml_w
    row = lambda v: v.reshape(1, -1).astype(F32)
    pad_lanes = lambda a: jnp.pad(a, ((0, 0), (0, LANES - a.shape[1])))

    ml_heads = ml_w // LANES
    w_main = w_in[:, :n_main].astype(BF16)
    gate_rows = lambda a: jnp.pad(a, ((0, SUBLANES - ml_heads), (0, 0)))
    w_i, w_f = w_in[:, n_main:n_main + ml_heads].T, w_in[:, n_main + ml_heads:].T
    w_gate_t = jnp.concatenate([gate_rows(w_i), gate_rows(w_f)]).astype(BF16)
    gate_bias = jnp.concatenate([gate_rows(b_igate[:, None]), gate_rows(b_fgate[:, None])])
    cos_t, sin_t = _rope_tables(positions, LANES // 2)
    decay, ws, wq, cd = _ret_tables(CHUNK)
    tm_in = 2 * CHUNK
    ops = _inproj(h2.reshape(B, S, D), row(attn_norm), w_main, w_gate_t, gate_bias.astype(F32),
                  cos_t.reshape(B, S, LANES), sin_t.reshape(B, S, LANES),
                  jnp.tile(ws, (tm_in // CHUNK, 1)), conv_w, conv_b.reshape(1, -1), ret_w, ml_w,
                  tm=tm_in)

    y = _mixer(ops, decay, wq, cd, ret_gn, ml_gn, B, S, ret_w, ml_w)

    pad_rows = lambda a: jnp.pad(a, ((0, ROUTER_ROWS - a.shape[0]), (0, 0)))
    wr = pad_rows(jnp.concatenate([w_router, w_group], axis=1).T)
    wr_hi = wr.astype(BF16)
    wr_lo = (wr - wr_hi.astype(F32)).astype(BF16)
    br = pad_rows(jnp.concatenate([b_router, b_group])[:, None].astype(F32))
    h_mid, xn, ri, rf, counts = _router(y, h2, w_out.astype(BF16), row(moe_norm), wr_hi, wr_lo, br)

    blk = EXPERT_BLOCK
    counts = counts[:, 0]
    padded = (counts + blk - 1) // blk * blk
    pad_end = jnp.cumsum(padded)
    pad_start = pad_end - padded
    n_blocks = (2 * T) // blk + N_EXPERTS
    block_start = jnp.arange(n_blocks, dtype=jnp.int32) * blk
    block_expert = jnp.minimum(jnp.sum(pad_end[None, :] <= block_start[:, None], axis=1),
                               N_EXPERTS - 1).astype(jnp.int32)
    n_used = (pad_end[-1:] // blk).astype(jnp.int32)
    dest = ri[:, 2:4, :]
    for e in range(N_EXPERTS):
        dest = dest + jnp.where(ri[:, 0:2, :] == e, pad_start[e], 0)
    dest = dest.astype(jnp.int32).reshape(-1)

    x_buf = _dispatch(dest, pad_end.astype(jnp.int32), n_used, xn, n_blocks * blk)
    y_buf = _experts(block_expert, n_used, x_buf, w1, w3, w2)

    return _final(dest, h_mid, rf, p2, w_ple_up.astype(BF16), row(ple_norm), row(ple_gate_norm),
                  w_ple_gate.astype(BF16), row(out_norm), y_buf)


def kernel(x, p, positions, attn_norm, w_in, conv_w, conv_b, b_igate, b_fgate, ret_gn, ml_gn,
           w_out, moe_norm, w_group, b_group, w_router, b_router, w1, w3, w2, w_ple_up, ple_norm,
           ple_gate_norm, w_ple_gate, final_norm):
    B, S, D = x.shape
    depth = p.shape[0]
    assert depth == 1, "the final RMSNorm is fused into the layer's last kernel"
    out = _layer(x.reshape(B * S, D), p[0].reshape(B * S, -1), positions, B, S,
                 attn_norm[0], w_in[0], conv_w[0], conv_b[0], b_igate[0], b_fgate[0], ret_gn[0],
                 ml_gn[0], w_out[0], moe_norm[0], w_group[0], b_group[0], w_router[0],
                 b_router[0], w1[0], w3[0], w2[0], w_ple_up[0], ple_norm[0], ple_gate_norm[0],
                 w_ple_gate[0], final_norm)
    return out.reshape(B, S, D)
```

```python
import functools

import jax
import jax.numpy as jnp
from jax import lax
from jax.experimental import pallas as pl
from jax.experimental.pallas import tpu as pltpu

F32 = jnp.float32
BF16 = jnp.bfloat16

RET_HEADS = 8
ML_HEADS = 4
CHUNK = 128
CONV_W = 4
ROPE_BASE = 10000.0
N_GROUPS = 4
EXPERTS_PER_GROUP = 8
N_EXPERTS = N_GROUPS * EXPERTS_PER_GROUP
EPS = 1e-6

LANES = 128
SUBLANES = 8
VMEM_LIMIT = 56 * 1024 * 1024
EXPERT_BLOCK = 512
GROUP_ROW0 = N_EXPERTS
ROUTER_ROWS = 64
ROUTE_TILE = 512
ROUTE_ROWS = 8
GATE_ROWS = 2 * SUBLANES


def _rms(x, g):
    return x * lax.rsqrt(jnp.mean(x * x, axis=-1, keepdims=True) + EPS) * g


def _dot(a, b):
    return jnp.dot(a, b, preferred_element_type=F32)


def _dot_nt(a, b):
    return lax.dot_general(a, b, (((1,), (1,)), ((), ())), preferred_element_type=F32)


def _split3(x):
    hi = x.astype(BF16)
    r1 = x - hi.astype(F32)
    mid = r1.astype(BF16)
    lo = (r1 - mid.astype(F32)).astype(BF16)
    return hi, mid, lo


def _rope_kernel(pos_ref, freq_ref, sign_ref, cos_ref, sin_ref, *, half):
    ang = pos_ref[...].astype(F32) * freq_ref[...]
    cos_c = jnp.cos(ang)
    sin_c = jnp.sin(ang)
    group = lax.broadcasted_iota(jnp.int32, ang.shape, 1) // half
    per_row = LANES // half
    for s in range(per_row):
        def spread(t):
            out = t
            for g in range(per_row):
                if g != s:
                    out = jnp.where(group == g, pltpu.roll(t, (half * (g - s)) % LANES, 1), out)
            return out
        cos_ref[s] = spread(cos_c)
        sin_ref[s] = spread(sin_c) * sign_ref[...]


def _rope_tables(positions, dh):
    half = dh // 2
    per_row = LANES // half
    T = positions.size
    rows = T // per_row
    tr = min(rows, 1024)
    freqs = ROPE_BASE ** (-jnp.arange(half, dtype=F32) / half)
    pos_c = jnp.repeat(positions.reshape(per_row, rows).T, half, axis=1)
    sign = jnp.tile(jnp.concatenate([-jnp.ones((half,), F32), jnp.ones((half,), F32)]),
                    LANES // dh)[None, :]
    const = pl.BlockSpec((1, LANES), lambda i: (0, 0))
    out = pl.BlockSpec((per_row, tr, LANES), lambda i: (0, i, 0))
    cos_t, sin_t = pl.pallas_call(
        functools.partial(_rope_kernel, half=half),
        grid=(rows // tr,),
        in_specs=[pl.BlockSpec((tr, LANES), lambda i: (i, 0)), const, const],
        out_specs=[out, out],
        out_shape=[jax.ShapeDtypeStruct((per_row, rows, LANES), F32)] * 2,
        name="rope",
    )(pos_c, jnp.tile(freqs, per_row)[None, :], sign)
    return cos_t.reshape(T, LANES), sin_t.reshape(T, LANES)


def _inproj_kernel(x_ref, g_ref, wm_ref, wgt_ref, gb_ref, cos_ref, sin_ref, ws_ref, convw_ref,
                   convb_ref, rq_ref, rk_ref, rkwt_ref, rv_ref, rg_ref, mq_ref, mk_ref, mvt_ref,
                   mo_ref, gt_ref, carry, *, ret_w, ml_w):
    tm = x_ref.shape[0]
    ret_dh = LANES // 2

    @pl.when(pl.program_id(1) == 0)
    def _():
        carry[...] = jnp.zeros_like(carry)

    xn = _rms(x_ref[...], g_ref[...]).astype(BF16)
    proj = lambda o, w: _dot(xn, wm_ref[:, o:o + w])
    tiles = lambda w: [slice(t, t + LANES) for t in range(0, w, LANES)]

    lane = lax.broadcasted_iota(jnp.int32, (tm, LANES), 1)
    first_half = (lane % ret_dh) < (ret_dh // 2)
    cos_t = cos_ref[...]
    sin_t = sin_ref[...]

    def rot(t):
        swapped = jnp.where(first_half, pltpu.roll(t, LANES - ret_dh // 2, 1),
                            pltpu.roll(t, ret_dh // 2, 1))
        return t * cos_t + swapped * sin_t

    rq = proj(0, ret_w)
    for ps in tiles(ret_w):
        rq_ref[:, ps] = rot(rq[:, ps]).astype(BF16)
    rk = proj(ret_w, ret_w)
    for ps in tiles(ret_w):
        k = rot(rk[:, ps]) * (ret_dh ** -0.5)
        rk_ref[:, ps] = k.astype(BF16)
        rkwt_ref[ps, :] = (k * ws_ref[:, ps]).T.astype(BF16)
    rv_ref[...] = proj(2 * ret_w, ret_w).astype(BF16)
    g = proj(3 * ret_w, ret_w)
    rg_ref[...] = g * jax.nn.sigmoid(g)

    o_mq = 4 * ret_w
    row8 = lax.broadcasted_iota(jnp.int32, (SUBLANES, ml_w), 0)
    for idx, dst in enumerate((mq_ref, mk_ref)):
        cs = slice(idx * ml_w, (idx + 1) * ml_w)
        xq = proj(o_mq + idx * ml_w, ml_w)
        prev8 = carry[:, cs]
        carry[:, cs] = xq[tm - SUBLANES:tm]
        acc = xq * convw_ref[CONV_W - 1:CONV_W, cs] + convb_ref[:, cs]
        for s in range(1, CONV_W):
            rolled = pltpu.roll(xq, s, 0)
            head = jnp.where(row8 < s, pltpu.roll(prev8, s, 0), rolled[0:SUBLANES])
            shifted = jnp.concatenate([head, rolled[SUBLANES:]], axis=0)
            acc = acc + shifted * convw_ref[CONV_W - 1 - s:CONV_W - s, cs]
        act = acc * jax.nn.sigmoid(acc)
        if idx == 1:
            act = act * (LANES ** -0.5)
        dst[...] = act.astype(BF16)
    mv = proj(o_mq + 2 * ml_w, ml_w)
    for hs in tiles(ml_w):
        mvt_ref[hs, :] = mv[:, hs].T
    mo_ref[...] = jax.nn.sigmoid(proj(o_mq + 3 * ml_w, ml_w))
    gt_ref[...] = _dot_nt(wgt_ref[...], xn) + gb_ref[...]


def _inproj(x3, g, w_main, w_gate_t, gate_bias, cos_t, sin_t, ws, conv_w, conv_b, ret_w, ml_w,
            tm=256):
    B, S, D = x3.shape
    n_main = w_main.shape[1]
    const = lambda b, i: (0, 0)
    tok = lambda w: pl.BlockSpec((None, tm, w), lambda b, i: (b, i, 0))
    tok_t = lambda r: pl.BlockSpec((None, r, tm), lambda b, i: (b, 0, i))
    act = lambda w, dt: jax.ShapeDtypeStruct((B, S, w), dt)
    act_t = lambda r, dt: jax.ShapeDtypeStruct((B, r, S), dt)
    return pl.pallas_call(
        functools.partial(_inproj_kernel, ret_w=ret_w, ml_w=ml_w),
        grid=(B, S // tm),
        in_specs=[
            tok(D),
            pl.BlockSpec((1, D), const),
            pl.BlockSpec((D, n_main), const),
            pl.BlockSpec((GATE_ROWS, D), const),
            pl.BlockSpec((GATE_ROWS, 1), const),
            tok(LANES),
            tok(LANES),
            pl.BlockSpec((tm, ret_w), const),
            pl.BlockSpec((CONV_W, 2 * ml_w), const),
            pl.BlockSpec((1, 2 * ml_w), const),
        ],
        out_specs=[tok(ret_w), tok(ret_w), tok_t(ret_w), tok(ret_w), tok(ret_w),
                   tok(ml_w), tok(ml_w), tok_t(ml_w), tok(ml_w), tok_t(GATE_ROWS)],
        out_shape=[
            act(ret_w, BF16),
            act(ret_w, BF16),
            act_t(ret_w, BF16),
            act(ret_w, BF16),
            act(ret_w, F32),
            act(ml_w, BF16),
            act(ml_w, BF16),
            act_t(ml_w, F32),
            act(ml_w, F32),
            act_t(GATE_ROWS, F32),
        ],
        scratch_shapes=[pltpu.VMEM((SUBLANES, 2 * ml_w), F32)],
        compiler_params=pltpu.CompilerParams(
            dimension_semantics=("arbitrary", "arbitrary"), vmem_limit_bytes=VMEM_LIMIT),
        name="inproj",
    )(x3, g, w_main, w_gate_t, gate_bias, cos_t, sin_t, ws, conv_w, conv_b)


def _mixer_kernel(rq_ref, rk_ref, rkwt_ref, rv_ref, rg_ref, mq_ref, mk_ref, mvt_ref, mo_ref,
                  gt_ref, decay_ref, wq_ref, cd_ref, retgn_ref, mlgn_ref,
                  y_ref, r_state, s_state, n_state, m_state, *, ret_w, ml_w):
    C = rq_ref.shape[0]
    n_pairs = ret_w // LANES
    ml_heads = ml_w // LANES
    ret_dh = LANES // 2

    @pl.when(pl.program_id(1) == 0)
    def _():
        r_state[...] = jnp.zeros_like(r_state)
        s_state[...] = jnp.zeros_like(s_state)
        n_state[...] = jnp.zeros_like(n_state)
        m_state[...] = jnp.zeros_like(m_state)

    lane = lax.broadcasted_iota(jnp.int32, (C, LANES), 1)
    row = lax.broadcasted_iota(jnp.int32, (C, LANES), 0)
    assert C == LANES
    lo = lane < ret_dh
    blockdiag = (row < ret_dh) == lo
    lo_b = jnp.where(lo, 1.0, 0.0).astype(BF16)
    hi_b = jnp.where(lo, 0.0, 1.0).astype(BF16)
    pairs = [slice(p * LANES, (p + 1) * LANES) for p in range(n_pairs)]
    heads = [slice(h * LANES, (h + 1) * LANES) for h in range(ml_heads)]


    row8 = lax.broadcasted_iota(jnp.int32, (SUBLANES, C), 0)
    lane8 = lax.broadcasted_iota(jnp.int32, (SUBLANES, C), 1)
    live = row8 < ml_heads
    ig = jnp.where(live, gt_ref[0:SUBLANES, :], 0.0)
    f_pre = jnp.where(live, gt_ref[SUBLANES:2 * SUBLANES, :], 30.0)
    logf = jax.nn.log_sigmoid(f_pre)
    triu = (row <= lane).astype(BF16)
    l_hi, l_mid, l_lo = _split3(logf)
    b = _dot(l_hi, triu) + _dot(l_mid, triu) + _dot(l_lo, triu)
    beta = ig - b
    cm = beta
    shift = 1
    while shift < C:
        cm = jnp.maximum(cm, jnp.where(lane8 >= shift, pltpu.roll(cm, shift, 1), -jnp.inf))
        shift *= 2
    m_prev = m_state[...]
    mx = jnp.maximum(cm, m_prev)
    mx_last = jnp.broadcast_to(mx[:, C - 1:C], (SUBLANES, C))
    w_inter = jnp.exp(m_prev - mx)
    e_negm = jnp.exp(-(b + mx))
    w_state = jnp.exp(beta - mx_last)
    dec = jnp.exp(m_prev - mx_last)
    m_state[...] = jnp.where(live, jnp.broadcast_to(b[:, C - 1:C], (SUBLANES, C)) + mx_last, 0.0)
    w_state_b = w_state.astype(BF16)
    beta_t = jnp.concatenate([beta, jnp.zeros((LANES - SUBLANES, C), F32)], axis=0).T
    before = row <= lane

    hrow = lambda t, h: t[h:h + 1, :]

    rq = [rq_ref[:, ps] for ps in pairs]
    rk = [rk_ref[:, ps] for ps in pairs]
    rv = [rv_ref[:, ps] for ps in pairs]
    r_prev = [r_state[p] for p in range(n_pairs)]
    s_a = [_dot_nt(rq[p] * lo_b, rk[p]) for p in range(n_pairs)]
    s_b = [_dot_nt(rq[p] * hi_b, rk[p]) for p in range(n_pairs)]
    r_read = [_dot(rq[p], r_prev[p].astype(BF16)) for p in range(n_pairs)]
    r_new = [_dot(rkwt_ref[pairs[p], :], rv[p]) for p in range(n_pairs)]
    mq = [mq_ref[:, hs] for hs in heads]
    mk = [mk_ref[:, hs] for hs in heads]
    mv_t = [mvt_ref[hs, :] for hs in heads]
    st_prev = [s_state[h] for h in range(ml_heads)]
    n_prev = [n_state[h] for h in range(ml_heads)]
    a_t = [_dot_nt(mk[h], mq[h]) for h in range(ml_heads)]
    s_read = [_dot_nt(st_prev[h].astype(BF16), mq[h]) for h in range(ml_heads)]
    qn = [_dot_nt(n_prev[h].astype(BF16), mq[h])[0:1, :] for h in range(ml_heads)]
    s_new = [_dot((mv_t[h] * hrow(w_state, h)).astype(BF16), mk[h]) for h in range(ml_heads)]
    n_new = [_dot(w_state_b, mk[h])[h:h + 1, :] for h in range(ml_heads)]
    for p in range(n_pairs):
        r_state[p] = cd_ref[p] * r_prev[p] + jnp.where(blockdiag, r_new[p], 0.0)
    for h in range(ml_heads):
        s_state[h] = hrow(dec, h) * st_prev[h] + s_new[h]
        n_state[h] = hrow(dec, h) * n_prev[h] + n_new[h]

    s_a = [(s_a[p] * decay_ref[2 * p]).astype(BF16) for p in range(n_pairs)]
    s_b = [(s_b[p] * decay_ref[2 * p + 1]).astype(BF16) for p in range(n_pairs)]
    p_t = [jnp.exp(jnp.where(before, beta_t[:, h:h + 1] - hrow(mx, h), -jnp.inf)) * a_t[h]
           for h in range(ml_heads)]

    o = [_dot(s_a[p], rv[p] * lo_b) + _dot(s_b[p], rv[p] * hi_b) + r_read[p] * wq_ref[p]
         for p in range(n_pairs)]
    num_t = [_dot(mv_t[h].astype(BF16), p_t[h].astype(BF16)) + hrow(w_inter, h) * s_read[h]
             for h in range(ml_heads)]

    for p in range(n_pairs):
        sq = o[p] * o[p]
        ms_a = jnp.sum(jnp.where(lo, sq, 0.0), axis=-1, keepdims=True)
        ms_b = jnp.sum(jnp.where(lo, 0.0, sq), axis=-1, keepdims=True)
        ms = jnp.where(lo, ms_a, ms_b) * (1.0 / ret_dh)
        r = o[p] * lax.rsqrt(ms + EPS) * retgn_ref[:, pairs[p]]
        y_ref[:, pairs[p]] = (rg_ref[:, pairs[p]] * r).astype(y_ref.dtype)
    for h in range(ml_heads):
        den = jnp.sum(p_t[h], axis=0, keepdims=True) + hrow(w_inter, h) * qn[h]
        hh = (num_t[h] * (1.0 / jnp.maximum(jnp.abs(den), hrow(e_negm, h)))).T
        hm = _rms(mo_ref[:, heads[h]] * hh, mlgn_ref[:, heads[h]])
        y_ref[:, ret_w + h * LANES:ret_w + (h + 1) * LANES] = hm.astype(y_ref.dtype)


def _ret_tables(C):
    H = RET_HEADS
    dh = LANES // 2
    log_gamma = jnp.log1p(-(2.0 ** (-5.0 - jnp.arange(H, dtype=F32))))
    idx = jnp.arange(C, dtype=F32)
    rel = idx[:, None] - idx[None, :]
    causal = rel >= 0
    decay = jnp.where(causal, jnp.exp(log_gamma[:, None, None] * jnp.where(causal, rel, 0.0)), 0.0)
    w_state = jnp.exp(log_gamma[:, None] * (C - 1 - idx))
    w_query = jnp.exp(log_gamma[:, None] * (idx + 1.0))
    chunk_decay = jnp.exp(log_gamma * C)
    pair = lambda t: jnp.repeat(t.reshape(H // 2, 2, C).transpose(0, 2, 1), dh, axis=2)
    cd = jnp.repeat(chunk_decay.reshape(H // 2, 2), dh, axis=1)
    cd = jnp.broadcast_to(cd[:, :, None], (H // 2, LANES, LANES))
    ws = pair(w_state).transpose(1, 0, 2).reshape(C, (H // 2) * LANES)
    return decay, ws, pair(w_query), cd


def _mixer(ops, decay, wq, cd, ret_gn, ml_gn, B, S, ret_w, ml_w):
    C = CHUNK
    N = S // C
    n_pairs = ret_w // LANES
    ml_heads = ml_w // LANES
    c2 = lambda b, n: (0, 0)
    c3 = lambda b, n: (0, 0, 0)
    tok = lambda w: pl.BlockSpec((None, C, w), lambda b, n: (b, n, 0))
    tok_t = lambda r: pl.BlockSpec((None, r, C), lambda b, n: (b, 0, n))
    y = pl.pallas_call(
        functools.partial(_mixer_kernel, ret_w=ret_w, ml_w=ml_w),
        grid=(B, N),
        in_specs=[
            tok(ret_w), tok(ret_w), tok_t(ret_w), tok(ret_w), tok(ret_w),
            tok(ml_w), tok(ml_w), tok_t(ml_w), tok(ml_w), tok_t(GATE_ROWS),
            pl.BlockSpec((RET_HEADS, C, C), c3),
            pl.BlockSpec((n_pairs, C, LANES), c3),
            pl.BlockSpec((n_pairs, LANES, LANES), c3),
            pl.BlockSpec((1, ret_w), c2),
            pl.BlockSpec((1, ml_w), c2),
        ],
        out_specs=pl.BlockSpec((None, C, ret_w + ml_w), lambda b, n: (b, n, 0)),
        out_shape=jax.ShapeDtypeStruct((B, S, ret_w + ml_w), BF16),
        scratch_shapes=[
            pltpu.VMEM((n_pairs, LANES, LANES), F32),
            pltpu.VMEM((ml_heads, LANES, LANES), F32),
            pltpu.VMEM((ml_heads, SUBLANES, LANES), F32),
            pltpu.VMEM((SUBLANES, LANES), F32),
        ],
        compiler_params=pltpu.CompilerParams(
            dimension_semantics=("arbitrary", "arbitrary"), vmem_limit_bytes=VMEM_LIMIT),
        name="mixer",
    )(*ops, decay, wq, cd, ret_gn.reshape(1, -1), ml_gn.reshape(1, -1))
    return y.reshape(B * S, ret_w + ml_w)


def _router_kernel(y_ref, x_ref, wo_ref, g_ref, wr_hi_ref, wr_lo_ref, br_ref,
                   h_ref, xn_ref, ri_ref, rf_ref, cnt_ref, run_cnt):
    tm = y_ref.shape[0]

    @pl.when(pl.program_id(0) == 0)
    def _():
        run_cnt[...] = jnp.zeros_like(run_cnt)

    h = x_ref[...] + _dot(y_ref[...], wo_ref[...])
    h_ref[...] = h
    xn = _rms(h, g_ref[...])
    xn_ref[...] = xn
    x_hi = xn.astype(BF16)
    x_lo = (xn - x_hi.astype(F32)).astype(BF16)
    logits = (_dot_nt(wr_hi_ref[...], x_hi) + _dot_nt(wr_hi_ref[...], x_lo)
              + _dot_nt(wr_lo_ref[...], x_hi) + br_ref[...])
    big = jnp.int32(LANES)
    neg = -jnp.inf
    gl = logits[GROUP_ROW0:GROUP_ROW0 + 8]
    grow = lax.broadcasted_iota(jnp.int32, gl.shape, 0)
    is_g = grow < N_GROUPS
    gl = jnp.where(is_g, gl, neg)
    gmax = jnp.max(gl, axis=0, keepdims=True)
    gsum = jnp.sum(jnp.where(is_g, jnp.exp(gl - gmax), 0.0), axis=0, keepdims=True)
    p_g = 1.0 / gsum
    g_sel = jnp.min(jnp.where(is_g & (gl == gmax), grow, big), axis=0, keepdims=True)
    el = logits[0:N_EXPERTS]
    erow = lax.broadcasted_iota(jnp.int32, el.shape, 0)
    in_grp = (erow // EXPERTS_PER_GROUP) == g_sel
    el = jnp.where(in_grp, el, neg)
    emax = jnp.max(el, axis=0, keepdims=True)
    eexp = jnp.where(in_grp, jnp.exp(el - emax), 0.0)
    prob = eexp / jnp.sum(eexp, axis=0, keepdims=True)
    pm1 = jnp.where(in_grp, prob, -1.0)
    p1 = jnp.max(pm1, axis=0, keepdims=True)
    i1 = jnp.min(jnp.where(pm1 == p1, erow, big), axis=0, keepdims=True)
    pm2 = jnp.where(erow == i1, -1.0, pm1)
    p2 = jnp.max(pm2, axis=0, keepdims=True)
    i2 = jnp.min(jnp.where(pm2 == p2, erow, big), axis=0, keepdims=True)
    denom = p1 + p2
    g1 = p_g * p1 / denom
    g2 = p_g * p2 / denom

    sel1 = erow == i1
    sel2 = erow == i2
    onehot = (sel1 | sel2).astype(BF16)
    r_i = lax.broadcasted_iota(jnp.int32, (tm, tm), 0)
    c_i = lax.broadcasted_iota(jnp.int32, (tm, tm), 1)
    tri = (r_i < c_i).astype(BF16)
    prefix = _dot(onehot, tri) + run_cnt[:, 0:1]
    rank1 = jnp.sum(jnp.where(sel1, prefix, 0.0), axis=0, keepdims=True).astype(jnp.int32)
    rank2 = jnp.sum(jnp.where(sel2, prefix, 0.0), axis=0, keepdims=True).astype(jnp.int32)
    new_cnt = run_cnt[...] + jnp.sum(onehot.astype(F32), axis=1, keepdims=True)
    run_cnt[...] = new_cnt
    cnt_ref[...] = new_cnt.astype(jnp.int32)

    rrow = lax.broadcasted_iota(jnp.int32, (ROUTE_ROWS, tm), 0)
    ri_ref[...] = jnp.where(rrow == 0, i1, jnp.where(rrow == 1, i2,
                            jnp.where(rrow == 2, rank1, jnp.where(rrow == 3, rank2, 0))))
    lrow = lax.broadcasted_iota(jnp.int32, (LANES, tm), 0)
    rf_ref[...] = jnp.where(lrow == 0, g1, jnp.where(lrow == 1, g2, 0.0)).T


def _router(y, x2, w_out, g, wr_hi, wr_lo, br, tm=ROUTE_TILE):
    T, D = x2.shape
    const = lambda i: (0, 0)
    tile = lambda i: (i, 0)
    return pl.pallas_call(
        _router_kernel,
        grid=(T // tm,),
        in_specs=[
            pl.BlockSpec((tm, y.shape[1]), tile),
            pl.BlockSpec((tm, D), tile),
            pl.BlockSpec(w_out.shape, const),
            pl.BlockSpec((1, D), const),
            pl.BlockSpec((ROUTER_ROWS, D), const),
            pl.BlockSpec((ROUTER_ROWS, D), const),
            pl.BlockSpec((ROUTER_ROWS, 1), const),
        ],
        out_specs=[
            pl.BlockSpec((tm, D), tile),
            pl.BlockSpec((tm, D), tile),
            pl.BlockSpec((None, ROUTE_ROWS, tm), lambda i: (i, 0, 0)),
            pl.BlockSpec((tm, LANES), tile),
            pl.BlockSpec((N_EXPERTS, LANES), const),
        ],
        out_shape=[
            jax.ShapeDtypeStruct((T, D), F32),
            jax.ShapeDtypeStruct((T, D), F32),
            jax.ShapeDtypeStruct((T // tm, ROUTE_ROWS, tm), jnp.int32),
            jax.ShapeDtypeStruct((T, LANES), F32),
            jax.ShapeDtypeStruct((N_EXPERTS, LANES), jnp.int32),
        ],
        scratch_shapes=[pltpu.VMEM((N_EXPERTS, LANES), F32)],
        compiler_params=pltpu.CompilerParams(
            dimension_semantics=("arbitrary",), vmem_limit_bytes=VMEM_LIMIT),
        name="router",
    )(y, x2, w_out, g, wr_hi, wr_lo, br)


def _slot_base(tok0):
    return (tok0 // ROUTE_TILE) * (2 * ROUTE_TILE) + tok0 % ROUTE_TILE


def _route_spans(tm):
    assert tm % ROUTE_TILE == 0 or ROUTE_TILE % tm == 0
    span = min(tm, ROUTE_TILE)
    return [(t0, span) for t0 in range(0, tm, span)]


def _for_each_pad_block(pend_ref, nu_ref, n_blocks, fn):
    blk = EXPERT_BLOCK
    for e in range(N_EXPERTS):
        prev_end = 0 if e == 0 else pend_ref[e - 1]

        @pl.when(pend_ref[e] > prev_end)
        def _():
            fn(pl.multiple_of(pend_ref[e] - blk, blk))

    def tail(j, c):
        fn(pl.multiple_of(j * blk, blk))
        return c

    lax.fori_loop(nu_ref[0], n_blocks, tail, 0)


def _dispatch_kernel(dest_ref, pend_ref, nu_ref, xn_ref, buf_ref, zeros, sem, zsem):
    tm = xn_ref.shape[0] * SUBLANES
    blk = EXPERT_BLOCK
    i = pl.program_id(0)

    @pl.when(i == 0)
    def _():
        zeros[...] = jnp.zeros_like(zeros)
        zcopy = lambda row: pltpu.make_async_copy(zeros, buf_ref.at[pl.ds(row, blk)], zsem)
        n_blocks = buf_ref.shape[0] // blk
        _for_each_pad_block(pend_ref, nu_ref, n_blocks, lambda row: zcopy(row).start())
        _for_each_pad_block(pend_ref, nu_ref, n_blocks, lambda row: zcopy(row).wait())

    for t0, span in _route_spans(tm):
        slot0 = _slot_base(i * tm + t0)

        def issue(g, c, t0=t0, slot0=slot0):
            for j in range(SUBLANES):
                for k in range(2):
                    d = dest_ref[slot0 + k * ROUTE_TILE + g * SUBLANES + j]
                    pltpu.make_async_copy(xn_ref.at[t0 // SUBLANES + g, pl.ds(j, 1)],
                                          buf_ref.at[pl.ds(d, 1)], sem).start(priority=k)
            return c

        lax.fori_loop(0, span // SUBLANES, issue, 0)
    for k in range(2):
        pltpu.make_async_copy(buf_ref.at[pl.ds(tm, tm)], buf_ref.at[pl.ds(0, tm)], sem).wait()


def _dispatch(dest, pad_end, n_used, xn, n_rows, tm=ROUTE_TILE):
    T, D = xn.shape
    xn = xn.reshape(T // SUBLANES, SUBLANES, D)
    return pl.pallas_call(
        _dispatch_kernel,
        grid_spec=pltpu.PrefetchScalarGridSpec(
            num_scalar_prefetch=3,
            grid=(T // tm,),
            in_specs=[pl.BlockSpec((tm // SUBLANES, SUBLANES, D), lambda i, *_: (i, 0, 0))],
            out_specs=pl.BlockSpec(memory_space=pl.ANY),
            scratch_shapes=[
                pltpu.VMEM((EXPERT_BLOCK, D), xn.dtype),
                pltpu.SemaphoreType.DMA,
                pltpu.SemaphoreType.DMA,
            ],
        ),
        out_shape=jax.ShapeDtypeStruct((n_rows, D), xn.dtype),
        compiler_params=pltpu.CompilerParams(
            dimension_semantics=("arbitrary",), vmem_limit_bytes=VMEM_LIMIT),
        name="dispatch",
    )(dest, pad_end, n_used, xn)


def _expert_kernel(be_ref, nu_ref, x_ref, w1_ref, w3_ref, w2_ref, y_ref, w1b, w3b, w2b):
    j = pl.program_id(0)
    used = j < nu_ref[0]
    prev_expert = be_ref[jnp.maximum(j, 1) - 1]

    @pl.when(used & ((j == 0) | (be_ref[j] != prev_expert)))
    def _():
        w1b[...] = w1_ref[...].astype(BF16)
        w3b[...] = w3_ref[...].astype(BF16)
        w2b[...] = w2_ref[...].astype(BF16)

    @pl.when(used)
    def _():
        x = x_ref[...].astype(BF16)
        a = _dot(x, w1b[...])
        hmid = a * jax.nn.sigmoid(a) * _dot(x, w3b[...])
        y_ref[...] = _dot(hmid.astype(BF16), w2b[...])

    @pl.when(jnp.logical_not(used))
    def _():
        y_ref[...] = jnp.zeros_like(y_ref)


def _experts(block_expert, n_used, x_buf, w1, w3, w2):
    P, D = x_buf.shape
    blk = EXPERT_BLOCK
    d_exp = w2.shape[1]
    clamp = lambda j, nu: jnp.minimum(j, nu[0] - 1)
    w_idx = lambda j, be, nu: (be[clamp(j, nu)], 0, 0)
    return pl.pallas_call(
        _expert_kernel,
        grid_spec=pltpu.PrefetchScalarGridSpec(
            num_scalar_prefetch=2,
            grid=(P // blk,),
            in_specs=[
                pl.BlockSpec((blk, D), lambda j, be, nu: (clamp(j, nu), 0)),
                pl.BlockSpec((None, D, d_exp), w_idx),
                pl.BlockSpec((None, D, d_exp), w_idx),
                pl.BlockSpec((None, d_exp, D), w_idx),
            ],
            out_specs=pl.BlockSpec((blk, D), lambda j, be, nu: (j, 0)),
            scratch_shapes=[
                pltpu.VMEM((D, d_exp), BF16),
                pltpu.VMEM((D, d_exp), BF16),
                pltpu.VMEM((d_exp, D), BF16),
            ],
        ),
        out_shape=jax.ShapeDtypeStruct((P, D), F32),
        compiler_params=pltpu.CompilerParams(
            dimension_semantics=("arbitrary",), vmem_limit_bytes=VMEM_LIMIT),
        name="experts",
    )(block_expert, n_used, x_buf, w1, w3, w2)


def _final_kernel(dest_ref, h_ref, rf_ref, p_ref, wup_ref, gple_ref, ggate_ref, wgate_ref,
                  gfin_ref, ybuf_ref, out_ref, ya0, ya1, yb0, yb1, sem_a, sem_b, *, tm):
    s = pl.program_id(0)
    D = h_ref.shape[1]
    assert ROUTE_TILE % tm == 0

    def gather(tile, bufs, sem, straight=False):
        slot0 = _slot_base(tile * tm)

        def issue(g, c):
            for j in range(SUBLANES):
                for k in range(2):
                    d = dest_ref[slot0 + k * ROUTE_TILE + g * SUBLANES + j]
                    pltpu.make_async_copy(ybuf_ref.at[pl.ds(d, 1)],
                                          bufs[k].at[g, pl.ds(j, 1)], sem).start(priority=k)
            return c

        if straight:
            for g in range(tm // SUBLANES):
                issue(g, 0)
        else:
            lax.fori_loop(0, tm // SUBLANES, issue, 0)

    def wait(sem):
        for k in range(2):
            pltpu.make_async_copy(ybuf_ref.at[pl.ds(0, tm)], ybuf_ref.at[pl.ds(tm, tm)], sem).wait()

    def compute(rows, bufs):
        e = _rms(_dot(p_ref[rows, :].astype(BF16), wup_ref[...]), gple_ref[...])
        rf = rf_ref[rows, :]
        h = (h_ref[rows, :] + rf[:, 0:1] * bufs[0][...].reshape(tm, D)
             + rf[:, 1:2] * bufs[1][...].reshape(tm, D))
        gate = jax.nn.sigmoid(_dot(_rms(h, ggate_ref[...]).astype(BF16), wgate_ref[...]))
        h = h + gate * e
        out_ref[rows, :] = _rms(h, gfin_ref[...])

    last = pl.num_programs(0) - 1

    @pl.when(s == 0)
    def _():
        gather(0, (ya0, ya1), sem_a)

    wait(sem_a)
    gather(2 * s + 1, (yb0, yb1), sem_b, straight=True)
    compute(slice(0, tm), (ya0, ya1))
    wait(sem_b)
    gather(2 * jnp.minimum(s + 1, last), (ya0, ya1), sem_a, straight=True)
    compute(slice(tm, 2 * tm), (yb0, yb1))

    @pl.when(s == last)
    def _():
        wait(sem_a)


def _final(dest, h, rf, p2, w_up, g_ple, g_gate, w_gate, g_fin, y_buf, tm=256):
    T, D = h.shape
    const = lambda i, d: (0, 0)
    tile = lambda i, d: (i, 0)
    row_buf = pltpu.VMEM((tm // SUBLANES, SUBLANES, D), F32)
    return pl.pallas_call(
        functools.partial(_final_kernel, tm=tm),
        grid_spec=pltpu.PrefetchScalarGridSpec(
            num_scalar_prefetch=1,
            grid=(T // (2 * tm),),
            in_specs=[
                pl.BlockSpec((2 * tm, D), tile),
                pl.BlockSpec((2 * tm, LANES), tile),
                pl.BlockSpec((2 * tm, p2.shape[1]), tile),
                pl.BlockSpec(w_up.shape, const),
                pl.BlockSpec((1, D), const),
                pl.BlockSpec((1, D), const),
                pl.BlockSpec(w_gate.shape, const),
                pl.BlockSpec((1, D), const),
                pl.BlockSpec(memory_space=pl.ANY),
            ],
            out_specs=pl.BlockSpec((2 * tm, D), tile),
            scratch_shapes=[row_buf, row_buf, row_buf, row_buf,
                            pltpu.SemaphoreType.DMA, pltpu.SemaphoreType.DMA],
        ),
        out_shape=jax.ShapeDtypeStruct((T, D), F32),
        compiler_params=pltpu.CompilerParams(
            dimension_semantics=("arbitrary",), vmem_limit_bytes=VMEM_LIMIT),
        name="final",
    )(dest, h, rf, p2, w_up, g_ple, g_gate, w_gate, g_fin, y_buf)


def _layer(h2, p2, positions, B, S, attn_norm, w_in, conv_w, conv_b, b_igate, b_fgate, ret_gn,
           ml_gn, w_out, moe_norm, w_group, b_group, w_router, b_router, w1, w3, w2, w_ple_up,
           ple_norm, ple_gate_norm, w_ple_gate, out_norm):
    T, D = h2.shape
    ret_w = ret_gn.shape[0]
    ml_w = ml_gn.shape[0]
    n_main = 4 * ret_w + 4 * ml_w
    row = lambda v: v.reshape(1, -1).astype(F32)
    pad_lanes = lambda a: jnp.pad(a, ((0, 0), (0, LANES - a.shape[1])))

    ml_heads = ml_w // LANES
    w_main = w_in[:, :n_main].astype(BF16)
    gate_rows = lambda a: jnp.pad(a, ((0, SUBLANES - ml_heads), (0, 0)))
    w_i, w_f = w_in[:, n_main:n_main + ml_heads].T, w_in[:, n_main + ml_heads:].T
    w_gate_t = jnp.concatenate([gate_rows(w_i), gate_rows(w_f)]).astype(BF16)
    gate_bias = jnp.concatenate([gate_rows(b_igate[:, None]), gate_rows(b_fgate[:, None])])
    cos_t, sin_t = _rope_tables(positions, LANES // 2)
    decay, ws, wq, cd = _ret_tables(CHUNK)
    tm_in = 2 * CHUNK
    ops = _inproj(h2.reshape(B, S, D), row(attn_norm), w_main, w_gate_t, gate_bias.astype(F32),
                  cos_t.reshape(B, S, LANES), sin_t.reshape(B, S, LANES),
                  jnp.tile(ws, (tm_in // CHUNK, 1)), conv_w, conv_b.reshape(1, -1), ret_w, ml_w,
                  tm=tm_in)

    y = _mixer(ops, decay, wq, cd, ret_gn, ml_gn, B, S, ret_w, ml_w)

    pad_rows = lambda a: jnp.pad(a, ((0, ROUTER_ROWS - a.shape[0]), (0, 0)))
    wr = pad_rows(jnp.concatenate([w_router, w_group], axis=1).T)
    wr_hi = wr.astype(BF16)
    wr_lo = (wr - wr_hi.astype(F32)).astype(BF16)
    br = pad_rows(jnp.concatenate([b_router, b_group])[:, None].astype(F32))
    h_mid, xn, ri, rf, counts = _router(y, h2, w_out.astype(BF16), row(moe_norm), wr_hi, wr_lo, br)

    blk = EXPERT_BLOCK
    counts = counts[:, 0]
    padded = (counts + blk - 1) // blk * blk
    pad_end = jnp.cumsum(padded)
    pad_start = pad_end - padded
    n_blocks = (2 * T) // blk + N_EXPERTS
    block_start = jnp.arange(n_blocks, dtype=jnp.int32) * blk
    block_expert = jnp.minimum(jnp.sum(pad_end[None, :] <= block_start[:, None], axis=1),
                               N_EXPERTS - 1).astype(jnp.int32)
    n_used = (pad_end[-1:] // blk).astype(jnp.int32)
    dest = ri[:, 2:4, :]
    for e in range(N_EXPERTS):
        dest = dest + jnp.where(ri[:, 0:2, :] == e, pad_start[e], 0)
    dest = dest.astype(jnp.int32).reshape(-1)

    x_buf = _dispatch(dest, pad_end.astype(jnp.int32), n_used, xn, n_blocks * blk)
    y_buf = _experts(block_expert, n_used, x_buf, w1, w3, w2)

    return _final(dest, h_mid, rf, p2, w_ple_up.astype(BF16), row(ple_norm), row(ple_gate_norm),
                  w_ple_gate.astype(BF16), row(out_norm), y_buf)


def kernel(x, p, positions, attn_norm, w_in, conv_w, conv_b, b_igate, b_fgate, ret_gn, ml_gn,
           w_out, moe_norm, w_group, b_group, w_router, b_router, w1, w3, w2, w_ple_up, ple_norm,
           ple_gate_norm, w_ple_gate, final_norm):
    B, S, D = x.shape
    depth = p.shape[0]
    assert depth == 1, "the final RMSNorm is fused into the layer's last kernel"
    out = _layer(x.reshape(B * S, D), p[0].reshape(B * S, -1), positions, B, S,
                 attn_norm[0], w_in[0], conv_w[0], conv_b[0], b_igate[0], b_fgate[0], ret_gn[0],
                 ml_gn[0], w_out[0], moe_norm[0], w_group[0], b_group[0], w_router[0],
                 b_router[0], w1[0], w3[0], w2[0], w_ple_up[0], ple_norm[0], ple_gate_norm[0],
                 w_ple_gate[0], final_norm)
    return out.reshape(B, S, D)
```

```python
import functools

import jax
import jax.numpy as jnp
from jax import lax
from jax.experimental import pallas as pl
from jax.experimental.pallas import tpu as pltpu

F32 = jnp.float32
BF16 = jnp.bfloat16

RET_HEADS = 8
ML_HEADS = 4
CHUNK = 128
CONV_W = 4
ROPE_BASE = 10000.0
N_GROUPS = 4
EXPERTS_PER_GROUP = 8
N_EXPERTS = N_GROUPS * EXPERTS_PER_GROUP
EPS = 1e-6

LANES = 128
SUBLANES = 8
VMEM_LIMIT = 56 * 1024 * 1024
EXPERT_BLOCK = 512
GROUP_ROW0 = N_EXPERTS
ROUTER_ROWS = 64
ROUTE_TILE = 512
ROUTE_ROWS = 8
GATE_ROWS = 2 * SUBLANES
ROW_TILE = (SUBLANES, LANES)


def _rms(x, g):
    return x * lax.rsqrt(jnp.mean(x * x, axis=-1, keepdims=True) + EPS) * g


def _dot(a, b):
    return jnp.dot(a, b, preferred_element_type=F32)


def _dot_nt(a, b):
    return lax.dot_general(a, b, (((1,), (1,)), ((), ())), preferred_element_type=F32)


def _split3(x):
    hi = x.astype(BF16)
    r1 = x - hi.astype(F32)
    mid = r1.astype(BF16)
    lo = (r1 - mid.astype(F32)).astype(BF16)
    return hi, mid, lo


def _rope_kernel(pos_ref, freq_ref, sign_ref, cos_ref, sin_ref, *, half):
    ang = pos_ref[...].astype(F32) * freq_ref[...]
    cos_c = jnp.cos(ang)
    sin_c = jnp.sin(ang)
    group = lax.broadcasted_iota(jnp.int32, ang.shape, 1) // half
    per_row = LANES // half
    for s in range(per_row):
        def spread(t):
            out = t
            for g in range(per_row):
                if g != s:
                    out = jnp.where(group == g, pltpu.roll(t, (half * (g - s)) % LANES, 1), out)
            return out
        cos_ref[s] = spread(cos_c)
        sin_ref[s] = spread(sin_c) * sign_ref[...]


def _rope_tables(positions, dh):
    half = dh // 2
    per_row = LANES // half
    T = positions.size
    rows = T // per_row
    tr = min(rows, 1024)
    freqs = ROPE_BASE ** (-jnp.arange(half, dtype=F32) / half)
    pos_c = jnp.repeat(positions.reshape(per_row, rows).T, half, axis=1)
    sign = jnp.tile(jnp.concatenate([-jnp.ones((half,), F32), jnp.ones((half,), F32)]),
                    LANES // dh)[None, :]
    const = pl.BlockSpec((1, LANES), lambda i: (0, 0))
    out = pl.BlockSpec((per_row, tr, LANES), lambda i: (0, i, 0))
    cos_t, sin_t = pl.pallas_call(
        functools.partial(_rope_kernel, half=half),
        grid=(rows // tr,),
        in_specs=[pl.BlockSpec((tr, LANES), lambda i: (i, 0)), const, const],
        out_specs=[out, out],
        out_shape=[jax.ShapeDtypeStruct((per_row, rows, LANES), F32)] * 2,
        name="rope",
    )(pos_c, jnp.tile(freqs, per_row)[None, :], sign)
    return cos_t.reshape(T, LANES), sin_t.reshape(T, LANES)


def _inproj_kernel(x_ref, g_ref, wm_ref, wgt_ref, gb_ref, cos_ref, sin_ref, ws_ref, convw_ref,
                   convb_ref, rq_ref, rk_ref, rkwt_ref, rv_ref, rg_ref, mq_ref, mk_ref, mvt_ref,
                   mo_ref, gt_ref, carry, *, ret_w, ml_w):
    tm = x_ref.shape[0]
    ret_dh = LANES // 2

    @pl.when(pl.program_id(1) == 0)
    def _():
        carry[...] = jnp.zeros_like(carry)

    xn = _rms(x_ref[...], g_ref[...]).astype(BF16)
    proj = lambda o, w: _dot(xn, wm_ref[:, o:o + w])
    tiles = lambda w: [slice(t, t + LANES) for t in range(0, w, LANES)]

    lane = lax.broadcasted_iota(jnp.int32, (tm, LANES), 1)
    first_half = (lane % ret_dh) < (ret_dh // 2)
    cos_t = cos_ref[...]
    sin_t = sin_ref[...]

    def rot(t):
        swapped = jnp.where(first_half, pltpu.roll(t, LANES - ret_dh // 2, 1),
                            pltpu.roll(t, ret_dh // 2, 1))
        return t * cos_t + swapped * sin_t

    rq = proj(0, ret_w)
    for ps in tiles(ret_w):
        rq_ref[:, ps] = rot(rq[:, ps]).astype(BF16)
    rk = proj(ret_w, ret_w)
    for ps in tiles(ret_w):
        k = rot(rk[:, ps]) * (ret_dh ** -0.5)
        rk_ref[:, ps] = k.astype(BF16)
        rkwt_ref[ps, :] = (k * ws_ref[:, ps]).T.astype(BF16)
    rv_ref[...] = proj(2 * ret_w, ret_w).astype(BF16)
    g = proj(3 * ret_w, ret_w)
    rg_ref[...] = g * jax.nn.sigmoid(g)

    o_mq = 4 * ret_w
    row8 = lax.broadcasted_iota(jnp.int32, (SUBLANES, ml_w), 0)
    for idx, dst in enumerate((mq_ref, mk_ref)):
        cs = slice(idx * ml_w, (idx + 1) * ml_w)
        xq = proj(o_mq + idx * ml_w, ml_w)
        prev8 = carry[:, cs]
        carry[:, cs] = xq[tm - SUBLANES:tm]
        acc = xq * convw_ref[CONV_W - 1:CONV_W, cs] + convb_ref[:, cs]
        for s in range(1, CONV_W):
            rolled = pltpu.roll(xq, s, 0)
            head = jnp.where(row8 < s, pltpu.roll(prev8, s, 0), rolled[0:SUBLANES])
            shifted = jnp.concatenate([head, rolled[SUBLANES:]], axis=0)
            acc = acc + shifted * convw_ref[CONV_W - 1 - s:CONV_W - s, cs]
        act = acc * jax.nn.sigmoid(acc)
        if idx == 1:
            act = act * (LANES ** -0.5)
        dst[...] = act.astype(BF16)
    mv = proj(o_mq + 2 * ml_w, ml_w)
    for hs in tiles(ml_w):
        mvt_ref[hs, :] = mv[:, hs].T
    mo_ref[...] = jax.nn.sigmoid(proj(o_mq + 3 * ml_w, ml_w))
    gt_ref[...] = _dot_nt(wgt_ref[...], xn) + gb_ref[...]


def _inproj(x3, g, w_main, w_gate_t, gate_bias, cos_t, sin_t, ws, conv_w, conv_b, ret_w, ml_w,
            tm=256):
    B, S, D = x3.shape
    n_main = w_main.shape[1]
    const = lambda b, i: (0, 0)
    tok = lambda w: pl.BlockSpec((None, tm, w), lambda b, i: (b, i, 0))
    tok_t = lambda r: pl.BlockSpec((None, r, tm), lambda b, i: (b, 0, i))
    act = lambda w, dt: jax.ShapeDtypeStruct((B, S, w), dt)
    act_t = lambda r, dt: jax.ShapeDtypeStruct((B, r, S), dt)
    return pl.pallas_call(
        functools.partial(_inproj_kernel, ret_w=ret_w, ml_w=ml_w),
        grid=(B, S // tm),
        in_specs=[
            tok(D),
            pl.BlockSpec((1, D), const),
            pl.BlockSpec((D, n_main), const),
            pl.BlockSpec((GATE_ROWS, D), const),
            pl.BlockSpec((GATE_ROWS, 1), const),
            tok(LANES),
            tok(LANES),
            pl.BlockSpec((tm, ret_w), const),
            pl.BlockSpec((CONV_W, 2 * ml_w), const),
            pl.BlockSpec((1, 2 * ml_w), const),
        ],
        out_specs=[tok(ret_w), tok(ret_w), tok_t(ret_w), tok(ret_w), tok(ret_w),
                   tok(ml_w), tok(ml_w), tok_t(ml_w), tok(ml_w), tok_t(GATE_ROWS)],
        out_shape=[
            act(ret_w, BF16),
            act(ret_w, BF16),
            act_t(ret_w, BF16),
            act(ret_w, BF16),
            act(ret_w, F32),
            act(ml_w, BF16),
            act(ml_w, BF16),
            act_t(ml_w, F32),
            act(ml_w, F32),
            act_t(GATE_ROWS, F32),
        ],
        scratch_shapes=[pltpu.VMEM((SUBLANES, 2 * ml_w), F32)],
        compiler_params=pltpu.CompilerParams(
            dimension_semantics=("arbitrary", "arbitrary"), vmem_limit_bytes=VMEM_LIMIT),
        name="inproj",
    )(x3, g, w_main, w_gate_t, gate_bias, cos_t, sin_t, ws, conv_w, conv_b)


def _mixer_kernel(rq_ref, rk_ref, rkwt_ref, rv_ref, rg_ref, mq_ref, mk_ref, mvt_ref, mo_ref,
                  gt_ref, decay_ref, wq_ref, cd_ref, retgn_ref, mlgn_ref,
                  y_ref, r_state, s_state, n_state, m_state, *, ret_w, ml_w):
    C = rq_ref.shape[0]
    n_pairs = ret_w // LANES
    ml_heads = ml_w // LANES
    ret_dh = LANES // 2

    @pl.when(pl.program_id(1) == 0)
    def _():
        r_state[...] = jnp.zeros_like(r_state)
        s_state[...] = jnp.zeros_like(s_state)
        n_state[...] = jnp.zeros_like(n_state)
        m_state[...] = jnp.zeros_like(m_state)

    lane = lax.broadcasted_iota(jnp.int32, (C, LANES), 1)
    row = lax.broadcasted_iota(jnp.int32, (C, LANES), 0)
    assert C == LANES
    lo = lane < ret_dh
    blockdiag = (row < ret_dh) == lo
    lo_b = jnp.where(lo, 1.0, 0.0).astype(BF16)
    hi_b = jnp.where(lo, 0.0, 1.0).astype(BF16)
    pairs = [slice(p * LANES, (p + 1) * LANES) for p in range(n_pairs)]
    heads = [slice(h * LANES, (h + 1) * LANES) for h in range(ml_heads)]


    row8 = lax.broadcasted_iota(jnp.int32, (SUBLANES, C), 0)
    lane8 = lax.broadcasted_iota(jnp.int32, (SUBLANES, C), 1)
    live = row8 < ml_heads
    ig = jnp.where(live, gt_ref[0:SUBLANES, :], 0.0)
    f_pre = jnp.where(live, gt_ref[SUBLANES:2 * SUBLANES, :], 30.0)
    logf = jax.nn.log_sigmoid(f_pre)
    triu = (row <= lane).astype(BF16)
    l_hi, l_mid, l_lo = _split3(logf)
    b = _dot(l_hi, triu) + _dot(l_mid, triu) + _dot(l_lo, triu)
    beta = ig - b
    cm = beta
    shift = 1
    while shift < C:
        cm = jnp.maximum(cm, jnp.where(lane8 >= shift, pltpu.roll(cm, shift, 1), -jnp.inf))
        shift *= 2
    m_prev = m_state[...]
    mx = jnp.maximum(cm, m_prev)
    mx_last = jnp.broadcast_to(mx[:, C - 1:C], (SUBLANES, C))
    w_inter = jnp.exp(m_prev - mx)
    e_negm = jnp.exp(-(b + mx))
    w_state = jnp.exp(beta - mx_last)
    dec = jnp.exp(m_prev - mx_last)
    m_state[...] = jnp.where(live, jnp.broadcast_to(b[:, C - 1:C], (SUBLANES, C)) + mx_last, 0.0)
    w_state_b = w_state.astype(BF16)
    beta_t = jnp.concatenate([beta, jnp.zeros((LANES - SUBLANES, C), F32)], axis=0).T
    before = row <= lane

    hrow = lambda t, h: t[h:h + 1, :]

    rq = [rq_ref[:, ps] for ps in pairs]
    rk = [rk_ref[:, ps] for ps in pairs]
    rv = [rv_ref[:, ps] for ps in pairs]
    r_prev = [r_state[p] for p in range(n_pairs)]
    s_a = [_dot_nt(rq[p] * lo_b, rk[p]) for p in range(n_pairs)]
    s_b = [_dot_nt(rq[p] * hi_b, rk[p]) for p in range(n_pairs)]
    r_read = [_dot(rq[p], r_prev[p].astype(BF16)) for p in range(n_pairs)]
    r_new = [_dot(rkwt_ref[pairs[p], :], rv[p]) for p in range(n_pairs)]
    mq = [mq_ref[:, hs] for hs in heads]
    mk = [mk_ref[:, hs] for hs in heads]
    mv_t = [mvt_ref[hs, :] for hs in heads]
    st_prev = [s_state[h] for h in range(ml_heads)]
    n_prev = [n_state[h] for h in range(ml_heads)]
    a_t = [_dot_nt(mk[h], mq[h]) for h in range(ml_heads)]
    s_read = [_dot_nt(st_prev[h].astype(BF16), mq[h]) for h in range(ml_heads)]
    qn = [_dot_nt(n_prev[h].astype(BF16), mq[h])[0:1, :] for h in range(ml_heads)]
    s_new = [_dot((mv_t[h] * hrow(w_state, h)).astype(BF16), mk[h]) for h in range(ml_heads)]
    n_new = [_dot(w_state_b, mk[h])[h:h + 1, :] for h in range(ml_heads)]
    for p in range(n_pairs):
        r_state[p] = cd_ref[p] * r_prev[p] + jnp.where(blockdiag, r_new[p], 0.0)
    for h in range(ml_heads):
        s_state[h] = hrow(dec, h) * st_prev[h] + s_new[h]
        n_state[h] = hrow(dec, h) * n_prev[h] + n_new[h]

    s_a = [(s_a[p] * decay_ref[2 * p]).astype(BF16) for p in range(n_pairs)]
    s_b = [(s_b[p] * decay_ref[2 * p + 1]).astype(BF16) for p in range(n_pairs)]
    p_t = [jnp.exp(jnp.where(before, beta_t[:, h:h + 1] - hrow(mx, h), -jnp.inf)) * a_t[h]
           for h in range(ml_heads)]

    o = [_dot(s_a[p], rv[p] * lo_b) + _dot(s_b[p], rv[p] * hi_b) + r_read[p] * wq_ref[p]
         for p in range(n_pairs)]
    num_t = [_dot(mv_t[h].astype(BF16), p_t[h].astype(BF16)) + hrow(w_inter, h) * s_read[h]
             for h in range(ml_heads)]

    for p in range(n_pairs):
        sq = o[p] * o[p]
        ms_a = jnp.sum(jnp.where(lo, sq, 0.0), axis=-1, keepdims=True)
        ms_b = jnp.sum(jnp.where(lo, 0.0, sq), axis=-1, keepdims=True)
        ms = jnp.where(lo, ms_a, ms_b) * (1.0 / ret_dh)
        r = o[p] * lax.rsqrt(ms + EPS) * retgn_ref[:, pairs[p]]
        y_ref[:, pairs[p]] = (rg_ref[:, pairs[p]] * r).astype(y_ref.dtype)
    for h in range(ml_heads):
        den = jnp.sum(p_t[h], axis=0, keepdims=True) + hrow(w_inter, h) * qn[h]
        hh = (num_t[h] * (1.0 / jnp.maximum(jnp.abs(den), hrow(e_negm, h)))).T
        hm = _rms(mo_ref[:, heads[h]] * hh, mlgn_ref[:, heads[h]])
        y_ref[:, ret_w + h * LANES:ret_w + (h + 1) * LANES] = hm.astype(y_ref.dtype)


def _ret_tables(C):
    H = RET_HEADS
    dh = LANES // 2
    log_gamma = jnp.log1p(-(2.0 ** (-5.0 - jnp.arange(H, dtype=F32))))
    idx = jnp.arange(C, dtype=F32)
    rel = idx[:, None] - idx[None, :]
    causal = rel >= 0
    decay = jnp.where(causal, jnp.exp(log_gamma[:, None, None] * jnp.where(causal, rel, 0.0)), 0.0)
    w_state = jnp.exp(log_gamma[:, None] * (C - 1 - idx))
    w_query = jnp.exp(log_gamma[:, None] * (idx + 1.0))
    chunk_decay = jnp.exp(log_gamma * C)
    pair = lambda t: jnp.repeat(t.reshape(H // 2, 2, C).transpose(0, 2, 1), dh, axis=2)
    cd = jnp.repeat(chunk_decay.reshape(H // 2, 2), dh, axis=1)
    cd = jnp.broadcast_to(cd[:, :, None], (H // 2, LANES, LANES))
    ws = pair(w_state).transpose(1, 0, 2).reshape(C, (H // 2) * LANES)
    return decay, ws, pair(w_query), cd


def _mixer(ops, decay, wq, cd, ret_gn, ml_gn, B, S, ret_w, ml_w):
    C = CHUNK
    N = S // C
    n_pairs = ret_w // LANES
    ml_heads = ml_w // LANES
    c2 = lambda b, n: (0, 0)
    c3 = lambda b, n: (0, 0, 0)
    tok = lambda w: pl.BlockSpec((None, C, w), lambda b, n: (b, n, 0))
    tok_t = lambda r: pl.BlockSpec((None, r, C), lambda b, n: (b, 0, n))
    y = pl.pallas_call(
        functools.partial(_mixer_kernel, ret_w=ret_w, ml_w=ml_w),
        grid=(B, N),
        in_specs=[
            tok(ret_w), tok(ret_w), tok_t(ret_w), tok(ret_w), tok(ret_w),
            tok(ml_w), tok(ml_w), tok_t(ml_w), tok(ml_w), tok_t(GATE_ROWS),
            pl.BlockSpec((RET_HEADS, C, C), c3),
            pl.BlockSpec((n_pairs, C, LANES), c3),
            pl.BlockSpec((n_pairs, LANES, LANES), c3),
            pl.BlockSpec((1, ret_w), c2),
            pl.BlockSpec((1, ml_w), c2),
        ],
        out_specs=pl.BlockSpec((None, C, ret_w + ml_w), lambda b, n: (b, n, 0)),
        out_shape=jax.ShapeDtypeStruct((B, S, ret_w + ml_w), BF16),
        scratch_shapes=[
            pltpu.VMEM((n_pairs, LANES, LANES), F32),
            pltpu.VMEM((ml_heads, LANES, LANES), F32),
            pltpu.VMEM((ml_heads, SUBLANES, LANES), F32),
            pltpu.VMEM((SUBLANES, LANES), F32),
        ],
        compiler_params=pltpu.CompilerParams(
            dimension_semantics=("arbitrary", "arbitrary"), vmem_limit_bytes=VMEM_LIMIT),
        name="mixer",
    )(*ops, decay, wq, cd, ret_gn.reshape(1, -1), ml_gn.reshape(1, -1))
    return y.reshape(B * S, ret_w + ml_w)


def _router_kernel(y_ref, x_ref, wo_ref, g_ref, wr_hi_ref, wr_lo_ref, br_ref,
                   h_ref, xn_ref, ri_ref, rf_ref, cnt_ref, run_cnt):
    tm = y_ref.shape[0]

    @pl.when(pl.program_id(0) == 0)
    def _():
        run_cnt[...] = jnp.zeros_like(run_cnt)

    h = x_ref[...] + _dot(y_ref[...], wo_ref[...])
    h_ref[...] = h
    xn = _rms(h, g_ref[...])
    xn_ref[...] = xn.reshape(xn_ref.shape)
    x_hi = xn.astype(BF16)
    x_lo = (xn - x_hi.astype(F32)).astype(BF16)
    logits = (_dot_nt(wr_hi_ref[...], x_hi) + _dot_nt(wr_hi_ref[...], x_lo)
              + _dot_nt(wr_lo_ref[...], x_hi) + br_ref[...])
    big = jnp.int32(LANES)
    neg = -jnp.inf
    gl = logits[GROUP_ROW0:GROUP_ROW0 + 8]
    grow = lax.broadcasted_iota(jnp.int32, gl.shape, 0)
    is_g = grow < N_GROUPS
    gl = jnp.where(is_g, gl, neg)
    gmax = jnp.max(gl, axis=0, keepdims=True)
    gsum = jnp.sum(jnp.where(is_g, jnp.exp(gl - gmax), 0.0), axis=0, keepdims=True)
    p_g = 1.0 / gsum
    g_sel = jnp.min(jnp.where(is_g & (gl == gmax), grow, big), axis=0, keepdims=True)
    el = logits[0:N_EXPERTS]
    erow = lax.broadcasted_iota(jnp.int32, el.shape, 0)
    in_grp = (erow // EXPERTS_PER_GROUP) == g_sel
    el = jnp.where(in_grp, el, neg)
    emax = jnp.max(el, axis=0, keepdims=True)
    eexp = jnp.where(in_grp, jnp.exp(el - emax), 0.0)
    prob = eexp / jnp.sum(eexp, axis=0, keepdims=True)
    pm1 = jnp.where(in_grp, prob, -1.0)
    p1 = jnp.max(pm1, axis=0, keepdims=True)
    i1 = jnp.min(jnp.where(pm1 == p1, erow, big), axis=0, keepdims=True)
    pm2 = jnp.where(erow == i1, -1.0, pm1)
    p2 = jnp.max(pm2, axis=0, keepdims=True)
    i2 = jnp.min(jnp.where(pm2 == p2, erow, big), axis=0, keepdims=True)
    denom = p1 + p2
    g1 = p_g * p1 / denom
    g2 = p_g * p2 / denom

    sel1 = erow == i1
    sel2 = erow == i2
    onehot = (sel1 | sel2).astype(BF16)
    r_i = lax.broadcasted_iota(jnp.int32, (tm, tm), 0)
    c_i = lax.broadcasted_iota(jnp.int32, (tm, tm), 1)
    tri = (r_i < c_i).astype(BF16)
    prefix = _dot(onehot, tri) + run_cnt[:, 0:1]
    rank1 = jnp.sum(jnp.where(sel1, prefix, 0.0), axis=0, keepdims=True).astype(jnp.int32)
    rank2 = jnp.sum(jnp.where(sel2, prefix, 0.0), axis=0, keepdims=True).astype(jnp.int32)
    new_cnt = run_cnt[...] + jnp.sum(onehot.astype(F32), axis=1, keepdims=True)
    run_cnt[...] = new_cnt
    cnt_ref[...] = new_cnt.astype(jnp.int32)

    rrow = lax.broadcasted_iota(jnp.int32, (ROUTE_ROWS, tm), 0)
    ri_ref[...] = jnp.where(rrow == 0, i1, jnp.where(rrow == 1, i2,
                            jnp.where(rrow == 2, rank1, jnp.where(rrow == 3, rank2, 0))))
    lrow = lax.broadcasted_iota(jnp.int32, (LANES, tm), 0)
    rf_ref[...] = jnp.where(lrow == 0, g1, jnp.where(lrow == 1, g2, 0.0)).T


def _router(y, x2, w_out, g, wr_hi, wr_lo, br, tm=ROUTE_TILE):
    T, D = x2.shape
    const = lambda i: (0, 0)
    tile = lambda i: (i, 0)
    return pl.pallas_call(
        _router_kernel,
        grid=(T // tm,),
        in_specs=[
            pl.BlockSpec((tm, y.shape[1]), tile),
            pl.BlockSpec((tm, D), tile),
            pl.BlockSpec(w_out.shape, const),
            pl.BlockSpec((1, D), const),
            pl.BlockSpec((ROUTER_ROWS, D), const),
            pl.BlockSpec((ROUTER_ROWS, D), const),
            pl.BlockSpec((ROUTER_ROWS, 1), const),
        ],
        out_specs=[
            pl.BlockSpec((tm, D), tile),
            pl.BlockSpec((tm,) + ROW_TILE, lambda i: (i, 0, 0)),
            pl.BlockSpec((None, ROUTE_ROWS, tm), lambda i: (i, 0, 0)),
            pl.BlockSpec((tm, LANES), tile),
            pl.BlockSpec((N_EXPERTS, LANES), const),
        ],
        out_shape=[
            jax.ShapeDtypeStruct((T, D), F32),
            jax.ShapeDtypeStruct((T,) + ROW_TILE, F32),
            jax.ShapeDtypeStruct((T // tm, ROUTE_ROWS, tm), jnp.int32),
            jax.ShapeDtypeStruct((T, LANES), F32),
            jax.ShapeDtypeStruct((N_EXPERTS, LANES), jnp.int32),
        ],
        scratch_shapes=[pltpu.VMEM((N_EXPERTS, LANES), F32)],
        compiler_params=pltpu.CompilerParams(
            dimension_semantics=("arbitrary",), vmem_limit_bytes=VMEM_LIMIT),
        name="router",
    )(y, x2, w_out, g, wr_hi, wr_lo, br)


def _slot_base(tok0):
    return (tok0 // ROUTE_TILE) * (2 * ROUTE_TILE) + tok0 % ROUTE_TILE


def _route_spans(tm):
    assert tm % ROUTE_TILE == 0 or ROUTE_TILE % tm == 0
    span = min(tm, ROUTE_TILE)
    return [(t0, span) for t0 in range(0, tm, span)]


def _for_each_pad_block(pend_ref, nu_ref, n_blocks, fn):
    blk = EXPERT_BLOCK
    for e in range(N_EXPERTS):
        prev_end = 0 if e == 0 else pend_ref[e - 1]

        @pl.when(pend_ref[e] > prev_end)
        def _():
            fn(pl.multiple_of(pend_ref[e] - blk, blk))

    def tail(j, c):
        fn(pl.multiple_of(j * blk, blk))
        return c

    lax.fori_loop(nu_ref[0], n_blocks, tail, 0)


def _dispatch_kernel(dest_ref, pend_ref, nu_ref, xn_ref, buf_ref, zeros, sem, zsem):
    tm = xn_ref.shape[0]
    blk = EXPERT_BLOCK
    i = pl.program_id(0)

    @pl.when(i == 0)
    def _():
        zeros[...] = jnp.zeros_like(zeros)
        zcopy = lambda row: pltpu.make_async_copy(zeros, buf_ref.at[pl.ds(row, blk)], zsem)
        n_blocks = buf_ref.shape[0] // blk
        _for_each_pad_block(pend_ref, nu_ref, n_blocks, lambda row: zcopy(row).start())
        _for_each_pad_block(pend_ref, nu_ref, n_blocks, lambda row: zcopy(row).wait())

    for t0, span in _route_spans(tm):
        slot0 = _slot_base(i * tm + t0)

        def issue(g, c, t0=t0, slot0=slot0):
            for j in range(SUBLANES):
                for k in range(2):
                    t = g * SUBLANES + j
                    d = dest_ref[slot0 + k * ROUTE_TILE + t]
                    pltpu.make_async_copy(xn_ref.at[t0 + t], buf_ref.at[d], sem).start(priority=k)
            return c

        lax.fori_loop(0, span // SUBLANES, issue, 0)
    for k in range(2):
        pltpu.make_async_copy(buf_ref.at[pl.ds(tm, tm)], buf_ref.at[pl.ds(0, tm)], sem).wait()


def _dispatch(dest, pad_end, n_used, xn, n_rows, tm=ROUTE_TILE):
    T = xn.shape[0]
    return pl.pallas_call(
        _dispatch_kernel,
        grid_spec=pltpu.PrefetchScalarGridSpec(
            num_scalar_prefetch=3,
            grid=(T // tm,),
            in_specs=[pl.BlockSpec((tm,) + ROW_TILE, lambda i, *_: (i, 0, 0))],
            out_specs=pl.BlockSpec(memory_space=pl.ANY),
            scratch_shapes=[
                pltpu.VMEM((EXPERT_BLOCK,) + ROW_TILE, xn.dtype),
                pltpu.SemaphoreType.DMA,
                pltpu.SemaphoreType.DMA,
            ],
        ),
        out_shape=jax.ShapeDtypeStruct((n_rows,) + ROW_TILE, xn.dtype),
        compiler_params=pltpu.CompilerParams(
            dimension_semantics=("arbitrary",), vmem_limit_bytes=VMEM_LIMIT),
        name="dispatch",
    )(dest, pad_end, n_used, xn)


def _expert_kernel(be_ref, nu_ref, x_ref, w1_ref, w3_ref, w2_ref, y_ref, w1b, w3b, w2b):
    j = pl.program_id(0)
    used = j < nu_ref[0]
    prev_expert = be_ref[jnp.maximum(j, 1) - 1]

    @pl.when(used & ((j == 0) | (be_ref[j] != prev_expert)))
    def _():
        w1b[...] = w1_ref[...].astype(BF16)
        w3b[...] = w3_ref[...].astype(BF16)
        w2b[...] = w2_ref[...].astype(BF16)

    @pl.when(used)
    def _():
        x = x_ref[...].reshape(x_ref.shape[0], -1).astype(BF16)
        a = _dot(x, w1b[...])
        hmid = a * jax.nn.sigmoid(a) * _dot(x, w3b[...])
        y_ref[...] = _dot(hmid.astype(BF16), w2b[...]).reshape(y_ref.shape)

    @pl.when(jnp.logical_not(used))
    def _():
        y_ref[...] = jnp.zeros_like(y_ref)


def _experts(block_expert, n_used, x_buf, w1, w3, w2):
    P = x_buf.shape[0]
    D = w1.shape[1]
    assert x_buf.shape[1:] == ROW_TILE and D == SUBLANES * LANES
    blk = EXPERT_BLOCK
    d_exp = w2.shape[1]
    clamp = lambda j, nu: jnp.minimum(j, nu[0] - 1)
    w_idx = lambda j, be, nu: (be[clamp(j, nu)], 0, 0)
    return pl.pallas_call(
        _expert_kernel,
        grid_spec=pltpu.PrefetchScalarGridSpec(
            num_scalar_prefetch=2,
            grid=(P // blk,),
            in_specs=[
                pl.BlockSpec((blk,) + ROW_TILE, lambda j, be, nu: (clamp(j, nu), 0, 0)),
                pl.BlockSpec((None, D, d_exp), w_idx),
                pl.BlockSpec((None, D, d_exp), w_idx),
                pl.BlockSpec((None, d_exp, D), w_idx),
            ],
            out_specs=pl.BlockSpec((blk,) + ROW_TILE, lambda j, be, nu: (j, 0, 0)),
            scratch_shapes=[
                pltpu.VMEM((D, d_exp), BF16),
                pltpu.VMEM((D, d_exp), BF16),
                pltpu.VMEM((d_exp, D), BF16),
            ],
        ),
        out_shape=jax.ShapeDtypeStruct((P,) + ROW_TILE, F32),
        compiler_params=pltpu.CompilerParams(
            dimension_semantics=("arbitrary",), vmem_limit_bytes=VMEM_LIMIT),
        name="experts",
    )(block_expert, n_used, x_buf, w1, w3, w2)


def _final_kernel(dest_ref, h_ref, rf_ref, p_ref, wup_ref, gple_ref, ggate_ref, wgate_ref,
                  gfin_ref, ybuf_ref, out_ref, ya0, ya1, yb0, yb1, sem_a, sem_b, *, tm):
    s = pl.program_id(0)
    D = h_ref.shape[1]
    assert ROUTE_TILE % tm == 0

    def gather(tile, bufs, sem, straight=False):
        slot0 = _slot_base(tile * tm)

        def issue(g, c):
            for j in range(SUBLANES):
                for k in range(2):
                    t = g * SUBLANES + j
                    d = dest_ref[slot0 + k * ROUTE_TILE + t]
                    pltpu.make_async_copy(ybuf_ref.at[d], bufs[k].at[t], sem).start(priority=k)
            return c

        if straight:
            for g in range(tm // SUBLANES):
                issue(g, 0)
        else:
            lax.fori_loop(0, tm // SUBLANES, issue, 0)

    def wait(sem):
        for k in range(2):
            pltpu.make_async_copy(ybuf_ref.at[pl.ds(0, tm)], ybuf_ref.at[pl.ds(tm, tm)], sem).wait()

    def compute(rows, bufs):
        e = _rms(_dot(p_ref[rows, :].astype(BF16), wup_ref[...]), gple_ref[...])
        rf = rf_ref[rows, :]
        h = (h_ref[rows, :] + rf[:, 0:1] * bufs[0][...].reshape(tm, D)
             + rf[:, 1:2] * bufs[1][...].reshape(tm, D))
        gate = jax.nn.sigmoid(_dot(_rms(h, ggate_ref[...]).astype(BF16), wgate_ref[...]))
        h = h + gate * e
        out_ref[rows, :] = _rms(h, gfin_ref[...])

    last = pl.num_programs(0) - 1

    @pl.when(s == 0)
    def _():
        gather(0, (ya0, ya1), sem_a)

    wait(sem_a)
    gather(2 * s + 1, (yb0, yb1), sem_b, straight=True)
    compute(slice(0, tm), (ya0, ya1))
    wait(sem_b)
    gather(2 * jnp.minimum(s + 1, last), (ya0, ya1), sem_a, straight=True)
    compute(slice(tm, 2 * tm), (yb0, yb1))

    @pl.when(s == last)
    def _():
        wait(sem_a)


def _final(dest, h, rf, p2, w_up, g_ple, g_gate, w_gate, g_fin, y_buf, tm=256):
    T, D = h.shape
    const = lambda i, d: (0, 0)
    tile = lambda i, d: (i, 0)
    assert y_buf.shape[1:] == ROW_TILE and D == SUBLANES * LANES
    row_buf = pltpu.VMEM((tm,) + ROW_TILE, F32)
    return pl.pallas_call(
        functools.partial(_final_kernel, tm=tm),
        grid_spec=pltpu.PrefetchScalarGridSpec(
            num_scalar_prefetch=1,
            grid=(T // (2 * tm),),
            in_specs=[
                pl.BlockSpec((2 * tm, D), tile),
                pl.BlockSpec((2 * tm, LANES), tile),
                pl.BlockSpec((2 * tm, p2.shape[1]), tile),
                pl.BlockSpec(w_up.shape, const),
                pl.BlockSpec((1, D), const),
                pl.BlockSpec((1, D), const),
                pl.BlockSpec(w_gate.shape, const),
                pl.BlockSpec((1, D), const),
                pl.BlockSpec(memory_space=pl.ANY),
            ],
            out_specs=pl.BlockSpec((2 * tm, D), tile),
            scratch_shapes=[row_buf, row_buf, row_buf, row_buf,
                            pltpu.SemaphoreType.DMA, pltpu.SemaphoreType.DMA],
        ),
        out_shape=jax.ShapeDtypeStruct((T, D), F32),
        compiler_params=pltpu.CompilerParams(
            dimension_semantics=("arbitrary",), vmem_limit_bytes=VMEM_LIMIT),
        name="final",
    )(dest, h, rf, p2, w_up, g_ple, g_gate, w_gate, g_fin, y_buf)


def _layer(h2, p2, positions, B, S, attn_norm, w_in, conv_w, conv_b, b_igate, b_fgate, ret_gn,
           ml_gn, w_out, moe_norm, w_group, b_group, w_router, b_router, w1, w3, w2, w_ple_up,
           ple_norm, ple_gate_norm, w_ple_gate, out_norm):
    T, D = h2.shape
    ret_w = ret_gn.shape[0]
    ml_w = ml_gn.shape[0]
    n_main = 4 * ret_w + 4 * ml_w
    row = lambda v: v.reshape(1, -1).astype(F32)
    pad_lanes = lambda a: jnp.pad(a, ((0, 0), (0, LANES - a.shape[1])))

    ml_heads = ml_w // LANES
    w_main = w_in[:, :n_main].astype(BF16)
    gate_rows = lambda a: jnp.pad(a, ((0, SUBLANES - ml_heads), (0, 0)))
    w_i, w_f = w_in[:, n_main:n_main + ml_heads].T, w_in[:, n_main + ml_heads:].T
    w_gate_t = jnp.concatenate([gate_rows(w_i), gate_rows(w_f)]).astype(BF16)
    gate_bias = jnp.concatenate([gate_rows(b_igate[:, None]), gate_rows(b_fgate[:, None])])
    cos_t, sin_t = _rope_tables(positions, LANES // 2)
    decay, ws, wq, cd = _ret_tables(CHUNK)
    tm_in = 2 * CHUNK
    ops = _inproj(h2.reshape(B, S, D), row(attn_norm), w_main, w_gate_t, gate_bias.astype(F32),
                  cos_t.reshape(B, S, LANES), sin_t.reshape(B, S, LANES),
                  jnp.tile(ws, (tm_in // CHUNK, 1)), conv_w, conv_b.reshape(1, -1), ret_w, ml_w,
                  tm=tm_in)

    y = _mixer(ops, decay, wq, cd, ret_gn, ml_gn, B, S, ret_w, ml_w)

    pad_rows = lambda a: jnp.pad(a, ((0, ROUTER_ROWS - a.shape[0]), (0, 0)))
    wr = pad_rows(jnp.concatenate([w_router, w_group], axis=1).T)
    wr_hi = wr.astype(BF16)
    wr_lo = (wr - wr_hi.astype(F32)).astype(BF16)
    br = pad_rows(jnp.concatenate([b_router, b_group])[:, None].astype(F32))
    h_mid, xn, ri, rf, counts = _router(y, h2, w_out.astype(BF16), row(moe_norm), wr_hi, wr_lo, br)

    blk = EXPERT_BLOCK
    counts = counts[:, 0]
    padded = (counts + blk - 1) // blk * blk
    pad_end = jnp.cumsum(padded)
    pad_start = pad_end - padded
    n_blocks = (2 * T) // blk + N_EXPERTS
    block_start = jnp.arange(n_blocks, dtype=jnp.int32) * blk
    block_expert = jnp.minimum(jnp.sum(pad_end[None, :] <= block_start[:, None], axis=1),
                               N_EXPERTS - 1).astype(jnp.int32)
    n_used = (pad_end[-1:] // blk).astype(jnp.int32)
    dest = ri[:, 2:4, :]
    for e in range(N_EXPERTS):
        dest = dest + jnp.where(ri[:, 0:2, :] == e, pad_start[e], 0)
    dest = dest.astype(jnp.int32).reshape(-1)

    x_buf = _dispatch(dest, pad_end.astype(jnp.int32), n_used, xn, n_blocks * blk)
    y_buf = _experts(block_expert, n_used, x_buf, w1, w3, w2)

    return _final(dest, h_mid, rf, p2, w_ple_up.astype(BF16), row(ple_norm), row(ple_gate_norm),
                  w_ple_gate.astype(BF16), row(out_norm), y_buf)


def kernel(x, p, positions, attn_norm, w_in, conv_w, conv_b, b_igate, b_fgate, ret_gn, ml_gn,
           w_out, moe_norm, w_group, b_group, w_router, b_router, w1, w3, w2, w_ple_up, ple_norm,
           ple_gate_norm, w_ple_gate, final_norm):
    B, S, D = x.shape
    depth = p.shape[0]
    assert depth == 1, "the final RMSNorm is fused into the layer's last kernel"
    out = _layer(x.reshape(B * S, D), p[0].reshape(B * S, -1), positions, B, S,
                 attn_norm[0], w_in[0], conv_w[0], conv_b[0], b_igate[0], b_fgate[0], ret_gn[0],
                 ml_gn[0], w_out[0], moe_norm[0], w_group[0], b_group[0], w_router[0],
                 b_router[0], w1[0], w3[0], w2[0], w_ple_up[0], ple_norm[0], ple_gate_norm[0],
                 w_ple_gate[0], final_norm)
    return out.reshape(B, S, D)
```

```python
import functools

import jax
import jax.numpy as jnp
from jax import lax
from jax.experimental import pallas as pl
from jax.experimental.pallas import tpu as pltpu

F32 = jnp.float32
BF16 = jnp.bfloat16

RET_HEADS = 8
ML_HEADS = 4
CHUNK = 128
CONV_W = 4
ROPE_BASE = 10000.0
N_GROUPS = 4
EXPERTS_PER_GROUP = 8
N_EXPERTS = N_GROUPS * EXPERTS_PER_GROUP
EPS = 1e-6

LANES = 128
SUBLANES = 8
MXU_COLS = 256
MIXER_SEQS = 2
VMEM_LIMIT = 56 * 1024 * 1024
EXPERT_BLOCK = 512
GROUP_ROW0 = N_EXPERTS
ROUTER_ROWS = 64
ROUTE_TILE = 512
ROUTE_ROWS = 8
GATE_ROWS = 2 * SUBLANES
ROW_TILE = (SUBLANES, LANES)


def _rms(x, g):
    return x * lax.rsqrt(jnp.mean(x * x, axis=-1, keepdims=True) + EPS) * g


def _dot(a, b):
    return jnp.dot(a, b, preferred_element_type=F32)


def _dot_nt(a, b):
    return lax.dot_general(a, b, (((1,), (1,)), ((), ())), preferred_element_type=F32)


def _split3(x):
    hi = x.astype(BF16)
    r1 = x - hi.astype(F32)
    mid = r1.astype(BF16)
    lo = (r1 - mid.astype(F32)).astype(BF16)
    return hi, mid, lo


def _rope_kernel(pos_ref, freq_ref, sign_ref, cos_ref, sin_ref, *, half):
    ang = pos_ref[...].astype(F32) * freq_ref[...]
    cos_c = jnp.cos(ang)
    sin_c = jnp.sin(ang)
    group = lax.broadcasted_iota(jnp.int32, ang.shape, 1) // half
    per_row = LANES // half
    for s in range(per_row):
        def spread(t):
            out = t
            for g in range(per_row):
                if g != s:
                    out = jnp.where(group == g, pltpu.roll(t, (half * (g - s)) % LANES, 1), out)
            return out
        cos_ref[s] = spread(cos_c)
        sin_ref[s] = spread(sin_c) * sign_ref[...]


def _rope_tables(positions, dh):
    half = dh // 2
    per_row = LANES // half
    T = positions.size
    rows = T // per_row
    tr = min(rows, 1024)
    freqs = ROPE_BASE ** (-jnp.arange(half, dtype=F32) / half)
    pos_c = jnp.repeat(positions.reshape(per_row, rows).T, half, axis=1)
    sign = jnp.tile(jnp.concatenate([-jnp.ones((half,), F32), jnp.ones((half,), F32)]),
                    LANES // dh)[None, :]
    const = pl.BlockSpec((1, LANES), lambda i: (0, 0))
    out = pl.BlockSpec((per_row, tr, LANES), lambda i: (0, i, 0))
    cos_t, sin_t = pl.pallas_call(
        functools.partial(_rope_kernel, half=half),
        grid=(rows // tr,),
        in_specs=[pl.BlockSpec((tr, LANES), lambda i: (i, 0)), const, const],
        out_specs=[out, out],
        out_shape=[jax.ShapeDtypeStruct((per_row, rows, LANES), F32)] * 2,
        name="rope",
    )(pos_c, jnp.tile(freqs, per_row)[None, :], sign)
    return cos_t.reshape(T, LANES), sin_t.reshape(T, LANES)


def _inproj_kernel(x_ref, g_ref, wm_ref, wgt_ref, gb_ref, cos_ref, sin_ref, ws_ref, convw_ref,
                   convb_ref, rq_ref, rk_ref, rkwt_ref, rv_ref, rg_ref, mq_ref, mk_ref, mvt_ref,
                   mo_ref, gt_ref, carry, *, ret_w, ml_w):
    tm = x_ref.shape[0]
    ret_dh = LANES // 2

    @pl.when(pl.program_id(1) == 0)
    def _():
        carry[...] = jnp.zeros_like(carry)

    xn = _rms(x_ref[...], g_ref[...]).astype(BF16)
    proj = lambda o, w: _dot(xn, wm_ref[:, o:o + w])
    tiles = lambda w: [slice(t, t + LANES) for t in range(0, w, LANES)]

    lane = lax.broadcasted_iota(jnp.int32, (tm, LANES), 1)
    first_half = (lane % ret_dh) < (ret_dh // 2)
    cos_t = cos_ref[...]
    sin_t = sin_ref[...]

    def rot(t):
        swapped = jnp.where(first_half, pltpu.roll(t, LANES - ret_dh // 2, 1),
                            pltpu.roll(t, ret_dh // 2, 1))
        return t * cos_t + swapped * sin_t

    for c0 in range(0, ret_w, MXU_COLS):
        rq = proj(c0, MXU_COLS)
        for ps in tiles(MXU_COLS):
            rq_ref[:, c0 + ps.start:c0 + ps.stop] = rot(rq[:, ps]).astype(BF16)
    for c0 in range(0, ret_w, MXU_COLS):
        rk = proj(ret_w + c0, MXU_COLS)
        for ps in tiles(MXU_COLS):
            cs = slice(c0 + ps.start, c0 + ps.stop)
            k = rot(rk[:, ps]) * (ret_dh ** -0.5)
            rk_ref[:, cs] = k.astype(BF16)
            rkwt_ref[cs, :] = (k * ws_ref[:, cs]).T.astype(BF16)
    for c0 in range(0, ret_w, MXU_COLS):
        cs = slice(c0, c0 + MXU_COLS)
        rv_ref[:, cs] = proj(2 * ret_w + c0, MXU_COLS).astype(BF16)
    for c0 in range(0, ret_w, MXU_COLS):
        g = proj(3 * ret_w + c0, MXU_COLS)
        rg_ref[:, c0:c0 + MXU_COLS] = g * jax.nn.sigmoid(g)

    o_mq = 4 * ret_w
    row8 = lax.broadcasted_iota(jnp.int32, (SUBLANES, MXU_COLS), 0)
    for c0 in range(0, 2 * ml_w, MXU_COLS):
        cs = slice(c0, c0 + MXU_COLS)
        xq = proj(o_mq + c0, MXU_COLS)
        prev8 = carry[:, cs]
        carry[:, cs] = xq[tm - SUBLANES:tm]
        acc = xq * convw_ref[CONV_W - 1:CONV_W, cs] + convb_ref[:, cs]
        for s in range(1, CONV_W):
            rolled = pltpu.roll(xq, s, 0)
            head = jnp.where(row8 < s, pltpu.roll(prev8, s, 0), rolled[0:SUBLANES])
            shifted = jnp.concatenate([head, rolled[SUBLANES:]], axis=0)
            acc = acc + shifted * convw_ref[CONV_W - 1 - s:CONV_W - s, cs]
        act = acc * jax.nn.sigmoid(acc)
        if c0 < ml_w:
            mq_ref[:, cs] = act.astype(BF16)
        else:
            mk_ref[:, c0 - ml_w:c0 - ml_w + MXU_COLS] = (act * (LANES ** -0.5)).astype(BF16)
    for c0 in range(0, ml_w, MXU_COLS):
        mv = proj(o_mq + 2 * ml_w + c0, MXU_COLS)
        for hs in tiles(MXU_COLS):
            mvt_ref[c0 + hs.start:c0 + hs.stop, :] = mv[:, hs].T
    for c0 in range(0, ml_w, MXU_COLS):
        mo_ref[:, c0:c0 + MXU_COLS] = jax.nn.sigmoid(proj(o_mq + 3 * ml_w + c0, MXU_COLS))
    gt_ref[...] = _dot_nt(wgt_ref[...], xn) + gb_ref[...]


def _inproj(x3, g, w_main, w_gate_t, gate_bias, cos_t, sin_t, ws, conv_w, conv_b, ret_w, ml_w,
            tm=256):
    B, S, D = x3.shape
    n_main = w_main.shape[1]
    const = lambda b, i: (0, 0)
    tok = lambda w: pl.BlockSpec((None, tm, w), lambda b, i: (b, i, 0))
    tok_t = lambda r: pl.BlockSpec((None, r, tm), lambda b, i: (b, 0, i))
    act = lambda w, dt: jax.ShapeDtypeStruct((B, S, w), dt)
    act_t = lambda r, dt: jax.ShapeDtypeStruct((B, r, S), dt)
    return pl.pallas_call(
        functools.partial(_inproj_kernel, ret_w=ret_w, ml_w=ml_w),
        grid=(B, S // tm),
        in_specs=[
            tok(D),
            pl.BlockSpec((1, D), const),
            pl.BlockSpec((D, n_main), const),
            pl.BlockSpec((GATE_ROWS, D), const),
            pl.BlockSpec((GATE_ROWS, 1), const),
            tok(LANES),
            tok(LANES),
            pl.BlockSpec((tm, ret_w), const),
            pl.BlockSpec((CONV_W, 2 * ml_w), const),
            pl.BlockSpec((1, 2 * ml_w), const),
        ],
        out_specs=[tok(ret_w), tok(ret_w), tok_t(ret_w), tok(ret_w), tok(ret_w),
                   tok(ml_w), tok(ml_w), tok_t(ml_w), tok(ml_w), tok_t(GATE_ROWS)],
        out_shape=[
            act(ret_w, BF16),
            act(ret_w, BF16),
            act_t(ret_w, BF16),
            act(ret_w, BF16),
            act(ret_w, F32),
            act(ml_w, BF16),
            act(ml_w, BF16),
            act_t(ml_w, F32),
            act(ml_w, F32),
            act_t(GATE_ROWS, F32),
        ],
        scratch_shapes=[pltpu.VMEM((SUBLANES, 2 * ml_w), F32)],
        compiler_params=pltpu.CompilerParams(
            dimension_semantics=("arbitrary", "arbitrary"), vmem_limit_bytes=VMEM_LIMIT),
        name="inproj",
    )(x3, g, w_main, w_gate_t, gate_bias, cos_t, sin_t, ws, conv_w, conv_b)


def _mixer_kernel(rq_ref, rk_ref, rkwt_ref, rv_ref, rg_ref, mq_ref, mk_ref, mvt_ref, mo_ref,
                  gt_ref, decay_ref, wq_ref, cd_ref, retgn_ref, mlgn_ref,
                  y_ref, r_state, s_state, n_state, m_state, *, ret_w, ml_w):
    BB, C = rq_ref.shape[0], rq_ref.shape[1]
    n_pairs = ret_w // LANES
    ml_heads = ml_w // LANES
    ret_dh = LANES // 2

    @pl.when(pl.program_id(1) == 0)
    def _():
        r_state[...] = jnp.zeros_like(r_state)
        s_state[...] = jnp.zeros_like(s_state)
        n_state[...] = jnp.zeros_like(n_state)
        m_state[...] = jnp.zeros_like(m_state)

    lane = lax.broadcasted_iota(jnp.int32, (C, LANES), 1)
    row = lax.broadcasted_iota(jnp.int32, (C, LANES), 0)
    assert C == LANES
    lo = lane < ret_dh
    blockdiag = (row < ret_dh) == lo
    lo_b = jnp.where(lo, 1.0, 0.0).astype(BF16)
    hi_b = jnp.where(lo, 0.0, 1.0).astype(BF16)
    seqs = range(BB)
    pair_units = [(s, p, slice(p * LANES, (p + 1) * LANES)) for s in seqs for p in range(n_pairs)]
    head_units = [(s, h, slice(h * LANES, (h + 1) * LANES)) for s in seqs for h in range(ml_heads)]


    row8 = lax.broadcasted_iota(jnp.int32, (SUBLANES, C), 0)
    lane8 = lax.broadcasted_iota(jnp.int32, (SUBLANES, C), 1)
    live = row8 < ml_heads
    triu = (row <= lane).astype(BF16)
    before = row <= lane
    mx, w_inter, e_negm, w_state, w_state_b, dec, beta_t = [], [], [], [], [], [], []
    for s in seqs:
        ig = jnp.where(live, gt_ref[s, 0:SUBLANES, :], 0.0)
        f_pre = jnp.where(live, gt_ref[s, SUBLANES:2 * SUBLANES, :], 30.0)
        l_hi, l_mid, l_lo = _split3(jax.nn.log_sigmoid(f_pre))
        b = _dot(l_hi, triu) + _dot(l_mid, triu) + _dot(l_lo, triu)
        beta = ig - b
        cm = beta
        shift = 1
        while shift < C:
            cm = jnp.maximum(cm, jnp.where(lane8 >= shift, pltpu.roll(cm, shift, 1), -jnp.inf))
            shift *= 2
        m_prev = m_state[s]
        mx.append(jnp.maximum(cm, m_prev))
        mx_last = jnp.broadcast_to(mx[s][:, C - 1:C], (SUBLANES, C))
        w_inter.append(jnp.exp(m_prev - mx[s]))
        e_negm.append(jnp.exp(-(b + mx[s])))
        w_state.append(jnp.exp(beta - mx_last))
        w_state_b.append(w_state[s].astype(BF16))
        dec.append(jnp.exp(m_prev - mx_last))
        m_state[s] = jnp.where(live, jnp.broadcast_to(b[:, C - 1:C], (SUBLANES, C)) + mx_last, 0.0)
        beta_t.append(jnp.concatenate([beta, jnp.zeros((LANES - SUBLANES, C), F32)], axis=0).T)

    hrow = lambda t, h: t[h:h + 1, :]

    rq = [rq_ref[s, :, ps] for s, p, ps in pair_units]
    rk = [rk_ref[s, :, ps] for s, p, ps in pair_units]
    rv = [rv_ref[s, :, ps] for s, p, ps in pair_units]
    r_prev = [r_state[s, p] for s, p, ps in pair_units]
    pu = range(len(pair_units))
    s_a = [_dot_nt(rq[u] * lo_b, rk[u]) for u in pu]
    s_b = [_dot_nt(rq[u] * hi_b, rk[u]) for u in pu]
    r_read = [_dot(rq[u], r_prev[u].astype(BF16)) for u in pu]
    r_new = [_dot(rkwt_ref[s, ps, :], rv[u]) for u, (s, p, ps) in enumerate(pair_units)]
    hu = range(len(head_units))
    mq = [mq_ref[s, :, hs] for s, h, hs in head_units]
    mk = [mk_ref[s, :, hs] for s, h, hs in head_units]
    mv_t = [mvt_ref[s, hs, :] for s, h, hs in head_units]
    st_prev = [s_state[s, h] for s, h, hs in head_units]
    n_prev = [n_state[s, h] for s, h, hs in head_units]
    a_t = [_dot_nt(mk[u], mq[u]) for u in hu]
    s_read = [_dot_nt(st_prev[u].astype(BF16), mq[u]) for u in hu]
    qn = [_dot_nt(n_prev[u].astype(BF16), mq[u])[0:1, :] for u in hu]
    s_new = [_dot((mv_t[u] * hrow(w_state[s], h)).astype(BF16), mk[u])
             for u, (s, h, hs) in enumerate(head_units)]
    n_new = [_dot(w_state_b[s], mk[u])[h:h + 1, :] for u, (s, h, hs) in enumerate(head_units)]
    for u, (s, p, ps) in enumerate(pair_units):
        r_state[s, p] = cd_ref[p] * r_prev[u] + jnp.where(blockdiag, r_new[u], 0.0)
    for u, (s, h, hs) in enumerate(head_units):
        s_state[s, h] = hrow(dec[s], h) * st_prev[u] + s_new[u]
        n_state[s, h] = hrow(dec[s], h) * n_prev[u] + n_new[u]

    s_a = [(s_a[u] * decay_ref[2 * p]).astype(BF16) for u, (s, p, ps) in enumerate(pair_units)]
    s_b = [(s_b[u] * decay_ref[2 * p + 1]).astype(BF16) for u, (s, p, ps) in enumerate(pair_units)]
    p_t = [jnp.exp(jnp.where(before, beta_t[s][:, h:h + 1] - hrow(mx[s], h), -jnp.inf)) * a_t[u]
           for u, (s, h, hs) in enumerate(head_units)]

    o = [_dot(s_a[u], rv[u] * lo_b) + _dot(s_b[u], rv[u] * hi_b) + r_read[u] * wq_ref[p]
         for u, (s, p, ps) in enumerate(pair_units)]
    num_t = [_dot(mv_t[u].astype(BF16), p_t[u].astype(BF16)) + hrow(w_inter[s], h) * s_read[u]
             for u, (s, h, hs) in enumerate(head_units)]

    for u, (s, p, ps) in enumerate(pair_units):
        sq = o[u] * o[u]
        ms_a = jnp.sum(jnp.where(lo, sq, 0.0), axis=-1, keepdims=True)
        ms_b = jnp.sum(jnp.where(lo, 0.0, sq), axis=-1, keepdims=True)
        ms = jnp.where(lo, ms_a, ms_b) * (1.0 / ret_dh)
        r = o[u] * lax.rsqrt(ms + EPS) * retgn_ref[:, ps]
        y_ref[s, :, ps] = (rg_ref[s, :, ps] * r).astype(y_ref.dtype)
    for u, (s, h, hs) in enumerate(head_units):
        den = jnp.sum(p_t[u], axis=0, keepdims=True) + hrow(w_inter[s], h) * qn[u]
        hh = (num_t[u] * (1.0 / jnp.maximum(jnp.abs(den), hrow(e_negm[s], h)))).T
        hm = _rms(mo_ref[s, :, hs] * hh, mlgn_ref[:, hs])
        y_ref[s, :, ret_w + h * LANES:ret_w + (h + 1) * LANES] = hm.astype(y_ref.dtype)


def _ret_tables(C):
    H = RET_HEADS
    dh = LANES // 2
    log_gamma = jnp.log1p(-(2.0 ** (-5.0 - jnp.arange(H, dtype=F32))))
    idx = jnp.arange(C, dtype=F32)
    rel = idx[:, None] - idx[None, :]
    causal = rel >= 0
    decay = jnp.where(causal, jnp.exp(log_gamma[:, None, None] * jnp.where(causal, rel, 0.0)), 0.0)
    w_state = jnp.exp(log_gamma[:, None] * (C - 1 - idx))
    w_query = jnp.exp(log_gamma[:, None] * (idx + 1.0))
    chunk_decay = jnp.exp(log_gamma * C)
    pair = lambda t: jnp.repeat(t.reshape(H // 2, 2, C).transpose(0, 2, 1), dh, axis=2)
    cd = jnp.repeat(chunk_decay.reshape(H // 2, 2), dh, axis=1)
    cd = jnp.broadcast_to(cd[:, :, None], (H // 2, LANES, LANES))
    ws = pair(w_state).transpose(1, 0, 2).reshape(C, (H // 2) * LANES)
    return decay, ws, pair(w_query), cd


def _mixer(ops, decay, wq, cd, ret_gn, ml_gn, B, S, ret_w, ml_w):
    C = CHUNK
    N = S // C
    n_pairs = ret_w // LANES
    ml_heads = ml_w // LANES
    c2 = lambda b, n: (0, 0)
    c3 = lambda b, n: (0, 0, 0)
    BB = MIXER_SEQS if B % MIXER_SEQS == 0 else 1
    tok = lambda w: pl.BlockSpec((BB, C, w), lambda b, n: (b, n, 0))
    tok_t = lambda r: pl.BlockSpec((BB, r, C), lambda b, n: (b, 0, n))
    y = pl.pallas_call(
        functools.partial(_mixer_kernel, ret_w=ret_w, ml_w=ml_w),
        grid=(B // BB, N),
        in_specs=[
            tok(ret_w), tok(ret_w), tok_t(ret_w), tok(ret_w), tok(ret_w),
            tok(ml_w), tok(ml_w), tok_t(ml_w), tok(ml_w), tok_t(GATE_ROWS),
            pl.BlockSpec((RET_HEADS, C, C), c3),
            pl.BlockSpec((n_pairs, C, LANES), c3),
            pl.BlockSpec((n_pairs, LANES, LANES), c3),
            pl.BlockSpec((1, ret_w), c2),
            pl.BlockSpec((1, ml_w), c2),
        ],
        out_specs=pl.BlockSpec((BB, C, ret_w + ml_w), lambda b, n: (b, n, 0)),
        out_shape=jax.ShapeDtypeStruct((B, S, ret_w + ml_w), BF16),
        scratch_shapes=[
            pltpu.VMEM((BB, n_pairs, LANES, LANES), F32),
            pltpu.VMEM((BB, ml_heads, LANES, LANES), F32),
            pltpu.VMEM((BB, ml_heads, SUBLANES, LANES), F32),
            pltpu.VMEM((BB, SUBLANES, LANES), F32),
        ],
        compiler_params=pltpu.CompilerParams(
            dimension_semantics=("arbitrary", "arbitrary"), vmem_limit_bytes=VMEM_LIMIT),
        name="mixer",
    )(*ops, decay, wq, cd, ret_gn.reshape(1, -1), ml_gn.reshape(1, -1))
    return y.reshape(B * S, ret_w + ml_w)


def _router_kernel(y_ref, x_ref, wo_ref, g_ref, wr_hi_ref, wr_lo_ref, br_ref,
                   h_ref, xn_ref, ri_ref, rf_ref, cnt_ref, run_cnt):
    tm = y_ref.shape[0]

    @pl.when(pl.program_id(0) == 0)
    def _():
        run_cnt[...] = jnp.zeros_like(run_cnt)

    h = x_ref[...] + _dot(y_ref[...], wo_ref[...])
    h_ref[...] = h
    xn = _rms(h, g_ref[...])
    xn_ref[...] = xn.reshape(xn_ref.shape)
    x_hi = xn.astype(BF16)
    x_lo = (xn - x_hi.astype(F32)).astype(BF16)
    logits = (_dot_nt(wr_hi_ref[...], x_hi) + _dot_nt(wr_hi_ref[...], x_lo)
              + _dot_nt(wr_lo_ref[...], x_hi) + br_ref[...])
    big = jnp.int32(LANES)
    neg = -jnp.inf
    gl = logits[GROUP_ROW0:GROUP_ROW0 + 8]
    grow = lax.broadcasted_iota(jnp.int32, gl.shape, 0)
    is_g = grow < N_GROUPS
    gl = jnp.where(is_g, gl, neg)
    gmax = jnp.max(gl, axis=0, keepdims=True)
    gsum = jnp.sum(jnp.where(is_g, jnp.exp(gl - gmax), 0.0), axis=0, keepdims=True)
    p_g = 1.0 / gsum
    g_sel = jnp.min(jnp.where(is_g & (gl == gmax), grow, big), axis=0, keepdims=True)
    el = logits[0:N_EXPERTS]
    erow = lax.broadcasted_iota(jnp.int32, el.shape, 0)
    in_grp = (erow // EXPERTS_PER_GROUP) == g_sel
    el = jnp.where(in_grp, el, neg)
    emax = jnp.max(el, axis=0, keepdims=True)
    eexp = jnp.where(in_grp, jnp.exp(el - emax), 0.0)
    prob = eexp / jnp.sum(eexp, axis=0, keepdims=True)
    pm1 = jnp.where(in_grp, prob, -1.0)
    p1 = jnp.max(pm1, axis=0, keepdims=True)
    i1 = jnp.min(jnp.where(pm1 == p1, erow, big), axis=0, keepdims=True)
    pm2 = jnp.where(erow == i1, -1.0, pm1)
    p2 = jnp.max(pm2, axis=0, keepdims=True)
    i2 = jnp.min(jnp.where(pm2 == p2, erow, big), axis=0, keepdims=True)
    denom = p1 + p2
    g1 = p_g * p1 / denom
    g2 = p_g * p2 / denom

    sel1 = erow == i1
    sel2 = erow == i2
    onehot = (sel1 | sel2).astype(BF16)
    r_i = lax.broadcasted_iota(jnp.int32, (tm, tm), 0)
    c_i = lax.broadcasted_iota(jnp.int32, (tm, tm), 1)
    tri = (r_i < c_i).astype(BF16)
    prefix = _dot(onehot, tri) + run_cnt[:, 0:1]
    rank1 = jnp.sum(jnp.where(sel1, prefix, 0.0), axis=0, keepdims=True).astype(jnp.int32)
    rank2 = jnp.sum(jnp.where(sel2, prefix, 0.0), axis=0, keepdims=True).astype(jnp.int32)
    new_cnt = run_cnt[...] + jnp.sum(onehot.astype(F32), axis=1, keepdims=True)
    run_cnt[...] = new_cnt
    cnt_ref[...] = new_cnt.astype(jnp.int32)

    rrow = lax.broadcasted_iota(jnp.int32, (ROUTE_ROWS, tm), 0)
    ri_ref[...] = jnp.where(rrow == 0, i1, jnp.where(rrow == 1, i2,
                            jnp.where(rrow == 2, rank1, jnp.where(rrow == 3, rank2, 0))))
    lrow = lax.broadcasted_iota(jnp.int32, (LANES, tm), 0)
    rf_ref[...] = jnp.where(lrow == 0, g1, jnp.where(lrow == 1, g2, 0.0)).T


def _router(y, x2, w_out, g, wr_hi, wr_lo, br, tm=ROUTE_TILE):
    T, D = x2.shape
    const = lambda i: (0, 0)
    tile = lambda i: (i, 0)
    return pl.pallas_call(
        _router_kernel,
        grid=(T // tm,),
        in_specs=[
            pl.BlockSpec((tm, y.shape[1]), tile),
            pl.BlockSpec((tm, D), tile),
            pl.BlockSpec(w_out.shape, const),
            pl.BlockSpec((1, D), const),
            pl.BlockSpec((ROUTER_ROWS, D), const),
            pl.BlockSpec((ROUTER_ROWS, D), const),
            pl.BlockSpec((ROUTER_ROWS, 1), const),
        ],
        out_specs=[
            pl.BlockSpec((tm, D), tile),
            pl.BlockSpec((tm,) + ROW_TILE, lambda i: (i, 0, 0)),
            pl.BlockSpec((None, ROUTE_ROWS, tm), lambda i: (i, 0, 0)),
            pl.BlockSpec((tm, LANES), tile),
            pl.BlockSpec((N_EXPERTS, LANES), const),
        ],
        out_shape=[
            jax.ShapeDtypeStruct((T, D), F32),
            jax.ShapeDtypeStruct((T,) + ROW_TILE, F32),
            jax.ShapeDtypeStruct((T // tm, ROUTE_ROWS, tm), jnp.int32),
            jax.ShapeDtypeStruct((T, LANES), F32),
            jax.ShapeDtypeStruct((N_EXPERTS, LANES), jnp.int32),
        ],
        scratch_shapes=[pltpu.VMEM((N_EXPERTS, LANES), F32)],
        compiler_params=pltpu.CompilerParams(
            dimension_semantics=("arbitrary",), vmem_limit_bytes=VMEM_LIMIT),
        name="router",
    )(y, x2, w_out, g, wr_hi, wr_lo, br)


def _slot_base(tok0):
    return (tok0 // ROUTE_TILE) * (2 * ROUTE_TILE) + tok0 % ROUTE_TILE


def _route_spans(tm):
    assert tm % ROUTE_TILE == 0 or ROUTE_TILE % tm == 0
    span = min(tm, ROUTE_TILE)
    return [(t0, span) for t0 in range(0, tm, span)]


def _for_each_pad_block(pend_ref, nu_ref, n_blocks, fn):
    blk = EXPERT_BLOCK
    for e in range(N_EXPERTS):
        prev_end = 0 if e == 0 else pend_ref[e - 1]

        @pl.when(pend_ref[e] > prev_end)
        def _():
            fn(pl.multiple_of(pend_ref[e] - blk, blk))

    def tail(j, c):
        fn(pl.multiple_of(j * blk, blk))
        return c

    lax.fori_loop(nu_ref[0], n_blocks, tail, 0)


def _dispatch_kernel(dest_ref, pend_ref, nu_ref, xn_ref, buf_ref, zeros, sem, zsem):
    tm = xn_ref.shape[0]
    blk = EXPERT_BLOCK
    i = pl.program_id(0)

    @pl.when(i == 0)
    def _():
        zeros[...] = jnp.zeros_like(zeros)
        zcopy = lambda row: pltpu.make_async_copy(zeros, buf_ref.at[pl.ds(row, blk)], zsem)
        n_blocks = buf_ref.shape[0] // blk
        _for_each_pad_block(pend_ref, nu_ref, n_blocks, lambda row: zcopy(row).start())
        _for_each_pad_block(pend_ref, nu_ref, n_blocks, lambda row: zcopy(row).wait())

    for t0, span in _route_spans(tm):
        slot0 = _slot_base(i * tm + t0)

        def issue(g, c, t0=t0, slot0=slot0):
            for j in range(SUBLANES):
                for k in range(2):
                    t = g * SUBLANES + j
                    d = dest_ref[slot0 + k * ROUTE_TILE + t]
                    pltpu.make_async_copy(xn_ref.at[t0 + t], buf_ref.at[d], sem).start(priority=k)
            return c

        lax.fori_loop(0, span // SUBLANES, issue, 0)
    for k in range(2):
        pltpu.make_async_copy(buf_ref.at[pl.ds(tm, tm)], buf_ref.at[pl.ds(0, tm)], sem).wait()


def _dispatch(dest, pad_end, n_used, xn, n_rows, tm=ROUTE_TILE):
    T = xn.shape[0]
    return pl.pallas_call(
        _dispatch_kernel,
        grid_spec=pltpu.PrefetchScalarGridSpec(
            num_scalar_prefetch=3,
            grid=(T // tm,),
            in_specs=[pl.BlockSpec((tm,) + ROW_TILE, lambda i, *_: (i, 0, 0))],
            out_specs=pl.BlockSpec(memory_space=pl.ANY),
            scratch_shapes=[
                pltpu.VMEM((EXPERT_BLOCK,) + ROW_TILE, xn.dtype),
                pltpu.SemaphoreType.DMA,
                pltpu.SemaphoreType.DMA,
            ],
        ),
        out_shape=jax.ShapeDtypeStruct((n_rows,) + ROW_TILE, xn.dtype),
        compiler_params=pltpu.CompilerParams(
            dimension_semantics=("arbitrary",), vmem_limit_bytes=VMEM_LIMIT),
        name="dispatch",
    )(dest, pad_end, n_used, xn)


def _expert_kernel(be_ref, nu_ref, x_ref, w1_ref, w3_ref, w2_ref, y_ref, w1b, w3b, w2b):
    j = pl.program_id(0)
    used = j < nu_ref[0]
    prev_expert = be_ref[jnp.maximum(j, 1) - 1]

    @pl.when(used & ((j == 0) | (be_ref[j] != prev_expert)))
    def _():
        w1b[...] = w1_ref[...].astype(BF16)
        w3b[...] = w3_ref[...].astype(BF16)
        w2b[...] = w2_ref[...].astype(BF16)

    @pl.when(used)
    def _():
        x = x_ref[...].reshape(x_ref.shape[0], -1).astype(BF16)
        a = _dot(x, w1b[...])
        hmid = a * jax.nn.sigmoid(a) * _dot(x, w3b[...])
        y_ref[...] = _dot(hmid.astype(BF16), w2b[...]).reshape(y_ref.shape)

    @pl.when(jnp.logical_not(used))
    def _():
        y_ref[...] = jnp.zeros_like(y_ref)


def _experts(block_expert, n_used, x_buf, w1, w3, w2):
    P = x_buf.shape[0]
    D = w1.shape[1]
    assert x_buf.shape[1:] == ROW_TILE and D == SUBLANES * LANES
    blk = EXPERT_BLOCK
    d_exp = w2.shape[1]
    clamp = lambda j, nu: jnp.minimum(j, nu[0] - 1)
    w_idx = lambda j, be, nu: (be[clamp(j, nu)], 0, 0)
    return pl.pallas_call(
        _expert_kernel,
        grid_spec=pltpu.PrefetchScalarGridSpec(
            num_scalar_prefetch=2,
            grid=(P // blk,),
            in_specs=[
                pl.BlockSpec((blk,) + ROW_TILE, lambda j, be, nu: (clamp(j, nu), 0, 0)),
                pl.BlockSpec((None, D, d_exp), w_idx),
                pl.BlockSpec((None, D, d_exp), w_idx),
                pl.BlockSpec((None, d_exp, D), w_idx),
            ],
            out_specs=pl.BlockSpec((blk,) + ROW_TILE, lambda j, be, nu: (j, 0, 0)),
            scratch_shapes=[
                pltpu.VMEM((D, d_exp), BF16),
                pltpu.VMEM((D, d_exp), BF16),
                pltpu.VMEM((d_exp, D), BF16),
            ],
        ),
        out_shape=jax.ShapeDtypeStruct((P,) + ROW_TILE, F32),
        compiler_params=pltpu.CompilerParams(
            dimension_semantics=("arbitrary",), vmem_limit_bytes=VMEM_LIMIT),
        name="experts",
    )(block_expert, n_used, x_buf, w1, w3, w2)


def _final_kernel(dest_ref, h_ref, rf_ref, p_ref, wup_ref, gple_ref, ggate_ref, wgate_ref,
                  gfin_ref, ybuf_ref, out_ref, ya0, ya1, yb0, yb1, sem_a, sem_b, *, tm):
    s = pl.program_id(0)
    D = h_ref.shape[1]
    assert ROUTE_TILE % tm == 0

    def gather(tile, bufs, sem, straight=False):
        slot0 = _slot_base(tile * tm)

        def issue(g, c):
            for j in range(SUBLANES):
                for k in range(2):
                    t = g * SUBLANES + j
                    d = dest_ref[slot0 + k * ROUTE_TILE + t]
                    pltpu.make_async_copy(ybuf_ref.at[d], bufs[k].at[t], sem).start(priority=k)
            return c

        if straight:
            for g in range(tm // SUBLANES):
                issue(g, 0)
        else:
            lax.fori_loop(0, tm // SUBLANES, issue, 0)

    def wait(sem):
        for k in range(2):
            pltpu.make_async_copy(ybuf_ref.at[pl.ds(0, tm)], ybuf_ref.at[pl.ds(tm, tm)], sem).wait()

    def compute(rows, bufs):
        e = _rms(_dot(p_ref[rows, :].astype(BF16), wup_ref[...]), gple_ref[...])
        rf = rf_ref[rows, :]
        h = (h_ref[rows, :] + rf[:, 0:1] * bufs[0][...].reshape(tm, D)
             + rf[:, 1:2] * bufs[1][...].reshape(tm, D))
        gate = jax.nn.sigmoid(_dot(_rms(h, ggate_ref[...]).astype(BF16), wgate_ref[...]))
        h = h + gate * e
        out_ref[rows, :] = _rms(h, gfin_ref[...])

    last = pl.num_programs(0) - 1

    @pl.when(s == 0)
    def _():
        gather(0, (ya0, ya1), sem_a)

    wait(sem_a)
    gather(2 * s + 1, (yb0, yb1), sem_b, straight=True)
    compute(slice(0, tm), (ya0, ya1))
    wait(sem_b)
    gather(2 * jnp.minimum(s + 1, last), (ya0, ya1), sem_a, straight=True)
    compute(slice(tm, 2 * tm), (yb0, yb1))

    @pl.when(s == last)
    def _():
        wait(sem_a)


def _final(dest, h, rf, p2, w_up, g_ple, g_gate, w_gate, g_fin, y_buf, tm=256):
    T, D = h.shape
    const = lambda i, d: (0, 0)
    tile = lambda i, d: (i, 0)
    assert y_buf.shape[1:] == ROW_TILE and D == SUBLANES * LANES
    row_buf = pltpu.VMEM((tm,) + ROW_TILE, F32)
    return pl.pallas_call(
        functools.partial(_final_kernel, tm=tm),
        grid_spec=pltpu.PrefetchScalarGridSpec(
            num_scalar_prefetch=1,
            grid=(T // (2 * tm),),
            in_specs=[
                pl.BlockSpec((2 * tm, D), tile),
                pl.BlockSpec((2 * tm, LANES), tile),
                pl.BlockSpec((2 * tm, p2.shape[1]), tile),
                pl.BlockSpec(w_up.shape, const),
                pl.BlockSpec((1, D), const),
                pl.BlockSpec((1, D), const),
                pl.BlockSpec(w_gate.shape, const),
                pl.BlockSpec((1, D), const),
                pl.BlockSpec(memory_space=pl.ANY),
            ],
            out_specs=pl.BlockSpec((2 * tm, D), tile),
            scratch_shapes=[row_buf, row_buf, row_buf, row_buf,
                            pltpu.SemaphoreType.DMA, pltpu.SemaphoreType.DMA],
        ),
        out_shape=jax.ShapeDtypeStruct((T, D), F32),
        compiler_params=pltpu.CompilerParams(
            dimension_semantics=("arbitrary",), vmem_limit_bytes=VMEM_LIMIT),
        name="final",
    )(dest, h, rf, p2, w_up, g_ple, g_gate, w_gate, g_fin, y_buf)


def _layer(h2, p2, positions, B, S, attn_norm, w_in, conv_w, conv_b, b_igate, b_fgate, ret_gn,
           ml_gn, w_out, moe_norm, w_group, b_group, w_router, b_router, w1, w3, w2, w_ple_up,
           ple_norm, ple_gate_norm, w_ple_gate, out_norm):
    T, D = h2.shape
    ret_w = ret_gn.shape[0]
    ml_w = ml_gn.shape[0]
    n_main = 4 * ret_w + 4 * ml_w
    row = lambda v: v.reshape(1, -1).astype(F32)
    pad_lanes = lambda a: jnp.pad(a, ((0, 0), (0, LANES - a.shape[1])))

    ml_heads = ml_w // LANES
    w_main = w_in[:, :n_main].astype(BF16)
    gate_rows = lambda a: jnp.pad(a, ((0, SUBLANES - ml_heads), (0, 0)))
    w_i, w_f = w_in[:, n_main:n_main + ml_heads].T, w_in[:, n_main + ml_heads:].T
    w_gate_t = jnp.concatenate([gate_rows(w_i), gate_rows(w_f)]).astype(BF16)
    gate_bias = jnp.concatenate([gate_rows(b_igate[:, None]), gate_rows(b_fgate[:, None])])
    cos_t, sin_t = _rope_tables(positions, LANES // 2)
    decay, ws, wq, cd = _ret_tables(CHUNK)
    tm_in = 2 * CHUNK
    ops = _inproj(h2.reshape(B, S, D), row(attn_norm), w_main, w_gate_t, gate_bias.astype(F32),
                  cos_t.reshape(B, S, LANES), sin_t.reshape(B, S, LANES),
                  jnp.tile(ws, (tm_in // CHUNK, 1)), conv_w, conv_b.reshape(1, -1), ret_w, ml_w,
                  tm=tm_in)

    y = _mixer(ops, decay, wq, cd, ret_gn, ml_gn, B, S, ret_w, ml_w)

    pad_rows = lambda a: jnp.pad(a, ((0, ROUTER_ROWS - a.shape[0]), (0, 0)))
    wr = pad_rows(jnp.concatenate([w_router, w_group], axis=1).T)
    wr_hi = wr.astype(BF16)
    wr_lo = (wr - wr_hi.astype(F32)).astype(BF16)
    br = pad_rows(jnp.concatenate([b_router, b_group])[:, None].astype(F32))
    h_mid, xn, ri, rf, counts = _router(y, h2, w_out.astype(BF16), row(moe_norm), wr_hi, wr_lo, br)

    blk = EXPERT_BLOCK
    counts = counts[:, 0]
    padded = (counts + blk - 1) // blk * blk
    pad_end = jnp.cumsum(padded)
    pad_start = pad_end - padded
    n_blocks = (2 * T) // blk + N_EXPERTS
    block_start = jnp.arange(n_blocks, dtype=jnp.int32) * blk
    block_expert = jnp.minimum(jnp.sum(pad_end[None, :] <= block_start[:, None], axis=1),
                               N_EXPERTS - 1).astype(jnp.int32)
    n_used = (pad_end[-1:] // blk).astype(jnp.int32)
    dest = ri[:, 2:4, :]
    for e in range(N_EXPERTS):
        dest = dest + jnp.where(ri[:, 0:2, :] == e, pad_start[e], 0)
    dest = dest.astype(jnp.int32).reshape(-1)

    x_buf = _dispatch(dest, pad_end.astype(jnp.int32), n_used, xn, n_blocks * blk)
    y_buf = _experts(block_expert, n_used, x_buf, w1, w3, w2)

    return _final(dest, h_mid, rf, p2, w_ple_up.astype(BF16), row(ple_norm), row(ple_gate_norm),
                  w_ple_gate.astype(BF16), row(out_norm), y_buf)


def kernel(x, p, positions, attn_norm, w_in, conv_w, conv_b, b_igate, b_fgate, ret_gn, ml_gn,
           w_out, moe_norm, w_group, b_group, w_router, b_router, w1, w3, w2, w_ple_up, ple_norm,
           ple_gate_norm, w_ple_gate, final_norm):
    B, S, D = x.shape
    depth = p.shape[0]
    assert depth == 1, "the final RMSNorm is fused into the layer's last kernel"
    out = _layer(x.reshape(B * S, D), p[0].reshape(B * S, -1), positions, B, S,
                 attn_norm[0], w_in[0], conv_w[0], conv_b[0], b_igate[0], b_fgate[0], ret_gn[0],
                 ml_gn[0], w_out[0], moe_norm[0], w_group[0], b_group[0], w_router[0],
                 b_router[0], w1[0], w3[0], w2[0], w_ple_up[0], ple_norm[0], ple_gate_norm[0],
                 w_ple_gate[0], final_norm)
    return out.reshape(B, S, D)
```

```python
import functools

import jax
import jax.numpy as jnp
from jax import lax
from jax.experimental import pallas as pl
from jax.experimental.pallas import tpu as pltpu

F32 = jnp.float32
BF16 = jnp.bfloat16

RET_HEADS = 8
ML_HEADS = 4
CHUNK = 128
CONV_W = 4
ROPE_BASE = 10000.0
N_GROUPS = 4
EXPERTS_PER_GROUP = 8
N_EXPERTS = N_GROUPS * EXPERTS_PER_GROUP
EPS = 1e-6

LANES = 128
SUBLANES = 8
MXU_COLS = 256
MIXER_SEQS = 2
FINAL_TILES = 4
FINAL_AHEAD = 2
VMEM_LIMIT = 56 * 1024 * 1024
EXPERT_BLOCK = 512
GROUP_ROW0 = N_EXPERTS
ROUTER_ROWS = 64
ROUTE_TILE = 512
ROUTE_ROWS = 8
DISPATCH_TILE = 4 * ROUTE_TILE
GATE_ROWS = 2 * SUBLANES
ROW_TILE = (SUBLANES, LANES)


def _rms(x, g):
    return x * lax.rsqrt(jnp.mean(x * x, axis=-1, keepdims=True) + EPS) * g


def _dot(a, b):
    return jnp.dot(a, b, preferred_element_type=F32)


def _dot_nt(a, b):
    return lax.dot_general(a, b, (((1,), (1,)), ((), ())), preferred_element_type=F32)


def _split3(x):
    hi = x.astype(BF16)
    r1 = x - hi.astype(F32)
    mid = r1.astype(BF16)
    lo = (r1 - mid.astype(F32)).astype(BF16)
    return hi, mid, lo


def _rope_kernel(pos_ref, freq_ref, sign_ref, cos_ref, sin_ref, *, half):
    ang = pos_ref[...].astype(F32) * freq_ref[...]
    cos_c = jnp.cos(ang)
    sin_c = jnp.sin(ang)
    group = lax.broadcasted_iota(jnp.int32, ang.shape, 1) // half
    per_row = LANES // half
    for s in range(per_row):
        def spread(t):
            out = t
            for g in range(per_row):
                if g != s:
                    out = jnp.where(group == g, pltpu.roll(t, (half * (g - s)) % LANES, 1), out)
            return out
        cos_ref[s] = spread(cos_c)
        sin_ref[s] = spread(sin_c) * sign_ref[...]


def _rope_tables(positions, dh):
    half = dh // 2
    per_row = LANES // half
    T = positions.size
    rows = T // per_row
    tr = min(rows, 1024)
    freqs = ROPE_BASE ** (-jnp.arange(half, dtype=F32) / half)
    pos_c = jnp.repeat(positions.reshape(per_row, rows).T, half, axis=1)
    sign = jnp.tile(jnp.concatenate([-jnp.ones((half,), F32), jnp.ones((half,), F32)]),
                    LANES // dh)[None, :]
    const = pl.BlockSpec((1, LANES), lambda i: (0, 0))
    out = pl.BlockSpec((per_row, tr, LANES), lambda i: (0, i, 0))
    cos_t, sin_t = pl.pallas_call(
        functools.partial(_rope_kernel, half=half),
        grid=(rows // tr,),
        in_specs=[pl.BlockSpec((tr, LANES), lambda i: (i, 0)), const, const],
        out_specs=[out, out],
        out_shape=[jax.ShapeDtypeStruct((per_row, rows, LANES), F32)] * 2,
        name="rope",
    )(pos_c, jnp.tile(freqs, per_row)[None, :], sign)
    return cos_t.reshape(T, LANES), sin_t.reshape(T, LANES)


def _inproj_kernel(x_ref, g_ref, wm_ref, wgt_ref, gb_ref, cos_ref, sin_ref, ws_ref, convw_ref,
                   convb_ref, rq_ref, rk_ref, rkwt_ref, rv_ref, rg_ref, mq_ref, mk_ref, mvt_ref,
                   mo_ref, gt_ref, carry, *, ret_w, ml_w):
    tm = x_ref.shape[0]
    ret_dh = LANES // 2

    @pl.when(pl.program_id(1) == 0)
    def _():
        carry[...] = jnp.zeros_like(carry)

    xn = _rms(x_ref[...], g_ref[...]).astype(BF16)
    proj = lambda o, w: _dot(xn, wm_ref[:, o:o + w])
    tiles = lambda w: [slice(t, t + LANES) for t in range(0, w, LANES)]

    lane = lax.broadcasted_iota(jnp.int32, (tm, LANES), 1)
    first_half = (lane % ret_dh) < (ret_dh // 2)
    cos_t = cos_ref[...]
    sin_t = sin_ref[...]

    def rot(t):
        swapped = jnp.where(first_half, pltpu.roll(t, LANES - ret_dh // 2, 1),
                            pltpu.roll(t, ret_dh // 2, 1))
        return t * cos_t + swapped * sin_t

    for c0 in range(0, ret_w, MXU_COLS):
        rq = proj(c0, MXU_COLS)
        for ps in tiles(MXU_COLS):
            rq_ref[:, c0 + ps.start:c0 + ps.stop] = rot(rq[:, ps]).astype(BF16)
    for c0 in range(0, ret_w, MXU_COLS):
        rk = proj(ret_w + c0, MXU_COLS)
        for ps in tiles(MXU_COLS):
            cs = slice(c0 + ps.start, c0 + ps.stop)
            k = rot(rk[:, ps]) * (ret_dh ** -0.5)
            rk_ref[:, cs] = k.astype(BF16)
            rkwt_ref[cs, :] = (k * ws_ref[:, cs]).T.astype(BF16)
    for c0 in range(0, ret_w, MXU_COLS):
        cs = slice(c0, c0 + MXU_COLS)
        rv_ref[:, cs] = proj(2 * ret_w + c0, MXU_COLS).astype(BF16)
    for c0 in range(0, ret_w, MXU_COLS):
        g = proj(3 * ret_w + c0, MXU_COLS)
        rg_ref[:, c0:c0 + MXU_COLS] = g * jax.nn.sigmoid(g)

    o_mq = 4 * ret_w
    row8 = lax.broadcasted_iota(jnp.int32, (SUBLANES, MXU_COLS), 0)
    for c0 in range(0, 2 * ml_w, MXU_COLS):
        cs = slice(c0, c0 + MXU_COLS)
        xq = proj(o_mq + c0, MXU_COLS)
        prev8 = carry[:, cs]
        carry[:, cs] = xq[tm - SUBLANES:tm]
        acc = xq * convw_ref[CONV_W - 1:CONV_W, cs] + convb_ref[:, cs]
        for s in range(1, CONV_W):
            rolled = pltpu.roll(xq, s, 0)
            head = jnp.where(row8 < s, pltpu.roll(prev8, s, 0), rolled[0:SUBLANES])
            shifted = jnp.concatenate([head, rolled[SUBLANES:]], axis=0)
            acc = acc + shifted * convw_ref[CONV_W - 1 - s:CONV_W - s, cs]
        act = acc * jax.nn.sigmoid(acc)
        if c0 < ml_w:
            mq_ref[:, cs] = act.astype(BF16)
        else:
            mk_ref[:, c0 - ml_w:c0 - ml_w + MXU_COLS] = (act * (LANES ** -0.5)).astype(BF16)
    for c0 in range(0, ml_w, MXU_COLS):
        mv = proj(o_mq + 2 * ml_w + c0, MXU_COLS)
        for hs in tiles(MXU_COLS):
            mvt_ref[c0 + hs.start:c0 + hs.stop, :] = mv[:, hs].T
    for c0 in range(0, ml_w, MXU_COLS):
        mo_ref[:, c0:c0 + MXU_COLS] = jax.nn.sigmoid(proj(o_mq + 3 * ml_w + c0, MXU_COLS))
    gt_ref[...] = _dot_nt(wgt_ref[...], xn) + gb_ref[...]


def _inproj(x3, g, w_main, w_gate_t, gate_bias, cos_t, sin_t, ws, conv_w, conv_b, ret_w, ml_w,
            tm=256):
    B, S, D = x3.shape
    n_main = w_main.shape[1]
    const = lambda b, i: (0, 0)
    tok = lambda w: pl.BlockSpec((None, tm, w), lambda b, i: (b, i, 0))
    tok_t = lambda r: pl.BlockSpec((None, r, tm), lambda b, i: (b, 0, i))
    act = lambda w, dt: jax.ShapeDtypeStruct((B, S, w), dt)
    act_t = lambda r, dt: jax.ShapeDtypeStruct((B, r, S), dt)
    return pl.pallas_call(
        functools.partial(_inproj_kernel, ret_w=ret_w, ml_w=ml_w),
        grid=(B, S // tm),
        in_specs=[
            tok(D),
            pl.BlockSpec((1, D), const),
            pl.BlockSpec((D, n_main), const),
            pl.BlockSpec((GATE_ROWS, D), const),
            pl.BlockSpec((GATE_ROWS, 1), const),
            tok(LANES),
            tok(LANES),
            pl.BlockSpec((tm, ret_w), const),
            pl.BlockSpec((CONV_W, 2 * ml_w), const),
            pl.BlockSpec((1, 2 * ml_w), const),
        ],
        out_specs=[tok(ret_w), tok(ret_w), tok_t(ret_w), tok(ret_w), tok(ret_w),
                   tok(ml_w), tok(ml_w), tok_t(ml_w), tok(ml_w), tok_t(GATE_ROWS)],
        out_shape=[
            act(ret_w, BF16),
            act(ret_w, BF16),
            act_t(ret_w, BF16),
            act(ret_w, BF16),
            act(ret_w, F32),
            act(ml_w, BF16),
            act(ml_w, BF16),
            act_t(ml_w, F32),
            act(ml_w, F32),
            act_t(GATE_ROWS, F32),
        ],
        scratch_shapes=[pltpu.VMEM((SUBLANES, 2 * ml_w), F32)],
        compiler_params=pltpu.CompilerParams(
            dimension_semantics=("arbitrary", "arbitrary"), vmem_limit_bytes=VMEM_LIMIT),
        name="inproj",
    )(x3, g, w_main, w_gate_t, gate_bias, cos_t, sin_t, ws, conv_w, conv_b)


def _mixer_kernel(rq_ref, rk_ref, rkwt_ref, rv_ref, rg_ref, mq_ref, mk_ref, mvt_ref, mo_ref,
                  gt_ref, decay_ref, wq_ref, cd_ref, retgn_ref, mlgn_ref,
                  y_ref, r_state, s_state, n_state, m_state, *, ret_w, ml_w):
    BB, C = rq_ref.shape[0], rq_ref.shape[1]
    n_pairs = ret_w // LANES
    ml_heads = ml_w // LANES
    ret_dh = LANES // 2

    @pl.when(pl.program_id(1) == 0)
    def _():
        r_state[...] = jnp.zeros_like(r_state)
        s_state[...] = jnp.zeros_like(s_state)
        n_state[...] = jnp.zeros_like(n_state)
        m_state[...] = jnp.zeros_like(m_state)

    lane = lax.broadcasted_iota(jnp.int32, (C, LANES), 1)
    row = lax.broadcasted_iota(jnp.int32, (C, LANES), 0)
    assert C == LANES
    lo = lane < ret_dh
    blockdiag = (row < ret_dh) == lo
    lo_b = jnp.where(lo, 1.0, 0.0).astype(BF16)
    hi_b = jnp.where(lo, 0.0, 1.0).astype(BF16)
    seqs = range(BB)
    pair_units = [(s, p, slice(p * LANES, (p + 1) * LANES)) for s in seqs for p in range(n_pairs)]
    head_units = [(s, h, slice(h * LANES, (h + 1) * LANES)) for s in seqs for h in range(ml_heads)]


    row8 = lax.broadcasted_iota(jnp.int32, (SUBLANES, C), 0)
    lane8 = lax.broadcasted_iota(jnp.int32, (SUBLANES, C), 1)
    live = row8 < ml_heads
    triu = (row <= lane).astype(BF16)
    before = row <= lane
    mx, w_inter, e_negm, w_state, w_state_b, dec, beta_t = [], [], [], [], [], [], []
    for s in seqs:
        ig = jnp.where(live, gt_ref[s, 0:SUBLANES, :], 0.0)
        f_pre = jnp.where(live, gt_ref[s, SUBLANES:2 * SUBLANES, :], 30.0)
        l_hi, l_mid, l_lo = _split3(jax.nn.log_sigmoid(f_pre))
        b = _dot(l_hi, triu) + _dot(l_mid, triu) + _dot(l_lo, triu)
        beta = ig - b
        cm = beta
        shift = 1
        while shift < C:
            cm = jnp.maximum(cm, jnp.where(lane8 >= shift, pltpu.roll(cm, shift, 1), -jnp.inf))
            shift *= 2
        m_prev = m_state[s]
        mx.append(jnp.maximum(cm, m_prev))
        mx_last = jnp.broadcast_to(mx[s][:, C - 1:C], (SUBLANES, C))
        w_inter.append(jnp.exp(m_prev - mx[s]))
        e_negm.append(jnp.exp(-(b + mx[s])))
        w_state.append(jnp.exp(beta - mx_last))
        w_state_b.append(w_state[s].astype(BF16))
        dec.append(jnp.exp(m_prev - mx_last))
        m_state[s] = jnp.where(live, jnp.broadcast_to(b[:, C - 1:C], (SUBLANES, C)) + mx_last, 0.0)
        beta_t.append(jnp.concatenate([beta, jnp.zeros((LANES - SUBLANES, C), F32)], axis=0).T)

    hrow = lambda t, h: t[h:h + 1, :]

    rq = [rq_ref[s, :, ps] for s, p, ps in pair_units]
    rk = [rk_ref[s, :, ps] for s, p, ps in pair_units]
    rv = [rv_ref[s, :, ps] for s, p, ps in pair_units]
    r_prev = [r_state[s, p] for s, p, ps in pair_units]
    pu = range(len(pair_units))
    s_a = [_dot_nt(rq[u] * lo_b, rk[u]) for u in pu]
    s_b = [_dot_nt(rq[u] * hi_b, rk[u]) for u in pu]
    r_read = [_dot(rq[u], r_prev[u].astype(BF16)) for u in pu]
    r_new = [_dot(rkwt_ref[s, ps, :], rv[u]) for u, (s, p, ps) in enumerate(pair_units)]
    hu = range(len(head_units))
    mq = [mq_ref[s, :, hs] for s, h, hs in head_units]
    mk = [mk_ref[s, :, hs] for s, h, hs in head_units]
    mv_t = [mvt_ref[s, hs, :] for s, h, hs in head_units]
    st_prev = [s_state[s, h] for s, h, hs in head_units]
    n_prev = [n_state[s, h] for s, h, hs in head_units]
    a_t = [_dot_nt(mk[u], mq[u]) for u in hu]
    s_read = [_dot_nt(st_prev[u].astype(BF16), mq[u]) for u in hu]
    qn = [_dot_nt(n_prev[u].astype(BF16), mq[u])[0:1, :] for u in hu]
    s_new = [_dot((mv_t[u] * hrow(w_state[s], h)).astype(BF16), mk[u])
             for u, (s, h, hs) in enumerate(head_units)]
    n_new = [_dot(w_state_b[s], mk[u])[h:h + 1, :] for u, (s, h, hs) in enumerate(head_units)]
    for u, (s, p, ps) in enumerate(pair_units):
        r_state[s, p] = cd_ref[p] * r_prev[u] + jnp.where(blockdiag, r_new[u], 0.0)
    for u, (s, h, hs) in enumerate(head_units):
        s_state[s, h] = hrow(dec[s], h) * st_prev[u] + s_new[u]
        n_state[s, h] = hrow(dec[s], h) * n_prev[u] + n_new[u]

    s_a = [(s_a[u] * decay_ref[2 * p]).astype(BF16) for u, (s, p, ps) in enumerate(pair_units)]
    s_b = [(s_b[u] * decay_ref[2 * p + 1]).astype(BF16) for u, (s, p, ps) in enumerate(pair_units)]
    p_t = [jnp.exp(jnp.where(before, beta_t[s][:, h:h + 1] - hrow(mx[s], h), -jnp.inf)) * a_t[u]
           for u, (s, h, hs) in enumerate(head_units)]

    o = [_dot(s_a[u], rv[u] * lo_b) + _dot(s_b[u], rv[u] * hi_b) + r_read[u] * wq_ref[p]
         for u, (s, p, ps) in enumerate(pair_units)]
    num_t = [_dot(mv_t[u].astype(BF16), p_t[u].astype(BF16)) + hrow(w_inter[s], h) * s_read[u]
             for u, (s, h, hs) in enumerate(head_units)]

    for u, (s, p, ps) in enumerate(pair_units):
        sq = o[u] * o[u]
        ms_a = jnp.sum(jnp.where(lo, sq, 0.0), axis=-1, keepdims=True)
        ms_b = jnp.sum(jnp.where(lo, 0.0, sq), axis=-1, keepdims=True)
        ms = jnp.where(lo, ms_a, ms_b) * (1.0 / ret_dh)
        r = o[u] * lax.rsqrt(ms + EPS) * retgn_ref[:, ps]
        y_ref[s, :, ps] = (rg_ref[s, :, ps] * r).astype(y_ref.dtype)
    for u, (s, h, hs) in enumerate(head_units):
        den = jnp.sum(p_t[u], axis=0, keepdims=True) + hrow(w_inter[s], h) * qn[u]
        hh = (num_t[u] * (1.0 / jnp.maximum(jnp.abs(den), hrow(e_negm[s], h)))).T
        hm = _rms(mo_ref[s, :, hs] * hh, mlgn_ref[:, hs])
        y_ref[s, :, ret_w + h * LANES:ret_w + (h + 1) * LANES] = hm.astype(y_ref.dtype)


def _ret_tables(C):
    H = RET_HEADS
    dh = LANES // 2
    log_gamma = jnp.log1p(-(2.0 ** (-5.0 - jnp.arange(H, dtype=F32))))
    idx = jnp.arange(C, dtype=F32)
    rel = idx[:, None] - idx[None, :]
    causal = rel >= 0
    decay = jnp.where(causal, jnp.exp(log_gamma[:, None, None] * jnp.where(causal, rel, 0.0)), 0.0)
    w_state = jnp.exp(log_gamma[:, None] * (C - 1 - idx))
    w_query = jnp.exp(log_gamma[:, None] * (idx + 1.0))
    chunk_decay = jnp.exp(log_gamma * C)
    pair = lambda t: jnp.repeat(t.reshape(H // 2, 2, C).transpose(0, 2, 1), dh, axis=2)
    cd = jnp.repeat(chunk_decay.reshape(H // 2, 2), dh, axis=1)
    cd = jnp.broadcast_to(cd[:, :, None], (H // 2, LANES, LANES))
    ws = pair(w_state).transpose(1, 0, 2).reshape(C, (H // 2) * LANES)
    return decay, ws, pair(w_query), cd


def _mixer(ops, decay, wq, cd, ret_gn, ml_gn, B, S, ret_w, ml_w):
    C = CHUNK
    N = S // C
    n_pairs = ret_w // LANES
    ml_heads = ml_w // LANES
    c2 = lambda b, n: (0, 0)
    c3 = lambda b, n: (0, 0, 0)
    BB = MIXER_SEQS if B % MIXER_SEQS == 0 else 1
    tok = lambda w: pl.BlockSpec((BB, C, w), lambda b, n: (b, n, 0))
    tok_t = lambda r: pl.BlockSpec((BB, r, C), lambda b, n: (b, 0, n))
    y = pl.pallas_call(
        functools.partial(_mixer_kernel, ret_w=ret_w, ml_w=ml_w),
        grid=(B // BB, N),
        in_specs=[
            tok(ret_w), tok(ret_w), tok_t(ret_w), tok(ret_w), tok(ret_w),
            tok(ml_w), tok(ml_w), tok_t(ml_w), tok(ml_w), tok_t(GATE_ROWS),
            pl.BlockSpec((RET_HEADS, C, C), c3),
            pl.BlockSpec((n_pairs, C, LANES), c3),
            pl.BlockSpec((n_pairs, LANES, LANES), c3),
            pl.BlockSpec((1, ret_w), c2),
            pl.BlockSpec((1, ml_w), c2),
        ],
        out_specs=pl.BlockSpec((BB, C, ret_w + ml_w), lambda b, n: (b, n, 0)),
        out_shape=jax.ShapeDtypeStruct((B, S, ret_w + ml_w), BF16),
        scratch_shapes=[
            pltpu.VMEM((BB, n_pairs, LANES, LANES), F32),
            pltpu.VMEM((BB, ml_heads, LANES, LANES), F32),
            pltpu.VMEM((BB, ml_heads, SUBLANES, LANES), F32),
            pltpu.VMEM((BB, SUBLANES, LANES), F32),
        ],
        compiler_params=pltpu.CompilerParams(
            dimension_semantics=("arbitrary", "arbitrary"), vmem_limit_bytes=VMEM_LIMIT),
        name="mixer",
    )(*ops, decay, wq, cd, ret_gn.reshape(1, -1), ml_gn.reshape(1, -1))
    return y.reshape(B * S, ret_w + ml_w)


def _router_kernel(y_ref, x_ref, wo_ref, g_ref, wr_hi_ref, wr_lo_ref, br_ref,
                   h_ref, xn_ref, ri_ref, rf_ref, cnt_ref, run_cnt):
    tm = y_ref.shape[0]

    @pl.when(pl.program_id(0) == 0)
    def _():
        run_cnt[...] = jnp.zeros_like(run_cnt)

    h = x_ref[...] + _dot(y_ref[...], wo_ref[...])
    h_ref[...] = h
    xn = _rms(h, g_ref[...])
    xn_ref[...] = xn.reshape(xn_ref.shape)
    x_hi = xn.astype(BF16)
    x_lo = (xn - x_hi.astype(F32)).astype(BF16)
    logits = (_dot_nt(wr_hi_ref[...], x_hi) + _dot_nt(wr_hi_ref[...], x_lo)
              + _dot_nt(wr_lo_ref[...], x_hi) + br_ref[...])
    big = jnp.int32(LANES)
    neg = -jnp.inf
    gl = logits[GROUP_ROW0:GROUP_ROW0 + 8]
    grow = lax.broadcasted_iota(jnp.int32, gl.shape, 0)
    is_g = grow < N_GROUPS
    gl = jnp.where(is_g, gl, neg)
    gmax = jnp.max(gl, axis=0, keepdims=True)
    gsum = jnp.sum(jnp.where(is_g, jnp.exp(gl - gmax), 0.0), axis=0, keepdims=True)
    p_g = 1.0 / gsum
    g_sel = jnp.min(jnp.where(is_g & (gl == gmax), grow, big), axis=0, keepdims=True)
    el = logits[0:N_EXPERTS]
    erow = lax.broadcasted_iota(jnp.int32, el.shape, 0)
    in_grp = (erow // EXPERTS_PER_GROUP) == g_sel
    el = jnp.where(in_grp, el, neg)
    emax = jnp.max(el, axis=0, keepdims=True)
    eexp = jnp.where(in_grp, jnp.exp(el - emax), 0.0)
    prob = eexp / jnp.sum(eexp, axis=0, keepdims=True)
    pm1 = jnp.where(in_grp, prob, -1.0)
    p1 = jnp.max(pm1, axis=0, keepdims=True)
    i1 = jnp.min(jnp.where(pm1 == p1, erow, big), axis=0, keepdims=True)
    pm2 = jnp.where(erow == i1, -1.0, pm1)
    p2 = jnp.max(pm2, axis=0, keepdims=True)
    i2 = jnp.min(jnp.where(pm2 == p2, erow, big), axis=0, keepdims=True)
    denom = p1 + p2
    g1 = p_g * p1 / denom
    g2 = p_g * p2 / denom

    sel1 = erow == i1
    sel2 = erow == i2
    onehot = (sel1 | sel2).astype(BF16)
    r_i = lax.broadcasted_iota(jnp.int32, (tm, tm), 0)
    c_i = lax.broadcasted_iota(jnp.int32, (tm, tm), 1)
    tri = (r_i < c_i).astype(BF16)
    prefix = _dot(onehot, tri) + run_cnt[:, 0:1]
    rank1 = jnp.sum(jnp.where(sel1, prefix, 0.0), axis=0, keepdims=True).astype(jnp.int32)
    rank2 = jnp.sum(jnp.where(sel2, prefix, 0.0), axis=0, keepdims=True).astype(jnp.int32)
    new_cnt = run_cnt[...] + jnp.sum(onehot.astype(F32), axis=1, keepdims=True)
    run_cnt[...] = new_cnt
    cnt_ref[...] = new_cnt.astype(jnp.int32)

    rrow = lax.broadcasted_iota(jnp.int32, (ROUTE_ROWS, tm), 0)
    ri_ref[...] = jnp.where(rrow == 0, i1, jnp.where(rrow == 1, i2,
                            jnp.where(rrow == 2, rank1, jnp.where(rrow == 3, rank2, 0))))
    lrow = lax.broadcasted_iota(jnp.int32, (LANES, tm), 0)
    rf_ref[...] = jnp.where(lrow == 0, g1, jnp.where(lrow == 1, g2, 0.0)).T


def _router(y, x2, w_out, g, wr_hi, wr_lo, br, tm=ROUTE_TILE):
    T, D = x2.shape
    const = lambda i: (0, 0)
    tile = lambda i: (i, 0)
    return pl.pallas_call(
        _router_kernel,
        grid=(T // tm,),
        in_specs=[
            pl.BlockSpec((tm, y.shape[1]), tile),
            pl.BlockSpec((tm, D), tile),
            pl.BlockSpec(w_out.shape, const),
            pl.BlockSpec((1, D), const),
            pl.BlockSpec((ROUTER_ROWS, D), const),
            pl.BlockSpec((ROUTER_ROWS, D), const),
            pl.BlockSpec((ROUTER_ROWS, 1), const),
        ],
        out_specs=[
            pl.BlockSpec((tm, D), tile),
            pl.BlockSpec((tm,) + ROW_TILE, lambda i: (i, 0, 0)),
            pl.BlockSpec((None, ROUTE_ROWS, tm), lambda i: (i, 0, 0)),
            pl.BlockSpec((tm, LANES), tile),
            pl.BlockSpec((N_EXPERTS, LANES), const),
        ],
        out_shape=[
            jax.ShapeDtypeStruct((T, D), F32),
            jax.ShapeDtypeStruct((T,) + ROW_TILE, F32),
            jax.ShapeDtypeStruct((T // tm, ROUTE_ROWS, tm), jnp.int32),
            jax.ShapeDtypeStruct((T, LANES), F32),
            jax.ShapeDtypeStruct((N_EXPERTS, LANES), jnp.int32),
        ],
        scratch_shapes=[pltpu.VMEM((N_EXPERTS, LANES), F32)],
        compiler_params=pltpu.CompilerParams(
            dimension_semantics=("arbitrary",), vmem_limit_bytes=VMEM_LIMIT),
        name="router",
    )(y, x2, w_out, g, wr_hi, wr_lo, br)


def _slot_base(tok0):
    return (tok0 // ROUTE_TILE) * (2 * ROUTE_TILE) + tok0 % ROUTE_TILE


def _route_spans(tm):
    assert tm % ROUTE_TILE == 0 or ROUTE_TILE % tm == 0
    span = min(tm, ROUTE_TILE)
    return [(t0, span) for t0 in range(0, tm, span)]


def _for_each_pad_block(pend_ref, nu_ref, n_blocks, fn):
    blk = EXPERT_BLOCK
    for e in range(N_EXPERTS):
        prev_end = 0 if e == 0 else pend_ref[e - 1]

        @pl.when(pend_ref[e] > prev_end)
        def _():
            fn(pl.multiple_of(pend_ref[e] - blk, blk))

    def tail(j, c):
        fn(pl.multiple_of(j * blk, blk))
        return c

    lax.fori_loop(nu_ref[0], n_blocks, tail, 0)


def _dispatch_kernel(dest_ref, pend_ref, nu_ref, xn_ref, buf_ref, zeros, sem, zsem):
    tm = xn_ref.shape[0]
    blk = EXPERT_BLOCK
    i = pl.program_id(0)

    @pl.when(i == 0)
    def _():
        zeros[...] = jnp.zeros_like(zeros)
        zcopy = lambda row: pltpu.make_async_copy(zeros, buf_ref.at[pl.ds(row, blk)], zsem)
        n_blocks = buf_ref.shape[0] // blk
        _for_each_pad_block(pend_ref, nu_ref, n_blocks, lambda row: zcopy(row).start())
        _for_each_pad_block(pend_ref, nu_ref, n_blocks, lambda row: zcopy(row).wait())

    for t0, span in _route_spans(tm):
        slot0 = _slot_base(i * tm + t0)

        def issue(g, c, t0=t0, slot0=slot0):
            for j in range(SUBLANES):
                for k in range(2):
                    t = g * SUBLANES + j
                    d = dest_ref[slot0 + k * ROUTE_TILE + t]
                    pltpu.make_async_copy(xn_ref.at[t0 + t], buf_ref.at[d], sem).start(priority=k)
            return c

        lax.fori_loop(0, span // SUBLANES, issue, 0)
    for k in range(2):
        pltpu.make_async_copy(buf_ref.at[pl.ds(tm, tm)], buf_ref.at[pl.ds(0, tm)], sem).wait()


def _dispatch(dest, pad_end, n_used, xn, n_rows, tm=ROUTE_TILE):
    T = xn.shape[0]
    return pl.pallas_call(
        _dispatch_kernel,
        grid_spec=pltpu.PrefetchScalarGridSpec(
            num_scalar_prefetch=3,
            grid=(T // tm,),
            in_specs=[pl.BlockSpec((tm,) + ROW_TILE, lambda i, *_: (i, 0, 0))],
            out_specs=pl.BlockSpec(memory_space=pl.ANY),
            scratch_shapes=[
                pltpu.VMEM((EXPERT_BLOCK,) + ROW_TILE, xn.dtype),
                pltpu.SemaphoreType.DMA,
                pltpu.SemaphoreType.DMA,
            ],
        ),
        out_shape=jax.ShapeDtypeStruct((n_rows,) + ROW_TILE, xn.dtype),
        compiler_params=pltpu.CompilerParams(
            dimension_semantics=("arbitrary",), vmem_limit_bytes=VMEM_LIMIT),
        name="dispatch",
    )(dest, pad_end, n_used, xn)


def _expert_kernel(be_ref, nu_ref, x_ref, w1_ref, w3_ref, w2_ref, y_ref, w1b, w3b, w2b):
    j = pl.program_id(0)
    used = j < nu_ref[0]
    prev_expert = be_ref[jnp.maximum(j, 1) - 1]

    @pl.when(used & ((j == 0) | (be_ref[j] != prev_expert)))
    def _():
        w1b[...] = w1_ref[...].astype(BF16)
        w3b[...] = w3_ref[...].astype(BF16)
        w2b[...] = w2_ref[...].astype(BF16)

    @pl.when(used)
    def _():
        x = x_ref[...].reshape(x_ref.shape[0], -1).astype(BF16)
        a = _dot(x, w1b[...])
        hmid = a * jax.nn.sigmoid(a) * _dot(x, w3b[...])
        y_ref[...] = _dot(hmid.astype(BF16), w2b[...]).reshape(y_ref.shape)

    @pl.when(jnp.logical_not(used))
    def _():
        y_ref[...] = jnp.zeros_like(y_ref)


def _experts(block_expert, n_used, x_buf, w1, w3, w2):
    P = x_buf.shape[0]
    D = w1.shape[1]
    assert x_buf.shape[1:] == ROW_TILE and D == SUBLANES * LANES
    blk = EXPERT_BLOCK
    d_exp = w2.shape[1]
    clamp = lambda j, nu: jnp.minimum(j, nu[0] - 1)
    w_idx = lambda j, be, nu: (be[clamp(j, nu)], 0, 0)
    return pl.pallas_call(
        _expert_kernel,
        grid_spec=pltpu.PrefetchScalarGridSpec(
            num_scalar_prefetch=2,
            grid=(P // blk,),
            in_specs=[
                pl.BlockSpec((blk,) + ROW_TILE, lambda j, be, nu: (clamp(j, nu), 0, 0)),
                pl.BlockSpec((None, D, d_exp), w_idx),
                pl.BlockSpec((None, D, d_exp), w_idx),
                pl.BlockSpec((None, d_exp, D), w_idx),
            ],
            out_specs=pl.BlockSpec((blk,) + ROW_TILE, lambda j, be, nu: (j, 0, 0)),
            scratch_shapes=[
                pltpu.VMEM((D, d_exp), BF16),
                pltpu.VMEM((D, d_exp), BF16),
                pltpu.VMEM((d_exp, D), BF16),
            ],
        ),
        out_shape=jax.ShapeDtypeStruct((P,) + ROW_TILE, F32),
        compiler_params=pltpu.CompilerParams(
            dimension_semantics=("arbitrary",), vmem_limit_bytes=VMEM_LIMIT),
        name="experts",
    )(block_expert, n_used, x_buf, w1, w3, w2)


def _final_kernel(dest_ref, h_ref, rf_ref, p_ref, wup_ref, gple_ref, ggate_ref, wgate_ref,
                  gfin_ref, ybuf_ref, out_ref, *scratch, tm):
    row_bufs, sems = scratch[:-1], scratch[-1]
    s = pl.program_id(0)
    D = h_ref.shape[1]
    assert ROUTE_TILE % tm == 0

    def gather(tile, bufs, sem, straight=False):
        slot0 = _slot_base(tile * tm)

        def issue(g, c):
            for j in range(SUBLANES):
                for k in range(2):
                    t = g * SUBLANES + j
                    d = dest_ref[slot0 + k * ROUTE_TILE + t]
                    pltpu.make_async_copy(ybuf_ref.at[d], bufs[k].at[t], sem).start(priority=k)
            return c

        if straight:
            for g in range(tm // SUBLANES):
                issue(g, 0)
        else:
            lax.fori_loop(0, tm // SUBLANES, issue, 0)

    def wait(sem):
        for k in range(2):
            pltpu.make_async_copy(ybuf_ref.at[pl.ds(0, tm)], ybuf_ref.at[pl.ds(tm, tm)], sem).wait()

    def compute(rows, bufs):
        e = _rms(_dot(p_ref[rows, :].astype(BF16), wup_ref[...]), gple_ref[...])
        rf = rf_ref[rows, :]
        h = (h_ref[rows, :] + rf[:, 0:1] * bufs[0][...].reshape(tm, D)
             + rf[:, 1:2] * bufs[1][...].reshape(tm, D))
        gate = jax.nn.sigmoid(_dot(_rms(h, ggate_ref[...]).astype(BF16), wgate_ref[...]))
        h = h + gate * e
        out_ref[rows, :] = _rms(h, gfin_ref[...])

    n_tiles = pl.num_programs(0) * FINAL_TILES
    bufs = [(row_bufs[2 * i], row_bufs[2 * i + 1]) for i in range(FINAL_TILES)]

    @pl.when(s == 0)
    def _():
        for i in range(FINAL_AHEAD):
            gather(i, bufs[i], sems.at[i])

    for i in range(FINAL_TILES):
        wait(sems.at[i])
        nxt = (i + FINAL_AHEAD) % FINAL_TILES
        gather(jnp.minimum(s * FINAL_TILES + i + FINAL_AHEAD, n_tiles - 1), bufs[nxt],
               sems.at[nxt], straight=True)
        compute(slice(i * tm, (i + 1) * tm), bufs[i])

    @pl.when(s == pl.num_programs(0) - 1)
    def _():
        for i in range(FINAL_AHEAD):
            wait(sems.at[i])


def _final(dest, h, rf, p2, w_up, g_ple, g_gate, w_gate, g_fin, y_buf, tm=256):
    T, D = h.shape
    const = lambda i, d: (0, 0)
    tile = lambda i, d: (i, 0)
    assert y_buf.shape[1:] == ROW_TILE and D == SUBLANES * LANES
    row_buf = pltpu.VMEM((tm,) + ROW_TILE, F32)
    return pl.pallas_call(
        functools.partial(_final_kernel, tm=tm),
        grid_spec=pltpu.PrefetchScalarGridSpec(
            num_scalar_prefetch=1,
            grid=(T // (FINAL_TILES * tm),),
            in_specs=[
                pl.BlockSpec((FINAL_TILES * tm, D), tile),
                pl.BlockSpec((FINAL_TILES * tm, LANES), tile),
                pl.BlockSpec((FINAL_TILES * tm, p2.shape[1]), tile),
                pl.BlockSpec(w_up.shape, const),
                pl.BlockSpec((1, D), const),
                pl.BlockSpec((1, D), const),
                pl.BlockSpec(w_gate.shape, const),
                pl.BlockSpec((1, D), const),
                pl.BlockSpec(memory_space=pl.ANY),
            ],
            out_specs=pl.BlockSpec((FINAL_TILES * tm, D), tile),
            scratch_shapes=[row_buf] * (2 * FINAL_TILES) + [pltpu.SemaphoreType.DMA((FINAL_TILES,))],
        ),
        out_shape=jax.ShapeDtypeStruct((T, D), F32),
        compiler_params=pltpu.CompilerParams(
            dimension_semantics=("arbitrary",), vmem_limit_bytes=VMEM_LIMIT),
        name="final",
    )(dest, h, rf, p2, w_up, g_ple, g_gate, w_gate, g_fin, y_buf)


def _layer(h2, p2, positions, B, S, attn_norm, w_in, conv_w, conv_b, b_igate, b_fgate, ret_gn,
           ml_gn, w_out, moe_norm, w_group, b_group, w_router, b_router, w1, w3, w2, w_ple_up,
           ple_norm, ple_gate_norm, w_ple_gate, out_norm):
    T, D = h2.shape
    ret_w = ret_gn.shape[0]
    ml_w = ml_gn.shape[0]
    n_main = 4 * ret_w + 4 * ml_w
    row = lambda v: v.reshape(1, -1).astype(F32)
    pad_lanes = lambda a: jnp.pad(a, ((0, 0), (0, LANES - a.shape[1])))

    ml_heads = ml_w // LANES
    w_main = w_in[:, :n_main].astype(BF16)
    gate_rows = lambda a: jnp.pad(a, ((0, SUBLANES - ml_heads), (0, 0)))
    w_i, w_f = w_in[:, n_main:n_main + ml_heads].T, w_in[:, n_main + ml_heads:].T
    w_gate_t = jnp.concatenate([gate_rows(w_i), gate_rows(w_f)]).astype(BF16)
    gate_bias = jnp.concatenate([gate_rows(b_igate[:, None]), gate_rows(b_fgate[:, None])])
    cos_t, sin_t = _rope_tables(positions, LANES // 2)
    decay, ws, wq, cd = _ret_tables(CHUNK)
    tm_in = 2 * CHUNK
    ops = _inproj(h2.reshape(B, S, D), row(attn_norm), w_main, w_gate_t, gate_bias.astype(F32),
                  cos_t.reshape(B, S, LANES), sin_t.reshape(B, S, LANES),
                  jnp.tile(ws, (tm_in // CHUNK, 1)), conv_w, conv_b.reshape(1, -1), ret_w, ml_w,
                  tm=tm_in)

    y = _mixer(ops, decay, wq, cd, ret_gn, ml_gn, B, S, ret_w, ml_w)

    pad_rows = lambda a: jnp.pad(a, ((0, ROUTER_ROWS - a.shape[0]), (0, 0)))
    wr = pad_rows(jnp.concatenate([w_router, w_group], axis=1).T)
    wr_hi = wr.astype(BF16)
    wr_lo = (wr - wr_hi.astype(F32)).astype(BF16)
    br = pad_rows(jnp.concatenate([b_router, b_group])[:, None].astype(F32))
    h_mid, xn, ri, rf, counts = _router(y, h2, w_out.astype(BF16), row(moe_norm), wr_hi, wr_lo, br)

    blk = EXPERT_BLOCK
    counts = counts[:, 0]
    padded = (counts + blk - 1) // blk * blk
    pad_end = jnp.cumsum(padded)
    pad_start = pad_end - padded
    n_blocks = (2 * T) // blk + N_EXPERTS
    block_start = jnp.arange(n_blocks, dtype=jnp.int32) * blk
    block_expert = jnp.minimum(jnp.sum(pad_end[None, :] <= block_start[:, None], axis=1),
                               N_EXPERTS - 1).astype(jnp.int32)
    n_used = (pad_end[-1:] // blk).astype(jnp.int32)
    dest = ri[:, 2:4, :]
    for e in range(N_EXPERTS):
        dest = dest + jnp.where(ri[:, 0:2, :] == e, pad_start[e], 0)
    dest = dest.astype(jnp.int32).reshape(-1)

    x_buf = _dispatch(dest, pad_end.astype(jnp.int32), n_used, xn, n_blocks * blk,
                      tm=min(T, DISPATCH_TILE))
    y_buf = _experts(block_expert, n_used, x_buf, w1, w3, w2)

    return _final(dest, h_mid, rf, p2, w_ple_up.astype(BF16), row(ple_norm), row(ple_gate_norm),
                  w_ple_gate.astype(BF16), row(out_norm), y_buf)


def kernel(x, p, positions, attn_norm, w_in, conv_w, conv_b, b_igate, b_fgate, ret_gn, ml_gn,
           w_out, moe_norm, w_group, b_group, w_router, b_router, w1, w3, w2, w_ple_up, ple_norm,
           ple_gate_norm, w_ple_gate, final_norm):
    B, S, D = x.shape
    depth = p.shape[0]
    assert depth == 1, "the final RMSNorm is fused into the layer's last kernel"
    out = _layer(x.reshape(B * S, D), p[0].reshape(B * S, -1), positions, B, S,
                 attn_norm[0], w_in[0], conv_w[0], conv_b[0], b_igate[0], b_fgate[0], ret_gn[0],
                 ml_gn[0], w_out[0], moe_norm[0], w_group[0], b_group[0], w_router[0],
                 b_router[0], w1[0], w3[0], w2[0], w_ple_up[0], ple_norm[0], ple_gate_norm[0],
                 w_ple_gate[0], final_norm)
    return out.reshape(B, S, D)
```

```python
import functools

import jax
import jax.numpy as jnp
from jax import lax
from jax.experimental import pallas as pl
from jax.experimental.pallas import tpu as pltpu

F32 = jnp.float32
BF16 = jnp.bfloat16

RET_HEADS = 8
ML_HEADS = 4
CHUNK = 128
CONV_W = 4
ROPE_BASE = 10000.0
N_GROUPS = 4
EXPERTS_PER_GROUP = 8
N_EXPERTS = N_GROUPS * EXPERTS_PER_GROUP
EPS = 1e-6

LANES = 128
SUBLANES = 8
MXU_COLS = 256
MIXER_SEQS = 2
FINAL_TILES = 4
FINAL_AHEAD = 2
VMEM_LIMIT = 56 * 1024 * 1024
EXPERT_BLOCK = 512
GROUP_ROW0 = N_EXPERTS
ROUTER_ROWS = 64
ROUTE_TILE = 1024
ROUTE_ROWS = 8
DISPATCH_TILE = 2 * ROUTE_TILE
GATE_ROWS = 2 * SUBLANES
ROW_TILE = (SUBLANES, LANES)


def _rms(x, g):
    return x * lax.rsqrt(jnp.mean(x * x, axis=-1, keepdims=True) + EPS) * g


def _dot(a, b):
    return jnp.dot(a, b, preferred_element_type=F32)


def _dot_nt(a, b):
    return lax.dot_general(a, b, (((1,), (1,)), ((), ())), preferred_element_type=F32)


def _split3(x):
    hi = x.astype(BF16)
    r1 = x - hi.astype(F32)
    mid = r1.astype(BF16)
    lo = (r1 - mid.astype(F32)).astype(BF16)
    return hi, mid, lo


def _rope_kernel(pos_ref, freq_ref, sign_ref, cos_ref, sin_ref, *, half):
    ang = pos_ref[...].astype(F32) * freq_ref[...]
    cos_c = jnp.cos(ang)
    sin_c = jnp.sin(ang)
    group = lax.broadcasted_iota(jnp.int32, ang.shape, 1) // half
    per_row = LANES // half
    for s in range(per_row):
        def spread(t):
            out = t
            for g in range(per_row):
                if g != s:
                    out = jnp.where(group == g, pltpu.roll(t, (half * (g - s)) % LANES, 1), out)
            return out
        cos_ref[s] = spread(cos_c)
        sin_ref[s] = spread(sin_c) * sign_ref[...]


def _rope_tables(positions, dh):
    half = dh // 2
    per_row = LANES // half
    T = positions.size
    rows = T // per_row
    tr = min(rows, 1024)
    freqs = ROPE_BASE ** (-jnp.arange(half, dtype=F32) / half)
    pos_c = jnp.repeat(positions.reshape(per_row, rows).T, half, axis=1)
    sign = jnp.tile(jnp.concatenate([-jnp.ones((half,), F32), jnp.ones((half,), F32)]),
                    LANES // dh)[None, :]
    const = pl.BlockSpec((1, LANES), lambda i: (0, 0))
    out = pl.BlockSpec((per_row, tr, LANES), lambda i: (0, i, 0))
    cos_t, sin_t = pl.pallas_call(
        functools.partial(_rope_kernel, half=half),
        grid=(rows // tr,),
        in_specs=[pl.BlockSpec((tr, LANES), lambda i: (i, 0)), const, const],
        out_specs=[out, out],
        out_shape=[jax.ShapeDtypeStruct((per_row, rows, LANES), F32)] * 2,
        name="rope",
    )(pos_c, jnp.tile(freqs, per_row)[None, :], sign)
    return cos_t.reshape(T, LANES), sin_t.reshape(T, LANES)


def _inproj_kernel(x_ref, g_ref, wm_ref, wgt_ref, gb_ref, cos_ref, sin_ref, ws_ref, convw_ref,
                   convb_ref, rq_ref, rk_ref, rkwt_ref, rv_ref, rg_ref, mq_ref, mk_ref, mvt_ref,
                   mo_ref, gt_ref, carry, *, ret_w, ml_w):
    tm = x_ref.shape[0]
    ret_dh = LANES // 2

    @pl.when(pl.program_id(1) == 0)
    def _():
        carry[...] = jnp.zeros_like(carry)

    xn = _rms(x_ref[...], g_ref[...]).astype(BF16)
    proj = lambda o, w: _dot(xn, wm_ref[:, o:o + w])
    tiles = lambda w: [slice(t, t + LANES) for t in range(0, w, LANES)]

    lane = lax.broadcasted_iota(jnp.int32, (tm, LANES), 1)
    first_half = (lane % ret_dh) < (ret_dh // 2)
    cos_t = cos_ref[...]
    sin_t = sin_ref[...]

    def rot(t):
        swapped = jnp.where(first_half, pltpu.roll(t, LANES - ret_dh // 2, 1),
                            pltpu.roll(t, ret_dh // 2, 1))
        return t * cos_t + swapped * sin_t

    for c0 in range(0, ret_w, MXU_COLS):
        rq = proj(c0, MXU_COLS)
        for ps in tiles(MXU_COLS):
            rq_ref[:, c0 + ps.start:c0 + ps.stop] = rot(rq[:, ps]).astype(BF16)
    for c0 in range(0, ret_w, MXU_COLS):
        rk = proj(ret_w + c0, MXU_COLS)
        for ps in tiles(MXU_COLS):
            cs = slice(c0 + ps.start, c0 + ps.stop)
            k = rot(rk[:, ps]) * (ret_dh ** -0.5)
            rk_ref[:, cs] = k.astype(BF16)
            rkwt_ref[cs, :] = (k * ws_ref[:, cs]).T.astype(BF16)
    for c0 in range(0, ret_w, MXU_COLS):
        cs = slice(c0, c0 + MXU_COLS)
        rv_ref[:, cs] = proj(2 * ret_w + c0, MXU_COLS).astype(BF16)
    for c0 in range(0, ret_w, MXU_COLS):
        g = proj(3 * ret_w + c0, MXU_COLS)
        rg_ref[:, c0:c0 + MXU_COLS] = g * jax.nn.sigmoid(g)

    o_mq = 4 * ret_w
    row8 = lax.broadcasted_iota(jnp.int32, (SUBLANES, MXU_COLS), 0)
    for c0 in range(0, 2 * ml_w, MXU_COLS):
        cs = slice(c0, c0 + MXU_COLS)
        xq = proj(o_mq + c0, MXU_COLS)
        prev8 = carry[:, cs]
        carry[:, cs] = xq[tm - SUBLANES:tm]
        acc = xq * convw_ref[CONV_W - 1:CONV_W, cs] + convb_ref[:, cs]
        for s in range(1, CONV_W):
            rolled = pltpu.roll(xq, s, 0)
            head = jnp.where(row8 < s, pltpu.roll(prev8, s, 0), rolled[0:SUBLANES])
            shifted = jnp.concatenate([head, rolled[SUBLANES:]], axis=0)
            acc = acc + shifted * convw_ref[CONV_W - 1 - s:CONV_W - s, cs]
        act = acc * jax.nn.sigmoid(acc)
        if c0 < ml_w:
            mq_ref[:, cs] = act.astype(BF16)
        else:
            mk_ref[:, c0 - ml_w:c0 - ml_w + MXU_COLS] = (act * (LANES ** -0.5)).astype(BF16)
    for c0 in range(0, ml_w, MXU_COLS):
        mv = proj(o_mq + 2 * ml_w + c0, MXU_COLS)
        for hs in tiles(MXU_COLS):
            mvt_ref[c0 + hs.start:c0 + hs.stop, :] = mv[:, hs].T
    for c0 in range(0, ml_w, MXU_COLS):
        mo_ref[:, c0:c0 + MXU_COLS] = jax.nn.sigmoid(proj(o_mq + 3 * ml_w + c0, MXU_COLS))
    gt_ref[...] = _dot_nt(wgt_ref[...], xn) + gb_ref[...]


def _inproj(x3, g, w_main, w_gate_t, gate_bias, cos_t, sin_t, ws, conv_w, conv_b, ret_w, ml_w,
            tm=256):
    B, S, D = x3.shape
    n_main = w_main.shape[1]
    const = lambda b, i: (0, 0)
    tok = lambda w: pl.BlockSpec((None, tm, w), lambda b, i: (b, i, 0))
    tok_t = lambda r: pl.BlockSpec((None, r, tm), lambda b, i: (b, 0, i))
    act = lambda w, dt: jax.ShapeDtypeStruct((B, S, w), dt)
    act_t = lambda r, dt: jax.ShapeDtypeStruct((B, r, S), dt)
    return pl.pallas_call(
        functools.partial(_inproj_kernel, ret_w=ret_w, ml_w=ml_w),
        grid=(B, S // tm),
        in_specs=[
            tok(D),
            pl.BlockSpec((1, D), const),
            pl.BlockSpec((D, n_main), const),
            pl.BlockSpec((GATE_ROWS, D), const),
            pl.BlockSpec((GATE_ROWS, 1), const),
            tok(LANES),
            tok(LANES),
            pl.BlockSpec((tm, ret_w), const),
            pl.BlockSpec((CONV_W, 2 * ml_w), const),
            pl.BlockSpec((1, 2 * ml_w), const),
        ],
        out_specs=[tok(ret_w), tok(ret_w), tok_t(ret_w), tok(ret_w), tok(ret_w),
                   tok(ml_w), tok(ml_w), tok_t(ml_w), tok(ml_w), tok_t(GATE_ROWS)],
        out_shape=[
            act(ret_w, BF16),
            act(ret_w, BF16),
            act_t(ret_w, BF16),
            act(ret_w, BF16),
            act(ret_w, F32),
            act(ml_w, BF16),
            act(ml_w, BF16),
            act_t(ml_w, F32),
            act(ml_w, F32),
            act_t(GATE_ROWS, F32),
        ],
        scratch_shapes=[pltpu.VMEM((SUBLANES, 2 * ml_w), F32)],
        compiler_params=pltpu.CompilerParams(
            dimension_semantics=("arbitrary", "arbitrary"), vmem_limit_bytes=VMEM_LIMIT),
        name="inproj",
    )(x3, g, w_main, w_gate_t, gate_bias, cos_t, sin_t, ws, conv_w, conv_b)


def _mixer_kernel(rq_ref, rk_ref, rkwt_ref, rv_ref, rg_ref, mq_ref, mk_ref, mvt_ref, mo_ref,
                  gt_ref, decay_ref, wq_ref, cd_ref, retgn_ref, mlgn_ref,
                  y_ref, r_state, s_state, n_state, m_state, *, ret_w, ml_w):
    BB, C = rq_ref.shape[0], rq_ref.shape[1]
    n_pairs = ret_w // LANES
    ml_heads = ml_w // LANES
    ret_dh = LANES // 2

    @pl.when(pl.program_id(1) == 0)
    def _():
        r_state[...] = jnp.zeros_like(r_state)
        s_state[...] = jnp.zeros_like(s_state)
        n_state[...] = jnp.zeros_like(n_state)
        m_state[...] = jnp.zeros_like(m_state)

    lane = lax.broadcasted_iota(jnp.int32, (C, LANES), 1)
    row = lax.broadcasted_iota(jnp.int32, (C, LANES), 0)
    assert C == LANES
    lo = lane < ret_dh
    blockdiag = (row < ret_dh) == lo
    lo_b = jnp.where(lo, 1.0, 0.0).astype(BF16)
    hi_b = jnp.where(lo, 0.0, 1.0).astype(BF16)
    seqs = range(BB)
    pair_units = [(s, p, slice(p * LANES, (p + 1) * LANES)) for s in seqs for p in range(n_pairs)]
    head_units = [(s, h, slice(h * LANES, (h + 1) * LANES)) for s in seqs for h in range(ml_heads)]


    row8 = lax.broadcasted_iota(jnp.int32, (SUBLANES, C), 0)
    lane8 = lax.broadcasted_iota(jnp.int32, (SUBLANES, C), 1)
    live = row8 < ml_heads
    triu = (row <= lane).astype(BF16)
    before = row <= lane
    mx, w_inter, e_negm, w_state, w_state_b, dec, beta_t = [], [], [], [], [], [], []
    for s in seqs:
        ig = jnp.where(live, gt_ref[s, 0:SUBLANES, :], 0.0)
        f_pre = jnp.where(live, gt_ref[s, SUBLANES:2 * SUBLANES, :], 30.0)
        l_hi, l_mid, l_lo = _split3(jax.nn.log_sigmoid(f_pre))
        b = _dot(l_hi, triu) + _dot(l_mid, triu) + _dot(l_lo, triu)
        beta = ig - b
        cm = beta
        shift = 1
        while shift < C:
            cm = jnp.maximum(cm, jnp.where(lane8 >= shift, pltpu.roll(cm, shift, 1), -jnp.inf))
            shift *= 2
        m_prev = m_state[s]
        mx.append(jnp.maximum(cm, m_prev))
        mx_last = jnp.broadcast_to(mx[s][:, C - 1:C], (SUBLANES, C))
        w_inter.append(jnp.exp(m_prev - mx[s]))
        e_negm.append(jnp.exp(-(b + mx[s])))
        w_state.append(jnp.exp(beta - mx_last))
        w_state_b.append(w_state[s].astype(BF16))
        dec.append(jnp.exp(m_prev - mx_last))
        m_state[s] = jnp.where(live, jnp.broadcast_to(b[:, C - 1:C], (SUBLANES, C)) + mx_last, 0.0)
        beta_t.append(jnp.concatenate([beta, jnp.zeros((LANES - SUBLANES, C), F32)], axis=0).T)

    hrow = lambda t, h: t[h:h + 1, :]

    rq = [rq_ref[s, :, ps] for s, p, ps in pair_units]
    rk = [rk_ref[s, :, ps] for s, p, ps in pair_units]
    rv = [rv_ref[s, :, ps] for s, p, ps in pair_units]
    r_prev = [r_state[s, p] for s, p, ps in pair_units]
    pu = range(len(pair_units))
    s_a = [_dot_nt(rq[u] * lo_b, rk[u]) for u in pu]
    s_b = [_dot_nt(rq[u] * hi_b, rk[u]) for u in pu]
    r_read = [_dot(rq[u], r_prev[u].astype(BF16)) for u in pu]
    r_new = [_dot(rkwt_ref[s, ps, :], rv[u]) for u, (s, p, ps) in enumerate(pair_units)]
    hu = range(len(head_units))
    mq = [mq_ref[s, :, hs] for s, h, hs in head_units]
    mk = [mk_ref[s, :, hs] for s, h, hs in head_units]
    mv_t = [mvt_ref[s, hs, :] for s, h, hs in head_units]
    st_prev = [s_state[s, h] for s, h, hs in head_units]
    n_prev = [n_state[s, h] for s, h, hs in head_units]
    a_t = [_dot_nt(mk[u], mq[u]) for u in hu]
    s_read = [_dot_nt(st_prev[u].astype(BF16), mq[u]) for u in hu]
    qn = [_dot_nt(n_prev[u].astype(BF16), mq[u])[0:1, :] for u in hu]
    s_new = [_dot((mv_t[u] * hrow(w_state[s], h)).astype(BF16), mk[u])
             for u, (s, h, hs) in enumerate(head_units)]
    n_new = [_dot(w_state_b[s], mk[u])[h:h + 1, :] for u, (s, h, hs) in enumerate(head_units)]
    for u, (s, p, ps) in enumerate(pair_units):
        r_state[s, p] = cd_ref[p] * r_prev[u] + jnp.where(blockdiag, r_new[u], 0.0)
    for u, (s, h, hs) in enumerate(head_units):
        s_state[s, h] = hrow(dec[s], h) * st_prev[u] + s_new[u]
        n_state[s, h] = hrow(dec[s], h) * n_prev[u] + n_new[u]

    s_a = [(s_a[u] * decay_ref[2 * p]).astype(BF16) for u, (s, p, ps) in enumerate(pair_units)]
    s_b = [(s_b[u] * decay_ref[2 * p + 1]).astype(BF16) for u, (s, p, ps) in enumerate(pair_units)]
    p_t = [jnp.exp(jnp.where(before, beta_t[s][:, h:h + 1] - hrow(mx[s], h), -jnp.inf)) * a_t[u]
           for u, (s, h, hs) in enumerate(head_units)]

    o = [_dot(s_a[u], rv[u] * lo_b) + _dot(s_b[u], rv[u] * hi_b) + r_read[u] * wq_ref[p]
         for u, (s, p, ps) in enumerate(pair_units)]
    num_t = [_dot(mv_t[u].astype(BF16), p_t[u].astype(BF16)) + hrow(w_inter[s], h) * s_read[u]
             for u, (s, h, hs) in enumerate(head_units)]

    for u, (s, p, ps) in enumerate(pair_units):
        sq = o[u] * o[u]
        ms_a = jnp.sum(jnp.where(lo, sq, 0.0), axis=-1, keepdims=True)
        ms_b = jnp.sum(jnp.where(lo, 0.0, sq), axis=-1, keepdims=True)
        ms = jnp.where(lo, ms_a, ms_b) * (1.0 / ret_dh)
        r = o[u] * lax.rsqrt(ms + EPS) * retgn_ref[:, ps]
        y_ref[s, :, ps] = (rg_ref[s, :, ps] * r).astype(y_ref.dtype)
    for u, (s, h, hs) in enumerate(head_units):
        den = jnp.sum(p_t[u], axis=0, keepdims=True) + hrow(w_inter[s], h) * qn[u]
        hh = (num_t[u] * (1.0 / jnp.maximum(jnp.abs(den), hrow(e_negm[s], h)))).T
        hm = _rms(mo_ref[s, :, hs] * hh, mlgn_ref[:, hs])
        y_ref[s, :, ret_w + h * LANES:ret_w + (h + 1) * LANES] = hm.astype(y_ref.dtype)


def _ret_tables(C):
    H = RET_HEADS
    dh = LANES // 2
    log_gamma = jnp.log1p(-(2.0 ** (-5.0 - jnp.arange(H, dtype=F32))))
    idx = jnp.arange(C, dtype=F32)
    rel = idx[:, None] - idx[None, :]
    causal = rel >= 0
    decay = jnp.where(causal, jnp.exp(log_gamma[:, None, None] * jnp.where(causal, rel, 0.0)), 0.0)
    w_state = jnp.exp(log_gamma[:, None] * (C - 1 - idx))
    w_query = jnp.exp(log_gamma[:, None] * (idx + 1.0))
    chunk_decay = jnp.exp(log_gamma * C)
    pair = lambda t: jnp.repeat(t.reshape(H // 2, 2, C).transpose(0, 2, 1), dh, axis=2)
    cd = jnp.repeat(chunk_decay.reshape(H // 2, 2), dh, axis=1)
    cd = jnp.broadcast_to(cd[:, :, None], (H // 2, LANES, LANES))
    ws = pair(w_state).transpose(1, 0, 2).reshape(C, (H // 2) * LANES)
    return decay, ws, pair(w_query), cd


def _mixer(ops, decay, wq, cd, ret_gn, ml_gn, B, S, ret_w, ml_w):
    C = CHUNK
    N = S // C
    n_pairs = ret_w // LANES
    ml_heads = ml_w // LANES
    c2 = lambda b, n: (0, 0)
    c3 = lambda b, n: (0, 0, 0)
    BB = MIXER_SEQS if B % MIXER_SEQS == 0 else 1
    tok = lambda w: pl.BlockSpec((BB, C, w), lambda b, n: (b, n, 0))
    tok_t = lambda r: pl.BlockSpec((BB, r, C), lambda b, n: (b, 0, n))
    y = pl.pallas_call(
        functools.partial(_mixer_kernel, ret_w=ret_w, ml_w=ml_w),
        grid=(B // BB, N),
        in_specs=[
            tok(ret_w), tok(ret_w), tok_t(ret_w), tok(ret_w), tok(ret_w),
            tok(ml_w), tok(ml_w), tok_t(ml_w), tok(ml_w), tok_t(GATE_ROWS),
            pl.BlockSpec((RET_HEADS, C, C), c3),
            pl.BlockSpec((n_pairs, C, LANES), c3),
            pl.BlockSpec((n_pairs, LANES, LANES), c3),
            pl.BlockSpec((1, ret_w), c2),
            pl.BlockSpec((1, ml_w), c2),
        ],
        out_specs=pl.BlockSpec((BB, C, ret_w + ml_w), lambda b, n: (b, n, 0)),
        out_shape=jax.ShapeDtypeStruct((B, S, ret_w + ml_w), BF16),
        scratch_shapes=[
            pltpu.VMEM((BB, n_pairs, LANES, LANES), F32),
            pltpu.VMEM((BB, ml_heads, LANES, LANES), F32),
            pltpu.VMEM((BB, ml_heads, SUBLANES, LANES), F32),
            pltpu.VMEM((BB, SUBLANES, LANES), F32),
        ],
        compiler_params=pltpu.CompilerParams(
            dimension_semantics=("arbitrary", "arbitrary"), vmem_limit_bytes=VMEM_LIMIT),
        name="mixer",
    )(*ops, decay, wq, cd, ret_gn.reshape(1, -1), ml_gn.reshape(1, -1))
    return y.reshape(B * S, ret_w + ml_w)


def _router_kernel(y_ref, x_ref, wo_ref, g_ref, wr_hi_ref, wr_lo_ref, br_ref,
                   h_ref, xn_ref, ri_ref, rf_ref, cnt_ref, run_cnt):
    tm = y_ref.shape[0]

    @pl.when(pl.program_id(0) == 0)
    def _():
        run_cnt[...] = jnp.zeros_like(run_cnt)

    h = x_ref[...] + _dot(y_ref[...], wo_ref[...])
    h_ref[...] = h
    xn = _rms(h, g_ref[...])
    xn_ref[...] = xn.reshape(xn_ref.shape)
    x_hi = xn.astype(BF16)
    x_lo = (xn - x_hi.astype(F32)).astype(BF16)
    logits = (_dot_nt(wr_hi_ref[...], x_hi) + _dot_nt(wr_hi_ref[...], x_lo)
              + _dot_nt(wr_lo_ref[...], x_hi) + br_ref[...])
    big = jnp.int32(LANES)
    neg = -jnp.inf
    gl = logits[GROUP_ROW0:GROUP_ROW0 + 8]
    grow = lax.broadcasted_iota(jnp.int32, gl.shape, 0)
    is_g = grow < N_GROUPS
    gl = jnp.where(is_g, gl, neg)
    gmax = jnp.max(gl, axis=0, keepdims=True)
    gsum = jnp.sum(jnp.where(is_g, jnp.exp(gl - gmax), 0.0), axis=0, keepdims=True)
    p_g = 1.0 / gsum
    g_sel = jnp.min(jnp.where(is_g & (gl == gmax), grow, big), axis=0, keepdims=True)
    el = logits[0:N_EXPERTS]
    erow = lax.broadcasted_iota(jnp.int32, el.shape, 0)
    in_grp = (erow // EXPERTS_PER_GROUP) == g_sel
    el = jnp.where(in_grp, el, neg)
    emax = jnp.max(el, axis=0, keepdims=True)
    eexp = jnp.where(in_grp, jnp.exp(el - emax), 0.0)
    prob = eexp / jnp.sum(eexp, axis=0, keepdims=True)
    pm1 = jnp.where(in_grp, prob, -1.0)
    p1 = jnp.max(pm1, axis=0, keepdims=True)
    i1 = jnp.min(jnp.where(pm1 == p1, erow, big), axis=0, keepdims=True)
    pm2 = jnp.where(erow == i1, -1.0, pm1)
    p2 = jnp.max(pm2, axis=0, keepdims=True)
    i2 = jnp.min(jnp.where(pm2 == p2, erow, big), axis=0, keepdims=True)
    denom = p1 + p2
    g1 = p_g * p1 / denom
    g2 = p_g * p2 / denom

    sel1 = erow == i1
    sel2 = erow == i2
    onehot = (sel1 | sel2).astype(BF16)
    r_i = lax.broadcasted_iota(jnp.int32, (tm, tm), 0)
    c_i = lax.broadcasted_iota(jnp.int32, (tm, tm), 1)
    tri = (r_i < c_i).astype(BF16)
    prefix = _dot(onehot, tri) + run_cnt[:, 0:1]
    rank1 = jnp.sum(jnp.where(sel1, prefix, 0.0), axis=0, keepdims=True).astype(jnp.int32)
    rank2 = jnp.sum(jnp.where(sel2, prefix, 0.0), axis=0, keepdims=True).astype(jnp.int32)
    new_cnt = run_cnt[...] + jnp.sum(onehot.astype(F32), axis=1, keepdims=True)
    run_cnt[...] = new_cnt
    cnt_ref[...] = new_cnt.astype(jnp.int32)

    rrow = lax.broadcasted_iota(jnp.int32, (ROUTE_ROWS, tm), 0)
    ri_ref[...] = jnp.where(rrow == 0, i1, jnp.where(rrow == 1, i2,
                            jnp.where(rrow == 2, rank1, jnp.where(rrow == 3, rank2, 0))))
    lrow = lax.broadcasted_iota(jnp.int32, (LANES, tm), 0)
    rf_ref[...] = jnp.where(lrow == 0, g1, jnp.where(lrow == 1, g2, 0.0)).T


def _router(y, x2, w_out, g, wr_hi, wr_lo, br, tm=ROUTE_TILE):
    T, D = x2.shape
    const = lambda i: (0, 0)
    tile = lambda i: (i, 0)
    return pl.pallas_call(
        _router_kernel,
        grid=(T // tm,),
        in_specs=[
            pl.BlockSpec((tm, y.shape[1]), tile),
            pl.BlockSpec((tm, D), tile),
            pl.BlockSpec(w_out.shape, const),
            pl.BlockSpec((1, D), const),
            pl.BlockSpec((ROUTER_ROWS, D), const),
            pl.BlockSpec((ROUTER_ROWS, D), const),
            pl.BlockSpec((ROUTER_ROWS, 1), const),
        ],
        out_specs=[
            pl.BlockSpec((tm, D), tile),
            pl.BlockSpec((tm,) + ROW_TILE, lambda i: (i, 0, 0)),
            pl.BlockSpec((None, ROUTE_ROWS, tm), lambda i: (i, 0, 0)),
            pl.BlockSpec((tm, LANES), tile),
            pl.BlockSpec((N_EXPERTS, LANES), const),
        ],
        out_shape=[
            jax.ShapeDtypeStruct((T, D), F32),
            jax.ShapeDtypeStruct((T,) + ROW_TILE, F32),
            jax.ShapeDtypeStruct((T // tm, ROUTE_ROWS, tm), jnp.int32),
            jax.ShapeDtypeStruct((T, LANES), F32),
            jax.ShapeDtypeStruct((N_EXPERTS, LANES), jnp.int32),
        ],
        scratch_shapes=[pltpu.VMEM((N_EXPERTS, LANES), F32)],
        compiler_params=pltpu.CompilerParams(
            dimension_semantics=("arbitrary",), vmem_limit_bytes=VMEM_LIMIT),
        name="router",
    )(y, x2, w_out, g, wr_hi, wr_lo, br)


def _slot_base(tok0):
    return (tok0 // ROUTE_TILE) * (2 * ROUTE_TILE) + tok0 % ROUTE_TILE


def _route_spans(tm):
    assert tm % ROUTE_TILE == 0 or ROUTE_TILE % tm == 0
    span = min(tm, ROUTE_TILE)
    return [(t0, span) for t0 in range(0, tm, span)]


def _for_each_pad_block(pend_ref, nu_ref, n_blocks, fn):
    blk = EXPERT_BLOCK
    for e in range(N_EXPERTS):
        prev_end = 0 if e == 0 else pend_ref[e - 1]

        @pl.when(pend_ref[e] > prev_end)
        def _():
            fn(pl.multiple_of(pend_ref[e] - blk, blk))

    def tail(j, c):
        fn(pl.multiple_of(j * blk, blk))
        return c

    lax.fori_loop(nu_ref[0], n_blocks, tail, 0)


def _dispatch_kernel(dest_ref, pend_ref, nu_ref, xn_ref, p_ref, wup_ref, gple_ref,
                     buf_ref, e_ref, zeros, sem, zsem):
    tm = xn_ref.shape[0]
    blk = EXPERT_BLOCK
    i = pl.program_id(0)

    @pl.when(i == 0)
    def _():
        zeros[...] = jnp.zeros_like(zeros)
        zcopy = lambda row: pltpu.make_async_copy(zeros, buf_ref.at[pl.ds(row, blk)], zsem)
        n_blocks = buf_ref.shape[0] // blk
        _for_each_pad_block(pend_ref, nu_ref, n_blocks, lambda row: zcopy(row).start())
        _for_each_pad_block(pend_ref, nu_ref, n_blocks, lambda row: zcopy(row).wait())

    for t0, span in _route_spans(tm):
        slot0 = _slot_base(i * tm + t0)

        def issue(g, c, t0=t0, slot0=slot0):
            for j in range(SUBLANES):
                for k in range(2):
                    t = g * SUBLANES + j
                    d = dest_ref[slot0 + k * ROUTE_TILE + t]
                    pltpu.make_async_copy(xn_ref.at[t0 + t], buf_ref.at[d], sem).start(priority=k)
            return c

        lax.fori_loop(0, span // SUBLANES, issue, 0)

    sub = min(tm, 2 * LANES)
    for r0 in range(0, tm, sub):
        rows = slice(r0, r0 + sub)
        e_ref[rows, :] = _rms(_dot(p_ref[rows, :].astype(BF16), wup_ref[...]), gple_ref[...])

    for k in range(2):
        pltpu.make_async_copy(buf_ref.at[pl.ds(tm, tm)], buf_ref.at[pl.ds(0, tm)], sem).wait()


def _dispatch(dest, pad_end, n_used, xn, p2, w_up, g_ple, n_rows, tm=ROUTE_TILE):
    T = xn.shape[0]
    D = w_up.shape[1]
    const = lambda i, *_: (0, 0)
    return pl.pallas_call(
        _dispatch_kernel,
        grid_spec=pltpu.PrefetchScalarGridSpec(
            num_scalar_prefetch=3,
            grid=(T // tm,),
            in_specs=[
                pl.BlockSpec((tm,) + ROW_TILE, lambda i, *_: (i, 0, 0)),
                pl.BlockSpec((tm, p2.shape[1]), lambda i, *_: (i, 0)),
                pl.BlockSpec(w_up.shape, const),
                pl.BlockSpec((1, D), const),
            ],
            out_specs=[
                pl.BlockSpec(memory_space=pl.ANY),
                pl.BlockSpec((tm, D), lambda i, *_: (i, 0)),
            ],
            scratch_shapes=[
                pltpu.VMEM((EXPERT_BLOCK,) + ROW_TILE, xn.dtype),
                pltpu.SemaphoreType.DMA,
                pltpu.SemaphoreType.DMA,
            ],
        ),
        out_shape=[
            jax.ShapeDtypeStruct((n_rows,) + ROW_TILE, xn.dtype),
            jax.ShapeDtypeStruct((T, D), F32),
        ],
        compiler_params=pltpu.CompilerParams(
            dimension_semantics=("arbitrary",), vmem_limit_bytes=VMEM_LIMIT),
        name="dispatch",
    )(dest, pad_end, n_used, xn, p2, w_up, g_ple)


def _expert_kernel(be_ref, nu_ref, x_ref, w1_ref, w3_ref, w2_ref, y_ref, w1b, w3b, w2b):
    j = pl.program_id(0)
    used = j < nu_ref[0]
    prev_expert = be_ref[jnp.maximum(j, 1) - 1]

    @pl.when(used & ((j == 0) | (be_ref[j] != prev_expert)))
    def _():
        w1b[...] = w1_ref[...].astype(BF16)
        w3b[...] = w3_ref[...].astype(BF16)
        w2b[...] = w2_ref[...].astype(BF16)

    @pl.when(used)
    def _():
        x = x_ref[...].reshape(x_ref.shape[0], -1).astype(BF16)
        a = _dot(x, w1b[...])
        hmid = a * jax.nn.sigmoid(a) * _dot(x, w3b[...])
        y_ref[...] = _dot(hmid.astype(BF16), w2b[...]).reshape(y_ref.shape)

    @pl.when(jnp.logical_not(used))
    def _():
        y_ref[...] = jnp.zeros_like(y_ref)


def _experts(block_expert, n_used, x_buf, w1, w3, w2):
    P = x_buf.shape[0]
    D = w1.shape[1]
    assert x_buf.shape[1:] == ROW_TILE and D == SUBLANES * LANES
    blk = EXPERT_BLOCK
    d_exp = w2.shape[1]
    clamp = lambda j, nu: jnp.minimum(j, nu[0] - 1)
    w_idx = lambda j, be, nu: (be[clamp(j, nu)], 0, 0)
    return pl.pallas_call(
        _expert_kernel,
        grid_spec=pltpu.PrefetchScalarGridSpec(
            num_scalar_prefetch=2,
            grid=(P // blk,),
            in_specs=[
                pl.BlockSpec((blk,) + ROW_TILE, lambda j, be, nu: (clamp(j, nu), 0, 0)),
                pl.BlockSpec((None, D, d_exp), w_idx),
                pl.BlockSpec((None, D, d_exp), w_idx),
                pl.BlockSpec((None, d_exp, D), w_idx),
            ],
            out_specs=pl.BlockSpec((blk,) + ROW_TILE, lambda j, be, nu: (j, 0, 0)),
            scratch_shapes=[
                pltpu.VMEM((D, d_exp), BF16),
                pltpu.VMEM((D, d_exp), BF16),
                pltpu.VMEM((d_exp, D), BF16),
            ],
        ),
        out_shape=jax.ShapeDtypeStruct((P,) + ROW_TILE, F32),
        compiler_params=pltpu.CompilerParams(
            dimension_semantics=("arbitrary",), vmem_limit_bytes=VMEM_LIMIT),
        name="experts",
    )(block_expert, n_used, x_buf, w1, w3, w2)


def _final_kernel(dest_ref, h_ref, rf_ref, e_ref, ggate_ref, wgate_ref,
                  gfin_ref, ybuf_ref, out_ref, *scratch, tm):
    row_bufs, sems = scratch[:-1], scratch[-1]
    s = pl.program_id(0)
    D = h_ref.shape[1]
    assert ROUTE_TILE % tm == 0

    def gather(tile, bufs, sem, straight=False):
        slot0 = _slot_base(tile * tm)

        def issue(g, c):
            for j in range(SUBLANES):
                for k in range(2):
                    t = g * SUBLANES + j
                    d = dest_ref[slot0 + k * ROUTE_TILE + t]
                    pltpu.make_async_copy(ybuf_ref.at[d], bufs[k].at[t], sem).start(priority=k)
            return c

        if straight:
            for g in range(tm // SUBLANES):
                issue(g, 0)
        else:
            lax.fori_loop(0, tm // SUBLANES, issue, 0)

    def wait(sem):
        for k in range(2):
            pltpu.make_async_copy(ybuf_ref.at[pl.ds(0, tm)], ybuf_ref.at[pl.ds(tm, tm)], sem).wait()

    def compute(rows, bufs):
        rf = rf_ref[rows, :]
        h = (h_ref[rows, :] + rf[:, 0:1] * bufs[0][...].reshape(tm, D)
             + rf[:, 1:2] * bufs[1][...].reshape(tm, D))
        gate = jax.nn.sigmoid(_dot(_rms(h, ggate_ref[...]).astype(BF16), wgate_ref[...]))
        h = h + gate * e_ref[rows, :]
        out_ref[rows, :] = _rms(h, gfin_ref[...])

    n_tiles = pl.num_programs(0) * FINAL_TILES
    bufs = [(row_bufs[2 * i], row_bufs[2 * i + 1]) for i in range(FINAL_TILES)]

    @pl.when(s == 0)
    def _():
        for i in range(FINAL_AHEAD):
            gather(i, bufs[i], sems.at[i])

    for i in range(FINAL_TILES):
        wait(sems.at[i])
        nxt = (i + FINAL_AHEAD) % FINAL_TILES
        gather(jnp.minimum(s * FINAL_TILES + i + FINAL_AHEAD, n_tiles - 1), bufs[nxt],
               sems.at[nxt], straight=True)
        compute(slice(i * tm, (i + 1) * tm), bufs[i])

    @pl.when(s == pl.num_programs(0) - 1)
    def _():
        for i in range(FINAL_AHEAD):
            wait(sems.at[i])


def _final(dest, h, rf, e, g_gate, w_gate, g_fin, y_buf, tm=256):
    T, D = h.shape
    const = lambda i, d: (0, 0)
    tile = lambda i, d: (i, 0)
    assert y_buf.shape[1:] == ROW_TILE and D == SUBLANES * LANES
    row_buf = pltpu.VMEM((tm,) + ROW_TILE, F32)
    return pl.pallas_call(
        functools.partial(_final_kernel, tm=tm),
        grid_spec=pltpu.PrefetchScalarGridSpec(
            num_scalar_prefetch=1,
            grid=(T // (FINAL_TILES * tm),),
            in_specs=[
                pl.BlockSpec((FINAL_TILES * tm, D), tile),
                pl.BlockSpec((FINAL_TILES * tm, LANES), tile),
                pl.BlockSpec((FINAL_TILES * tm, D), tile),
                pl.BlockSpec((1, D), const),
                pl.BlockSpec(w_gate.shape, const),
                pl.BlockSpec((1, D), const),
                pl.BlockSpec(memory_space=pl.ANY),
            ],
            out_specs=pl.BlockSpec((FINAL_TILES * tm, D), tile),
            scratch_shapes=[row_buf] * (2 * FINAL_TILES) + [pltpu.SemaphoreType.DMA((FINAL_TILES,))],
        ),
        out_shape=jax.ShapeDtypeStruct((T, D), F32),
        compiler_params=pltpu.CompilerParams(
            dimension_semantics=("arbitrary",), vmem_limit_bytes=VMEM_LIMIT),
        name="final",
    )(dest, h, rf, e, g_gate, w_gate, g_fin, y_buf)


def _layer(h2, p2, positions, B, S, attn_norm, w_in, conv_w, conv_b, b_igate, b_fgate, ret_gn,
           ml_gn, w_out, moe_norm, w_group, b_group, w_router, b_router, w1, w3, w2, w_ple_up,
           ple_norm, ple_gate_norm, w_ple_gate, out_norm):
    T, D = h2.shape
    ret_w = ret_gn.shape[0]
    ml_w = ml_gn.shape[0]
    n_main = 4 * ret_w + 4 * ml_w
    row = lambda v: v.reshape(1, -1).astype(F32)
    pad_lanes = lambda a: jnp.pad(a, ((0, 0), (0, LANES - a.shape[1])))

    ml_heads = ml_w // LANES
    w_main = w_in[:, :n_main].astype(BF16)
    gate_rows = lambda a: jnp.pad(a, ((0, SUBLANES - ml_heads), (0, 0)))
    w_i, w_f = w_in[:, n_main:n_main + ml_heads].T, w_in[:, n_main + ml_heads:].T
    w_gate_t = jnp.concatenate([gate_rows(w_i), gate_rows(w_f)]).astype(BF16)
    gate_bias = jnp.concatenate([gate_rows(b_igate[:, None]), gate_rows(b_fgate[:, None])])
    cos_t, sin_t = _rope_tables(positions, LANES // 2)
    decay, ws, wq, cd = _ret_tables(CHUNK)
    tm_in = 2 * CHUNK
    ops = _inproj(h2.reshape(B, S, D), row(attn_norm), w_main, w_gate_t, gate_bias.astype(F32),
                  cos_t.reshape(B, S, LANES), sin_t.reshape(B, S, LANES),
                  jnp.tile(ws, (tm_in // CHUNK, 1)), conv_w, conv_b.reshape(1, -1), ret_w, ml_w,
                  tm=tm_in)

    y = _mixer(ops, decay, wq, cd, ret_gn, ml_gn, B, S, ret_w, ml_w)

    pad_rows = lambda a: jnp.pad(a, ((0, ROUTER_ROWS - a.shape[0]), (0, 0)))
    wr = pad_rows(jnp.concatenate([w_router, w_group], axis=1).T)
    wr_hi = wr.astype(BF16)
    wr_lo = (wr - wr_hi.astype(F32)).astype(BF16)
    br = pad_rows(jnp.concatenate([b_router, b_group])[:, None].astype(F32))
    h_mid, xn, ri, rf, counts = _router(y, h2, w_out.astype(BF16), row(moe_norm), wr_hi, wr_lo, br)

    blk = EXPERT_BLOCK
    counts = counts[:, 0]
    padded = (counts + blk - 1) // blk * blk
    pad_end = jnp.cumsum(padded)
    pad_start = pad_end - padded
    n_blocks = (2 * T) // blk + N_EXPERTS
    block_start = jnp.arange(n_blocks, dtype=jnp.int32) * blk
    block_expert = jnp.minimum(jnp.sum(pad_end[None, :] <= block_start[:, None], axis=1),
                               N_EXPERTS - 1).astype(jnp.int32)
    n_used = (pad_end[-1:] // blk).astype(jnp.int32)
    dest = ri[:, 2:4, :]
    for e in range(N_EXPERTS):
        dest = dest + jnp.where(ri[:, 0:2, :] == e, pad_start[e], 0)
    dest = dest.astype(jnp.int32).reshape(-1)

    x_buf, e = _dispatch(dest, pad_end.astype(jnp.int32), n_used, xn, p2, w_ple_up.astype(BF16),
                         row(ple_norm), n_blocks * blk, tm=min(T, DISPATCH_TILE))
    y_buf = _experts(block_expert, n_used, x_buf, w1, w3, w2)

    return _final(dest, h_mid, rf, e, row(ple_gate_norm), w_ple_gate.astype(BF16), row(out_norm),
                  y_buf)


def kernel(x, p, positions, attn_norm, w_in, conv_w, conv_b, b_igate, b_fgate, ret_gn, ml_gn,
           w_out, moe_norm, w_group, b_group, w_router, b_router, w1, w3, w2, w_ple_up, ple_norm,
           ple_gate_norm, w_ple_gate, final_norm):
    B, S, D = x.shape
    depth = p.shape[0]
    assert depth == 1, "the final RMSNorm is fused into the layer's last kernel"
    out = _layer(x.reshape(B * S, D), p[0].reshape(B * S, -1), positions, B, S,
                 attn_norm[0], w_in[0], conv_w[0], conv_b[0], b_igate[0], b_fgate[0], ret_gn[0],
                 ml_gn[0], w_out[0], moe_norm[0], w_group[0], b_group[0], w_router[0],
                 b_router[0], w1[0], w3[0], w2[0], w_ple_up[0], ple_norm[0], ple_gate_norm[0],
                 w_ple_gate[0], final_norm)
    return out.reshape(B, S, D)
```

```python
import functools

import jax
import jax.numpy as jnp
from jax import lax
from jax.experimental import pallas as pl
from jax.experimental.pallas import tpu as pltpu

F32 = jnp.float32
BF16 = jnp.bfloat16

RET_HEADS = 8
ML_HEADS = 4
CHUNK = 128
CONV_W = 4
ROPE_BASE = 10000.0
N_GROUPS = 4
EXPERTS_PER_GROUP = 8
N_EXPERTS = N_GROUPS * EXPERTS_PER_GROUP
EPS = 1e-6

LANES = 128
SUBLANES = 8
MXU_COLS = 256
MIXER_SEQS = 2
FINAL_TILES = 4
FINAL_AHEAD = 2
VMEM_LIMIT = 56 * 1024 * 1024
EXPERT_BLOCK = 512
GROUP_ROW0 = N_EXPERTS
ROUTER_ROWS = 64
ROUTE_TILE = 1024
ROUTE_ROWS = 8
DISPATCH_TILE = 2 * ROUTE_TILE
GATE_ROWS = 2 * SUBLANES
ROW_TILE = (SUBLANES, LANES)


def _rms(x, g):
    return x * lax.rsqrt(jnp.mean(x * x, axis=-1, keepdims=True) + EPS) * g


def _dot(a, b):
    return jnp.dot(a, b, preferred_element_type=F32)


def _dot_nt(a, b):
    return lax.dot_general(a, b, (((1,), (1,)), ((), ())), preferred_element_type=F32)


def _split3(x):
    hi = x.astype(BF16)
    r1 = x - hi.astype(F32)
    mid = r1.astype(BF16)
    lo = (r1 - mid.astype(F32)).astype(BF16)
    return hi, mid, lo


def _rope_kernel(pos_ref, freq_ref, sign_ref, cos_ref, sin_ref, *, half):
    ang = pos_ref[...].astype(F32) * freq_ref[...]
    cos_c = jnp.cos(ang)
    sin_c = jnp.sin(ang)
    group = lax.broadcasted_iota(jnp.int32, ang.shape, 1) // half
    per_row = LANES // half
    for s in range(per_row):
        def spread(t):
            out = t
            for g in range(per_row):
                if g != s:
                    out = jnp.where(group == g, pltpu.roll(t, (half * (g - s)) % LANES, 1), out)
            return out
        cos_ref[s] = spread(cos_c)
        sin_ref[s] = spread(sin_c) * sign_ref[...]


def _rope_tables(positions, dh):
    half = dh // 2
    per_row = LANES // half
    T = positions.size
    rows = T // per_row
    tr = min(rows, 1024)
    freqs = ROPE_BASE ** (-jnp.arange(half, dtype=F32) / half)
    pos_c = jnp.repeat(positions.reshape(per_row, rows).T, half, axis=1)
    sign = jnp.tile(jnp.concatenate([-jnp.ones((half,), F32), jnp.ones((half,), F32)]),
                    LANES // dh)[None, :]
    const = pl.BlockSpec((1, LANES), lambda i: (0, 0))
    out = pl.BlockSpec((per_row, tr, LANES), lambda i: (0, i, 0))
    cos_t, sin_t = pl.pallas_call(
        functools.partial(_rope_kernel, half=half),
        grid=(rows // tr,),
        in_specs=[pl.BlockSpec((tr, LANES), lambda i: (i, 0)), const, const],
        out_specs=[out, out],
        out_shape=[jax.ShapeDtypeStruct((per_row, rows, LANES), F32)] * 2,
        name="rope",
    )(pos_c, jnp.tile(freqs, per_row)[None, :], sign)
    return cos_t.reshape(T, LANES), sin_t.reshape(T, LANES)


def _inproj_kernel(x_ref, g_ref, wm_ref, wgt_ref, gb_ref, cos_ref, sin_ref, ws_ref, convw_ref,
                   convb_ref, rq_ref, rk_ref, rkwt_ref, rv_ref, rg_ref, mq_ref, mk_ref, mvt_ref,
                   mo_ref, gt_ref, carry, *, ret_w, ml_w):
    tm = x_ref.shape[0]
    ret_dh = LANES // 2

    @pl.when(pl.program_id(1) == 0)
    def _():
        carry[...] = jnp.zeros_like(carry)

    xn = _rms(x_ref[...], g_ref[...]).astype(BF16)
    proj = lambda o, w: _dot(xn, wm_ref[:, o:o + w])
    tiles = lambda w: [slice(t, t + LANES) for t in range(0, w, LANES)]

    lane = lax.broadcasted_iota(jnp.int32, (tm, LANES), 1)
    first_half = (lane % ret_dh) < (ret_dh // 2)
    cos_t = cos_ref[...]
    sin_t = sin_ref[...]

    def rot(t):
        swapped = jnp.where(first_half, pltpu.roll(t, LANES - ret_dh // 2, 1),
                            pltpu.roll(t, ret_dh // 2, 1))
        return t * cos_t + swapped * sin_t

    for c0 in range(0, ret_w, MXU_COLS):
        rq = proj(c0, MXU_COLS)
        for ps in tiles(MXU_COLS):
            rq_ref[:, c0 + ps.start:c0 + ps.stop] = rot(rq[:, ps]).astype(BF16)
    for c0 in range(0, ret_w, MXU_COLS):
        rk = proj(ret_w + c0, MXU_COLS)
        for ps in tiles(MXU_COLS):
            cs = slice(c0 + ps.start, c0 + ps.stop)
            k = rot(rk[:, ps]) * (ret_dh ** -0.5)
            rk_ref[:, cs] = k.astype(BF16)
            rkwt_ref[cs, :] = (k * ws_ref[:, cs]).T.astype(BF16)
    for c0 in range(0, ret_w, MXU_COLS):
        cs = slice(c0, c0 + MXU_COLS)
        rv_ref[:, cs] = proj(2 * ret_w + c0, MXU_COLS).astype(BF16)
    for c0 in range(0, ret_w, MXU_COLS):
        g = proj(3 * ret_w + c0, MXU_COLS)
        rg_ref[:, c0:c0 + MXU_COLS] = g * jax.nn.sigmoid(g)

    o_mq = 4 * ret_w
    row8 = lax.broadcasted_iota(jnp.int32, (SUBLANES, MXU_COLS), 0)
    for c0 in range(0, 2 * ml_w, MXU_COLS):
        cs = slice(c0, c0 + MXU_COLS)
        xq = proj(o_mq + c0, MXU_COLS)
        prev8 = carry[:, cs]
        carry[:, cs] = xq[tm - SUBLANES:tm]
        acc = xq * convw_ref[CONV_W - 1:CONV_W, cs] + convb_ref[:, cs]
        for s in range(1, CONV_W):
            rolled = pltpu.roll(xq, s, 0)
            head = jnp.where(row8 < s, pltpu.roll(prev8, s, 0), rolled[0:SUBLANES])
            shifted = jnp.concatenate([head, rolled[SUBLANES:]], axis=0)
            acc = acc + shifted * convw_ref[CONV_W - 1 - s:CONV_W - s, cs]
        act = acc * jax.nn.sigmoid(acc)
        if c0 < ml_w:
            mq_ref[:, cs] = act.astype(BF16)
        else:
            mk_ref[:, c0 - ml_w:c0 - ml_w + MXU_COLS] = (act * (LANES ** -0.5)).astype(BF16)
    for c0 in range(0, ml_w, MXU_COLS):
        mv = proj(o_mq + 2 * ml_w + c0, MXU_COLS)
        for hs in tiles(MXU_COLS):
            mvt_ref[c0 + hs.start:c0 + hs.stop, :] = mv[:, hs].T
    for c0 in range(0, ml_w, MXU_COLS):
        mo_ref[:, c0:c0 + MXU_COLS] = jax.nn.sigmoid(proj(o_mq + 3 * ml_w + c0, MXU_COLS))
    gt_ref[...] = _dot_nt(wgt_ref[...], xn) + gb_ref[...]


def _inproj(x3, g, w_main, w_gate_t, gate_bias, cos_t, sin_t, ws, conv_w, conv_b, ret_w, ml_w,
            tm=256):
    B, S, D = x3.shape
    n_main = w_main.shape[1]
    const = lambda b, i: (0, 0)
    tok = lambda w: pl.BlockSpec((None, tm, w), lambda b, i: (b, i, 0))
    tok_t = lambda r: pl.BlockSpec((None, r, tm), lambda b, i: (b, 0, i))
    act = lambda w, dt: jax.ShapeDtypeStruct((B, S, w), dt)
    act_t = lambda r, dt: jax.ShapeDtypeStruct((B, r, S), dt)
    return pl.pallas_call(
        functools.partial(_inproj_kernel, ret_w=ret_w, ml_w=ml_w),
        grid=(B, S // tm),
        in_specs=[
            tok(D),
            pl.BlockSpec((1, D), const),
            pl.BlockSpec((D, n_main), const),
            pl.BlockSpec((GATE_ROWS, D), const),
            pl.BlockSpec((GATE_ROWS, 1), const),
            tok(LANES),
            tok(LANES),
            pl.BlockSpec((tm, ret_w), const),
            pl.BlockSpec((CONV_W, 2 * ml_w), const),
            pl.BlockSpec((1, 2 * ml_w), const),
        ],
        out_specs=[tok(ret_w), tok(ret_w), tok_t(ret_w), tok(ret_w), tok(ret_w),
                   tok(ml_w), tok(ml_w), tok_t(ml_w), tok(ml_w), tok_t(GATE_ROWS)],
        out_shape=[
            act(ret_w, BF16),
            act(ret_w, BF16),
            act_t(ret_w, BF16),
            act(ret_w, BF16),
            act(ret_w, F32),
            act(ml_w, BF16),
            act(ml_w, BF16),
            act_t(ml_w, F32),
            act(ml_w, F32),
            act_t(GATE_ROWS, F32),
        ],
        scratch_shapes=[pltpu.VMEM((SUBLANES, 2 * ml_w), F32)],
        compiler_params=pltpu.CompilerParams(
            dimension_semantics=("arbitrary", "arbitrary"), vmem_limit_bytes=VMEM_LIMIT),
        name="inproj",
    )(x3, g, w_main, w_gate_t, gate_bias, cos_t, sin_t, ws, conv_w, conv_b)


def _mixer_kernel(rq_ref, rk_ref, rkwt_ref, rv_ref, rg_ref, mq_ref, mk_ref, mvt_ref, mo_ref,
                  gt_ref, decay_ref, wq_ref, cd_ref, retgn_ref, mlgn_ref,
                  y_ref, r_state, s_state, n_state, m_state, *, ret_w, ml_w):
    BB, C = rq_ref.shape[0], rq_ref.shape[1]
    n_pairs = ret_w // LANES
    ml_heads = ml_w // LANES
    ret_dh = LANES // 2

    @pl.when(pl.program_id(1) == 0)
    def _():
        r_state[...] = jnp.zeros_like(r_state)
        s_state[...] = jnp.zeros_like(s_state)
        n_state[...] = jnp.zeros_like(n_state)
        m_state[...] = jnp.zeros_like(m_state)

    lane = lax.broadcasted_iota(jnp.int32, (C, LANES), 1)
    row = lax.broadcasted_iota(jnp.int32, (C, LANES), 0)
    assert C == LANES
    lo = lane < ret_dh
    blockdiag = (row < ret_dh) == lo
    lo_b = jnp.where(lo, 1.0, 0.0).astype(BF16)
    hi_b = jnp.where(lo, 0.0, 1.0).astype(BF16)
    seqs = range(BB)
    pair_units = [(s, p, slice(p * LANES, (p + 1) * LANES)) for s in seqs for p in range(n_pairs)]
    head_units = [(s, h, slice(h * LANES, (h + 1) * LANES)) for s in seqs for h in range(ml_heads)]


    row8 = lax.broadcasted_iota(jnp.int32, (SUBLANES, C), 0)
    lane8 = lax.broadcasted_iota(jnp.int32, (SUBLANES, C), 1)
    live = row8 < ml_heads
    triu = (row <= lane).astype(BF16)
    before = row <= lane
    mx, w_inter, e_negm, w_state, w_state_b, dec, beta_t = [], [], [], [], [], [], []
    for s in seqs:
        ig = jnp.where(live, gt_ref[s, 0:SUBLANES, :], 0.0)
        f_pre = jnp.where(live, gt_ref[s, SUBLANES:2 * SUBLANES, :], 30.0)
        l_hi, l_mid, l_lo = _split3(jax.nn.log_sigmoid(f_pre))
        b = _dot(l_hi, triu) + _dot(l_mid, triu) + _dot(l_lo, triu)
        beta = ig - b
        cm = beta
        shift = 1
        while shift < C:
            cm = jnp.maximum(cm, jnp.where(lane8 >= shift, pltpu.roll(cm, shift, 1), -jnp.inf))
            shift *= 2
        m_prev = m_state[s]
        mx.append(jnp.maximum(cm, m_prev))
        mx_last = jnp.broadcast_to(mx[s][:, C - 1:C], (SUBLANES, C))
        w_inter.append(jnp.exp(m_prev - mx[s]))
        e_negm.append(jnp.exp(-(b + mx[s])))
        w_state.append(jnp.exp(beta - mx_last))
        w_state_b.append(w_state[s].astype(BF16))
        dec.append(jnp.exp(m_prev - mx_last))
        m_state[s] = jnp.where(live, jnp.broadcast_to(b[:, C - 1:C], (SUBLANES, C)) + mx_last, 0.0)
        beta_t.append(jnp.concatenate([beta, jnp.zeros((LANES - SUBLANES, C), F32)], axis=0).T)

    hrow = lambda t, h: t[h:h + 1, :]

    rq = [rq_ref[s, :, ps] for s, p, ps in pair_units]
    rk = [rk_ref[s, :, ps] for s, p, ps in pair_units]
    rv = [rv_ref[s, :, ps] for s, p, ps in pair_units]
    r_prev = [r_state[s, p] for s, p, ps in pair_units]
    pu = range(len(pair_units))
    s_a = [_dot_nt(rq[u] * lo_b, rk[u]) for u in pu]
    s_b = [_dot_nt(rq[u] * hi_b, rk[u]) for u in pu]
    r_read = [_dot(rq[u], r_prev[u].astype(BF16)) for u in pu]
    r_new = [_dot(rkwt_ref[s, ps, :], rv[u]) for u, (s, p, ps) in enumerate(pair_units)]
    hu = range(len(head_units))
    mq = [mq_ref[s, :, hs] for s, h, hs in head_units]
    mk = [mk_ref[s, :, hs] for s, h, hs in head_units]
    mv_t = [mvt_ref[s, hs, :] for s, h, hs in head_units]
    st_prev = [s_state[s, h] for s, h, hs in head_units]
    n_prev = [n_state[s, h] for s, h, hs in head_units]
    a_t = [_dot_nt(mk[u], mq[u]) for u in hu]
    s_read = [_dot_nt(st_prev[u].astype(BF16), mq[u]) for u in hu]
    qn = [_dot_nt(n_prev[u].astype(BF16), mq[u])[0:1, :] for u in hu]
    s_new = [_dot((mv_t[u] * hrow(w_state[s], h)).astype(BF16), mk[u])
             for u, (s, h, hs) in enumerate(head_units)]
    n_new = [_dot(w_state_b[s], mk[u])[h:h + 1, :] for u, (s, h, hs) in enumerate(head_units)]
    for u, (s, p, ps) in enumerate(pair_units):
        r_state[s, p] = cd_ref[p] * r_prev[u] + jnp.where(blockdiag, r_new[u], 0.0)
    for u, (s, h, hs) in enumerate(head_units):
        s_state[s, h] = hrow(dec[s], h) * st_prev[u] + s_new[u]
        n_state[s, h] = hrow(dec[s], h) * n_prev[u] + n_new[u]

    s_a = [(s_a[u] * decay_ref[2 * p]).astype(BF16) for u, (s, p, ps) in enumerate(pair_units)]
    s_b = [(s_b[u] * decay_ref[2 * p + 1]).astype(BF16) for u, (s, p, ps) in enumerate(pair_units)]
    p_t = [jnp.exp(jnp.where(before, beta_t[s][:, h:h + 1] - hrow(mx[s], h), -jnp.inf)) * a_t[u]
           for u, (s, h, hs) in enumerate(head_units)]

    o = [_dot(s_a[u], rv[u] * lo_b) + _dot(s_b[u], rv[u] * hi_b) + r_read[u] * wq_ref[p]
         for u, (s, p, ps) in enumerate(pair_units)]
    num_t = [_dot(mv_t[u].astype(BF16), p_t[u].astype(BF16)) + hrow(w_inter[s], h) * s_read[u]
             for u, (s, h, hs) in enumerate(head_units)]

    for u, (s, p, ps) in enumerate(pair_units):
        sq = o[u] * o[u]
        ms_a = jnp.sum(jnp.where(lo, sq, 0.0), axis=-1, keepdims=True)
        ms_b = jnp.sum(jnp.where(lo, 0.0, sq), axis=-1, keepdims=True)
        ms = jnp.where(lo, ms_a, ms_b) * (1.0 / ret_dh)
        r = o[u] * lax.rsqrt(ms + EPS) * retgn_ref[:, ps]
        y_ref[s, :, ps] = (rg_ref[s, :, ps] * r).astype(y_ref.dtype)
    for u, (s, h, hs) in enumerate(head_units):
        den = jnp.sum(p_t[u], axis=0, keepdims=True) + hrow(w_inter[s], h) * qn[u]
        hh = (num_t[u] * (1.0 / jnp.maximum(jnp.abs(den), hrow(e_negm[s], h)))).T
        hm = _rms(mo_ref[s, :, hs] * hh, mlgn_ref[:, hs])
        y_ref[s, :, ret_w + h * LANES:ret_w + (h + 1) * LANES] = hm.astype(y_ref.dtype)


def _ret_tables(C):
    H = RET_HEADS
    dh = LANES // 2
    log_gamma = jnp.log1p(-(2.0 ** (-5.0 - jnp.arange(H, dtype=F32))))
    idx = jnp.arange(C, dtype=F32)
    rel = idx[:, None] - idx[None, :]
    causal = rel >= 0
    decay = jnp.where(causal, jnp.exp(log_gamma[:, None, None] * jnp.where(causal, rel, 0.0)), 0.0)
    w_state = jnp.exp(log_gamma[:, None] * (C - 1 - idx))
    w_query = jnp.exp(log_gamma[:, None] * (idx + 1.0))
    chunk_decay = jnp.exp(log_gamma * C)
    pair = lambda t: jnp.repeat(t.reshape(H // 2, 2, C).transpose(0, 2, 1), dh, axis=2)
    cd = jnp.repeat(chunk_decay.reshape(H // 2, 2), dh, axis=1)
    cd = jnp.broadcast_to(cd[:, :, None], (H // 2, LANES, LANES))
    ws = pair(w_state).transpose(1, 0, 2).reshape(C, (H // 2) * LANES)
    return decay, ws, pair(w_query), cd


def _mixer(ops, decay, wq, cd, ret_gn, ml_gn, B, S, ret_w, ml_w):
    C = CHUNK
    N = S // C
    n_pairs = ret_w // LANES
    ml_heads = ml_w // LANES
    c2 = lambda b, n: (0, 0)
    c3 = lambda b, n: (0, 0, 0)
    BB = MIXER_SEQS if B % MIXER_SEQS == 0 else 1
    tok = lambda w: pl.BlockSpec((BB, C, w), lambda b, n: (b, n, 0))
    tok_t = lambda r: pl.BlockSpec((BB, r, C), lambda b, n: (b, 0, n))
    y = pl.pallas_call(
        functools.partial(_mixer_kernel, ret_w=ret_w, ml_w=ml_w),
        grid=(B // BB, N),
        in_specs=[
            tok(ret_w), tok(ret_w), tok_t(ret_w), tok(ret_w), tok(ret_w),
            tok(ml_w), tok(ml_w), tok_t(ml_w), tok(ml_w), tok_t(GATE_ROWS),
            pl.BlockSpec((RET_HEADS, C, C), c3),
            pl.BlockSpec((n_pairs, C, LANES), c3),
            pl.BlockSpec((n_pairs, LANES, LANES), c3),
            pl.BlockSpec((1, ret_w), c2),
            pl.BlockSpec((1, ml_w), c2),
        ],
        out_specs=pl.BlockSpec((BB, C, ret_w + ml_w), lambda b, n: (b, n, 0)),
        out_shape=jax.ShapeDtypeStruct((B, S, ret_w + ml_w), BF16),
        scratch_shapes=[
            pltpu.VMEM((BB, n_pairs, LANES, LANES), F32),
            pltpu.VMEM((BB, ml_heads, LANES, LANES), F32),
            pltpu.VMEM((BB, ml_heads, SUBLANES, LANES), F32),
            pltpu.VMEM((BB, SUBLANES, LANES), F32),
        ],
        compiler_params=pltpu.CompilerParams(
            dimension_semantics=("arbitrary", "arbitrary"), vmem_limit_bytes=VMEM_LIMIT),
        name="mixer",
    )(*ops, decay, wq, cd, ret_gn.reshape(1, -1), ml_gn.reshape(1, -1))
    return y.reshape(B * S, ret_w + ml_w)


def _router_kernel(y_ref, x_ref, wo_ref, g_ref, wr_hi_ref, wr_lo_ref, br_ref,
                   h_ref, xn_ref, ri_ref, rf_ref, cnt_ref, run_cnt):
    tm = y_ref.shape[0]

    @pl.when(pl.program_id(0) == 0)
    def _():
        run_cnt[...] = jnp.zeros_like(run_cnt)

    h = x_ref[...] + _dot(y_ref[...], wo_ref[...])
    h_ref[...] = h
    xn = _rms(h, g_ref[...])
    x_hi = xn.astype(BF16)
    xn_ref[...] = x_hi.reshape(xn_ref.shape)
    x_lo = (xn - x_hi.astype(F32)).astype(BF16)
    logits = (_dot_nt(wr_hi_ref[...], x_hi) + _dot_nt(wr_hi_ref[...], x_lo)
              + _dot_nt(wr_lo_ref[...], x_hi) + br_ref[...])
    big = jnp.int32(LANES)
    neg = -jnp.inf
    gl = logits[GROUP_ROW0:GROUP_ROW0 + 8]
    grow = lax.broadcasted_iota(jnp.int32, gl.shape, 0)
    is_g = grow < N_GROUPS
    gl = jnp.where(is_g, gl, neg)
    gmax = jnp.max(gl, axis=0, keepdims=True)
    gsum = jnp.sum(jnp.where(is_g, jnp.exp(gl - gmax), 0.0), axis=0, keepdims=True)
    p_g = 1.0 / gsum
    g_sel = jnp.min(jnp.where(is_g & (gl == gmax), grow, big), axis=0, keepdims=True)
    el = logits[0:N_EXPERTS]
    erow = lax.broadcasted_iota(jnp.int32, el.shape, 0)
    in_grp = (erow // EXPERTS_PER_GROUP) == g_sel
    el = jnp.where(in_grp, el, neg)
    emax = jnp.max(el, axis=0, keepdims=True)
    eexp = jnp.where(in_grp, jnp.exp(el - emax), 0.0)
    prob = eexp / jnp.sum(eexp, axis=0, keepdims=True)
    pm1 = jnp.where(in_grp, prob, -1.0)
    p1 = jnp.max(pm1, axis=0, keepdims=True)
    i1 = jnp.min(jnp.where(pm1 == p1, erow, big), axis=0, keepdims=True)
    pm2 = jnp.where(erow == i1, -1.0, pm1)
    p2 = jnp.max(pm2, axis=0, keepdims=True)
    i2 = jnp.min(jnp.where(pm2 == p2, erow, big), axis=0, keepdims=True)
    denom = p1 + p2
    g1 = p_g * p1 / denom
    g2 = p_g * p2 / denom

    sel1 = erow == i1
    sel2 = erow == i2
    onehot = (sel1 | sel2).astype(BF16)
    r_i = lax.broadcasted_iota(jnp.int32, (tm, tm), 0)
    c_i = lax.broadcasted_iota(jnp.int32, (tm, tm), 1)
    tri = (r_i < c_i).astype(BF16)
    prefix = _dot(onehot, tri) + run_cnt[:, 0:1]
    rank1 = jnp.sum(jnp.where(sel1, prefix, 0.0), axis=0, keepdims=True).astype(jnp.int32)
    rank2 = jnp.sum(jnp.where(sel2, prefix, 0.0), axis=0, keepdims=True).astype(jnp.int32)
    new_cnt = run_cnt[...] + jnp.sum(onehot.astype(F32), axis=1, keepdims=True)
    run_cnt[...] = new_cnt
    cnt_ref[...] = new_cnt.astype(jnp.int32)

    rrow = lax.broadcasted_iota(jnp.int32, (ROUTE_ROWS, tm), 0)
    ri_ref[...] = jnp.where(rrow == 0, rank1 * N_EXPERTS + i1,
                            jnp.where(rrow == 1, rank2 * N_EXPERTS + i2, 0))
    lrow = lax.broadcasted_iota(jnp.int32, (LANES, tm), 0)
    rf_ref[...] = jnp.where(lrow == 0, g1, jnp.where(lrow == 1, g2, 0.0)).T


def _router(y, x2, w_out, g, wr_hi, wr_lo, br, tm=ROUTE_TILE):
    T, D = x2.shape
    const = lambda i: (0, 0)
    tile = lambda i: (i, 0)
    return pl.pallas_call(
        _router_kernel,
        grid=(T // tm,),
        in_specs=[
            pl.BlockSpec((tm, y.shape[1]), tile),
            pl.BlockSpec((tm, D), tile),
            pl.BlockSpec(w_out.shape, const),
            pl.BlockSpec((1, D), const),
            pl.BlockSpec((ROUTER_ROWS, D), const),
            pl.BlockSpec((ROUTER_ROWS, D), const),
            pl.BlockSpec((ROUTER_ROWS, 1), const),
        ],
        out_specs=[
            pl.BlockSpec((tm, D), tile),
            pl.BlockSpec((tm,) + ROW_TILE, lambda i: (i, 0, 0)),
            pl.BlockSpec((None, ROUTE_ROWS, tm), lambda i: (i, 0, 0)),
            pl.BlockSpec((tm, LANES), tile),
            pl.BlockSpec((N_EXPERTS, LANES), const),
        ],
        out_shape=[
            jax.ShapeDtypeStruct((T, D), F32),
            jax.ShapeDtypeStruct((T,) + ROW_TILE, BF16),
            jax.ShapeDtypeStruct((T // tm, ROUTE_ROWS, tm), jnp.int32),
            jax.ShapeDtypeStruct((T, LANES), F32),
            jax.ShapeDtypeStruct((N_EXPERTS, LANES), jnp.int32),
        ],
        scratch_shapes=[pltpu.VMEM((N_EXPERTS, LANES), F32)],
        compiler_params=pltpu.CompilerParams(
            dimension_semantics=("arbitrary",), vmem_limit_bytes=VMEM_LIMIT),
        name="router",
    )(y, x2, w_out, g, wr_hi, wr_lo, br)


def _slot_base(tok0):
    return (tok0 // ROUTE_TILE) * (2 * ROUTE_TILE) + tok0 % ROUTE_TILE


def _route_spans(tm):
    assert tm % ROUTE_TILE == 0 or ROUTE_TILE % tm == 0
    span = min(tm, ROUTE_TILE)
    return [(t0, span) for t0 in range(0, tm, span)]


def _for_each_pad_block(pend_ref, nu_ref, n_blocks, fn):
    blk = EXPERT_BLOCK
    for e in range(N_EXPERTS):
        prev_end = 0 if e == 0 else pend_ref[e - 1]

        @pl.when(pend_ref[e] > prev_end)
        def _():
            fn(pl.multiple_of(pend_ref[e] - blk, blk))

    def tail(j, c):
        fn(pl.multiple_of(j * blk, blk))
        return c

    lax.fori_loop(nu_ref[0], n_blocks, tail, 0)


def _dispatch_kernel(dest_ref, pend_ref, nu_ref, xn_ref, buf_ref, zeros, sem, zsem):
    tm = xn_ref.shape[0]
    blk = EXPERT_BLOCK
    i = pl.program_id(0)

    @pl.when(i == 0)
    def _():
        zeros[...] = jnp.zeros_like(zeros)
        zcopy = lambda row: pltpu.make_async_copy(zeros, buf_ref.at[pl.ds(row, blk)], zsem)
        n_blocks = buf_ref.shape[0] // blk
        _for_each_pad_block(pend_ref, nu_ref, n_blocks, lambda row: zcopy(row).start())
        _for_each_pad_block(pend_ref, nu_ref, n_blocks, lambda row: zcopy(row).wait())

    for t0, span in _route_spans(tm):
        slot0 = _slot_base(i * tm + t0)

        def issue(g, c, t0=t0, slot0=slot0):
            for j in range(SUBLANES):
                for k in range(2):
                    t = g * SUBLANES + j
                    d = dest_ref[slot0 + k * ROUTE_TILE + t]
                    pltpu.make_async_copy(xn_ref.at[t0 + t], buf_ref.at[d], sem).start(priority=k)
            return c

        lax.fori_loop(0, span // SUBLANES, issue, 0)
    for k in range(2):
        pltpu.make_async_copy(buf_ref.at[pl.ds(tm, tm)], buf_ref.at[pl.ds(0, tm)], sem).wait()


def _dispatch(dest, pad_end, n_used, xn, n_rows, tm=ROUTE_TILE):
    T = xn.shape[0]
    return pl.pallas_call(
        _dispatch_kernel,
        grid_spec=pltpu.PrefetchScalarGridSpec(
            num_scalar_prefetch=3,
            grid=(T // tm,),
            in_specs=[pl.BlockSpec((tm,) + ROW_TILE, lambda i, *_: (i, 0, 0))],
            out_specs=pl.BlockSpec(memory_space=pl.ANY),
            scratch_shapes=[
                pltpu.VMEM((EXPERT_BLOCK,) + ROW_TILE, xn.dtype),
                pltpu.SemaphoreType.DMA,
                pltpu.SemaphoreType.DMA,
            ],
        ),
        out_shape=jax.ShapeDtypeStruct((n_rows,) + ROW_TILE, xn.dtype),
        compiler_params=pltpu.CompilerParams(
            dimension_semantics=("arbitrary",), vmem_limit_bytes=VMEM_LIMIT),
        name="dispatch",
    )(dest, pad_end, n_used, xn)


def _expert_kernel(be_ref, nu_ref, nv_ref, x_ref, w1_ref, w3_ref, w2_ref, y_ref, w1b, w3b, w2b):
    j = pl.program_id(0)
    used = j < nu_ref[0]
    prev_expert = be_ref[jnp.maximum(j, 1) - 1]
    blk = x_ref.shape[0]
    half = blk // 2

    @pl.when(used & ((j == 0) | (be_ref[j] != prev_expert)))
    def _():
        w1b[...] = w1_ref[...].astype(BF16)
        w3b[...] = w3_ref[...].astype(BF16)
        w2b[...] = w2_ref[...].astype(BF16)

    def swiglu(rows):
        n = rows.stop - rows.start
        x = x_ref[rows].reshape(n, -1)
        a = _dot(x, w1b[...])
        hmid = a * jax.nn.sigmoid(a) * _dot(x, w3b[...])
        y_ref[rows] = _dot(hmid.astype(BF16), w2b[...]).reshape((n,) + y_ref.shape[1:])

    @pl.when(used & (nv_ref[j] > half))
    def _():
        swiglu(slice(0, blk))

    @pl.when(used & (nv_ref[j] <= half))
    def _():
        swiglu(slice(0, half))
        y_ref[half:blk] = jnp.zeros((blk - half,) + y_ref.shape[1:], y_ref.dtype)

    @pl.when(jnp.logical_not(used))
    def _():
        y_ref[...] = jnp.zeros_like(y_ref)


def _experts(block_expert, n_used, n_valid, x_buf, w1, w3, w2):
    P = x_buf.shape[0]
    D = w1.shape[1]
    assert x_buf.shape[1:] == ROW_TILE and D == SUBLANES * LANES
    blk = EXPERT_BLOCK
    d_exp = w2.shape[1]
    clamp = lambda j, nu: jnp.minimum(j, nu[0] - 1)
    w_idx = lambda j, be, nu, nv: (be[clamp(j, nu)], 0, 0)
    return pl.pallas_call(
        _expert_kernel,
        grid_spec=pltpu.PrefetchScalarGridSpec(
            num_scalar_prefetch=3,
            grid=(P // blk,),
            in_specs=[
                pl.BlockSpec((blk,) + ROW_TILE, lambda j, be, nu, nv: (clamp(j, nu), 0, 0)),
                pl.BlockSpec((None, D, d_exp), w_idx),
                pl.BlockSpec((None, D, d_exp), w_idx),
                pl.BlockSpec((None, d_exp, D), w_idx),
            ],
            out_specs=pl.BlockSpec((blk,) + ROW_TILE, lambda j, be, nu, nv: (j, 0, 0)),
            scratch_shapes=[
                pltpu.VMEM((D, d_exp), BF16),
                pltpu.VMEM((D, d_exp), BF16),
                pltpu.VMEM((d_exp, D), BF16),
            ],
        ),
        out_shape=jax.ShapeDtypeStruct((P,) + ROW_TILE, F32),
        compiler_params=pltpu.CompilerParams(
            dimension_semantics=("arbitrary",), vmem_limit_bytes=VMEM_LIMIT),
        name="experts",
    )(block_expert, n_used, n_valid, x_buf, w1, w3, w2)


def _final_kernel(dest_ref, h_ref, rf_ref, p_ref, wup_ref, gple_ref, ggate_ref, wgate_ref,
                  gfin_ref, ybuf_ref, out_ref, *scratch, tm):
    row_bufs, sems = scratch[:-1], scratch[-1]
    s = pl.program_id(0)
    D = h_ref.shape[1]
    assert ROUTE_TILE % tm == 0

    def gather(tile, bufs, sem, straight=False):
        slot0 = _slot_base(tile * tm)

        def issue(g, c):
            for j in range(SUBLANES):
                for k in range(2):
                    t = g * SUBLANES + j
                    d = dest_ref[slot0 + k * ROUTE_TILE + t]
                    pltpu.make_async_copy(ybuf_ref.at[d], bufs[k].at[t], sem).start(priority=k)
            return c

        if straight:
            for g in range(tm // SUBLANES):
                issue(g, 0)
        else:
            lax.fori_loop(0, tm // SUBLANES, issue, 0)

    def wait(sem):
        for k in range(2):
            pltpu.make_async_copy(ybuf_ref.at[pl.ds(0, tm)], ybuf_ref.at[pl.ds(tm, tm)], sem).wait()

    def compute(rows, bufs):
        e = _rms(_dot(p_ref[rows, :].astype(BF16), wup_ref[...]), gple_ref[...])
        rf = rf_ref[rows, :]
        h = (h_ref[rows, :] + rf[:, 0:1] * bufs[0][...].reshape(tm, D)
             + rf[:, 1:2] * bufs[1][...].reshape(tm, D))
        gate = jax.nn.sigmoid(_dot(_rms(h, ggate_ref[...]).astype(BF16), wgate_ref[...]))
        h = h + gate * e
        out_ref[rows, :] = _rms(h, gfin_ref[...])

    n_tiles = pl.num_programs(0) * FINAL_TILES
    bufs = [(row_bufs[2 * i], row_bufs[2 * i + 1]) for i in range(FINAL_TILES)]

    @pl.when(s == 0)
    def _():
        for i in range(FINAL_AHEAD):
            gather(i, bufs[i], sems.at[i])

    for i in range(FINAL_TILES):
        wait(sems.at[i])
        nxt = (i + FINAL_AHEAD) % FINAL_TILES
        gather(jnp.minimum(s * FINAL_TILES + i + FINAL_AHEAD, n_tiles - 1), bufs[nxt],
               sems.at[nxt], straight=True)
        compute(slice(i * tm, (i + 1) * tm), bufs[i])

    @pl.when(s == pl.num_programs(0) - 1)
    def _():
        for i in range(FINAL_AHEAD):
            wait(sems.at[i])


def _final(dest, h, rf, p2, w_up, g_ple, g_gate, w_gate, g_fin, y_buf, tm=256):
    T, D = h.shape
    const = lambda i, *_: (0, 0)
    tile = lambda i, *_: (i, 0)
    assert y_buf.shape[1:] == ROW_TILE and D == SUBLANES * LANES
    row_buf = pltpu.VMEM((tm,) + ROW_TILE, F32)
    return pl.pallas_call(
        functools.partial(_final_kernel, tm=tm),
        grid_spec=pltpu.PrefetchScalarGridSpec(
            num_scalar_prefetch=1,
            grid=(T // (FINAL_TILES * tm),),
            in_specs=[
                pl.BlockSpec((FINAL_TILES * tm, D), tile),
                pl.BlockSpec((FINAL_TILES * tm, LANES), tile),
                pl.BlockSpec((FINAL_TILES * tm, p2.shape[1]), tile),
                pl.BlockSpec(w_up.shape, const),
                pl.BlockSpec((1, D), const),
                pl.BlockSpec((1, D), const),
                pl.BlockSpec(w_gate.shape, const),
                pl.BlockSpec((1, D), const),
                pl.BlockSpec(memory_space=pl.ANY),
            ],
            out_specs=pl.BlockSpec((FINAL_TILES * tm, D), tile),
            scratch_shapes=[row_buf] * (2 * FINAL_TILES) + [pltpu.SemaphoreType.DMA((FINAL_TILES,))],
        ),
        out_shape=jax.ShapeDtypeStruct((T, D), F32),
        compiler_params=pltpu.CompilerParams(
            dimension_semantics=("arbitrary",), vmem_limit_bytes=VMEM_LIMIT),
        name="final",
    )(dest, h, rf, p2, w_up, g_ple, g_gate, w_gate, g_fin, y_buf)


def _layer(h2, p2, positions, B, S, attn_norm, w_in, conv_w, conv_b, b_igate, b_fgate, ret_gn,
           ml_gn, w_out, moe_norm, w_group, b_group, w_router, b_router, w1, w3, w2, w_ple_up,
           ple_norm, ple_gate_norm, w_ple_gate, out_norm):
    T, D = h2.shape
    ret_w = ret_gn.shape[0]
    ml_w = ml_gn.shape[0]
    n_main = 4 * ret_w + 4 * ml_w
    row = lambda v: v.reshape(1, -1).astype(F32)
    pad_lanes = lambda a: jnp.pad(a, ((0, 0), (0, LANES - a.shape[1])))

    ml_heads = ml_w // LANES
    w_main = w_in[:, :n_main].astype(BF16)
    gate_rows = lambda a: jnp.pad(a, ((0, SUBLANES - ml_heads), (0, 0)))
    w_i, w_f = w_in[:, n_main:n_main + ml_heads].T, w_in[:, n_main + ml_heads:].T
    w_gate_t = jnp.concatenate([gate_rows(w_i), gate_rows(w_f)]).astype(BF16)
    gate_bias = jnp.concatenate([gate_rows(b_igate[:, None]), gate_rows(b_fgate[:, None])])
    cos_t, sin_t = _rope_tables(positions, LANES // 2)
    decay, ws, wq, cd = _ret_tables(CHUNK)
    tm_in = 2 * CHUNK
    ops = _inproj(h2.reshape(B, S, D), row(attn_norm), w_main, w_gate_t, gate_bias.astype(F32),
                  cos_t.reshape(B, S, LANES), sin_t.reshape(B, S, LANES),
                  jnp.tile(ws, (tm_in // CHUNK, 1)), conv_w, conv_b.reshape(1, -1), ret_w, ml_w,
                  tm=tm_in)

    y = _mixer(ops, decay, wq, cd, ret_gn, ml_gn, B, S, ret_w, ml_w)

    pad_rows = lambda a: jnp.pad(a, ((0, ROUTER_ROWS - a.shape[0]), (0, 0)))
    wr = pad_rows(jnp.concatenate([w_router, w_group], axis=1).T)
    wr_hi = wr.astype(BF16)
    wr_lo = (wr - wr_hi.astype(F32)).astype(BF16)
    br = pad_rows(jnp.concatenate([b_router, b_group])[:, None].astype(F32))
    h_mid, xn, ri, rf, counts = _router(y, h2, w_out.astype(BF16), row(moe_norm), wr_hi, wr_lo, br)

    blk = EXPERT_BLOCK
    counts = counts[:, 0]
    padded = (counts + blk - 1) // blk * blk
    pad_end = jnp.cumsum(padded)
    pad_start = pad_end - padded
    n_blocks = (2 * T) // blk + N_EXPERTS
    block_start = jnp.arange(n_blocks, dtype=jnp.int32) * blk
    block_expert = jnp.minimum(jnp.sum(pad_end[None, :] <= block_start[:, None], axis=1),
                               N_EXPERTS - 1).astype(jnp.int32)
    n_used = (pad_end[-1:] // blk).astype(jnp.int32)
    real_end = jnp.sum(jnp.where(block_expert[:, None] == jnp.arange(N_EXPERTS)[None, :],
                                 (pad_start + counts)[None, :], 0), axis=1)
    n_valid = jnp.clip(real_end - block_start, 0, blk).astype(jnp.int32)
    codes = ri[:, 0:2, :].reshape(-1, ROUTE_TILE)
    expert = codes % N_EXPERTS
    dest = codes // N_EXPERTS
    for e in range(N_EXPERTS):
        dest = dest + jnp.where(expert == e, pad_start[e], 0)
    dest = dest.astype(jnp.int32).reshape(-1)

    x_buf = _dispatch(dest, pad_end.astype(jnp.int32), n_used, xn, n_blocks * blk,
                      tm=min(T, DISPATCH_TILE))
    y_buf = _experts(block_expert, n_used, n_valid, x_buf, w1, w3, w2)

    return _final(dest, h_mid, rf, p2, w_ple_up.astype(BF16), row(ple_norm), row(ple_gate_norm),
                  w_ple_gate.astype(BF16), row(out_norm), y_buf)


def kernel(x, p, positions, attn_norm, w_in, conv_w, conv_b, b_igate, b_fgate, ret_gn, ml_gn,
           w_out, moe_norm, w_group, b_group, w_router, b_router, w1, w3, w2, w_ple_up, ple_norm,
           ple_gate_norm, w_ple_gate, final_norm):
    B, S, D = x.shape
    depth = p.shape[0]
    assert depth == 1, "the final RMSNorm is fused into the layer's last kernel"
    out = _layer(x.reshape(B * S, D), p[0].reshape(B * S, -1), positions, B, S,
                 attn_norm[0], w_in[0], conv_w[0], conv_b[0], b_igate[0], b_fgate[0], ret_gn[0],
                 ml_gn[0], w_out[0], moe_norm[0], w_group[0], b_group[0], w_router[0],
                 b_router[0], w1[0], w3[0], w2[0], w_ple_up[0], ple_norm[0], ple_gate_norm[0],
                 w_ple_gate[0], final_norm)
    return out.reshape(B, S, D)
```

```python
import functools

import jax
import jax.numpy as jnp
from jax import lax
from jax.experimental import pallas as pl
from jax.experimental.pallas import tpu as pltpu

F32 = jnp.float32
BF16 = jnp.bfloat16

RET_HEADS = 8
ML_HEADS = 4
CHUNK = 128
CONV_W = 4
ROPE_BASE = 10000.0
N_GROUPS = 4
EXPERTS_PER_GROUP = 8
N_EXPERTS = N_GROUPS * EXPERTS_PER_GROUP
EPS = 1e-6

LANES = 128
SUBLANES = 8
MXU_COLS = 256
MIXER_SEQS = 2
FINAL_TILES = 4
FINAL_AHEAD = 2
VMEM_LIMIT = 56 * 1024 * 1024
EXPERT_BLOCK = 512
GROUP_ROW0 = N_EXPERTS
ROUTER_ROWS = 64
ROUTE_TILE = 1024
ROUTE_ROWS = 8
DISPATCH_TILE = 2 * ROUTE_TILE
GATE_ROWS = 2 * SUBLANES
ROW_TILE = (SUBLANES, LANES)


def _rms(x, g):
    return x * lax.rsqrt(jnp.mean(x * x, axis=-1, keepdims=True) + EPS) * g


def _dot(a, b):
    return jnp.dot(a, b, preferred_element_type=F32)


def _dot_nt(a, b):
    return lax.dot_general(a, b, (((1,), (1,)), ((), ())), preferred_element_type=F32)


def _split3(x):
    hi = x.astype(BF16)
    r1 = x - hi.astype(F32)
    mid = r1.astype(BF16)
    lo = (r1 - mid.astype(F32)).astype(BF16)
    return hi, mid, lo


def _rope_kernel(pos_ref, freq_ref, sign_ref, cos_ref, sin_ref, *, half):
    ang = pos_ref[...].astype(F32) * freq_ref[...]
    cos_c = jnp.cos(ang)
    sin_c = jnp.sin(ang)
    group = lax.broadcasted_iota(jnp.int32, ang.shape, 1) // half
    per_row = LANES // half
    for s in range(per_row):
        def spread(t):
            out = t
            for g in range(per_row):
                if g != s:
                    out = jnp.where(group == g, pltpu.roll(t, (half * (g - s)) % LANES, 1), out)
            return out
        cos_ref[s] = spread(cos_c)
        sin_ref[s] = spread(sin_c) * sign_ref[...]


def _rope_tables(positions, dh):
    half = dh // 2
    per_row = LANES // half
    T = positions.size
    rows = T // per_row
    tr = min(rows, 1024)
    freqs = ROPE_BASE ** (-jnp.arange(half, dtype=F32) / half)
    pos_c = jnp.repeat(positions.reshape(per_row, rows).T, half, axis=1)
    sign = jnp.tile(jnp.concatenate([-jnp.ones((half,), F32), jnp.ones((half,), F32)]),
                    LANES // dh)[None, :]
    const = pl.BlockSpec((1, LANES), lambda i: (0, 0))
    out = pl.BlockSpec((per_row, tr, LANES), lambda i: (0, i, 0))
    cos_t, sin_t = pl.pallas_call(
        functools.partial(_rope_kernel, half=half),
        grid=(rows // tr,),
        in_specs=[pl.BlockSpec((tr, LANES), lambda i: (i, 0)), const, const],
        out_specs=[out, out],
        out_shape=[jax.ShapeDtypeStruct((per_row, rows, LANES), F32)] * 2,
        name="rope",
    )(pos_c, jnp.tile(freqs, per_row)[None, :], sign)
    return cos_t.reshape(T, LANES), sin_t.reshape(T, LANES)


def _inproj_kernel(x_ref, g_ref, wm_ref, wgt_ref, gb_ref, cos_ref, sin_ref, ws_ref, convw_ref,
                   convb_ref, rq_ref, rk_ref, rkwt_ref, rv_ref, rg_ref, mq_ref, mk_ref, mvt_ref,
                   mo_ref, gt_ref, carry, *, ret_w, ml_w):
    tm = x_ref.shape[0]
    ret_dh = LANES // 2

    @pl.when(pl.program_id(1) == 0)
    def _():
        carry[...] = jnp.zeros_like(carry)

    xn = _rms(x_ref[...], g_ref[...]).astype(BF16)
    proj = lambda o, w: _dot(xn, wm_ref[:, o:o + w])
    tiles = lambda w: [slice(t, t + LANES) for t in range(0, w, LANES)]

    lane = lax.broadcasted_iota(jnp.int32, (tm, LANES), 1)
    first_half = (lane % ret_dh) < (ret_dh // 2)
    cos_t = cos_ref[...]
    sin_t = sin_ref[...]

    def rot(t):
        swapped = jnp.where(first_half, pltpu.roll(t, LANES - ret_dh // 2, 1),
                            pltpu.roll(t, ret_dh // 2, 1))
        return t * cos_t + swapped * sin_t

    for c0 in range(0, ret_w, MXU_COLS):
        rq = proj(c0, MXU_COLS)
        for ps in tiles(MXU_COLS):
            rq_ref[:, c0 + ps.start:c0 + ps.stop] = rot(rq[:, ps]).astype(BF16)
    for c0 in range(0, ret_w, MXU_COLS):
        rk = proj(ret_w + c0, MXU_COLS)
        for ps in tiles(MXU_COLS):
            cs = slice(c0 + ps.start, c0 + ps.stop)
            k = rot(rk[:, ps]) * (ret_dh ** -0.5)
            rk_ref[:, cs] = k.astype(BF16)
            rkwt_ref[cs, :] = (k * ws_ref[:, cs]).T.astype(BF16)
    for c0 in range(0, ret_w, MXU_COLS):
        cs = slice(c0, c0 + MXU_COLS)
        rv_ref[:, cs] = proj(2 * ret_w + c0, MXU_COLS).astype(BF16)
    for c0 in range(0, ret_w, MXU_COLS):
        g = proj(3 * ret_w + c0, MXU_COLS)
        rg_ref[:, c0:c0 + MXU_COLS] = g * jax.nn.sigmoid(g)

    o_mq = 4 * ret_w
    row8 = lax.broadcasted_iota(jnp.int32, (SUBLANES, MXU_COLS), 0)
    for c0 in range(0, 2 * ml_w, MXU_COLS):
        cs = slice(c0, c0 + MXU_COLS)
        xq = proj(o_mq + c0, MXU_COLS)
        prev8 = carry[:, cs]
        carry[:, cs] = xq[tm - SUBLANES:tm]
        acc = xq * convw_ref[CONV_W - 1:CONV_W, cs] + convb_ref[:, cs]
        for s in range(1, CONV_W):
            rolled = pltpu.roll(xq, s, 0)
            head = jnp.where(row8 < s, pltpu.roll(prev8, s, 0), rolled[0:SUBLANES])
            shifted = jnp.concatenate([head, rolled[SUBLANES:]], axis=0)
            acc = acc + shifted * convw_ref[CONV_W - 1 - s:CONV_W - s, cs]
        act = acc * jax.nn.sigmoid(acc)
        if c0 < ml_w:
            mq_ref[:, cs] = act.astype(BF16)
        else:
            mk_ref[:, c0 - ml_w:c0 - ml_w + MXU_COLS] = (act * (LANES ** -0.5)).astype(BF16)
    for c0 in range(0, ml_w, MXU_COLS):
        mv = proj(o_mq + 2 * ml_w + c0, MXU_COLS)
        for hs in tiles(MXU_COLS):
            mvt_ref[c0 + hs.start:c0 + hs.stop, :] = mv[:, hs].T
    for c0 in range(0, ml_w, MXU_COLS):
        mo_ref[:, c0:c0 + MXU_COLS] = jax.nn.sigmoid(proj(o_mq + 3 * ml_w + c0, MXU_COLS))
    gt_ref[...] = _dot_nt(wgt_ref[...], xn) + gb_ref[...]


def _inproj(x3, g, w_main, w_gate_t, gate_bias, cos_t, sin_t, ws, conv_w, conv_b, ret_w, ml_w,
            tm=256):
    B, S, D = x3.shape
    n_main = w_main.shape[1]
    const = lambda b, i: (0, 0)
    tok = lambda w: pl.BlockSpec((None, tm, w), lambda b, i: (b, i, 0))
    tok_t = lambda r: pl.BlockSpec((None, r, tm), lambda b, i: (b, 0, i))
    act = lambda w, dt: jax.ShapeDtypeStruct((B, S, w), dt)
    act_t = lambda r, dt: jax.ShapeDtypeStruct((B, r, S), dt)
    return pl.pallas_call(
        functools.partial(_inproj_kernel, ret_w=ret_w, ml_w=ml_w),
        grid=(B, S // tm),
        in_specs=[
            tok(D),
            pl.BlockSpec((1, D), const),
            pl.BlockSpec((D, n_main), const),
            pl.BlockSpec((GATE_ROWS, D), const),
            pl.BlockSpec((GATE_ROWS, 1), const),
            tok(LANES),
            tok(LANES),
            pl.BlockSpec((tm, ret_w), const),
            pl.BlockSpec((CONV_W, 2 * ml_w), const),
            pl.BlockSpec((1, 2 * ml_w), const),
        ],
        out_specs=[tok(ret_w), tok(ret_w), tok_t(ret_w), tok(ret_w), tok(ret_w),
                   tok(ml_w), tok(ml_w), tok_t(ml_w), tok(ml_w), tok_t(GATE_ROWS)],
        out_shape=[
            act(ret_w, BF16),
            act(ret_w, BF16),
            act_t(ret_w, BF16),
            act(ret_w, BF16),
            act(ret_w, F32),
            act(ml_w, BF16),
            act(ml_w, BF16),
            act_t(ml_w, F32),
            act(ml_w, F32),
            act_t(GATE_ROWS, F32),
        ],
        scratch_shapes=[pltpu.VMEM((SUBLANES, 2 * ml_w), F32)],
        compiler_params=pltpu.CompilerParams(
            dimension_semantics=("arbitrary", "arbitrary"), vmem_limit_bytes=VMEM_LIMIT),
        name="inproj",
    )(x3, g, w_main, w_gate_t, gate_bias, cos_t, sin_t, ws, conv_w, conv_b)


def _mixer_kernel(rq_ref, rk_ref, rkwt_ref, rv_ref, rg_ref, mq_ref, mk_ref, mvt_ref, mo_ref,
                  gt_ref, decay_ref, wq_ref, cd_ref, retgn_ref, mlgn_ref,
                  y_ref, r_state, s_state, n_state, m_state, *, ret_w, ml_w):
    BB, C = rq_ref.shape[0], rq_ref.shape[1]
    n_pairs = ret_w // LANES
    ml_heads = ml_w // LANES
    ret_dh = LANES // 2

    @pl.when(pl.program_id(1) == 0)
    def _():
        r_state[...] = jnp.zeros_like(r_state)
        s_state[...] = jnp.zeros_like(s_state)
        n_state[...] = jnp.zeros_like(n_state)
        m_state[...] = jnp.zeros_like(m_state)

    lane = lax.broadcasted_iota(jnp.int32, (C, LANES), 1)
    row = lax.broadcasted_iota(jnp.int32, (C, LANES), 0)
    assert C == LANES
    lo = lane < ret_dh
    blockdiag = (row < ret_dh) == lo
    lo_b = jnp.where(lo, 1.0, 0.0).astype(BF16)
    hi_b = jnp.where(lo, 0.0, 1.0).astype(BF16)
    seqs = range(BB)
    pair_units = [(s, p, slice(p * LANES, (p + 1) * LANES)) for s in seqs for p in range(n_pairs)]
    head_units = [(s, h, slice(h * LANES, (h + 1) * LANES)) for s in seqs for h in range(ml_heads)]


    row8 = lax.broadcasted_iota(jnp.int32, (SUBLANES, C), 0)
    lane8 = lax.broadcasted_iota(jnp.int32, (SUBLANES, C), 1)
    live = row8 < ml_heads
    triu = (row <= lane).astype(BF16)
    before = row <= lane
    mx, w_inter, e_negm, w_state, w_state_b, dec, beta_t = [], [], [], [], [], [], []
    for s in seqs:
        ig = jnp.where(live, gt_ref[s, 0:SUBLANES, :], 0.0)
        f_pre = jnp.where(live, gt_ref[s, SUBLANES:2 * SUBLANES, :], 30.0)
        l_hi, l_mid, l_lo = _split3(jax.nn.log_sigmoid(f_pre))
        b = _dot(l_hi, triu) + _dot(l_mid, triu) + _dot(l_lo, triu)
        beta = ig - b
        cm = beta
        shift = 1
        while shift < C:
            cm = jnp.maximum(cm, jnp.where(lane8 >= shift, pltpu.roll(cm, shift, 1), -jnp.inf))
            shift *= 2
        m_prev = m_state[s]
        mx.append(jnp.maximum(cm, m_prev))
        mx_last = jnp.broadcast_to(mx[s][:, C - 1:C], (SUBLANES, C))
        w_inter.append(jnp.exp(m_prev - mx[s]))
        e_negm.append(jnp.exp(-(b + mx[s])))
        w_state.append(jnp.exp(beta - mx_last))
        w_state_b.append(w_state[s].astype(BF16))
        dec.append(jnp.exp(m_prev - mx_last))
        m_state[s] = jnp.where(live, jnp.broadcast_to(b[:, C - 1:C], (SUBLANES, C)) + mx_last, 0.0)
        beta_t.append(jnp.concatenate([beta, jnp.zeros((LANES - SUBLANES, C), F32)], axis=0).T)

    hrow = lambda t, h: t[h:h + 1, :]

    rq = [rq_ref[s, :, ps] for s, p, ps in pair_units]
    rk = [rk_ref[s, :, ps] for s, p, ps in pair_units]
    rv = [rv_ref[s, :, ps] for s, p, ps in pair_units]
    r_prev = [r_state[s, p] for s, p, ps in pair_units]
    pu = range(len(pair_units))
    s_a = [_dot_nt(rq[u] * lo_b, rk[u]) for u in pu]
    s_b = [_dot_nt(rq[u] * hi_b, rk[u]) for u in pu]
    r_read = [_dot(rq[u], r_prev[u].astype(BF16)) for u in pu]
    r_new = [_dot(rkwt_ref[s, ps, :], rv[u]) for u, (s, p, ps) in enumerate(pair_units)]
    hu = range(len(head_units))
    mq = [mq_ref[s, :, hs] for s, h, hs in head_units]
    mk = [mk_ref[s, :, hs] for s, h, hs in head_units]
    mv_t = [mvt_ref[s, hs, :] for s, h, hs in head_units]
    st_prev = [s_state[s, h] for s, h, hs in head_units]
    n_prev = [n_state[s, h] for s, h, hs in head_units]
    a_t = [_dot_nt(mk[u], mq[u]) for u in hu]
    s_read = [_dot_nt(st_prev[u].astype(BF16), mq[u]) for u in hu]
    qn = [_dot_nt(n_prev[u].astype(BF16), mq[u])[0:1, :] for u in hu]
    s_new = [_dot((mv_t[u] * hrow(w_state[s], h)).astype(BF16), mk[u])
             for u, (s, h, hs) in enumerate(head_units)]
    n_new = [_dot(w_state_b[s], mk[u])[h:h + 1, :] for u, (s, h, hs) in enumerate(head_units)]
    for u, (s, p, ps) in enumerate(pair_units):
        r_state[s, p] = cd_ref[p] * r_prev[u] + jnp.where(blockdiag, r_new[u], 0.0)
    for u, (s, h, hs) in enumerate(head_units):
        s_state[s, h] = hrow(dec[s], h) * st_prev[u] + s_new[u]
        n_state[s, h] = hrow(dec[s], h) * n_prev[u] + n_new[u]

    s_a = [(s_a[u] * decay_ref[2 * p]).astype(BF16) for u, (s, p, ps) in enumerate(pair_units)]
    s_b = [(s_b[u] * decay_ref[2 * p + 1]).astype(BF16) for u, (s, p, ps) in enumerate(pair_units)]
    p_t = [jnp.exp(jnp.where(before, beta_t[s][:, h:h + 1] - hrow(mx[s], h), -jnp.inf)) * a_t[u]
           for u, (s, h, hs) in enumerate(head_units)]

    o = [_dot(s_a[u], rv[u] * lo_b) + _dot(s_b[u], rv[u] * hi_b) + r_read[u] * wq_ref[p]
         for u, (s, p, ps) in enumerate(pair_units)]
    num_t = [_dot(mv_t[u].astype(BF16), p_t[u].astype(BF16)) + hrow(w_inter[s], h) * s_read[u]
             for u, (s, h, hs) in enumerate(head_units)]

    for u, (s, p, ps) in enumerate(pair_units):
        sq = o[u] * o[u]
        ms_a = jnp.sum(jnp.where(lo, sq, 0.0), axis=-1, keepdims=True)
        ms_b = jnp.sum(jnp.where(lo, 0.0, sq), axis=-1, keepdims=True)
        ms = jnp.where(lo, ms_a, ms_b) * (1.0 / ret_dh)
        r = o[u] * lax.rsqrt(ms + EPS) * retgn_ref[:, ps]
        y_ref[s, :, ps] = (rg_ref[s, :, ps] * r).astype(y_ref.dtype)
    for u, (s, h, hs) in enumerate(head_units):
        den = jnp.sum(p_t[u], axis=0, keepdims=True) + hrow(w_inter[s], h) * qn[u]
        hh = (num_t[u] * (1.0 / jnp.maximum(jnp.abs(den), hrow(e_negm[s], h)))).T
        hm = _rms(mo_ref[s, :, hs] * hh, mlgn_ref[:, hs])
        y_ref[s, :, ret_w + h * LANES:ret_w + (h + 1) * LANES] = hm.astype(y_ref.dtype)


def _ret_tables(C):
    H = RET_HEADS
    dh = LANES // 2
    log_gamma = jnp.log1p(-(2.0 ** (-5.0 - jnp.arange(H, dtype=F32))))
    idx = jnp.arange(C, dtype=F32)
    rel = idx[:, None] - idx[None, :]
    causal = rel >= 0
    decay = jnp.where(causal, jnp.exp(log_gamma[:, None, None] * jnp.where(causal, rel, 0.0)), 0.0)
    w_state = jnp.exp(log_gamma[:, None] * (C - 1 - idx))
    w_query = jnp.exp(log_gamma[:, None] * (idx + 1.0))
    chunk_decay = jnp.exp(log_gamma * C)
    pair = lambda t: jnp.repeat(t.reshape(H // 2, 2, C).transpose(0, 2, 1), dh, axis=2)
    cd = jnp.repeat(chunk_decay.reshape(H // 2, 2), dh, axis=1)
    cd = jnp.broadcast_to(cd[:, :, None], (H // 2, LANES, LANES))
    ws = pair(w_state).transpose(1, 0, 2).reshape(C, (H // 2) * LANES)
    return decay, ws, pair(w_query), cd


def _mixer(ops, decay, wq, cd, ret_gn, ml_gn, B, S, ret_w, ml_w):
    C = CHUNK
    N = S // C
    n_pairs = ret_w // LANES
    ml_heads = ml_w // LANES
    c2 = lambda b, n: (0, 0)
    c3 = lambda b, n: (0, 0, 0)
    BB = MIXER_SEQS if B % MIXER_SEQS == 0 else 1
    tok = lambda w: pl.BlockSpec((BB, C, w), lambda b, n: (b, n, 0))
    tok_t = lambda r: pl.BlockSpec((BB, r, C), lambda b, n: (b, 0, n))
    y = pl.pallas_call(
        functools.partial(_mixer_kernel, ret_w=ret_w, ml_w=ml_w),
        grid=(B // BB, N),
        in_specs=[
            tok(ret_w), tok(ret_w), tok_t(ret_w), tok(ret_w), tok(ret_w),
            tok(ml_w), tok(ml_w), tok_t(ml_w), tok(ml_w), tok_t(GATE_ROWS),
            pl.BlockSpec((RET_HEADS, C, C), c3),
            pl.BlockSpec((n_pairs, C, LANES), c3),
            pl.BlockSpec((n_pairs, LANES, LANES), c3),
            pl.BlockSpec((1, ret_w), c2),
            pl.BlockSpec((1, ml_w), c2),
        ],
        out_specs=pl.BlockSpec((BB, C, ret_w + ml_w), lambda b, n: (b, n, 0)),
        out_shape=jax.ShapeDtypeStruct((B, S, ret_w + ml_w), BF16),
        scratch_shapes=[
            pltpu.VMEM((BB, n_pairs, LANES, LANES), F32),
            pltpu.VMEM((BB, ml_heads, LANES, LANES), F32),
            pltpu.VMEM((BB, ml_heads, SUBLANES, LANES), F32),
            pltpu.VMEM((BB, SUBLANES, LANES), F32),
        ],
        compiler_params=pltpu.CompilerParams(
            dimension_semantics=("arbitrary", "arbitrary"), vmem_limit_bytes=VMEM_LIMIT),
        name="mixer",
    )(*ops, decay, wq, cd, ret_gn.reshape(1, -1), ml_gn.reshape(1, -1))
    return y.reshape(B * S, ret_w + ml_w)


def _router_kernel(y_ref, x_ref, wo_ref, g_ref, wr_hi_ref, wr_lo_ref, br_ref,
                   h_ref, xn_ref, ri_ref, rf_ref, cnt_ref, run_cnt):
    tm = y_ref.shape[0]

    @pl.when(pl.program_id(0) == 0)
    def _():
        run_cnt[...] = jnp.zeros_like(run_cnt)

    h = x_ref[...] + _dot(y_ref[...], wo_ref[...])
    h_ref[...] = h
    xn = _rms(h, g_ref[...])
    x_hi = xn.astype(BF16)
    xn_ref[...] = x_hi.reshape(xn_ref.shape)
    x_lo = (xn - x_hi.astype(F32)).astype(BF16)
    logits = (_dot_nt(wr_hi_ref[...], x_hi) + _dot_nt(wr_hi_ref[...], x_lo)
              + _dot_nt(wr_lo_ref[...], x_hi) + br_ref[...])
    big = jnp.int32(LANES)
    neg = -jnp.inf
    gl = logits[GROUP_ROW0:GROUP_ROW0 + 8]
    grow = lax.broadcasted_iota(jnp.int32, gl.shape, 0)
    is_g = grow < N_GROUPS
    gl = jnp.where(is_g, gl, neg)
    gmax = jnp.max(gl, axis=0, keepdims=True)
    gsum = jnp.sum(jnp.where(is_g, jnp.exp(gl - gmax), 0.0), axis=0, keepdims=True)
    p_g = 1.0 / gsum
    g_sel = jnp.min(jnp.where(is_g & (gl == gmax), grow, big), axis=0, keepdims=True)
    el = logits[0:N_EXPERTS]
    erow = lax.broadcasted_iota(jnp.int32, el.shape, 0)
    in_grp = (erow // EXPERTS_PER_GROUP) == g_sel
    el = jnp.where(in_grp, el, neg)
    emax = jnp.max(el, axis=0, keepdims=True)
    eexp = jnp.where(in_grp, jnp.exp(el - emax), 0.0)
    prob = eexp / jnp.sum(eexp, axis=0, keepdims=True)
    pm1 = jnp.where(in_grp, prob, -1.0)
    p1 = jnp.max(pm1, axis=0, keepdims=True)
    i1 = jnp.min(jnp.where(pm1 == p1, erow, big), axis=0, keepdims=True)
    pm2 = jnp.where(erow == i1, -1.0, pm1)
    p2 = jnp.max(pm2, axis=0, keepdims=True)
    i2 = jnp.min(jnp.where(pm2 == p2, erow, big), axis=0, keepdims=True)
    denom = p1 + p2
    g1 = p_g * p1 / denom
    g2 = p_g * p2 / denom

    sel1 = erow == i1
    sel2 = erow == i2
    onehot = (sel1 | sel2).astype(BF16)
    r_i = lax.broadcasted_iota(jnp.int32, (tm, tm), 0)
    c_i = lax.broadcasted_iota(jnp.int32, (tm, tm), 1)
    tri = (r_i < c_i).astype(BF16)
    prefix = _dot(onehot, tri) + run_cnt[:, 0:1]
    rank1 = jnp.sum(jnp.where(sel1, prefix, 0.0), axis=0, keepdims=True).astype(jnp.int32)
    rank2 = jnp.sum(jnp.where(sel2, prefix, 0.0), axis=0, keepdims=True).astype(jnp.int32)
    new_cnt = run_cnt[...] + jnp.sum(onehot.astype(F32), axis=1, keepdims=True)
    run_cnt[...] = new_cnt
    cnt_ref[...] = new_cnt.astype(jnp.int32)

    rrow = lax.broadcasted_iota(jnp.int32, (ROUTE_ROWS, tm), 0)
    ri_ref[...] = jnp.where(rrow == 0, rank1 * N_EXPERTS + i1,
                            jnp.where(rrow == 1, rank2 * N_EXPERTS + i2, 0))
    lrow = lax.broadcasted_iota(jnp.int32, (LANES, tm), 0)
    rf_ref[...] = jnp.where(lrow == 0, g1, jnp.where(lrow == 1, g2, 0.0)).T


def _router(y, x2, w_out, g, wr_hi, wr_lo, br, tm=ROUTE_TILE):
    T, D = x2.shape
    const = lambda i: (0, 0)
    tile = lambda i: (i, 0)
    return pl.pallas_call(
        _router_kernel,
        grid=(T // tm,),
        in_specs=[
            pl.BlockSpec((tm, y.shape[1]), tile),
            pl.BlockSpec((tm, D), tile),
            pl.BlockSpec(w_out.shape, const),
            pl.BlockSpec((1, D), const),
            pl.BlockSpec((ROUTER_ROWS, D), const),
            pl.BlockSpec((ROUTER_ROWS, D), const),
            pl.BlockSpec((ROUTER_ROWS, 1), const),
        ],
        out_specs=[
            pl.BlockSpec((tm, D), tile),
            pl.BlockSpec((tm,) + ROW_TILE, lambda i: (i, 0, 0)),
            pl.BlockSpec((None, ROUTE_ROWS, tm), lambda i: (i, 0, 0)),
            pl.BlockSpec((tm, LANES), tile),
            pl.BlockSpec((N_EXPERTS, LANES), const),
        ],
        out_shape=[
            jax.ShapeDtypeStruct((T, D), F32),
            jax.ShapeDtypeStruct((T,) + ROW_TILE, BF16),
            jax.ShapeDtypeStruct((T // tm, ROUTE_ROWS, tm), jnp.int32),
            jax.ShapeDtypeStruct((T, LANES), F32),
            jax.ShapeDtypeStruct((N_EXPERTS, LANES), jnp.int32),
        ],
        scratch_shapes=[pltpu.VMEM((N_EXPERTS, LANES), F32)],
        compiler_params=pltpu.CompilerParams(
            dimension_semantics=("arbitrary",), vmem_limit_bytes=VMEM_LIMIT),
        name="router",
    )(y, x2, w_out, g, wr_hi, wr_lo, br)


def _slot_base(tok0):
    return (tok0 // ROUTE_TILE) * (2 * ROUTE_TILE) + tok0 % ROUTE_TILE


def _route_spans(tm):
    assert tm % ROUTE_TILE == 0 or ROUTE_TILE % tm == 0
    span = min(tm, ROUTE_TILE)
    return [(t0, span) for t0 in range(0, tm, span)]


def _for_each_pad_block(pend_ref, nu_ref, n_blocks, fn):
    blk = EXPERT_BLOCK
    for e in range(N_EXPERTS):
        prev_end = 0 if e == 0 else pend_ref[e - 1]

        @pl.when(pend_ref[e] > prev_end)
        def _():
            fn(pl.multiple_of(pend_ref[e] - blk, blk))

    def tail(j, c):
        fn(pl.multiple_of(j * blk, blk))
        return c

    lax.fori_loop(nu_ref[0], n_blocks, tail, 0)


def _dispatch_kernel(dest_ref, pend_ref, nu_ref, xn_ref, buf_ref, zeros, sem, zsem):
    tm = xn_ref.shape[0]
    blk = EXPERT_BLOCK
    i = pl.program_id(0)

    @pl.when(i == 0)
    def _():
        zeros[...] = jnp.zeros_like(zeros)
        zcopy = lambda row: pltpu.make_async_copy(zeros, buf_ref.at[pl.ds(row, blk)], zsem)
        n_blocks = buf_ref.shape[0] // blk
        _for_each_pad_block(pend_ref, nu_ref, n_blocks, lambda row: zcopy(row).start())
        _for_each_pad_block(pend_ref, nu_ref, n_blocks, lambda row: zcopy(row).wait())

    for t0, span in _route_spans(tm):
        slot0 = _slot_base(i * tm + t0)

        def issue(g, c, t0=t0, slot0=slot0):
            for j in range(SUBLANES):
                for k in range(2):
                    t = g * SUBLANES + j
                    d = dest_ref[slot0 + k * ROUTE_TILE + t]
                    pltpu.make_async_copy(xn_ref.at[t0 + t], buf_ref.at[d], sem).start(priority=k)
            return c

        lax.fori_loop(0, span // SUBLANES, issue, 0)
    for k in range(2):
        pltpu.make_async_copy(buf_ref.at[pl.ds(tm, tm)], buf_ref.at[pl.ds(0, tm)], sem).wait()


def _dispatch(dest, pad_end, n_used, xn, n_rows, tm=ROUTE_TILE):
    T = xn.shape[0]
    return pl.pallas_call(
        _dispatch_kernel,
        grid_spec=pltpu.PrefetchScalarGridSpec(
            num_scalar_prefetch=3,
            grid=(T // tm,),
            in_specs=[pl.BlockSpec((tm,) + ROW_TILE, lambda i, *_: (i, 0, 0))],
            out_specs=pl.BlockSpec(memory_space=pl.ANY),
            scratch_shapes=[
                pltpu.VMEM((EXPERT_BLOCK,) + ROW_TILE, xn.dtype),
                pltpu.SemaphoreType.DMA,
                pltpu.SemaphoreType.DMA,
            ],
        ),
        out_shape=jax.ShapeDtypeStruct((n_rows,) + ROW_TILE, xn.dtype),
        compiler_params=pltpu.CompilerParams(
            dimension_semantics=("arbitrary",), vmem_limit_bytes=VMEM_LIMIT),
        name="dispatch",
    )(dest, pad_end, n_used, xn)


def _expert_kernel(be_ref, nu_ref, nv_ref, first_ref, next_ref, slot_ref, x_ref, w1_hbm, w3_hbm,
                   w2_hbm, y_ref, s1, s3, s2, w1b, w3b, w2b, sems):
    j = pl.program_id(0)
    used = j < nu_ref[0]
    blk = x_ref.shape[0]
    half = blk // 2

    def weight_copies(e, slot):
        return [pltpu.make_async_copy(w.at[e], s.at[slot], sems.at[slot])
                for w, s in ((w1_hbm, s1), (w3_hbm, s3), (w2_hbm, s2))]

    @pl.when(j == 0)
    def _():
        for c in weight_copies(be_ref[0], 0):
            c.start()

    @pl.when(used & (first_ref[j] == 1))
    def _():
        slot = slot_ref[j]
        for c in weight_copies(be_ref[j], slot):
            c.wait()

        @pl.when(next_ref[j] >= 0)
        def _():
            for c in weight_copies(next_ref[j], 1 - slot):
                c.start()

        w1b[...] = s1[slot].astype(BF16)
        w3b[...] = s3[slot].astype(BF16)
        w2b[...] = s2[slot].astype(BF16)

    def swiglu(rows):
        n = rows.stop - rows.start
        x = x_ref[rows].reshape(n, -1)
        a = _dot(x, w1b[...])
        hmid = a * jax.nn.sigmoid(a) * _dot(x, w3b[...])
        y_ref[rows] = _dot(hmid.astype(BF16), w2b[...]).reshape((n,) + y_ref.shape[1:])

    @pl.when(used & (nv_ref[j] > half))
    def _():
        swiglu(slice(0, blk))

    @pl.when(used & (nv_ref[j] <= half))
    def _():
        swiglu(slice(0, half))
        y_ref[half:blk] = jnp.zeros((blk - half,) + y_ref.shape[1:], y_ref.dtype)

    @pl.when(jnp.logical_not(used))
    def _():
        y_ref[...] = jnp.zeros_like(y_ref)


def _experts(block_expert, n_used, n_valid, first, next_expert, slot, x_buf, w1, w3, w2):
    P = x_buf.shape[0]
    D = w1.shape[1]
    assert x_buf.shape[1:] == ROW_TILE and D == SUBLANES * LANES
    blk = EXPERT_BLOCK
    d_exp = w2.shape[1]
    hbm = pl.BlockSpec(memory_space=pl.ANY)
    return pl.pallas_call(
        _expert_kernel,
        grid_spec=pltpu.PrefetchScalarGridSpec(
            num_scalar_prefetch=6,
            grid=(P // blk,),
            in_specs=[
                pl.BlockSpec((blk,) + ROW_TILE,
                             lambda j, be, nu, *_: (jnp.minimum(j, nu[0] - 1), 0, 0)),
                hbm, hbm, hbm,
            ],
            out_specs=pl.BlockSpec((blk,) + ROW_TILE, lambda j, *_: (j, 0, 0)),
            scratch_shapes=[
                pltpu.VMEM((2, D, d_exp), F32),
                pltpu.VMEM((2, D, d_exp), F32),
                pltpu.VMEM((2, d_exp, D), F32),
                pltpu.VMEM((D, d_exp), BF16),
                pltpu.VMEM((D, d_exp), BF16),
                pltpu.VMEM((d_exp, D), BF16),
                pltpu.SemaphoreType.DMA((2,)),
            ],
        ),
        out_shape=jax.ShapeDtypeStruct((P,) + ROW_TILE, F32),
        compiler_params=pltpu.CompilerParams(
            dimension_semantics=("arbitrary",), vmem_limit_bytes=VMEM_LIMIT),
        name="experts",
    )(block_expert, n_used, n_valid, first, next_expert, slot, x_buf, w1, w3, w2)


def _final_kernel(dest_ref, h_ref, rf_ref, p_ref, wup_ref, gple_ref, ggate_ref, wgate_ref,
                  gfin_ref, ybuf_ref, out_ref, *scratch, tm):
    row_bufs, sems = scratch[:-1], scratch[-1]
    s = pl.program_id(0)
    D = h_ref.shape[1]
    assert ROUTE_TILE % tm == 0

    def gather(tile, bufs, sem, straight=False):
        slot0 = _slot_base(tile * tm)

        def issue(g, c):
            for j in range(SUBLANES):
                for k in range(2):
                    t = g * SUBLANES + j
                    d = dest_ref[slot0 + k * ROUTE_TILE + t]
                    pltpu.make_async_copy(ybuf_ref.at[d], bufs[k].at[t], sem).start(priority=k)
            return c

        if straight:
            for g in range(tm // SUBLANES):
                issue(g, 0)
        else:
            lax.fori_loop(0, tm // SUBLANES, issue, 0)

    def wait(sem):
        for k in range(2):
            pltpu.make_async_copy(ybuf_ref.at[pl.ds(0, tm)], ybuf_ref.at[pl.ds(tm, tm)], sem).wait()

    def compute(rows, bufs):
        e = _rms(_dot(p_ref[rows, :].astype(BF16), wup_ref[...]), gple_ref[...])
        rf = rf_ref[rows, :]
        h = (h_ref[rows, :] + rf[:, 0:1] * bufs[0][...].reshape(tm, D)
             + rf[:, 1:2] * bufs[1][...].reshape(tm, D))
        gate = jax.nn.sigmoid(_dot(_rms(h, ggate_ref[...]).astype(BF16), wgate_ref[...]))
        h = h + gate * e
        out_ref[rows, :] = _rms(h, gfin_ref[...])

    n_tiles = pl.num_programs(0) * FINAL_TILES
    bufs = [(row_bufs[2 * i], row_bufs[2 * i + 1]) for i in range(FINAL_TILES)]

    @pl.when(s == 0)
    def _():
        for i in range(FINAL_AHEAD):
            gather(i, bufs[i], sems.at[i])

    for i in range(FINAL_TILES):
        wait(sems.at[i])
        nxt = (i + FINAL_AHEAD) % FINAL_TILES
        gather(jnp.minimum(s * FINAL_TILES + i + FINAL_AHEAD, n_tiles - 1), bufs[nxt],
               sems.at[nxt], straight=True)
        compute(slice(i * tm, (i + 1) * tm), bufs[i])

    @pl.when(s == pl.num_programs(0) - 1)
    def _():
        for i in range(FINAL_AHEAD):
            wait(sems.at[i])


def _final(dest, h, rf, p2, w_up, g_ple, g_gate, w_gate, g_fin, y_buf, tm=256):
    T, D = h.shape
    const = lambda i, *_: (0, 0)
    tile = lambda i, *_: (i, 0)
    assert y_buf.shape[1:] == ROW_TILE and D == SUBLANES * LANES
    row_buf = pltpu.VMEM((tm,) + ROW_TILE, F32)
    return pl.pallas_call(
        functools.partial(_final_kernel, tm=tm),
        grid_spec=pltpu.PrefetchScalarGridSpec(
            num_scalar_prefetch=1,
            grid=(T // (FINAL_TILES * tm),),
            in_specs=[
                pl.BlockSpec((FINAL_TILES * tm, D), tile),
                pl.BlockSpec((FINAL_TILES * tm, LANES), tile),
                pl.BlockSpec((FINAL_TILES * tm, p2.shape[1]), tile),
                pl.BlockSpec(w_up.shape, const),
                pl.BlockSpec((1, D), const),
                pl.BlockSpec((1, D), const),
                pl.BlockSpec(w_gate.shape, const),
                pl.BlockSpec((1, D), const),
                pl.BlockSpec(memory_space=pl.ANY),
            ],
            out_specs=pl.BlockSpec((FINAL_TILES * tm, D), tile),
            scratch_shapes=[row_buf] * (2 * FINAL_TILES) + [pltpu.SemaphoreType.DMA((FINAL_TILES,))],
        ),
        out_shape=jax.ShapeDtypeStruct((T, D), F32),
        compiler_params=pltpu.CompilerParams(
            dimension_semantics=("arbitrary",), vmem_limit_bytes=VMEM_LIMIT),
        name="final",
    )(dest, h, rf, p2, w_up, g_ple, g_gate, w_gate, g_fin, y_buf)


def _layer(h2, p2, positions, B, S, attn_norm, w_in, conv_w, conv_b, b_igate, b_fgate, ret_gn,
           ml_gn, w_out, moe_norm, w_group, b_group, w_router, b_router, w1, w3, w2, w_ple_up,
           ple_norm, ple_gate_norm, w_ple_gate, out_norm):
    T, D = h2.shape
    ret_w = ret_gn.shape[0]
    ml_w = ml_gn.shape[0]
    n_main = 4 * ret_w + 4 * ml_w
    row = lambda v: v.reshape(1, -1).astype(F32)
    pad_lanes = lambda a: jnp.pad(a, ((0, 0), (0, LANES - a.shape[1])))

    ml_heads = ml_w // LANES
    w_main = w_in[:, :n_main].astype(BF16)
    gate_rows = lambda a: jnp.pad(a, ((0, SUBLANES - ml_heads), (0, 0)))
    w_i, w_f = w_in[:, n_main:n_main + ml_heads].T, w_in[:, n_main + ml_heads:].T
    w_gate_t = jnp.concatenate([gate_rows(w_i), gate_rows(w_f)]).astype(BF16)
    gate_bias = jnp.concatenate([gate_rows(b_igate[:, None]), gate_rows(b_fgate[:, None])])
    cos_t, sin_t = _rope_tables(positions, LANES // 2)
    decay, ws, wq, cd = _ret_tables(CHUNK)
    tm_in = 2 * CHUNK
    ops = _inproj(h2.reshape(B, S, D), row(attn_norm), w_main, w_gate_t, gate_bias.astype(F32),
                  cos_t.reshape(B, S, LANES), sin_t.reshape(B, S, LANES),
                  jnp.tile(ws, (tm_in // CHUNK, 1)), conv_w, conv_b.reshape(1, -1), ret_w, ml_w,
                  tm=tm_in)

    y = _mixer(ops, decay, wq, cd, ret_gn, ml_gn, B, S, ret_w, ml_w)

    pad_rows = lambda a: jnp.pad(a, ((0, ROUTER_ROWS - a.shape[0]), (0, 0)))
    wr = pad_rows(jnp.concatenate([w_router, w_group], axis=1).T)
    wr_hi = wr.astype(BF16)
    wr_lo = (wr - wr_hi.astype(F32)).astype(BF16)
    br = pad_rows(jnp.concatenate([b_router, b_group])[:, None].astype(F32))
    h_mid, xn, ri, rf, counts = _router(y, h2, w_out.astype(BF16), row(moe_norm), wr_hi, wr_lo, br)

    blk = EXPERT_BLOCK
    counts = counts[:, 0]
    padded = (counts + blk - 1) // blk * blk
    pad_end = jnp.cumsum(padded)
    pad_start = pad_end - padded
    n_blocks = (2 * T) // blk + N_EXPERTS
    block_start = jnp.arange(n_blocks, dtype=jnp.int32) * blk
    block_expert = jnp.minimum(jnp.sum(pad_end[None, :] <= block_start[:, None], axis=1),
                               N_EXPERTS - 1).astype(jnp.int32)
    n_used = (pad_end[-1:] // blk).astype(jnp.int32)
    real_end = jnp.sum(jnp.where(block_expert[:, None] == jnp.arange(N_EXPERTS)[None, :],
                                 (pad_start + counts)[None, :], 0), axis=1)
    n_valid = jnp.clip(real_end - block_start, 0, blk).astype(jnp.int32)
    blk_id = jnp.arange(n_blocks, dtype=jnp.int32)
    is_used = blk_id < n_used[0]
    first = is_used & jnp.concatenate([jnp.ones((1,), bool), block_expert[1:] != block_expert[:-1]])
    seg_end = jnp.sum(jnp.where(block_expert[:, None] == jnp.arange(N_EXPERTS)[None, :],
                                (pad_end // blk)[None, :], 0), axis=1)
    next_blk = jnp.minimum(seg_end, n_blocks - 1)
    next_expert = jnp.sum(jnp.where(next_blk[:, None] == blk_id[None, :], block_expert[None, :], 0),
                          axis=1)
    next_expert = jnp.where(seg_end < n_used[0], next_expert, -1).astype(jnp.int32)
    slot = ((jnp.cumsum(first.astype(jnp.int32)) - 1) % 2).astype(jnp.int32)
    codes = ri[:, 0:2, :].reshape(-1, ROUTE_TILE)
    expert = codes % N_EXPERTS
    dest = codes // N_EXPERTS
    for e in range(N_EXPERTS):
        dest = dest + jnp.where(expert == e, pad_start[e], 0)
    dest = dest.astype(jnp.int32).reshape(-1)

    x_buf = _dispatch(dest, pad_end.astype(jnp.int32), n_used, xn, n_blocks * blk,
                      tm=min(T, DISPATCH_TILE))
    y_buf = _experts(block_expert, n_used, n_valid, first.astype(jnp.int32), next_expert, slot,
                     x_buf, w1, w3, w2)

    return _final(dest, h_mid, rf, p2, w_ple_up.astype(BF16), row(ple_norm), row(ple_gate_norm),
                  w_ple_gate.astype(BF16), row(out_norm), y_buf)


def kernel(x, p, positions, attn_norm, w_in, conv_w, conv_b, b_igate, b_fgate, ret_gn, ml_gn,
           w_out, moe_norm, w_group, b_group, w_router, b_router, w1, w3, w2, w_ple_up, ple_norm,
           ple_gate_norm, w_ple_gate, final_norm):
    B, S, D = x.shape
    depth = p.shape[0]
    assert depth == 1, "the final RMSNorm is fused into the layer's last kernel"
    out = _layer(x.reshape(B * S, D), p[0].reshape(B * S, -1), positions, B, S,
                 attn_norm[0], w_in[0], conv_w[0], conv_b[0], b_igate[0], b_fgate[0], ret_gn[0],
                 ml_gn[0], w_out[0], moe_norm[0], w_group[0], b_group[0], w_router[0],
                 b_router[0], w1[0], w3[0], w2[0], w_ple_up[0], ple_norm[0], ple_gate_norm[0],
                 w_ple_gate[0], final_norm)
    return out.reshape(B, S, D)
```

```python
import functools

import jax
import jax.numpy as jnp
from jax import lax
from jax.experimental import pallas as pl
from jax.experimental.pallas import tpu as pltpu

F32 = jnp.float32
BF16 = jnp.bfloat16

RET_HEADS = 8
ML_HEADS = 4
CHUNK = 128
CONV_W = 4
ROPE_BASE = 10000.0
N_GROUPS = 4
EXPERTS_PER_GROUP = 8
N_EXPERTS = N_GROUPS * EXPERTS_PER_GROUP
EPS = 1e-6

LANES = 128
SUBLANES = 8
MXU_COLS = 256
MIXER_SEQS = 2
FINAL_TILES = 4
FINAL_AHEAD = 2
VMEM_LIMIT = 56 * 1024 * 1024
EXPERT_BLOCK = 512
GROUP_ROW0 = N_EXPERTS
ROUTER_ROWS = 64
ROUTE_TILE = 1024
ROUTE_ROWS = 8
DISPATCH_TILE = 2 * ROUTE_TILE
GATE_ROWS = 2 * SUBLANES
ROW_TILE = (SUBLANES, LANES)


def _rms(x, g):
    return x * lax.rsqrt(jnp.mean(x * x, axis=-1, keepdims=True) + EPS) * g


def _dot(a, b):
    return jnp.dot(a, b, preferred_element_type=F32)


def _dot_nt(a, b):
    return lax.dot_general(a, b, (((1,), (1,)), ((), ())), preferred_element_type=F32)


def _split3(x):
    hi = x.astype(BF16)
    r1 = x - hi.astype(F32)
    mid = r1.astype(BF16)
    lo = (r1 - mid.astype(F32)).astype(BF16)
    return hi, mid, lo


def _rope_kernel(pos_ref, freq_ref, sign_ref, cos_ref, sin_ref, *, half):
    ang = pos_ref[...].astype(F32) * freq_ref[...]
    cos_c = jnp.cos(ang)
    sin_c = jnp.sin(ang)
    group = lax.broadcasted_iota(jnp.int32, ang.shape, 1) // half
    per_row = LANES // half
    for s in range(per_row):
        def spread(t):
            out = t
            for g in range(per_row):
                if g != s:
                    out = jnp.where(group == g, pltpu.roll(t, (half * (g - s)) % LANES, 1), out)
            return out
        cos_ref[s] = spread(cos_c)
        sin_ref[s] = spread(sin_c) * sign_ref[...]


def _rope_tables(positions, dh):
    half = dh // 2
    per_row = LANES // half
    T = positions.size
    rows = T // per_row
    tr = min(rows, 1024)
    freqs = ROPE_BASE ** (-jnp.arange(half, dtype=F32) / half)
    pos_c = jnp.repeat(positions.reshape(per_row, rows).T, half, axis=1)
    sign = jnp.tile(jnp.concatenate([-jnp.ones((half,), F32), jnp.ones((half,), F32)]),
                    LANES // dh)[None, :]
    const = pl.BlockSpec((1, LANES), lambda i: (0, 0))
    out = pl.BlockSpec((per_row, tr, LANES), lambda i: (0, i, 0))
    cos_t, sin_t = pl.pallas_call(
        functools.partial(_rope_kernel, half=half),
        grid=(rows // tr,),
        in_specs=[pl.BlockSpec((tr, LANES), lambda i: (i, 0)), const, const],
        out_specs=[out, out],
        out_shape=[jax.ShapeDtypeStruct((per_row, rows, LANES), F32)] * 2,
        name="rope",
    )(pos_c, jnp.tile(freqs, per_row)[None, :], sign)
    return cos_t.reshape(T, LANES), sin_t.reshape(T, LANES)


def _inproj_kernel(x_ref, g_ref, wm_ref, wgt_ref, gb_ref, cos_ref, sin_ref, ws_ref, convw_ref,
                   convb_ref, rq_ref, rk_ref, rkwt_ref, rv_ref, rg_ref, mq_ref, mk_ref, mvt_ref,
                   mo_ref, gt_ref, carry, *, ret_w, ml_w):
    tm = x_ref.shape[0]
    ret_dh = LANES // 2

    @pl.when(pl.program_id(1) == 0)
    def _():
        carry[0:SUBLANES] = jnp.zeros((SUBLANES, carry.shape[1]), F32)

    xn = _rms(x_ref[...], g_ref[...]).astype(BF16)
    proj = lambda o, w: _dot(xn, wm_ref[:, o:o + w])
    tiles = lambda w: [slice(t, t + LANES) for t in range(0, w, LANES)]

    lane = lax.broadcasted_iota(jnp.int32, (tm, LANES), 1)
    first_half = (lane % ret_dh) < (ret_dh // 2)
    cos_t = cos_ref[...]
    sin_t = sin_ref[...]

    def rot(t):
        swapped = jnp.where(first_half, pltpu.roll(t, LANES - ret_dh // 2, 1),
                            pltpu.roll(t, ret_dh // 2, 1))
        return t * cos_t + swapped * sin_t

    for c0 in range(0, ret_w, MXU_COLS):
        rq = proj(c0, MXU_COLS)
        for ps in tiles(MXU_COLS):
            rq_ref[:, c0 + ps.start:c0 + ps.stop] = rot(rq[:, ps]).astype(BF16)
    for c0 in range(0, ret_w, MXU_COLS):
        rk = proj(ret_w + c0, MXU_COLS)
        for ps in tiles(MXU_COLS):
            cs = slice(c0 + ps.start, c0 + ps.stop)
            k = rot(rk[:, ps]) * (ret_dh ** -0.5)
            rk_ref[:, cs] = k.astype(BF16)
            rkwt_ref[cs, :] = (k * ws_ref[:, cs]).T.astype(BF16)
    for c0 in range(0, ret_w, MXU_COLS):
        cs = slice(c0, c0 + MXU_COLS)
        rv_ref[:, cs] = proj(2 * ret_w + c0, MXU_COLS).astype(BF16)
    for c0 in range(0, ret_w, MXU_COLS):
        g = proj(3 * ret_w + c0, MXU_COLS)
        rg_ref[:, c0:c0 + MXU_COLS] = g * jax.nn.sigmoid(g)

    o_mq = 4 * ret_w
    for c0 in range(0, 2 * ml_w, MXU_COLS):
        cs = slice(c0, c0 + MXU_COLS)
        xq = proj(o_mq + c0, MXU_COLS)
        carry[SUBLANES:SUBLANES + tm, cs] = xq
        acc = xq * convw_ref[CONV_W - 1:CONV_W, cs] + convb_ref[:, cs]
        for s in range(1, CONV_W):
            shifted = carry[SUBLANES - s:SUBLANES - s + tm, cs]
            acc = acc + shifted * convw_ref[CONV_W - 1 - s:CONV_W - s, cs]
        carry[0:SUBLANES, cs] = xq[tm - SUBLANES:tm]
        act = acc * jax.nn.sigmoid(acc)
        if c0 < ml_w:
            mq_ref[:, cs] = act.astype(BF16)
        else:
            mk_ref[:, c0 - ml_w:c0 - ml_w + MXU_COLS] = (act * (LANES ** -0.5)).astype(BF16)
    for c0 in range(0, ml_w, MXU_COLS):
        mv = proj(o_mq + 2 * ml_w + c0, MXU_COLS)
        for hs in tiles(MXU_COLS):
            mvt_ref[c0 + hs.start:c0 + hs.stop, :] = mv[:, hs].T
    for c0 in range(0, ml_w, MXU_COLS):
        mo_ref[:, c0:c0 + MXU_COLS] = jax.nn.sigmoid(proj(o_mq + 3 * ml_w + c0, MXU_COLS))
    gt_ref[...] = _dot_nt(wgt_ref[...], xn) + gb_ref[...]


def _inproj(x3, g, w_main, w_gate_t, gate_bias, cos_t, sin_t, ws, conv_w, conv_b, ret_w, ml_w,
            tm=256):
    B, S, D = x3.shape
    n_main = 4 * (ret_w + ml_w)
    const = lambda b, i: (0, 0)
    tok = lambda w: pl.BlockSpec((None, tm, w), lambda b, i: (b, i, 0))
    tok_t = lambda r: pl.BlockSpec((None, r, tm), lambda b, i: (b, 0, i))
    act = lambda w, dt: jax.ShapeDtypeStruct((B, S, w), dt)
    act_t = lambda r, dt: jax.ShapeDtypeStruct((B, r, S), dt)
    return pl.pallas_call(
        functools.partial(_inproj_kernel, ret_w=ret_w, ml_w=ml_w),
        grid=(B, S // tm),
        in_specs=[
            tok(D),
            pl.BlockSpec((1, D), const),
            pl.BlockSpec((D, n_main), const),
            pl.BlockSpec((GATE_ROWS, D), const),
            pl.BlockSpec((GATE_ROWS, 1), const),
            tok(LANES),
            tok(LANES),
            pl.BlockSpec((tm, ret_w), const),
            pl.BlockSpec((CONV_W, 2 * ml_w), const),
            pl.BlockSpec((1, 2 * ml_w), const),
        ],
        out_specs=[tok(ret_w), tok(ret_w), tok_t(ret_w), tok(ret_w), tok(ret_w),
                   tok(ml_w), tok(ml_w), tok_t(ml_w), tok(ml_w), tok_t(GATE_ROWS)],
        out_shape=[
            act(ret_w, BF16),
            act(ret_w, BF16),
            act_t(ret_w, BF16),
            act(ret_w, BF16),
            act(ret_w, F32),
            act(ml_w, BF16),
            act(ml_w, BF16),
            act_t(ml_w, F32),
            act(ml_w, F32),
            act_t(GATE_ROWS, F32),
        ],
        scratch_shapes=[pltpu.VMEM((SUBLANES + tm, 2 * ml_w), F32)],
        compiler_params=pltpu.CompilerParams(
            dimension_semantics=("arbitrary", "arbitrary"), vmem_limit_bytes=VMEM_LIMIT),
        name="inproj",
    )(x3, g, w_main, w_gate_t, gate_bias, cos_t, sin_t, ws, conv_w, conv_b)


def _mixer_kernel(rq_ref, rk_ref, rkwt_ref, rv_ref, rg_ref, mq_ref, mk_ref, mvt_ref, mo_ref,
                  gt_ref, decay_ref, wq_ref, cd_ref, retgn_ref, mlgn_ref,
                  y_ref, r_state, s_state, n_state, m_state, *, ret_w, ml_w):
    BB, C = rq_ref.shape[0], rq_ref.shape[1]
    n_pairs = ret_w // LANES
    ml_heads = ml_w // LANES
    ret_dh = LANES // 2

    @pl.when(pl.program_id(1) == 0)
    def _():
        r_state[...] = jnp.zeros_like(r_state)
        s_state[...] = jnp.zeros_like(s_state)
        n_state[...] = jnp.zeros_like(n_state)
        m_state[...] = jnp.zeros_like(m_state)

    lane = lax.broadcasted_iota(jnp.int32, (C, LANES), 1)
    row = lax.broadcasted_iota(jnp.int32, (C, LANES), 0)
    assert C == LANES
    lo = lane < ret_dh
    blockdiag = (row < ret_dh) == lo
    lo_b = jnp.where(lo, 1.0, 0.0).astype(BF16)
    hi_b = jnp.where(lo, 0.0, 1.0).astype(BF16)
    seqs = range(BB)
    pair_units = [(s, p, slice(p * LANES, (p + 1) * LANES)) for s in seqs for p in range(n_pairs)]
    head_units = [(s, h, slice(h * LANES, (h + 1) * LANES)) for s in seqs for h in range(ml_heads)]


    row8 = lax.broadcasted_iota(jnp.int32, (SUBLANES, C), 0)
    lane8 = lax.broadcasted_iota(jnp.int32, (SUBLANES, C), 1)
    live = row8 < ml_heads
    triu = (row <= lane).astype(BF16)
    before = row <= lane
    mx, w_inter, e_negm, w_state, w_state_b, dec, beta_t = [], [], [], [], [], [], []
    for s in seqs:
        ig = jnp.where(live, gt_ref[s, 0:SUBLANES, :], 0.0)
        f_pre = jnp.where(live, gt_ref[s, SUBLANES:2 * SUBLANES, :], 30.0)
        l_hi, l_mid, l_lo = _split3(jax.nn.log_sigmoid(f_pre))
        b = _dot(l_hi, triu) + _dot(l_mid, triu) + _dot(l_lo, triu)
        beta = ig - b
        cm = beta
        shift = 1
        while shift < C:
            cm = jnp.maximum(cm, jnp.where(lane8 >= shift, pltpu.roll(cm, shift, 1), -jnp.inf))
            shift *= 2
        m_prev = m_state[s]
        mx.append(jnp.maximum(cm, m_prev))
        mx_last = jnp.broadcast_to(mx[s][:, C - 1:C], (SUBLANES, C))
        w_inter.append(jnp.exp(m_prev - mx[s]))
        e_negm.append(jnp.exp(-(b + mx[s])))
        w_state.append(jnp.exp(beta - mx_last))
        w_state_b.append(w_state[s].astype(BF16))
        dec.append(jnp.exp(m_prev - mx_last))
        m_state[s] = jnp.where(live, jnp.broadcast_to(b[:, C - 1:C], (SUBLANES, C)) + mx_last, 0.0)
        beta_t.append(jnp.concatenate([beta, jnp.zeros((LANES - SUBLANES, C), F32)], axis=0).T)

    hrow = lambda t, h: t[h:h + 1, :]

    rq = [rq_ref[s, :, ps] for s, p, ps in pair_units]
    rk = [rk_ref[s, :, ps] for s, p, ps in pair_units]
    rv = [rv_ref[s, :, ps] for s, p, ps in pair_units]
    r_prev = [r_state[s, p] for s, p, ps in pair_units]
    pu = range(len(pair_units))
    s_a = [_dot_nt(rq[u] * lo_b, rk[u]) for u in pu]
    s_b = [_dot_nt(rq[u] * hi_b, rk[u]) for u in pu]
    r_read = [_dot(rq[u], r_prev[u].astype(BF16)) for u in pu]
    r_new = [_dot(rkwt_ref[s, ps, :], rv[u]) for u, (s, p, ps) in enumerate(pair_units)]
    hu = range(len(head_units))
    mq = [mq_ref[s, :, hs] for s, h, hs in head_units]
    mk = [mk_ref[s, :, hs] for s, h, hs in head_units]
    mv_t = [mvt_ref[s, hs, :] for s, h, hs in head_units]
    st_prev = [s_state[s, h] for s, h, hs in head_units]
    n_prev = [n_state[s, h] for s, h, hs in head_units]
    a_t = [_dot_nt(mk[u], mq[u]) for u in hu]
    s_read = [_dot_nt(st_prev[u].astype(BF16), mq[u]) for u in hu]
    qn = [_dot_nt(n_prev[u].astype(BF16), mq[u])[0:1, :] for u in hu]
    s_new = [_dot((mv_t[u] * hrow(w_state[s], h)).astype(BF16), mk[u])
             for u, (s, h, hs) in enumerate(head_units)]
    n_new = [_dot(w_state_b[s], mk[u])[h:h + 1, :] for u, (s, h, hs) in enumerate(head_units)]
    for u, (s, p, ps) in enumerate(pair_units):
        r_state[s, p] = cd_ref[p] * r_prev[u] + jnp.where(blockdiag, r_new[u], 0.0)
    for u, (s, h, hs) in enumerate(head_units):
        s_state[s, h] = hrow(dec[s], h) * st_prev[u] + s_new[u]
        n_state[s, h] = hrow(dec[s], h) * n_prev[u] + n_new[u]

    s_a = [(s_a[u] * decay_ref[2 * p]).astype(BF16) for u, (s, p, ps) in enumerate(pair_units)]
    s_b = [(s_b[u] * decay_ref[2 * p + 1]).astype(BF16) for u, (s, p, ps) in enumerate(pair_units)]
    p_t = [jnp.exp(jnp.where(before, beta_t[s][:, h:h + 1] - hrow(mx[s], h), -jnp.inf)) * a_t[u]
           for u, (s, h, hs) in enumerate(head_units)]

    o = [_dot(s_a[u], rv[u] * lo_b) + _dot(s_b[u], rv[u] * hi_b) + r_read[u] * wq_ref[p]
         for u, (s, p, ps) in enumerate(pair_units)]
    num_t = [_dot(mv_t[u].astype(BF16), p_t[u].astype(BF16)) + hrow(w_inter[s], h) * s_read[u]
             for u, (s, h, hs) in enumerate(head_units)]

    for u, (s, p, ps) in enumerate(pair_units):
        sq = o[u] * o[u]
        ms_a = jnp.sum(jnp.where(lo, sq, 0.0), axis=-1, keepdims=True)
        ms_b = jnp.sum(jnp.where(lo, 0.0, sq), axis=-1, keepdims=True)
        ms = jnp.where(lo, ms_a, ms_b) * (1.0 / ret_dh)
        r = o[u] * lax.rsqrt(ms + EPS) * retgn_ref[:, ps]
        y_ref[s, :, ps] = (rg_ref[s, :, ps] * r).astype(y_ref.dtype)
    for u, (s, h, hs) in enumerate(head_units):
        den = jnp.sum(p_t[u], axis=0, keepdims=True) + hrow(w_inter[s], h) * qn[u]
        hh = (num_t[u] * (1.0 / jnp.maximum(jnp.abs(den), hrow(e_negm[s], h)))).T
        hm = _rms(mo_ref[s, :, hs] * hh, mlgn_ref[:, hs])
        y_ref[s, :, ret_w + h * LANES:ret_w + (h + 1) * LANES] = hm.astype(y_ref.dtype)


def _ret_tables(C):
    H = RET_HEADS
    dh = LANES // 2
    log_gamma = jnp.log1p(-(2.0 ** (-5.0 - jnp.arange(H, dtype=F32))))
    idx = jnp.arange(C, dtype=F32)
    rel = idx[:, None] - idx[None, :]
    causal = rel >= 0
    decay = jnp.where(causal, jnp.exp(log_gamma[:, None, None] * jnp.where(causal, rel, 0.0)), 0.0)
    w_state = jnp.exp(log_gamma[:, None] * (C - 1 - idx))
    w_query = jnp.exp(log_gamma[:, None] * (idx + 1.0))
    chunk_decay = jnp.exp(log_gamma * C)
    pair = lambda t: jnp.repeat(t.reshape(H // 2, 2, C).transpose(0, 2, 1), dh, axis=2)
    cd = jnp.repeat(chunk_decay.reshape(H // 2, 2), dh, axis=1)
    cd = jnp.broadcast_to(cd[:, :, None], (H // 2, LANES, LANES))
    ws = pair(w_state).transpose(1, 0, 2).reshape(C, (H // 2) * LANES)
    return decay, ws, pair(w_query), cd


def _mixer(ops, decay, wq, cd, ret_gn, ml_gn, B, S, ret_w, ml_w):
    C = CHUNK
    N = S // C
    n_pairs = ret_w // LANES
    ml_heads = ml_w // LANES
    c2 = lambda b, n: (0, 0)
    c3 = lambda b, n: (0, 0, 0)
    BB = MIXER_SEQS if B % MIXER_SEQS == 0 else 1
    tok = lambda w: pl.BlockSpec((BB, C, w), lambda b, n: (b, n, 0))
    tok_t = lambda r: pl.BlockSpec((BB, r, C), lambda b, n: (b, 0, n))
    y = pl.pallas_call(
        functools.partial(_mixer_kernel, ret_w=ret_w, ml_w=ml_w),
        grid=(B // BB, N),
        in_specs=[
            tok(ret_w), tok(ret_w), tok_t(ret_w), tok(ret_w), tok(ret_w),
            tok(ml_w), tok(ml_w), tok_t(ml_w), tok(ml_w), tok_t(GATE_ROWS),
            pl.BlockSpec((RET_HEADS, C, C), c3),
            pl.BlockSpec((n_pairs, C, LANES), c3),
            pl.BlockSpec((n_pairs, LANES, LANES), c3),
            pl.BlockSpec((1, ret_w), c2),
            pl.BlockSpec((1, ml_w), c2),
        ],
        out_specs=pl.BlockSpec((BB, C, ret_w + ml_w), lambda b, n: (b, n, 0)),
        out_shape=jax.ShapeDtypeStruct((B, S, ret_w + ml_w), BF16),
        scratch_shapes=[
            pltpu.VMEM((BB, n_pairs, LANES, LANES), F32),
            pltpu.VMEM((BB, ml_heads, LANES, LANES), F32),
            pltpu.VMEM((BB, ml_heads, SUBLANES, LANES), F32),
            pltpu.VMEM((BB, SUBLANES, LANES), F32),
        ],
        compiler_params=pltpu.CompilerParams(
            dimension_semantics=("arbitrary", "arbitrary"), vmem_limit_bytes=VMEM_LIMIT),
        name="mixer",
    )(*ops, decay, wq, cd, ret_gn.reshape(1, -1), ml_gn.reshape(1, -1))
    return y.reshape(B * S, ret_w + ml_w)


def _router_kernel(y_ref, x_ref, wo_ref, g_ref, wr_hi_ref, wr_lo_ref, br_ref,
                   h_ref, xn_ref, ri_ref, rf_ref, cnt_ref, run_cnt):
    tm = y_ref.shape[0]

    @pl.when(pl.program_id(0) == 0)
    def _():
        run_cnt[...] = jnp.zeros_like(run_cnt)

    h = x_ref[...] + _dot(y_ref[...], wo_ref[...])
    h_ref[...] = h
    xn = _rms(h, g_ref[...])
    x_hi = xn.astype(BF16)
    xn_ref[...] = x_hi.reshape(xn_ref.shape)
    x_lo = (xn - x_hi.astype(F32)).astype(BF16)
    logits = (_dot_nt(wr_hi_ref[...], x_hi) + _dot_nt(wr_hi_ref[...], x_lo)
              + _dot_nt(wr_lo_ref[...], x_hi) + br_ref[...])
    big = jnp.int32(LANES)
    neg = -jnp.inf
    gl = logits[GROUP_ROW0:GROUP_ROW0 + 8]
    grow = lax.broadcasted_iota(jnp.int32, gl.shape, 0)
    is_g = grow < N_GROUPS
    gl = jnp.where(is_g, gl, neg)
    gmax = jnp.max(gl, axis=0, keepdims=True)
    gsum = jnp.sum(jnp.where(is_g, jnp.exp(gl - gmax), 0.0), axis=0, keepdims=True)
    p_g = 1.0 / gsum
    g_sel = jnp.min(jnp.where(is_g & (gl == gmax), grow, big), axis=0, keepdims=True)
    el = logits[0:N_EXPERTS]
    erow = lax.broadcasted_iota(jnp.int32, el.shape, 0)
    in_grp = (erow // EXPERTS_PER_GROUP) == g_sel
    el = jnp.where(in_grp, el, neg)
    emax = jnp.max(el, axis=0, keepdims=True)
    eexp = jnp.where(in_grp, jnp.exp(el - emax), 0.0)
    prob = eexp / jnp.sum(eexp, axis=0, keepdims=True)
    pm1 = jnp.where(in_grp, prob, -1.0)
    p1 = jnp.max(pm1, axis=0, keepdims=True)
    i1 = jnp.min(jnp.where(pm1 == p1, erow, big), axis=0, keepdims=True)
    pm2 = jnp.where(erow == i1, -1.0, pm1)
    p2 = jnp.max(pm2, axis=0, keepdims=True)
    i2 = jnp.min(jnp.where(pm2 == p2, erow, big), axis=0, keepdims=True)
    denom = p1 + p2
    g1 = p_g * p1 / denom
    g2 = p_g * p2 / denom

    sel1 = erow == i1
    sel2 = erow == i2
    onehot = (sel1 | sel2).astype(BF16)
    r_i = lax.broadcasted_iota(jnp.int32, (tm, tm), 0)
    c_i = lax.broadcasted_iota(jnp.int32, (tm, tm), 1)
    tri = (r_i < c_i).astype(BF16)
    prefix = _dot(onehot, tri) + run_cnt[:, 0:1]
    rank1 = jnp.sum(jnp.where(sel1, prefix, 0.0), axis=0, keepdims=True).astype(jnp.int32)
    rank2 = jnp.sum(jnp.where(sel2, prefix, 0.0), axis=0, keepdims=True).astype(jnp.int32)
    new_cnt = run_cnt[...] + jnp.sum(onehot.astype(F32), axis=1, keepdims=True)
    run_cnt[...] = new_cnt
    cnt_ref[...] = new_cnt.astype(jnp.int32)

    rrow = lax.broadcasted_iota(jnp.int32, (ROUTE_ROWS, tm), 0)
    ri_ref[...] = jnp.where(rrow == 0, rank1 * N_EXPERTS + i1,
                            jnp.where(rrow == 1, rank2 * N_EXPERTS + i2, 0))
    lrow = lax.broadcasted_iota(jnp.int32, (LANES, tm), 0)
    rf_ref[...] = jnp.where(lrow == 0, g1, jnp.where(lrow == 1, g2, 0.0)).T


def _router(y, x2, w_out, g, wr_hi, wr_lo, br, tm=ROUTE_TILE):
    T, D = x2.shape
    const = lambda i: (0, 0)
    tile = lambda i: (i, 0)
    return pl.pallas_call(
        _router_kernel,
        grid=(T // tm,),
        in_specs=[
            pl.BlockSpec((tm, y.shape[1]), tile),
            pl.BlockSpec((tm, D), tile),
            pl.BlockSpec(w_out.shape, const),
            pl.BlockSpec((1, D), const),
            pl.BlockSpec((ROUTER_ROWS, D), const),
            pl.BlockSpec((ROUTER_ROWS, D), const),
            pl.BlockSpec((ROUTER_ROWS, 1), const),
        ],
        out_specs=[
            pl.BlockSpec((tm, D), tile),
            pl.BlockSpec((tm,) + ROW_TILE, lambda i: (i, 0, 0)),
            pl.BlockSpec((None, ROUTE_ROWS, tm), lambda i: (i, 0, 0)),
            pl.BlockSpec((tm, LANES), tile),
            pl.BlockSpec((N_EXPERTS, LANES), const),
        ],
        out_shape=[
            jax.ShapeDtypeStruct((T, D), F32),
            jax.ShapeDtypeStruct((T,) + ROW_TILE, BF16),
            jax.ShapeDtypeStruct((T // tm, ROUTE_ROWS, tm), jnp.int32),
            jax.ShapeDtypeStruct((T, LANES), F32),
            jax.ShapeDtypeStruct((N_EXPERTS, LANES), jnp.int32),
        ],
        scratch_shapes=[pltpu.VMEM((N_EXPERTS, LANES), F32)],
        compiler_params=pltpu.CompilerParams(
            dimension_semantics=("arbitrary",), vmem_limit_bytes=VMEM_LIMIT),
        name="router",
    )(y, x2, w_out, g, wr_hi, wr_lo, br)


def _slot_base(tok0):
    return (tok0 // ROUTE_TILE) * (2 * ROUTE_TILE) + tok0 % ROUTE_TILE


def _route_spans(tm):
    assert tm % ROUTE_TILE == 0 or ROUTE_TILE % tm == 0
    span = min(tm, ROUTE_TILE)
    return [(t0, span) for t0 in range(0, tm, span)]


def _for_each_pad_block(pend_ref, nu_ref, n_blocks, fn):
    blk = EXPERT_BLOCK
    for e in range(N_EXPERTS):
        prev_end = 0 if e == 0 else pend_ref[e - 1]

        @pl.when(pend_ref[e] > prev_end)
        def _():
            fn(pl.multiple_of(pend_ref[e] - blk, blk))

    def tail(j, c):
        fn(pl.multiple_of(j * blk, blk))
        return c

    lax.fori_loop(nu_ref[0], n_blocks, tail, 0)


def _dispatch_kernel(dest_ref, pend_ref, nu_ref, xn_ref, buf_ref, zeros, sem, zsem):
    tm = xn_ref.shape[0]
    blk = EXPERT_BLOCK
    i = pl.program_id(0)

    @pl.when(i == 0)
    def _():
        zeros[...] = jnp.zeros_like(zeros)
        zcopy = lambda row: pltpu.make_async_copy(zeros, buf_ref.at[pl.ds(row, blk)], zsem)
        n_blocks = buf_ref.shape[0] // blk
        _for_each_pad_block(pend_ref, nu_ref, n_blocks, lambda row: zcopy(row).start())
        _for_each_pad_block(pend_ref, nu_ref, n_blocks, lambda row: zcopy(row).wait())

    for t0, span in _route_spans(tm):
        slot0 = _slot_base(i * tm + t0)

        def issue(g, c, t0=t0, slot0=slot0):
            for j in range(SUBLANES):
                for k in range(2):
                    t = g * SUBLANES + j
                    d = dest_ref[slot0 + k * ROUTE_TILE + t]
                    pltpu.make_async_copy(xn_ref.at[t0 + t], buf_ref.at[d], sem).start(priority=k)
            return c

        lax.fori_loop(0, span // SUBLANES, issue, 0)
    for k in range(2):
        pltpu.make_async_copy(buf_ref.at[pl.ds(tm, tm)], buf_ref.at[pl.ds(0, tm)], sem).wait()


def _dispatch(dest, pad_end, n_used, xn, n_rows, tm=ROUTE_TILE):
    T = xn.shape[0]
    return pl.pallas_call(
        _dispatch_kernel,
        grid_spec=pltpu.PrefetchScalarGridSpec(
            num_scalar_prefetch=3,
            grid=(T // tm,),
            in_specs=[pl.BlockSpec((tm,) + ROW_TILE, lambda i, *_: (i, 0, 0))],
            out_specs=pl.BlockSpec(memory_space=pl.ANY),
            scratch_shapes=[
                pltpu.VMEM((EXPERT_BLOCK,) + ROW_TILE, xn.dtype),
                pltpu.SemaphoreType.DMA,
                pltpu.SemaphoreType.DMA,
            ],
        ),
        out_shape=jax.ShapeDtypeStruct((n_rows,) + ROW_TILE, xn.dtype),
        compiler_params=pltpu.CompilerParams(
            dimension_semantics=("arbitrary",), vmem_limit_bytes=VMEM_LIMIT),
        name="dispatch",
    )(dest, pad_end, n_used, xn)


def _expert_kernel(be_ref, nu_ref, nv_ref, first_ref, next_ref, slot_ref, x_ref, w1_hbm, w3_hbm,
                   w2_hbm, y_ref, s1, s3, s2, w1b, w3b, w2b, sems):
    j = pl.program_id(0)
    used = j < nu_ref[0]
    blk = x_ref.shape[0]
    half = blk // 2

    def weight_copies(e, slot):
        return [pltpu.make_async_copy(w.at[e], s.at[slot], sems.at[slot])
                for w, s in ((w1_hbm, s1), (w3_hbm, s3), (w2_hbm, s2))]

    @pl.when(j == 0)
    def _():
        for c in weight_copies(be_ref[0], 0):
            c.start()

    @pl.when(used & (first_ref[j] == 1))
    def _():
        slot = slot_ref[j]
        for c in weight_copies(be_ref[j], slot):
            c.wait()

        @pl.when(next_ref[j] >= 0)
        def _():
            for c in weight_copies(next_ref[j], 1 - slot):
                c.start()

        w1b[...] = s1[slot].astype(BF16)
        w3b[...] = s3[slot].astype(BF16)
        w2b[...] = s2[slot].astype(BF16)

    def swiglu(rows):
        n = rows.stop - rows.start
        x = x_ref[rows].reshape(n, -1)
        a = _dot(x, w1b[...])
        hmid = a * jax.nn.sigmoid(a) * _dot(x, w3b[...])
        y_ref[rows] = _dot(hmid.astype(BF16), w2b[...]).reshape((n,) + y_ref.shape[1:])

    @pl.when(used & (nv_ref[j] > half))
    def _():
        swiglu(slice(0, blk))

    @pl.when(used & (nv_ref[j] <= half))
    def _():
        swiglu(slice(0, half))
        y_ref[half:blk] = jnp.zeros((blk - half,) + y_ref.shape[1:], y_ref.dtype)

    @pl.when(jnp.logical_not(used))
    def _():
        y_ref[...] = jnp.zeros_like(y_ref)


def _experts(block_expert, n_used, n_valid, first, next_expert, slot, x_buf, w1, w3, w2):
    P = x_buf.shape[0]
    D = w1.shape[1]
    assert x_buf.shape[1:] == ROW_TILE and D == SUBLANES * LANES
    blk = EXPERT_BLOCK
    d_exp = w2.shape[1]
    hbm = pl.BlockSpec(memory_space=pl.ANY)
    return pl.pallas_call(
        _expert_kernel,
        grid_spec=pltpu.PrefetchScalarGridSpec(
            num_scalar_prefetch=6,
            grid=(P // blk,),
            in_specs=[
                pl.BlockSpec((blk,) + ROW_TILE,
                             lambda j, be, nu, *_: (jnp.minimum(j, nu[0] - 1), 0, 0)),
                hbm, hbm, hbm,
            ],
            out_specs=pl.BlockSpec((blk,) + ROW_TILE, lambda j, *_: (j, 0, 0)),
            scratch_shapes=[
                pltpu.VMEM((2, D, d_exp), F32),
                pltpu.VMEM((2, D, d_exp), F32),
                pltpu.VMEM((2, d_exp, D), F32),
                pltpu.VMEM((D, d_exp), BF16),
                pltpu.VMEM((D, d_exp), BF16),
                pltpu.VMEM((d_exp, D), BF16),
                pltpu.SemaphoreType.DMA((2,)),
            ],
        ),
        out_shape=jax.ShapeDtypeStruct((P,) + ROW_TILE, F32),
        compiler_params=pltpu.CompilerParams(
            dimension_semantics=("arbitrary",), vmem_limit_bytes=VMEM_LIMIT),
        name="experts",
    )(block_expert, n_used, n_valid, first, next_expert, slot, x_buf, w1, w3, w2)


def _final_kernel(dest_ref, h_ref, rf_ref, p_ref, wup_ref, gple_ref, ggate_ref, wgate_ref,
                  gfin_ref, ybuf_ref, out_ref, *scratch, tm):
    row_bufs, sems = scratch[:-1], scratch[-1]
    s = pl.program_id(0)
    D = h_ref.shape[1]
    assert ROUTE_TILE % tm == 0

    def gather(tile, bufs, sem, straight=False):
        slot0 = _slot_base(tile * tm)

        def issue(g, c):
            for j in range(SUBLANES):
                for k in range(2):
                    t = g * SUBLANES + j
                    d = dest_ref[slot0 + k * ROUTE_TILE + t]
                    pltpu.make_async_copy(ybuf_ref.at[d], bufs[k].at[t], sem).start(priority=k)
            return c

        if straight:
            for g in range(tm // SUBLANES):
                issue(g, 0)
        else:
            lax.fori_loop(0, tm // SUBLANES, issue, 0)

    def wait(sem):
        for k in range(2):
            pltpu.make_async_copy(ybuf_ref.at[pl.ds(0, tm)], ybuf_ref.at[pl.ds(tm, tm)], sem).wait()

    def compute(rows, bufs):
        e = _rms(_dot(p_ref[rows, :].astype(BF16), wup_ref[...]), gple_ref[...])
        rf = rf_ref[rows, :]
        h = (h_ref[rows, :] + rf[:, 0:1] * bufs[0][...].reshape(tm, D)
             + rf[:, 1:2] * bufs[1][...].reshape(tm, D))
        gate = jax.nn.sigmoid(_dot(_rms(h, ggate_ref[...]).astype(BF16), wgate_ref[...]))
        h = h + gate * e
        out_ref[rows, :] = _rms(h, gfin_ref[...])

    n_tiles = pl.num_programs(0) * FINAL_TILES
    bufs = [(row_bufs[2 * i], row_bufs[2 * i + 1]) for i in range(FINAL_TILES)]

    @pl.when(s == 0)
    def _():
        for i in range(FINAL_AHEAD):
            gather(i, bufs[i], sems.at[i])

    for i in range(FINAL_TILES):
        wait(sems.at[i])
        nxt = (i + FINAL_AHEAD) % FINAL_TILES
        gather(jnp.minimum(s * FINAL_TILES + i + FINAL_AHEAD, n_tiles - 1), bufs[nxt],
               sems.at[nxt], straight=True)
        compute(slice(i * tm, (i + 1) * tm), bufs[i])

    @pl.when(s == pl.num_programs(0) - 1)
    def _():
        for i in range(FINAL_AHEAD):
            wait(sems.at[i])


def _final(dest, h, rf, p2, w_up, g_ple, g_gate, w_gate, g_fin, y_buf, tm=256):
    T, D = h.shape
    const = lambda i, *_: (0, 0)
    tile = lambda i, *_: (i, 0)
    assert y_buf.shape[1:] == ROW_TILE and D == SUBLANES * LANES
    row_buf = pltpu.VMEM((tm,) + ROW_TILE, F32)
    return pl.pallas_call(
        functools.partial(_final_kernel, tm=tm),
        grid_spec=pltpu.PrefetchScalarGridSpec(
            num_scalar_prefetch=1,
            grid=(T // (FINAL_TILES * tm),),
            in_specs=[
                pl.BlockSpec((FINAL_TILES * tm, D), tile),
                pl.BlockSpec((FINAL_TILES * tm, LANES), tile),
                pl.BlockSpec((FINAL_TILES * tm, p2.shape[1]), tile),
                pl.BlockSpec(w_up.shape, const),
                pl.BlockSpec((1, D), const),
                pl.BlockSpec((1, D), const),
                pl.BlockSpec(w_gate.shape, const),
                pl.BlockSpec((1, D), const),
                pl.BlockSpec(memory_space=pl.ANY),
            ],
            out_specs=pl.BlockSpec((FINAL_TILES * tm, D), tile),
            scratch_shapes=[row_buf] * (2 * FINAL_TILES) + [pltpu.SemaphoreType.DMA((FINAL_TILES,))],
        ),
        out_shape=jax.ShapeDtypeStruct((T, D), F32),
        compiler_params=pltpu.CompilerParams(
            dimension_semantics=("arbitrary",), vmem_limit_bytes=VMEM_LIMIT),
        name="final",
    )(dest, h, rf, p2, w_up, g_ple, g_gate, w_gate, g_fin, y_buf)


def _layer(h2, p2, positions, B, S, attn_norm, w_in, conv_w, conv_b, b_igate, b_fgate, ret_gn,
           ml_gn, w_out, moe_norm, w_group, b_group, w_router, b_router, w1, w3, w2, w_ple_up,
           ple_norm, ple_gate_norm, w_ple_gate, out_norm):
    T, D = h2.shape
    ret_w = ret_gn.shape[0]
    ml_w = ml_gn.shape[0]
    n_main = 4 * ret_w + 4 * ml_w
    row = lambda v: v.reshape(1, -1).astype(F32)
    pad_lanes = lambda a: jnp.pad(a, ((0, 0), (0, LANES - a.shape[1])))

    ml_heads = ml_w // LANES
    w_main = w_in.astype(BF16)
    gate_rows = lambda a: jnp.pad(a, ((0, SUBLANES - ml_heads), (0, 0)))
    w_i, w_f = w_in[:, n_main:n_main + ml_heads].T, w_in[:, n_main + ml_heads:].T
    w_gate_t = jnp.concatenate([gate_rows(w_i), gate_rows(w_f)]).astype(BF16)
    gate_bias = jnp.concatenate([gate_rows(b_igate[:, None]), gate_rows(b_fgate[:, None])])
    cos_t, sin_t = _rope_tables(positions, LANES // 2)
    decay, ws, wq, cd = _ret_tables(CHUNK)
    tm_in = 2 * CHUNK
    ops = _inproj(h2.reshape(B, S, D), row(attn_norm), w_main, w_gate_t, gate_bias.astype(F32),
                  cos_t.reshape(B, S, LANES), sin_t.reshape(B, S, LANES),
                  jnp.tile(ws, (tm_in // CHUNK, 1)), conv_w, conv_b.reshape(1, -1), ret_w, ml_w,
                  tm=tm_in)

    y = _mixer(ops, decay, wq, cd, ret_gn, ml_gn, B, S, ret_w, ml_w)

    pad_rows = lambda a: jnp.pad(a, ((0, ROUTER_ROWS - a.shape[0]), (0, 0)))
    wr = pad_rows(jnp.concatenate([w_router, w_group], axis=1).T)
    wr_hi = wr.astype(BF16)
    wr_lo = (wr - wr_hi.astype(F32)).astype(BF16)
    br = pad_rows(jnp.concatenate([b_router, b_group])[:, None].astype(F32))
    h_mid, xn, ri, rf, counts = _router(y, h2, w_out.astype(BF16), row(moe_norm), wr_hi, wr_lo, br)

    blk = EXPERT_BLOCK
    counts = counts[:, 0]
    padded = (counts + blk - 1) // blk * blk
    pad_end = jnp.cumsum(padded)
    pad_start = pad_end - padded
    n_blocks = (2 * T) // blk + N_EXPERTS
    block_start = jnp.arange(n_blocks, dtype=jnp.int32) * blk
    block_expert = jnp.minimum(jnp.sum(pad_end[None, :] <= block_start[:, None], axis=1),
                               N_EXPERTS - 1).astype(jnp.int32)
    n_used = (pad_end[-1:] // blk).astype(jnp.int32)
    real_end = jnp.sum(jnp.where(block_expert[:, None] == jnp.arange(N_EXPERTS)[None, :],
                                 (pad_start + counts)[None, :], 0), axis=1)
    n_valid = jnp.clip(real_end - block_start, 0, blk).astype(jnp.int32)
    blk_id = jnp.arange(n_blocks, dtype=jnp.int32)
    is_used = blk_id < n_used[0]
    first = is_used & jnp.concatenate([jnp.ones((1,), bool), block_expert[1:] != block_expert[:-1]])
    seg_end = jnp.sum(jnp.where(block_expert[:, None] == jnp.arange(N_EXPERTS)[None, :],
                                (pad_end // blk)[None, :], 0), axis=1)
    next_blk = jnp.minimum(seg_end, n_blocks - 1)
    next_expert = jnp.sum(jnp.where(next_blk[:, None] == blk_id[None, :], block_expert[None, :], 0),
                          axis=1)
    next_expert = jnp.where(seg_end < n_used[0], next_expert, -1).astype(jnp.int32)
    slot = ((jnp.cumsum(first.astype(jnp.int32)) - 1) % 2).astype(jnp.int32)
    codes = ri[:, 0:2, :].reshape(-1, ROUTE_TILE)
    expert = codes % N_EXPERTS
    dest = codes // N_EXPERTS
    for e in range(N_EXPERTS):
        dest = dest + jnp.where(expert == e, pad_start[e], 0)
    dest = dest.astype(jnp.int32).reshape(-1)

    x_buf = _dispatch(dest, pad_end.astype(jnp.int32), n_used, xn, n_blocks * blk,
                      tm=min(T, DISPATCH_TILE))
    y_buf = _experts(block_expert, n_used, n_valid, first.astype(jnp.int32), next_expert, slot,
                     x_buf, w1, w3, w2)

    return _final(dest, h_mid, rf, p2, w_ple_up.astype(BF16), row(ple_norm), row(ple_gate_norm),
                  w_ple_gate.astype(BF16), row(out_norm), y_buf)


def kernel(x, p, positions, attn_norm, w_in, conv_w, conv_b, b_igate, b_fgate, ret_gn, ml_gn,
           w_out, moe_norm, w_group, b_group, w_router, b_router, w1, w3, w2, w_ple_up, ple_norm,
           ple_gate_norm, w_ple_gate, final_norm):
    B, S, D = x.shape
    depth = p.shape[0]
    assert depth == 1, "the final RMSNorm is fused into the layer's last kernel"
    out = _layer(x.reshape(B * S, D), p[0].reshape(B * S, -1), positions, B, S,
                 attn_norm[0], w_in[0], conv_w[0], conv_b[0], b_igate[0], b_fgate[0], ret_gn[0],
                 ml_gn[0], w_out[0], moe_norm[0], w_group[0], b_group[0], w_router[0],
                 b_router[0], w1[0], w3[0], w2[0], w_ple_up[0], ple_norm[0], ple_gate_norm[0],
                 w_ple_gate[0], final_norm)
    return out.reshape(B, S, D)
```

```python
import functools

import jax
import jax.numpy as jnp
from jax import lax
from jax.experimental import pallas as pl
from jax.experimental.pallas import tpu as pltpu

F32 = jnp.float32
BF16 = jnp.bfloat16

RET_HEADS = 8
ML_HEADS = 4
CHUNK = 128
CONV_W = 4
ROPE_BASE = 10000.0
N_GROUPS = 4
EXPERTS_PER_GROUP = 8
N_EXPERTS = N_GROUPS * EXPERTS_PER_GROUP
EPS = 1e-6

LANES = 128
SUBLANES = 8
MXU_COLS = 256
MIXER_SEQS = 2
FINAL_TILES = 4
FINAL_AHEAD = 2
VMEM_LIMIT = 56 * 1024 * 1024
EXPERT_BLOCK = 512
GROUP_ROW0 = N_EXPERTS
ROUTER_ROWS = 64
ROUTE_TILE = 1024
ROUTE_ROWS = 8
DISPATCH_TILE = 2 * ROUTE_TILE
GATE_ROWS = 2 * SUBLANES
ROW_TILE = (SUBLANES, LANES)


def _rms(x, g):
    return x * lax.rsqrt(jnp.mean(x * x, axis=-1, keepdims=True) + EPS) * g


def _dot(a, b):
    return jnp.dot(a, b, preferred_element_type=F32)


def _dot_nt(a, b):
    return lax.dot_general(a, b, (((1,), (1,)), ((), ())), preferred_element_type=F32)


def _split3(x):
    hi = x.astype(BF16)
    r1 = x - hi.astype(F32)
    mid = r1.astype(BF16)
    lo = (r1 - mid.astype(F32)).astype(BF16)
    return hi, mid, lo


def _rope_kernel(pos_ref, freq_ref, sign_ref, cos_ref, sin_ref, *, half):
    ang = pos_ref[...].astype(F32) * freq_ref[...]
    cos_c = jnp.cos(ang)
    sin_c = jnp.sin(ang)
    group = lax.broadcasted_iota(jnp.int32, ang.shape, 1) // half
    per_row = LANES // half
    for s in range(per_row):
        def spread(t):
            out = t
            for g in range(per_row):
                if g != s:
                    out = jnp.where(group == g, pltpu.roll(t, (half * (g - s)) % LANES, 1), out)
            return out
        cos_ref[s] = spread(cos_c)
        sin_ref[s] = spread(sin_c) * sign_ref[...]


def _rope_tables(positions, dh):
    half = dh // 2
    per_row = LANES // half
    T = positions.size
    rows = T // per_row
    tr = min(rows, 1024)
    freqs = ROPE_BASE ** (-jnp.arange(half, dtype=F32) / half)
    pos_c = jnp.repeat(positions.reshape(per_row, rows).T, half, axis=1)
    sign = jnp.tile(jnp.concatenate([-jnp.ones((half,), F32), jnp.ones((half,), F32)]),
                    LANES // dh)[None, :]
    const = pl.BlockSpec((1, LANES), lambda i: (0, 0))
    out = pl.BlockSpec((per_row, tr, LANES), lambda i: (0, i, 0))
    cos_t, sin_t = pl.pallas_call(
        functools.partial(_rope_kernel, half=half),
        grid=(rows // tr,),
        in_specs=[pl.BlockSpec((tr, LANES), lambda i: (i, 0)), const, const],
        out_specs=[out, out],
        out_shape=[jax.ShapeDtypeStruct((per_row, rows, LANES), F32)] * 2,
        name="rope",
    )(pos_c, jnp.tile(freqs, per_row)[None, :], sign)
    return cos_t.reshape(T, LANES), sin_t.reshape(T, LANES)


def _inproj_kernel(x_ref, g_ref, wm_ref, wgt_ref, gb_ref, cos_ref, sin_ref, ws_ref, convw_ref,
                   convb_ref, rq_ref, rk_ref, rkwt_ref, rv_ref, rg_ref, mq_ref, mk_ref, mvt_ref,
                   mo_ref, gt_ref, carry, *, ret_w, ml_w):
    tm = x_ref.shape[0]
    ret_dh = LANES // 2

    @pl.when(pl.program_id(1) == 0)
    def _():
        carry[0:SUBLANES] = jnp.zeros((SUBLANES, carry.shape[1]), F32)

    xn = _rms(x_ref[...], g_ref[...]).astype(BF16)
    proj = lambda o, w: _dot(xn, wm_ref[:, o:o + w])
    tiles = lambda w: [slice(t, t + LANES) for t in range(0, w, LANES)]

    lane = lax.broadcasted_iota(jnp.int32, (tm, LANES), 1)
    first_half = (lane % ret_dh) < (ret_dh // 2)
    cos_t = cos_ref[...]
    sin_t = sin_ref[...]

    def rot(t):
        swapped = jnp.where(first_half, pltpu.roll(t, LANES - ret_dh // 2, 1),
                            pltpu.roll(t, ret_dh // 2, 1))
        return t * cos_t + swapped * sin_t

    for c0 in range(0, ret_w, MXU_COLS):
        rq = proj(c0, MXU_COLS)
        for ps in tiles(MXU_COLS):
            rq_ref[:, c0 + ps.start:c0 + ps.stop] = rot(rq[:, ps]).astype(BF16)
    for c0 in range(0, ret_w, MXU_COLS):
        rk = proj(ret_w + c0, MXU_COLS)
        for ps in tiles(MXU_COLS):
            cs = slice(c0 + ps.start, c0 + ps.stop)
            k = rot(rk[:, ps]) * (ret_dh ** -0.5)
            rk_ref[:, cs] = k.astype(BF16)
            rkwt_ref[cs, :] = (k * ws_ref[:, cs]).T.astype(BF16)
    for c0 in range(0, ret_w, MXU_COLS):
        cs = slice(c0, c0 + MXU_COLS)
        rv_ref[:, cs] = proj(2 * ret_w + c0, MXU_COLS).astype(BF16)
    for c0 in range(0, ret_w, MXU_COLS):
        g = proj(3 * ret_w + c0, MXU_COLS)
        rg_ref[:, c0:c0 + MXU_COLS] = g * jax.nn.sigmoid(g)

    o_mq = 4 * ret_w
    for c0 in range(0, 2 * ml_w, MXU_COLS):
        cs = slice(c0, c0 + MXU_COLS)
        xq = proj(o_mq + c0, MXU_COLS)
        carry[SUBLANES:SUBLANES + tm, cs] = xq
        acc = xq * convw_ref[CONV_W - 1:CONV_W, cs] + convb_ref[:, cs]
        for s in range(1, CONV_W):
            shifted = carry[SUBLANES - s:SUBLANES - s + tm, cs]
            acc = acc + shifted * convw_ref[CONV_W - 1 - s:CONV_W - s, cs]
        carry[0:SUBLANES, cs] = xq[tm - SUBLANES:tm]
        act = acc * jax.nn.sigmoid(acc)
        if c0 < ml_w:
            mq_ref[:, cs] = act.astype(BF16)
        else:
            mk_ref[:, c0 - ml_w:c0 - ml_w + MXU_COLS] = (act * (LANES ** -0.5)).astype(BF16)
    for c0 in range(0, ml_w, MXU_COLS):
        mv = proj(o_mq + 2 * ml_w + c0, MXU_COLS)
        for hs in tiles(MXU_COLS):
            mvt_ref[c0 + hs.start:c0 + hs.stop, :] = mv[:, hs].T
    for c0 in range(0, ml_w, MXU_COLS):
        mo_ref[:, c0:c0 + MXU_COLS] = jax.nn.sigmoid(proj(o_mq + 3 * ml_w + c0, MXU_COLS))
    gt_ref[...] = _dot_nt(wgt_ref[...], xn) + gb_ref[...]


def _inproj(x3, g, w_main, w_gate_t, gate_bias, cos_t, sin_t, ws, conv_w, conv_b, ret_w, ml_w,
            tm=256):
    B, S, D = x3.shape
    n_main = 4 * (ret_w + ml_w)
    const = lambda b, i: (0, 0)
    tok = lambda w: pl.BlockSpec((None, tm, w), lambda b, i: (b, i, 0))
    tok_t = lambda r: pl.BlockSpec((None, r, tm), lambda b, i: (b, 0, i))
    act = lambda w, dt: jax.ShapeDtypeStruct((B, S, w), dt)
    act_t = lambda r, dt: jax.ShapeDtypeStruct((B, r, S), dt)
    return pl.pallas_call(
        functools.partial(_inproj_kernel, ret_w=ret_w, ml_w=ml_w),
        grid=(B, S // tm),
        in_specs=[
            tok(D),
            pl.BlockSpec((1, D), const),
            pl.BlockSpec((D, n_main), const),
            pl.BlockSpec((GATE_ROWS, D), const),
            pl.BlockSpec((GATE_ROWS, 1), const),
            tok(LANES),
            tok(LANES),
            pl.BlockSpec((tm, ret_w), const),
            pl.BlockSpec((CONV_W, 2 * ml_w), const),
            pl.BlockSpec((1, 2 * ml_w), const),
        ],
        out_specs=[tok(ret_w), tok(ret_w), tok_t(ret_w), tok(ret_w), tok(ret_w),
                   tok(ml_w), tok(ml_w), tok_t(ml_w), tok(ml_w), tok_t(GATE_ROWS)],
        out_shape=[
            act(ret_w, BF16),
            act(ret_w, BF16),
            act_t(ret_w, BF16),
            act(ret_w, BF16),
            act(ret_w, F32),
            act(ml_w, BF16),
            act(ml_w, BF16),
            act_t(ml_w, F32),
            act(ml_w, F32),
            act_t(GATE_ROWS, F32),
        ],
        scratch_shapes=[pltpu.VMEM((SUBLANES + tm, 2 * ml_w), F32)],
        compiler_params=pltpu.CompilerParams(
            dimension_semantics=("arbitrary", "arbitrary"), vmem_limit_bytes=VMEM_LIMIT),
        name="inproj",
    )(x3, g, w_main, w_gate_t, gate_bias, cos_t, sin_t, ws, conv_w, conv_b)


def _mixer_kernel(rq_ref, rk_ref, rkwt_ref, rv_ref, rg_ref, mq_ref, mk_ref, mvt_ref, mo_ref,
                  gt_ref, decay_ref, wq_ref, cd_ref, retgn_ref, mlgn_ref,
                  y_ref, r_state, s_state, n_state, m_state, *, ret_w, ml_w):
    BB, C = rq_ref.shape[0], rq_ref.shape[1]
    n_pairs = ret_w // LANES
    ml_heads = ml_w // LANES
    ret_dh = LANES // 2

    @pl.when(pl.program_id(1) == 0)
    def _():
        r_state[...] = jnp.zeros_like(r_state)
        s_state[...] = jnp.zeros_like(s_state)
        n_state[...] = jnp.zeros_like(n_state)
        m_state[...] = jnp.zeros_like(m_state)

    lane = lax.broadcasted_iota(jnp.int32, (C, LANES), 1)
    row = lax.broadcasted_iota(jnp.int32, (C, LANES), 0)
    assert C == LANES
    lo = lane < ret_dh
    blockdiag = (row < ret_dh) == lo
    lo_b = jnp.where(lo, 1.0, 0.0).astype(BF16)
    hi_b = jnp.where(lo, 0.0, 1.0).astype(BF16)
    seqs = range(BB)
    pair_units = [(s, p, slice(p * LANES, (p + 1) * LANES)) for s in seqs for p in range(n_pairs)]
    head_units = [(s, h, slice(h * LANES, (h + 1) * LANES)) for s in seqs for h in range(ml_heads)]


    row8 = lax.broadcasted_iota(jnp.int32, (SUBLANES, C), 0)
    lane8 = lax.broadcasted_iota(jnp.int32, (SUBLANES, C), 1)
    live = row8 < ml_heads
    triu = (row <= lane).astype(BF16)
    before = row <= lane
    mx, w_inter, e_negm, w_state, w_state_b, dec, beta_t = [], [], [], [], [], [], []
    for s in seqs:
        ig = jnp.where(live, gt_ref[s, 0:SUBLANES, :], 0.0)
        f_pre = jnp.where(live, gt_ref[s, SUBLANES:2 * SUBLANES, :], 30.0)
        l_hi, l_mid, l_lo = _split3(jax.nn.log_sigmoid(f_pre))
        b = _dot(l_hi, triu) + _dot(l_mid, triu) + _dot(l_lo, triu)
        beta = ig - b
        cm = beta
        shift = 1
        while shift < C:
            cm = jnp.maximum(cm, jnp.where(lane8 >= shift, pltpu.roll(cm, shift, 1), -jnp.inf))
            shift *= 2
        m_prev = m_state[s]
        mx.append(jnp.maximum(cm, m_prev))
        mx_last = jnp.broadcast_to(mx[s][:, C - 1:C], (SUBLANES, C))
        w_inter.append(jnp.exp(m_prev - mx[s]))
        e_negm.append(jnp.exp(-(b + mx[s])))
        w_state.append(jnp.exp(beta - mx_last))
        w_state_b.append(w_state[s].astype(BF16))
        dec.append(jnp.exp(m_prev - mx_last))
        m_state[s] = jnp.where(live, jnp.broadcast_to(b[:, C - 1:C], (SUBLANES, C)) + mx_last, 0.0)
        beta_t.append(jnp.concatenate([beta, jnp.zeros((LANES - SUBLANES, C), F32)], axis=0).T)

    hrow = lambda t, h: t[h:h + 1, :]

    rq = [rq_ref[s, :, ps] for s, p, ps in pair_units]
    rk = [rk_ref[s, :, ps] for s, p, ps in pair_units]
    rv = [rv_ref[s, :, ps] for s, p, ps in pair_units]
    r_prev = [r_state[s, p] for s, p, ps in pair_units]
    pu = range(len(pair_units))
    s_ab = [_dot_nt(jnp.concatenate([rq[u] * lo_b, rq[u] * hi_b], axis=0), rk[u]) for u in pu]
    r_read = [_dot(rq[u], r_prev[u].astype(BF16)) for u in pu]
    r_new = [_dot(rkwt_ref[s, ps, :], rv[u]) for u, (s, p, ps) in enumerate(pair_units)]
    hu = range(len(head_units))
    mq = [mq_ref[s, :, hs] for s, h, hs in head_units]
    mk = [mk_ref[s, :, hs] for s, h, hs in head_units]
    mv_t = [mvt_ref[s, hs, :] for s, h, hs in head_units]
    st_prev = [s_state[s, h] for s, h, hs in head_units]
    n_prev = [n_state[s, h] for s, h, hs in head_units]
    a_t = [_dot_nt(mk[u], mq[u]) for u in hu]
    s_read = [_dot_nt(st_prev[u].astype(BF16), mq[u]) for u in hu]
    qn = [_dot_nt(n_prev[u].astype(BF16), mq[u])[0:1, :] for u in hu]
    s_new = [_dot((mv_t[u] * hrow(w_state[s], h)).astype(BF16), mk[u])
             for u, (s, h, hs) in enumerate(head_units)]
    n_new = [_dot(w_state_b[s], mk[u])[h:h + 1, :] for u, (s, h, hs) in enumerate(head_units)]
    for u, (s, p, ps) in enumerate(pair_units):
        r_state[s, p] = cd_ref[p] * r_prev[u] + jnp.where(blockdiag, r_new[u], 0.0)
    for u, (s, h, hs) in enumerate(head_units):
        s_state[s, h] = hrow(dec[s], h) * st_prev[u] + s_new[u]
        n_state[s, h] = hrow(dec[s], h) * n_prev[u] + n_new[u]

    s_ab = [jnp.concatenate([(s_ab[u][0:C] * decay_ref[2 * p]).astype(BF16),
                             (s_ab[u][C:2 * C] * decay_ref[2 * p + 1]).astype(BF16)], axis=1)
            for u, (s, p, ps) in enumerate(pair_units)]
    p_t = [jnp.exp(jnp.where(before, beta_t[s][:, h:h + 1] - hrow(mx[s], h), -jnp.inf)) * a_t[u]
           for u, (s, h, hs) in enumerate(head_units)]

    o = [_dot(s_ab[u], jnp.concatenate([rv[u] * lo_b, rv[u] * hi_b], axis=0)) + r_read[u] * wq_ref[p]
         for u, (s, p, ps) in enumerate(pair_units)]
    num_t = [_dot(mv_t[u].astype(BF16), p_t[u].astype(BF16)) + hrow(w_inter[s], h) * s_read[u]
             for u, (s, h, hs) in enumerate(head_units)]

    for u, (s, p, ps) in enumerate(pair_units):
        sq = o[u] * o[u]
        ms_a = jnp.sum(jnp.where(lo, sq, 0.0), axis=-1, keepdims=True)
        ms_b = jnp.sum(jnp.where(lo, 0.0, sq), axis=-1, keepdims=True)
        ms = jnp.where(lo, ms_a, ms_b) * (1.0 / ret_dh)
        r = o[u] * lax.rsqrt(ms + EPS) * retgn_ref[:, ps]
        y_ref[s, :, ps] = (rg_ref[s, :, ps] * r).astype(y_ref.dtype)
    for u, (s, h, hs) in enumerate(head_units):
        den = jnp.sum(p_t[u], axis=0, keepdims=True) + hrow(w_inter[s], h) * qn[u]
        hh = (num_t[u] * (1.0 / jnp.maximum(jnp.abs(den), hrow(e_negm[s], h)))).T
        hm = _rms(mo_ref[s, :, hs] * hh, mlgn_ref[:, hs])
        y_ref[s, :, ret_w + h * LANES:ret_w + (h + 1) * LANES] = hm.astype(y_ref.dtype)


def _ret_tables(C):
    H = RET_HEADS
    dh = LANES // 2
    log_gamma = jnp.log1p(-(2.0 ** (-5.0 - jnp.arange(H, dtype=F32))))
    idx = jnp.arange(C, dtype=F32)
    rel = idx[:, None] - idx[None, :]
    causal = rel >= 0
    decay = jnp.where(causal, jnp.exp(log_gamma[:, None, None] * jnp.where(causal, rel, 0.0)), 0.0)
    w_state = jnp.exp(log_gamma[:, None] * (C - 1 - idx))
    w_query = jnp.exp(log_gamma[:, None] * (idx + 1.0))
    chunk_decay = jnp.exp(log_gamma * C)
    pair = lambda t: jnp.repeat(t.reshape(H // 2, 2, C).transpose(0, 2, 1), dh, axis=2)
    cd = jnp.repeat(chunk_decay.reshape(H // 2, 2), dh, axis=1)
    cd = jnp.broadcast_to(cd[:, :, None], (H // 2, LANES, LANES))
    ws = pair(w_state).transpose(1, 0, 2).reshape(C, (H // 2) * LANES)
    return decay, ws, pair(w_query), cd


def _mixer(ops, decay, wq, cd, ret_gn, ml_gn, B, S, ret_w, ml_w):
    C = CHUNK
    N = S // C
    n_pairs = ret_w // LANES
    ml_heads = ml_w // LANES
    c2 = lambda b, n: (0, 0)
    c3 = lambda b, n: (0, 0, 0)
    BB = MIXER_SEQS if B % MIXER_SEQS == 0 else 1
    tok = lambda w: pl.BlockSpec((BB, C, w), lambda b, n: (b, n, 0))
    tok_t = lambda r: pl.BlockSpec((BB, r, C), lambda b, n: (b, 0, n))
    y = pl.pallas_call(
        functools.partial(_mixer_kernel, ret_w=ret_w, ml_w=ml_w),
        grid=(B // BB, N),
        in_specs=[
            tok(ret_w), tok(ret_w), tok_t(ret_w), tok(ret_w), tok(ret_w),
            tok(ml_w), tok(ml_w), tok_t(ml_w), tok(ml_w), tok_t(GATE_ROWS),
            pl.BlockSpec((RET_HEADS, C, C), c3),
            pl.BlockSpec((n_pairs, C, LANES), c3),
            pl.BlockSpec((n_pairs, LANES, LANES), c3),
            pl.BlockSpec((1, ret_w), c2),
            pl.BlockSpec((1, ml_w), c2),
        ],
        out_specs=pl.BlockSpec((BB, C, ret_w + ml_w), lambda b, n: (b, n, 0)),
        out_shape=jax.ShapeDtypeStruct((B, S, ret_w + ml_w), BF16),
        scratch_shapes=[
            pltpu.VMEM((BB, n_pairs, LANES, LANES), F32),
            pltpu.VMEM((BB, ml_heads, LANES, LANES), F32),
            pltpu.VMEM((BB, ml_heads, SUBLANES, LANES), F32),
            pltpu.VMEM((BB, SUBLANES, LANES), F32),
        ],
        compiler_params=pltpu.CompilerParams(
            dimension_semantics=("arbitrary", "arbitrary"), vmem_limit_bytes=VMEM_LIMIT),
        name="mixer",
    )(*ops, decay, wq, cd, ret_gn.reshape(1, -1), ml_gn.reshape(1, -1))
    return y.reshape(B * S, ret_w + ml_w)


def _router_kernel(y_ref, x_ref, wo_ref, g_ref, wr_hi_ref, wr_lo_ref, br_ref,
                   h_ref, xn_ref, ri_ref, rf_ref, cnt_ref, run_cnt):
    tm = y_ref.shape[0]

    @pl.when(pl.program_id(0) == 0)
    def _():
        run_cnt[...] = jnp.zeros_like(run_cnt)

    h = x_ref[...] + _dot(y_ref[...], wo_ref[...])
    h_ref[...] = h
    xn = _rms(h, g_ref[...])
    x_hi = xn.astype(BF16)
    xn_ref[...] = x_hi.reshape(xn_ref.shape)
    x_lo = (xn - x_hi.astype(F32)).astype(BF16)
    logits = (_dot_nt(wr_hi_ref[...], x_hi) + _dot_nt(wr_hi_ref[...], x_lo)
              + _dot_nt(wr_lo_ref[...], x_hi) + br_ref[...])
    big = jnp.int32(LANES)
    neg = -jnp.inf
    gl = logits[GROUP_ROW0:GROUP_ROW0 + 8]
    grow = lax.broadcasted_iota(jnp.int32, gl.shape, 0)
    is_g = grow < N_GROUPS
    gl = jnp.where(is_g, gl, neg)
    gmax = jnp.max(gl, axis=0, keepdims=True)
    gsum = jnp.sum(jnp.where(is_g, jnp.exp(gl - gmax), 0.0), axis=0, keepdims=True)
    p_g = 1.0 / gsum
    g_sel = jnp.min(jnp.where(is_g & (gl == gmax), grow, big), axis=0, keepdims=True)
    el = logits[0:N_EXPERTS]
    erow = lax.broadcasted_iota(jnp.int32, el.shape, 0)
    in_grp = (erow // EXPERTS_PER_GROUP) == g_sel
    el = jnp.where(in_grp, el, neg)
    emax = jnp.max(el, axis=0, keepdims=True)
    eexp = jnp.where(in_grp, jnp.exp(el - emax), 0.0)
    prob = eexp / jnp.sum(eexp, axis=0, keepdims=True)
    pm1 = jnp.where(in_grp, prob, -1.0)
    p1 = jnp.max(pm1, axis=0, keepdims=True)
    i1 = jnp.min(jnp.where(pm1 == p1, erow, big), axis=0, keepdims=True)
    pm2 = jnp.where(erow == i1, -1.0, pm1)
    p2 = jnp.max(pm2, axis=0, keepdims=True)
    i2 = jnp.min(jnp.where(pm2 == p2, erow, big), axis=0, keepdims=True)
    denom = p1 + p2
    g1 = p_g * p1 / denom
    g2 = p_g * p2 / denom

    sel1 = erow == i1
    sel2 = erow == i2
    onehot = (sel1 | sel2).astype(BF16)
    r_i = lax.broadcasted_iota(jnp.int32, (tm, tm), 0)
    c_i = lax.broadcasted_iota(jnp.int32, (tm, tm), 1)
    tri = (r_i < c_i).astype(BF16)
    prefix = _dot(onehot, tri) + run_cnt[:, 0:1]
    rank1 = jnp.sum(jnp.where(sel1, prefix, 0.0), axis=0, keepdims=True).astype(jnp.int32)
    rank2 = jnp.sum(jnp.where(sel2, prefix, 0.0), axis=0, keepdims=True).astype(jnp.int32)
    new_cnt = run_cnt[...] + jnp.sum(onehot.astype(F32), axis=1, keepdims=True)
    run_cnt[...] = new_cnt
    cnt_ref[...] = new_cnt.astype(jnp.int32)

    rrow = lax.broadcasted_iota(jnp.int32, (ROUTE_ROWS, tm), 0)
    ri_ref[...] = jnp.where(rrow == 0, rank1 * N_EXPERTS + i1,
                            jnp.where(rrow == 1, rank2 * N_EXPERTS + i2, 0))
    lrow = lax.broadcasted_iota(jnp.int32, (LANES, tm), 0)
    rf_ref[...] = jnp.where(lrow == 0, g1, jnp.where(lrow == 1, g2, 0.0)).T


def _router(y, x2, w_out, g, wr_hi, wr_lo, br, tm=ROUTE_TILE):
    T, D = x2.shape
    const = lambda i: (0, 0)
    tile = lambda i: (i, 0)
    return pl.pallas_call(
        _router_kernel,
        grid=(T // tm,),
        in_specs=[
            pl.BlockSpec((tm, y.shape[1]), tile),
            pl.BlockSpec((tm, D), tile),
            pl.BlockSpec(w_out.shape, const),
            pl.BlockSpec((1, D), const),
            pl.BlockSpec((ROUTER_ROWS, D), const),
            pl.BlockSpec((ROUTER_ROWS, D), const),
            pl.BlockSpec((ROUTER_ROWS, 1), const),
        ],
        out_specs=[
            pl.BlockSpec((tm, D), tile),
            pl.BlockSpec((tm,) + ROW_TILE, lambda i: (i, 0, 0)),
            pl.BlockSpec((None, ROUTE_ROWS, tm), lambda i: (i, 0, 0)),
            pl.BlockSpec((tm, LANES), tile),
            pl.BlockSpec((N_EXPERTS, LANES), const),
        ],
        out_shape=[
            jax.ShapeDtypeStruct((T, D), F32),
            jax.ShapeDtypeStruct((T,) + ROW_TILE, BF16),
            jax.ShapeDtypeStruct((T // tm, ROUTE_ROWS, tm), jnp.int32),
            jax.ShapeDtypeStruct((T, LANES), F32),
            jax.ShapeDtypeStruct((N_EXPERTS, LANES), jnp.int32),
        ],
        scratch_shapes=[pltpu.VMEM((N_EXPERTS, LANES), F32)],
        compiler_params=pltpu.CompilerParams(
            dimension_semantics=("arbitrary",), vmem_limit_bytes=VMEM_LIMIT),
        name="router",
    )(y, x2, w_out, g, wr_hi, wr_lo, br)


def _slot_base(tok0):
    return (tok0 // ROUTE_TILE) * (2 * ROUTE_TILE) + tok0 % ROUTE_TILE


def _route_spans(tm):
    assert tm % ROUTE_TILE == 0 or ROUTE_TILE % tm == 0
    span = min(tm, ROUTE_TILE)
    return [(t0, span) for t0 in range(0, tm, span)]


def _for_each_pad_block(pend_ref, nu_ref, n_blocks, fn):
    blk = EXPERT_BLOCK
    for e in range(N_EXPERTS):
        prev_end = 0 if e == 0 else pend_ref[e - 1]

        @pl.when(pend_ref[e] > prev_end)
        def _():
            fn(pl.multiple_of(pend_ref[e] - blk, blk))

    def tail(j, c):
        fn(pl.multiple_of(j * blk, blk))
        return c

    lax.fori_loop(nu_ref[0], n_blocks, tail, 0)


def _dispatch_kernel(dest_ref, pend_ref, nu_ref, xn_ref, buf_ref, zeros, sem, zsem):
    tm = xn_ref.shape[0]
    blk = EXPERT_BLOCK
    i = pl.program_id(0)

    @pl.when(i == 0)
    def _():
        zeros[...] = jnp.zeros_like(zeros)
        zcopy = lambda row: pltpu.make_async_copy(zeros, buf_ref.at[pl.ds(row, blk)], zsem)
        n_blocks = buf_ref.shape[0] // blk
        _for_each_pad_block(pend_ref, nu_ref, n_blocks, lambda row: zcopy(row).start())
        _for_each_pad_block(pend_ref, nu_ref, n_blocks, lambda row: zcopy(row).wait())

    for t0, span in _route_spans(tm):
        slot0 = _slot_base(i * tm + t0)

        def issue(g, c, t0=t0, slot0=slot0):
            for j in range(SUBLANES):
                for k in range(2):
                    t = g * SUBLANES + j
                    d = dest_ref[slot0 + k * ROUTE_TILE + t]
                    pltpu.make_async_copy(xn_ref.at[t0 + t], buf_ref.at[d], sem).start(priority=k)
            return c

        lax.fori_loop(0, span // SUBLANES, issue, 0)
    for k in range(2):
        pltpu.make_async_copy(buf_ref.at[pl.ds(tm, tm)], buf_ref.at[pl.ds(0, tm)], sem).wait()


def _dispatch(dest, pad_end, n_used, xn, n_rows, tm=ROUTE_TILE):
    T = xn.shape[0]
    return pl.pallas_call(
        _dispatch_kernel,
        grid_spec=pltpu.PrefetchScalarGridSpec(
            num_scalar_prefetch=3,
            grid=(T // tm,),
            in_specs=[pl.BlockSpec((tm,) + ROW_TILE, lambda i, *_: (i, 0, 0))],
            out_specs=pl.BlockSpec(memory_space=pl.ANY),
            scratch_shapes=[
                pltpu.VMEM((EXPERT_BLOCK,) + ROW_TILE, xn.dtype),
                pltpu.SemaphoreType.DMA,
                pltpu.SemaphoreType.DMA,
            ],
        ),
        out_shape=jax.ShapeDtypeStruct((n_rows,) + ROW_TILE, xn.dtype),
        compiler_params=pltpu.CompilerParams(
            dimension_semantics=("arbitrary",), vmem_limit_bytes=VMEM_LIMIT),
        name="dispatch",
    )(dest, pad_end, n_used, xn)


def _expert_kernel(be_ref, nu_ref, nv_ref, first_ref, next_ref, slot_ref, x_ref, w1_hbm, w3_hbm,
                   w2_hbm, y_ref, s1, s3, s2, w1b, w3b, w2b, sems):
    j = pl.program_id(0)
    used = j < nu_ref[0]
    blk = x_ref.shape[0]
    half = blk // 2

    def weight_copies(e, slot):
        return [pltpu.make_async_copy(w.at[e], s.at[slot], sems.at[slot])
                for w, s in ((w1_hbm, s1), (w3_hbm, s3), (w2_hbm, s2))]

    @pl.when(j == 0)
    def _():
        for c in weight_copies(be_ref[0], 0):
            c.start()

    @pl.when(used & (first_ref[j] == 1))
    def _():
        slot = slot_ref[j]
        for c in weight_copies(be_ref[j], slot):
            c.wait()

        @pl.when(next_ref[j] >= 0)
        def _():
            for c in weight_copies(next_ref[j], 1 - slot):
                c.start()

        w1b[...] = s1[slot].astype(BF16)
        w3b[...] = s3[slot].astype(BF16)
        w2b[...] = s2[slot].astype(BF16)

    def swiglu(rows):
        n = rows.stop - rows.start
        x = x_ref[rows].reshape(n, -1)
        a = _dot(x, w1b[...])
        hmid = a * jax.nn.sigmoid(a) * _dot(x, w3b[...])
        y_ref[rows] = _dot(hmid.astype(BF16), w2b[...]).reshape((n,) + y_ref.shape[1:])

    @pl.when(used & (nv_ref[j] > half))
    def _():
        swiglu(slice(0, blk))

    @pl.when(used & (nv_ref[j] <= half))
    def _():
        swiglu(slice(0, half))
        y_ref[half:blk] = jnp.zeros((blk - half,) + y_ref.shape[1:], y_ref.dtype)

    @pl.when(jnp.logical_not(used))
    def _():
        y_ref[...] = jnp.zeros_like(y_ref)


def _experts(block_expert, n_used, n_valid, first, next_expert, slot, x_buf, w1, w3, w2):
    P = x_buf.shape[0]
    D = w1.shape[1]
    assert x_buf.shape[1:] == ROW_TILE and D == SUBLANES * LANES
    blk = EXPERT_BLOCK
    d_exp = w2.shape[1]
    hbm = pl.BlockSpec(memory_space=pl.ANY)
    return pl.pallas_call(
        _expert_kernel,
        grid_spec=pltpu.PrefetchScalarGridSpec(
            num_scalar_prefetch=6,
            grid=(P // blk,),
            in_specs=[
                pl.BlockSpec((blk,) + ROW_TILE,
                             lambda j, be, nu, *_: (jnp.minimum(j, nu[0] - 1), 0, 0)),
                hbm, hbm, hbm,
            ],
            out_specs=pl.BlockSpec((blk,) + ROW_TILE, lambda j, *_: (j, 0, 0)),
            scratch_shapes=[
                pltpu.VMEM((2, D, d_exp), F32),
                pltpu.VMEM((2, D, d_exp), F32),
                pltpu.VMEM((2, d_exp, D), F32),
                pltpu.VMEM((D, d_exp), BF16),
                pltpu.VMEM((D, d_exp), BF16),
                pltpu.VMEM((d_exp, D), BF16),
                pltpu.SemaphoreType.DMA((2,)),
            ],
        ),
        out_shape=jax.ShapeDtypeStruct((P,) + ROW_TILE, F32),
        compiler_params=pltpu.CompilerParams(
            dimension_semantics=("arbitrary",), vmem_limit_bytes=VMEM_LIMIT),
        name="experts",
    )(block_expert, n_used, n_valid, first, next_expert, slot, x_buf, w1, w3, w2)


def _final_kernel(dest_ref, h_ref, rf_ref, p_ref, wup_ref, gple_ref, ggate_ref, wgate_ref,
                  gfin_ref, ybuf_ref, out_ref, *scratch, tm):
    row_bufs, sems = scratch[:-1], scratch[-1]
    s = pl.program_id(0)
    D = h_ref.shape[1]
    assert ROUTE_TILE % tm == 0

    def gather(tile, bufs, sem, straight=False):
        slot0 = _slot_base(tile * tm)

        def issue(g, c):
            for j in range(SUBLANES):
                for k in range(2):
                    t = g * SUBLANES + j
                    d = dest_ref[slot0 + k * ROUTE_TILE + t]
                    pltpu.make_async_copy(ybuf_ref.at[d], bufs[k].at[t], sem).start(priority=k)
            return c

        if straight:
            for g in range(tm // SUBLANES):
                issue(g, 0)
        else:
            lax.fori_loop(0, tm // SUBLANES, issue, 0)

    def wait(sem):
        for k in range(2):
            pltpu.make_async_copy(ybuf_ref.at[pl.ds(0, tm)], ybuf_ref.at[pl.ds(tm, tm)], sem).wait()

    def compute(rows, bufs):
        e = _rms(_dot(p_ref[rows, :].astype(BF16), wup_ref[...]), gple_ref[...])
        rf = rf_ref[rows, :]
        h = (h_ref[rows, :] + rf[:, 0:1] * bufs[0][...].reshape(tm, D)
             + rf[:, 1:2] * bufs[1][...].reshape(tm, D))
        gate = jax.nn.sigmoid(_dot(_rms(h, ggate_ref[...]).astype(BF16), wgate_ref[...]))
        h = h + gate * e
        out_ref[rows, :] = _rms(h, gfin_ref[...])

    n_tiles = pl.num_programs(0) * FINAL_TILES
    bufs = [(row_bufs[2 * i], row_bufs[2 * i + 1]) for i in range(FINAL_TILES)]

    @pl.when(s == 0)
    def _():
        for i in range(FINAL_AHEAD):
            gather(i, bufs[i], sems.at[i])

    for i in range(FINAL_TILES):
        wait(sems.at[i])
        nxt = (i + FINAL_AHEAD) % FINAL_TILES
        gather(jnp.minimum(s * FINAL_TILES + i + FINAL_AHEAD, n_tiles - 1), bufs[nxt],
               sems.at[nxt], straight=True)
        compute(slice(i * tm, (i + 1) * tm), bufs[i])

    @pl.when(s == pl.num_programs(0) - 1)
    def _():
        for i in range(FINAL_AHEAD):
            wait(sems.at[i])


def _final(dest, h, rf, p2, w_up, g_ple, g_gate, w_gate, g_fin, y_buf, tm=256):
    T, D = h.shape
    const = lambda i, *_: (0, 0)
    tile = lambda i, *_: (i, 0)
    assert y_buf.shape[1:] == ROW_TILE and D == SUBLANES * LANES
    row_buf = pltpu.VMEM((tm,) + ROW_TILE, F32)
    return pl.pallas_call(
        functools.partial(_final_kernel, tm=tm),
        grid_spec=pltpu.PrefetchScalarGridSpec(
            num_scalar_prefetch=1,
            grid=(T // (FINAL_TILES * tm),),
            in_specs=[
                pl.BlockSpec((FINAL_TILES * tm, D), tile),
                pl.BlockSpec((FINAL_TILES * tm, LANES), tile),
                pl.BlockSpec((FINAL_TILES * tm, p2.shape[1]), tile),
                pl.BlockSpec(w_up.shape, const),
                pl.BlockSpec((1, D), const),
                pl.BlockSpec((1, D), const),
                pl.BlockSpec(w_gate.shape, const),
                pl.BlockSpec((1, D), const),
                pl.BlockSpec(memory_space=pl.ANY),
            ],
            out_specs=pl.BlockSpec((FINAL_TILES * tm, D), tile),
            scratch_shapes=[row_buf] * (2 * FINAL_TILES) + [pltpu.SemaphoreType.DMA((FINAL_TILES,))],
        ),
        out_shape=jax.ShapeDtypeStruct((T, D), F32),
        compiler_params=pltpu.CompilerParams(
            dimension_semantics=("arbitrary",), vmem_limit_bytes=VMEM_LIMIT),
        name="final",
    )(dest, h, rf, p2, w_up, g_ple, g_gate, w_gate, g_fin, y_buf)


def _layer(h2, p2, positions, B, S, attn_norm, w_in, conv_w, conv_b, b_igate, b_fgate, ret_gn,
           ml_gn, w_out, moe_norm, w_group, b_group, w_router, b_router, w1, w3, w2, w_ple_up,
           ple_norm, ple_gate_norm, w_ple_gate, out_norm):
    T, D = h2.shape
    ret_w = ret_gn.shape[0]
    ml_w = ml_gn.shape[0]
    n_main = 4 * ret_w + 4 * ml_w
    row = lambda v: v.reshape(1, -1).astype(F32)
    pad_lanes = lambda a: jnp.pad(a, ((0, 0), (0, LANES - a.shape[1])))

    ml_heads = ml_w // LANES
    w_main = w_in.astype(BF16)
    gate_rows = lambda a: jnp.pad(a, ((0, SUBLANES - ml_heads), (0, 0)))
    w_i, w_f = w_in[:, n_main:n_main + ml_heads].T, w_in[:, n_main + ml_heads:].T
    w_gate_t = jnp.concatenate([gate_rows(w_i), gate_rows(w_f)]).astype(BF16)
    gate_bias = jnp.concatenate([gate_rows(b_igate[:, None]), gate_rows(b_fgate[:, None])])
    cos_t, sin_t = _rope_tables(positions, LANES // 2)
    decay, ws, wq, cd = _ret_tables(CHUNK)
    tm_in = 2 * CHUNK
    ops = _inproj(h2.reshape(B, S, D), row(attn_norm), w_main, w_gate_t, gate_bias.astype(F32),
                  cos_t.reshape(B, S, LANES), sin_t.reshape(B, S, LANES),
                  jnp.tile(ws, (tm_in // CHUNK, 1)), conv_w, conv_b.reshape(1, -1), ret_w, ml_w,
                  tm=tm_in)

    y = _mixer(ops, decay, wq, cd, ret_gn, ml_gn, B, S, ret_w, ml_w)

    pad_rows = lambda a: jnp.pad(a, ((0, ROUTER_ROWS - a.shape[0]), (0, 0)))
    wr = pad_rows(jnp.concatenate([w_router, w_group], axis=1).T)
    wr_hi = wr.astype(BF16)
    wr_lo = (wr - wr_hi.astype(F32)).astype(BF16)
    br = pad_rows(jnp.concatenate([b_router, b_group])[:, None].astype(F32))
    h_mid, xn, ri, rf, counts = _router(y, h2, w_out.astype(BF16), row(moe_norm), wr_hi, wr_lo, br)

    blk = EXPERT_BLOCK
    counts = counts[:, 0]
    padded = (counts + blk - 1) // blk * blk
    pad_end = jnp.cumsum(padded)
    pad_start = pad_end - padded
    n_blocks = (2 * T) // blk + N_EXPERTS
    assert n_blocks <= 256
    block_start = jnp.arange(n_blocks, dtype=jnp.int32) * blk
    block_expert = jnp.minimum(jnp.sum(pad_end[None, :] <= block_start[:, None], axis=1),
                               N_EXPERTS - 1).astype(jnp.int32)
    n_used = (pad_end[-1:] // blk).astype(jnp.int32)
    real_end = jnp.sum(jnp.where(block_expert[:, None] == jnp.arange(N_EXPERTS)[None, :],
                                 (pad_start + counts)[None, :], 0), axis=1)
    n_valid = jnp.clip(real_end - block_start, 0, blk).astype(jnp.int32)
    blk_id = jnp.arange(n_blocks, dtype=jnp.int32)
    is_used = blk_id < n_used[0]
    first = is_used & jnp.concatenate([jnp.ones((1,), bool), block_expert[1:] != block_expert[:-1]])
    seg_end = jnp.sum(jnp.where(block_expert[:, None] == jnp.arange(N_EXPERTS)[None, :],
                                (pad_end // blk)[None, :], 0), axis=1)
    next_blk = jnp.minimum(seg_end, n_blocks - 1)
    next_expert = jnp.sum(jnp.where(next_blk[:, None] == blk_id[None, :], block_expert[None, :], 0),
                          axis=1)
    next_expert = jnp.where(seg_end < n_used[0], next_expert, -1).astype(jnp.int32)
    slot = ((jnp.cumsum(first.astype(jnp.int32)) - 1) % 2).astype(jnp.int32)
    codes = ri[:, 0:2, :].reshape(-1)
    start_blk = jnp.dot(jax.nn.one_hot(codes % N_EXPERTS, N_EXPERTS, dtype=BF16),
                        (pad_start // blk).astype(BF16), preferred_element_type=F32)
    dest = (codes // N_EXPERTS + start_blk.astype(jnp.int32) * blk).astype(jnp.int32)

    x_buf = _dispatch(dest, pad_end.astype(jnp.int32), n_used, xn, n_blocks * blk,
                      tm=min(T, DISPATCH_TILE))
    y_buf = _experts(block_expert, n_used, n_valid, first.astype(jnp.int32), next_expert, slot,
                     x_buf, w1, w3, w2)

    return _final(dest, h_mid, rf, p2, w_ple_up.astype(BF16), row(ple_norm), row(ple_gate_norm),
                  w_ple_gate.astype(BF16), row(out_norm), y_buf)


def kernel(x, p, positions, attn_norm, w_in, conv_w, conv_b, b_igate, b_fgate, ret_gn, ml_gn,
           w_out, moe_norm, w_group, b_group, w_router, b_router, w1, w3, w2, w_ple_up, ple_norm,
           ple_gate_norm, w_ple_gate, final_norm):
    B, S, D = x.shape
    depth = p.shape[0]
    assert depth == 1, "the final RMSNorm is fused into the layer's last kernel"
    out = _layer(x.reshape(B * S, D), p[0].reshape(B * S, -1), positions, B, S,
                 attn_norm[0], w_in[0], conv_w[0], conv_b[0], b_igate[0], b_fgate[0], ret_gn[0],
                 ml_gn[0], w_out[0], moe_norm[0], w_group[0], b_group[0], w_router[0],
                 b_router[0], w1[0], w3[0], w2[0], w_ple_up[0], ple_norm[0], ple_gate_norm[0],
                 w_ple_gate[0], final_norm)
    return out.reshape(B, S, D)
```

```python
import functools

import jax
import jax.numpy as jnp
from jax import lax
from jax.experimental import pallas as pl
from jax.experimental.pallas import tpu as pltpu

F32 = jnp.float32
BF16 = jnp.bfloat16

RET_HEADS = 8
ML_HEADS = 4
CHUNK = 128
CONV_W = 4
ROPE_BASE = 10000.0
N_GROUPS = 4
EXPERTS_PER_GROUP = 8
N_EXPERTS = N_GROUPS * EXPERTS_PER_GROUP
EPS = 1e-6

LANES = 128
SUBLANES = 8
MXU_COLS = 256
MIXER_SEQS = 2
FINAL_TILES = 4
FINAL_AHEAD = 2
VMEM_LIMIT = 56 * 1024 * 1024
EXPERT_BLOCK = 512
GROUP_ROW0 = N_EXPERTS
ROUTER_ROWS = 64
ROUTE_TILE = 1024
ROUTE_ROWS = 8
DISPATCH_TILE = 2 * ROUTE_TILE
GATE_ROWS = 2 * SUBLANES
ROW_TILE = (SUBLANES, LANES)


def _rms(x, g):
    return x * lax.rsqrt(jnp.mean(x * x, axis=-1, keepdims=True) + EPS) * g


def _dot(a, b):
    return jnp.dot(a, b, preferred_element_type=F32)


def _dot_nt(a, b):
    return lax.dot_general(a, b, (((1,), (1,)), ((), ())), preferred_element_type=F32)


def _split3(x):
    hi = x.astype(BF16)
    r1 = x - hi.astype(F32)
    mid = r1.astype(BF16)
    lo = (r1 - mid.astype(F32)).astype(BF16)
    return hi, mid, lo


def _rope_kernel(pos_ref, freq_ref, sign_ref, cos_ref, sin_ref, *, half):
    ang = pos_ref[...].astype(F32) * freq_ref[...]
    cos_c = jnp.cos(ang)
    sin_c = jnp.sin(ang)
    group = lax.broadcasted_iota(jnp.int32, ang.shape, 1) // half
    per_row = LANES // half
    for s in range(per_row):
        def spread(t):
            out = t
            for g in range(per_row):
                if g != s:
                    out = jnp.where(group == g, pltpu.roll(t, (half * (g - s)) % LANES, 1), out)
            return out
        cos_ref[s] = spread(cos_c)
        sin_ref[s] = spread(sin_c) * sign_ref[...]


def _rope_tables(positions, dh):
    half = dh // 2
    per_row = LANES // half
    T = positions.size
    rows = T // per_row
    tr = min(rows, 1024)
    freqs = ROPE_BASE ** (-jnp.arange(half, dtype=F32) / half)
    pos_c = jnp.repeat(positions.reshape(per_row, rows).T, half, axis=1)
    sign = jnp.tile(jnp.concatenate([-jnp.ones((half,), F32), jnp.ones((half,), F32)]),
                    LANES // dh)[None, :]
    const = pl.BlockSpec((1, LANES), lambda i: (0, 0))
    out = pl.BlockSpec((per_row, tr, LANES), lambda i: (0, i, 0))
    cos_t, sin_t = pl.pallas_call(
        functools.partial(_rope_kernel, half=half),
        grid=(rows // tr,),
        in_specs=[pl.BlockSpec((tr, LANES), lambda i: (i, 0)), const, const],
        out_specs=[out, out],
        out_shape=[jax.ShapeDtypeStruct((per_row, rows, LANES), F32)] * 2,
        name="rope",
    )(pos_c, jnp.tile(freqs, per_row)[None, :], sign)
    return cos_t.reshape(T, LANES), sin_t.reshape(T, LANES)


def _inproj_kernel(x_ref, g_ref, w_hbm, wgt_ref, gb_ref, cos_ref, sin_ref, ws_ref, convw_ref,
                   convb_ref, rq_ref, rk_ref, rkwt_ref, rv_ref, rg_ref, mq_ref, mk_ref, mvt_ref,
                   mo_ref, gt_ref, carry, wm_ref, stage, wsem, *, ret_w, ml_w):
    tm = x_ref.shape[0]
    ret_dh = LANES // 2

    @pl.when((pl.program_id(0) == 0) & (pl.program_id(1) == 0))
    def _():
        wch = stage.shape[2]
        n_ch = wm_ref.shape[1] // wch
        chunk = lambda c: pltpu.make_async_copy(w_hbm.at[:, pl.ds(c * wch, wch)],
                                                stage.at[c % 2], wsem.at[c % 2])
        chunk(0).start()
        for c in range(n_ch):
            if c + 1 < n_ch:
                chunk(c + 1).start()
            chunk(c).wait()
            wm_ref[:, c * wch:(c + 1) * wch] = stage[c % 2].astype(BF16)

    @pl.when(pl.program_id(1) == 0)
    def _():
        carry[0:SUBLANES] = jnp.zeros((SUBLANES, carry.shape[1]), F32)

    xn = _rms(x_ref[...], g_ref[...]).astype(BF16)
    proj = lambda o, w: _dot(xn, wm_ref[:, o:o + w])
    tiles = lambda w: [slice(t, t + LANES) for t in range(0, w, LANES)]

    lane = lax.broadcasted_iota(jnp.int32, (tm, LANES), 1)
    first_half = (lane % ret_dh) < (ret_dh // 2)
    cos_t = cos_ref[...]
    sin_t = sin_ref[...]

    def rot(t):
        swapped = jnp.where(first_half, pltpu.roll(t, LANES - ret_dh // 2, 1),
                            pltpu.roll(t, ret_dh // 2, 1))
        return t * cos_t + swapped * sin_t

    for c0 in range(0, ret_w, MXU_COLS):
        rq = proj(c0, MXU_COLS)
        for ps in tiles(MXU_COLS):
            rq_ref[:, c0 + ps.start:c0 + ps.stop] = rot(rq[:, ps]).astype(BF16)
    for c0 in range(0, ret_w, MXU_COLS):
        rk = proj(ret_w + c0, MXU_COLS)
        for ps in tiles(MXU_COLS):
            cs = slice(c0 + ps.start, c0 + ps.stop)
            k = rot(rk[:, ps]) * (ret_dh ** -0.5)
            rk_ref[:, cs] = k.astype(BF16)
            rkwt_ref[cs, :] = (k * ws_ref[:, cs]).T.astype(BF16)
    for c0 in range(0, ret_w, MXU_COLS):
        cs = slice(c0, c0 + MXU_COLS)
        rv_ref[:, cs] = proj(2 * ret_w + c0, MXU_COLS).astype(BF16)
    for c0 in range(0, ret_w, MXU_COLS):
        g = proj(3 * ret_w + c0, MXU_COLS)
        rg_ref[:, c0:c0 + MXU_COLS] = g * jax.nn.sigmoid(g)

    o_mq = 4 * ret_w
    for c0 in range(0, 2 * ml_w, MXU_COLS):
        cs = slice(c0, c0 + MXU_COLS)
        xq = proj(o_mq + c0, MXU_COLS)
        carry[SUBLANES:SUBLANES + tm, cs] = xq
        acc = xq * convw_ref[CONV_W - 1:CONV_W, cs] + convb_ref[:, cs]
        for s in range(1, CONV_W):
            shifted = carry[SUBLANES - s:SUBLANES - s + tm, cs]
            acc = acc + shifted * convw_ref[CONV_W - 1 - s:CONV_W - s, cs]
        carry[0:SUBLANES, cs] = xq[tm - SUBLANES:tm]
        act = acc * jax.nn.sigmoid(acc)
        if c0 < ml_w:
            mq_ref[:, cs] = act.astype(BF16)
        else:
            mk_ref[:, c0 - ml_w:c0 - ml_w + MXU_COLS] = (act * (LANES ** -0.5)).astype(BF16)
    for c0 in range(0, ml_w, MXU_COLS):
        mv = proj(o_mq + 2 * ml_w + c0, MXU_COLS)
        for hs in tiles(MXU_COLS):
            mvt_ref[c0 + hs.start:c0 + hs.stop, :] = mv[:, hs].T
    for c0 in range(0, ml_w, MXU_COLS):
        mo_ref[:, c0:c0 + MXU_COLS] = jax.nn.sigmoid(proj(o_mq + 3 * ml_w + c0, MXU_COLS))
    gt_ref[...] = _dot_nt(wgt_ref[...], xn) + gb_ref[...]


def _inproj(x3, g, w_main, w_gate_t, gate_bias, cos_t, sin_t, ws, conv_w, conv_b, ret_w, ml_w,
            tm=256):
    B, S, D = x3.shape
    n_main = 4 * (ret_w + ml_w)
    const = lambda b, i: (0, 0)
    tok = lambda w: pl.BlockSpec((None, tm, w), lambda b, i: (b, i, 0))
    tok_t = lambda r: pl.BlockSpec((None, r, tm), lambda b, i: (b, 0, i))
    act = lambda w, dt: jax.ShapeDtypeStruct((B, S, w), dt)
    act_t = lambda r, dt: jax.ShapeDtypeStruct((B, r, S), dt)
    return pl.pallas_call(
        functools.partial(_inproj_kernel, ret_w=ret_w, ml_w=ml_w),
        grid=(B, S // tm),
        in_specs=[
            tok(D),
            pl.BlockSpec((1, D), const),
            pl.BlockSpec(memory_space=pl.ANY),
            pl.BlockSpec((GATE_ROWS, D), const),
            pl.BlockSpec((GATE_ROWS, 1), const),
            tok(LANES),
            tok(LANES),
            pl.BlockSpec((tm, ret_w), const),
            pl.BlockSpec((CONV_W, 2 * ml_w), const),
            pl.BlockSpec((1, 2 * ml_w), const),
        ],
        out_specs=[tok(ret_w), tok(ret_w), tok_t(ret_w), tok(ret_w), tok(ret_w),
                   tok(ml_w), tok(ml_w), tok_t(ml_w), tok(ml_w), tok_t(GATE_ROWS)],
        out_shape=[
            act(ret_w, BF16),
            act(ret_w, BF16),
            act_t(ret_w, BF16),
            act(ret_w, BF16),
            act(ret_w, F32),
            act(ml_w, BF16),
            act(ml_w, BF16),
            act_t(ml_w, F32),
            act(ml_w, F32),
            act_t(GATE_ROWS, F32),
        ],
        scratch_shapes=[
            pltpu.VMEM((SUBLANES + tm, 2 * ml_w), F32),
            pltpu.VMEM((D, n_main), BF16),
            pltpu.VMEM((2, D, 2 * MXU_COLS), F32),
            pltpu.SemaphoreType.DMA((2,)),
        ],
        compiler_params=pltpu.CompilerParams(
            dimension_semantics=("arbitrary", "arbitrary"), vmem_limit_bytes=VMEM_LIMIT),
        name="inproj",
    )(x3, g, w_main, w_gate_t, gate_bias, cos_t, sin_t, ws, conv_w, conv_b)


def _mixer_kernel(rq_ref, rk_ref, rkwt_ref, rv_ref, rg_ref, mq_ref, mk_ref, mvt_ref, mo_ref,
                  gt_ref, decay_ref, wq_ref, cd_ref, retgn_ref, mlgn_ref,
                  y_ref, r_state, s_state, n_state, m_state, *, ret_w, ml_w):
    BB, C = rq_ref.shape[0], rq_ref.shape[1]
    n_pairs = ret_w // LANES
    ml_heads = ml_w // LANES
    ret_dh = LANES // 2

    @pl.when(pl.program_id(1) == 0)
    def _():
        r_state[...] = jnp.zeros_like(r_state)
        s_state[...] = jnp.zeros_like(s_state)
        n_state[...] = jnp.zeros_like(n_state)
        m_state[...] = jnp.zeros_like(m_state)

    lane = lax.broadcasted_iota(jnp.int32, (C, LANES), 1)
    row = lax.broadcasted_iota(jnp.int32, (C, LANES), 0)
    assert C == LANES
    lo = lane < ret_dh
    blockdiag = (row < ret_dh) == lo
    lo_b = jnp.where(lo, 1.0, 0.0).astype(BF16)
    hi_b = jnp.where(lo, 0.0, 1.0).astype(BF16)
    seqs = range(BB)
    pair_units = [(s, p, slice(p * LANES, (p + 1) * LANES)) for s in seqs for p in range(n_pairs)]
    head_units = [(s, h, slice(h * LANES, (h + 1) * LANES)) for s in seqs for h in range(ml_heads)]


    row8 = lax.broadcasted_iota(jnp.int32, (SUBLANES, C), 0)
    lane8 = lax.broadcasted_iota(jnp.int32, (SUBLANES, C), 1)
    live = row8 < ml_heads
    triu = (row <= lane).astype(BF16)
    before = row <= lane
    mx, w_inter, e_negm, w_state, w_state_b, dec, beta_t = [], [], [], [], [], [], []
    for s in seqs:
        ig = jnp.where(live, gt_ref[s, 0:SUBLANES, :], 0.0)
        f_pre = jnp.where(live, gt_ref[s, SUBLANES:2 * SUBLANES, :], 30.0)
        l_hi, l_mid, l_lo = _split3(jax.nn.log_sigmoid(f_pre))
        b = _dot(l_hi, triu) + _dot(l_mid, triu) + _dot(l_lo, triu)
        beta = ig - b
        cm = beta
        shift = 1
        while shift < C:
            cm = jnp.maximum(cm, jnp.where(lane8 >= shift, pltpu.roll(cm, shift, 1), -jnp.inf))
            shift *= 2
        m_prev = m_state[s]
        mx.append(jnp.maximum(cm, m_prev))
        mx_last = jnp.broadcast_to(mx[s][:, C - 1:C], (SUBLANES, C))
        w_inter.append(jnp.exp(m_prev - mx[s]))
        e_negm.append(jnp.exp(-(b + mx[s])))
        w_state.append(jnp.exp(beta - mx_last))
        w_state_b.append(w_state[s].astype(BF16))
        dec.append(jnp.exp(m_prev - mx_last))
        m_state[s] = jnp.where(live, jnp.broadcast_to(b[:, C - 1:C], (SUBLANES, C)) + mx_last, 0.0)
        beta_t.append(jnp.concatenate([beta, jnp.zeros((LANES - SUBLANES, C), F32)], axis=0).T)

    hrow = lambda t, h: t[h:h + 1, :]

    rq = [rq_ref[s, :, ps] for s, p, ps in pair_units]
    rk = [rk_ref[s, :, ps] for s, p, ps in pair_units]
    rv = [rv_ref[s, :, ps] for s, p, ps in pair_units]
    r_prev = [r_state[s, p] for s, p, ps in pair_units]
    pu = range(len(pair_units))
    s_ab = [_dot_nt(jnp.concatenate([rq[u] * lo_b, rq[u] * hi_b], axis=0), rk[u]) for u in pu]
    r_read = [_dot(rq[u], r_prev[u].astype(BF16)) for u in pu]
    r_new = [_dot(rkwt_ref[s, ps, :], rv[u]) for u, (s, p, ps) in enumerate(pair_units)]
    hu = range(len(head_units))
    mq = [mq_ref[s, :, hs] for s, h, hs in head_units]
    mk = [mk_ref[s, :, hs] for s, h, hs in head_units]
    mv_t = [mvt_ref[s, hs, :] for s, h, hs in head_units]
    st_prev = [s_state[s, h] for s, h, hs in head_units]
    n_prev = [n_state[s, h] for s, h, hs in head_units]
    a_t = [_dot_nt(mk[u], mq[u]) for u in hu]
    s_read = [_dot_nt(st_prev[u].astype(BF16), mq[u]) for u in hu]
    qn = [_dot_nt(n_prev[u].astype(BF16), mq[u])[0:1, :] for u in hu]
    s_new = [_dot((mv_t[u] * hrow(w_state[s], h)).astype(BF16), mk[u])
             for u, (s, h, hs) in enumerate(head_units)]
    n_new = [_dot(w_state_b[s], mk[u])[h:h + 1, :] for u, (s, h, hs) in enumerate(head_units)]
    for u, (s, p, ps) in enumerate(pair_units):
        r_state[s, p] = cd_ref[p] * r_prev[u] + jnp.where(blockdiag, r_new[u], 0.0)
    for u, (s, h, hs) in enumerate(head_units):
        s_state[s, h] = hrow(dec[s], h) * st_prev[u] + s_new[u]
        n_state[s, h] = hrow(dec[s], h) * n_prev[u] + n_new[u]

    s_ab = [jnp.concatenate([(s_ab[u][0:C] * decay_ref[2 * p]).astype(BF16),
                             (s_ab[u][C:2 * C] * decay_ref[2 * p + 1]).astype(BF16)], axis=1)
            for u, (s, p, ps) in enumerate(pair_units)]
    p_t = [jnp.exp(jnp.where(before, beta_t[s][:, h:h + 1] - hrow(mx[s], h), -jnp.inf)) * a_t[u]
           for u, (s, h, hs) in enumerate(head_units)]

    o = [_dot(s_ab[u], jnp.concatenate([rv[u] * lo_b, rv[u] * hi_b], axis=0)) + r_read[u] * wq_ref[p]
         for u, (s, p, ps) in enumerate(pair_units)]
    num_t = [_dot(mv_t[u].astype(BF16), p_t[u].astype(BF16)) + hrow(w_inter[s], h) * s_read[u]
             for u, (s, h, hs) in enumerate(head_units)]

    for u, (s, p, ps) in enumerate(pair_units):
        sq = o[u] * o[u]
        ms_a = jnp.sum(jnp.where(lo, sq, 0.0), axis=-1, keepdims=True)
        ms_b = jnp.sum(jnp.where(lo, 0.0, sq), axis=-1, keepdims=True)
        ms = jnp.where(lo, ms_a, ms_b) * (1.0 / ret_dh)
        r = o[u] * lax.rsqrt(ms + EPS) * retgn_ref[:, ps]
        y_ref[s, :, ps] = (rg_ref[s, :, ps] * r).astype(y_ref.dtype)
    for u, (s, h, hs) in enumerate(head_units):
        den = jnp.sum(p_t[u], axis=0, keepdims=True) + hrow(w_inter[s], h) * qn[u]
        hh = (num_t[u] * (1.0 / jnp.maximum(jnp.abs(den), hrow(e_negm[s], h)))).T
        hm = _rms(mo_ref[s, :, hs] * hh, mlgn_ref[:, hs])
        y_ref[s, :, ret_w + h * LANES:ret_w + (h + 1) * LANES] = hm.astype(y_ref.dtype)


def _ret_tables(C):
    H = RET_HEADS
    dh = LANES // 2
    log_gamma = jnp.log1p(-(2.0 ** (-5.0 - jnp.arange(H, dtype=F32))))
    idx = jnp.arange(C, dtype=F32)
    rel = idx[:, None] - idx[None, :]
    causal = rel >= 0
    decay = jnp.where(causal, jnp.exp(log_gamma[:, None, None] * jnp.where(causal, rel, 0.0)), 0.0)
    w_state = jnp.exp(log_gamma[:, None] * (C - 1 - idx))
    w_query = jnp.exp(log_gamma[:, None] * (idx + 1.0))
    chunk_decay = jnp.exp(log_gamma * C)
    pair = lambda t: jnp.repeat(t.reshape(H // 2, 2, C).transpose(0, 2, 1), dh, axis=2)
    cd = jnp.repeat(chunk_decay.reshape(H // 2, 2), dh, axis=1)
    cd = jnp.broadcast_to(cd[:, :, None], (H // 2, LANES, LANES))
    ws = pair(w_state).transpose(1, 0, 2).reshape(C, (H // 2) * LANES)
    return decay, ws, pair(w_query), cd


def _mixer(ops, decay, wq, cd, ret_gn, ml_gn, B, S, ret_w, ml_w):
    C = CHUNK
    N = S // C
    n_pairs = ret_w // LANES
    ml_heads = ml_w // LANES
    c2 = lambda b, n: (0, 0)
    c3 = lambda b, n: (0, 0, 0)
    BB = MIXER_SEQS if B % MIXER_SEQS == 0 else 1
    tok = lambda w: pl.BlockSpec((BB, C, w), lambda b, n: (b, n, 0))
    tok_t = lambda r: pl.BlockSpec((BB, r, C), lambda b, n: (b, 0, n))
    y = pl.pallas_call(
        functools.partial(_mixer_kernel, ret_w=ret_w, ml_w=ml_w),
        grid=(B // BB, N),
        in_specs=[
            tok(ret_w), tok(ret_w), tok_t(ret_w), tok(ret_w), tok(ret_w),
            tok(ml_w), tok(ml_w), tok_t(ml_w), tok(ml_w), tok_t(GATE_ROWS),
            pl.BlockSpec((RET_HEADS, C, C), c3),
            pl.BlockSpec((n_pairs, C, LANES), c3),
            pl.BlockSpec((n_pairs, LANES, LANES), c3),
            pl.BlockSpec((1, ret_w), c2),
            pl.BlockSpec((1, ml_w), c2),
        ],
        out_specs=pl.BlockSpec((BB, C, ret_w + ml_w), lambda b, n: (b, n, 0)),
        out_shape=jax.ShapeDtypeStruct((B, S, ret_w + ml_w), BF16),
        scratch_shapes=[
            pltpu.VMEM((BB, n_pairs, LANES, LANES), F32),
            pltpu.VMEM((BB, ml_heads, LANES, LANES), F32),
            pltpu.VMEM((BB, ml_heads, SUBLANES, LANES), F32),
            pltpu.VMEM((BB, SUBLANES, LANES), F32),
        ],
        compiler_params=pltpu.CompilerParams(
            dimension_semantics=("arbitrary", "arbitrary"), vmem_limit_bytes=VMEM_LIMIT),
        name="mixer",
    )(*ops, decay, wq, cd, ret_gn.reshape(1, -1), ml_gn.reshape(1, -1))
    return y.reshape(B * S, ret_w + ml_w)


def _router_kernel(y_ref, x_ref, wo_ref, g_ref, wr_hi_ref, wr_lo_ref, br_ref,
                   h_ref, xn_ref, ri_ref, rf_ref, cnt_ref, run_cnt):
    tm = y_ref.shape[0]

    @pl.when(pl.program_id(0) == 0)
    def _():
        run_cnt[...] = jnp.zeros_like(run_cnt)

    h = x_ref[...] + _dot(y_ref[...], wo_ref[...])
    h_ref[...] = h
    xn = _rms(h, g_ref[...])
    x_hi = xn.astype(BF16)
    xn_ref[...] = x_hi.reshape(xn_ref.shape)
    x_lo = (xn - x_hi.astype(F32)).astype(BF16)
    logits = (_dot_nt(wr_hi_ref[...], x_hi) + _dot_nt(wr_hi_ref[...], x_lo)
              + _dot_nt(wr_lo_ref[...], x_hi) + br_ref[...])
    big = jnp.int32(LANES)
    neg = -jnp.inf
    gl = logits[GROUP_ROW0:GROUP_ROW0 + 8]
    grow = lax.broadcasted_iota(jnp.int32, gl.shape, 0)
    is_g = grow < N_GROUPS
    gl = jnp.where(is_g, gl, neg)
    gmax = jnp.max(gl, axis=0, keepdims=True)
    gsum = jnp.sum(jnp.where(is_g, jnp.exp(gl - gmax), 0.0), axis=0, keepdims=True)
    p_g = 1.0 / gsum
    g_sel = jnp.min(jnp.where(is_g & (gl == gmax), grow, big), axis=0, keepdims=True)
    el = logits[0:N_EXPERTS]
    erow = lax.broadcasted_iota(jnp.int32, el.shape, 0)
    in_grp = (erow // EXPERTS_PER_GROUP) == g_sel
    el = jnp.where(in_grp, el, neg)
    emax = jnp.max(el, axis=0, keepdims=True)
    eexp = jnp.where(in_grp, jnp.exp(el - emax), 0.0)
    prob = eexp / jnp.sum(eexp, axis=0, keepdims=True)
    pm1 = jnp.where(in_grp, prob, -1.0)
    p1 = jnp.max(pm1, axis=0, keepdims=True)
    i1 = jnp.min(jnp.where(pm1 == p1, erow, big), axis=0, keepdims=True)
    pm2 = jnp.where(erow == i1, -1.0, pm1)
    p2 = jnp.max(pm2, axis=0, keepdims=True)
    i2 = jnp.min(jnp.where(pm2 == p2, erow, big), axis=0, keepdims=True)
    denom = p1 + p2
    g1 = p_g * p1 / denom
    g2 = p_g * p2 / denom

    sel1 = erow == i1
    sel2 = erow == i2
    onehot = (sel1 | sel2).astype(BF16)
    r_i = lax.broadcasted_iota(jnp.int32, (tm, tm), 0)
    c_i = lax.broadcasted_iota(jnp.int32, (tm, tm), 1)
    tri = (r_i < c_i).astype(BF16)
    prefix = _dot(onehot, tri) + run_cnt[:, 0:1]
    rank1 = jnp.sum(jnp.where(sel1, prefix, 0.0), axis=0, keepdims=True).astype(jnp.int32)
    rank2 = jnp.sum(jnp.where(sel2, prefix, 0.0), axis=0, keepdims=True).astype(jnp.int32)
    new_cnt = run_cnt[...] + jnp.sum(onehot.astype(F32), axis=1, keepdims=True)
    run_cnt[...] = new_cnt
    cnt_ref[...] = new_cnt.astype(jnp.int32)

    rrow = lax.broadcasted_iota(jnp.int32, (ROUTE_ROWS, tm), 0)
    ri_ref[...] = jnp.where(rrow == 0, rank1 * N_EXPERTS + i1,
                            jnp.where(rrow == 1, rank2 * N_EXPERTS + i2, 0))
    lrow = lax.broadcasted_iota(jnp.int32, (LANES, tm), 0)
    rf_ref[...] = jnp.where(lrow == 0, g1, jnp.where(lrow == 1, g2, 0.0)).T


def _router(y, x2, w_out, g, wr_hi, wr_lo, br, tm=ROUTE_TILE):
    T, D = x2.shape
    const = lambda i: (0, 0)
    tile = lambda i: (i, 0)
    return pl.pallas_call(
        _router_kernel,
        grid=(T // tm,),
        in_specs=[
            pl.BlockSpec((tm, y.shape[1]), tile),
            pl.BlockSpec((tm, D), tile),
            pl.BlockSpec(w_out.shape, const),
            pl.BlockSpec((1, D), const),
            pl.BlockSpec((ROUTER_ROWS, D), const),
            pl.BlockSpec((ROUTER_ROWS, D), const),
            pl.BlockSpec((ROUTER_ROWS, 1), const),
        ],
        out_specs=[
            pl.BlockSpec((tm, D), tile),
            pl.BlockSpec((tm,) + ROW_TILE, lambda i: (i, 0, 0)),
            pl.BlockSpec((None, ROUTE_ROWS, tm), lambda i: (i, 0, 0)),
            pl.BlockSpec((tm, LANES), tile),
            pl.BlockSpec((N_EXPERTS, LANES), const),
        ],
        out_shape=[
            jax.ShapeDtypeStruct((T, D), F32),
            jax.ShapeDtypeStruct((T,) + ROW_TILE, BF16),
            jax.ShapeDtypeStruct((T // tm, ROUTE_ROWS, tm), jnp.int32),
            jax.ShapeDtypeStruct((T, LANES), F32),
            jax.ShapeDtypeStruct((N_EXPERTS, LANES), jnp.int32),
        ],
        scratch_shapes=[pltpu.VMEM((N_EXPERTS, LANES), F32)],
        compiler_params=pltpu.CompilerParams(
            dimension_semantics=("arbitrary",), vmem_limit_bytes=VMEM_LIMIT),
        name="router",
    )(y, x2, w_out, g, wr_hi, wr_lo, br)


def _slot_base(tok0):
    return (tok0 // ROUTE_TILE) * (2 * ROUTE_TILE) + tok0 % ROUTE_TILE


def _route_spans(tm):
    assert tm % ROUTE_TILE == 0 or ROUTE_TILE % tm == 0
    span = min(tm, ROUTE_TILE)
    return [(t0, span) for t0 in range(0, tm, span)]


def _for_each_pad_block(pend_ref, nu_ref, n_blocks, fn):
    blk = EXPERT_BLOCK
    for e in range(N_EXPERTS):
        prev_end = 0 if e == 0 else pend_ref[e - 1]

        @pl.when(pend_ref[e] > prev_end)
        def _():
            fn(pl.multiple_of(pend_ref[e] - blk, blk))

    def tail(j, c):
        fn(pl.multiple_of(j * blk, blk))
        return c

    lax.fori_loop(nu_ref[0], n_blocks, tail, 0)


def _dispatch_kernel(dest_ref, pend_ref, nu_ref, xn_ref, buf_ref, zeros, sem, zsem):
    tm = xn_ref.shape[0]
    blk = EXPERT_BLOCK
    i = pl.program_id(0)

    @pl.when(i == 0)
    def _():
        zeros[...] = jnp.zeros_like(zeros)
        zcopy = lambda row: pltpu.make_async_copy(zeros, buf_ref.at[pl.ds(row, blk)], zsem)
        n_blocks = buf_ref.shape[0] // blk
        _for_each_pad_block(pend_ref, nu_ref, n_blocks, lambda row: zcopy(row).start())
        _for_each_pad_block(pend_ref, nu_ref, n_blocks, lambda row: zcopy(row).wait())

    for t0, span in _route_spans(tm):
        slot0 = _slot_base(i * tm + t0)

        def issue(g, c, t0=t0, slot0=slot0):
            for j in range(SUBLANES):
                for k in range(2):
                    t = g * SUBLANES + j
                    d = dest_ref[slot0 + k * ROUTE_TILE + t]
                    pltpu.make_async_copy(xn_ref.at[t0 + t], buf_ref.at[d], sem).start(priority=k)
            return c

        lax.fori_loop(0, span // SUBLANES, issue, 0)
    for k in range(2):
        pltpu.make_async_copy(buf_ref.at[pl.ds(tm, tm)], buf_ref.at[pl.ds(0, tm)], sem).wait()


def _dispatch(dest, pad_end, n_used, xn, n_rows, tm=ROUTE_TILE):
    T = xn.shape[0]
    return pl.pallas_call(
        _dispatch_kernel,
        grid_spec=pltpu.PrefetchScalarGridSpec(
            num_scalar_prefetch=3,
            grid=(T // tm,),
            in_specs=[pl.BlockSpec((tm,) + ROW_TILE, lambda i, *_: (i, 0, 0))],
            out_specs=pl.BlockSpec(memory_space=pl.ANY),
            scratch_shapes=[
                pltpu.VMEM((EXPERT_BLOCK,) + ROW_TILE, xn.dtype),
                pltpu.SemaphoreType.DMA,
                pltpu.SemaphoreType.DMA,
            ],
        ),
        out_shape=jax.ShapeDtypeStruct((n_rows,) + ROW_TILE, xn.dtype),
        compiler_params=pltpu.CompilerParams(
            dimension_semantics=("arbitrary",), vmem_limit_bytes=VMEM_LIMIT),
        name="dispatch",
    )(dest, pad_end, n_used, xn)


def _expert_kernel(be_ref, nu_ref, nv_ref, first_ref, next_ref, slot_ref, x_ref, w1_hbm, w3_hbm,
                   w2_hbm, y_ref, s1, s3, s2, w1b, w3b, w2b, sems):
    j = pl.program_id(0)
    used = j < nu_ref[0]
    blk = x_ref.shape[0]
    half = blk // 2

    def weight_copies(e, slot):
        return [pltpu.make_async_copy(w.at[e], s.at[slot], sems.at[slot])
                for w, s in ((w1_hbm, s1), (w3_hbm, s3), (w2_hbm, s2))]

    @pl.when(j == 0)
    def _():
        for c in weight_copies(be_ref[0], 0):
            c.start()

    @pl.when(used & (first_ref[j] == 1))
    def _():
        slot = slot_ref[j]
        for c in weight_copies(be_ref[j], slot):
            c.wait()

        @pl.when(next_ref[j] >= 0)
        def _():
            for c in weight_copies(next_ref[j], 1 - slot):
                c.start()

        w1b[...] = s1[slot].astype(BF16)
        w3b[...] = s3[slot].astype(BF16)
        w2b[...] = s2[slot].astype(BF16)

    def swiglu(rows):
        n = rows.stop - rows.start
        x = x_ref[rows].reshape(n, -1)
        a = _dot(x, w1b[...])
        hmid = a * jax.nn.sigmoid(a) * _dot(x, w3b[...])
        y_ref[rows] = _dot(hmid.astype(BF16), w2b[...]).reshape((n,) + y_ref.shape[1:])

    @pl.when(used & (nv_ref[j] > half))
    def _():
        swiglu(slice(0, blk))

    @pl.when(used & (nv_ref[j] <= half))
    def _():
        swiglu(slice(0, half))
        y_ref[half:blk] = jnp.zeros((blk - half,) + y_ref.shape[1:], y_ref.dtype)

    @pl.when(jnp.logical_not(used))
    def _():
        y_ref[...] = jnp.zeros_like(y_ref)


def _experts(block_expert, n_used, n_valid, first, next_expert, slot, x_buf, w1, w3, w2):
    P = x_buf.shape[0]
    D = w1.shape[1]
    assert x_buf.shape[1:] == ROW_TILE and D == SUBLANES * LANES
    blk = EXPERT_BLOCK
    d_exp = w2.shape[1]
    hbm = pl.BlockSpec(memory_space=pl.ANY)
    return pl.pallas_call(
        _expert_kernel,
        grid_spec=pltpu.PrefetchScalarGridSpec(
            num_scalar_prefetch=6,
            grid=(P // blk,),
            in_specs=[
                pl.BlockSpec((blk,) + ROW_TILE,
                             lambda j, be, nu, *_: (jnp.minimum(j, nu[0] - 1), 0, 0)),
                hbm, hbm, hbm,
            ],
            out_specs=pl.BlockSpec((blk,) + ROW_TILE, lambda j, *_: (j, 0, 0)),
            scratch_shapes=[
                pltpu.VMEM((2, D, d_exp), F32),
                pltpu.VMEM((2, D, d_exp), F32),
                pltpu.VMEM((2, d_exp, D), F32),
                pltpu.VMEM((D, d_exp), BF16),
                pltpu.VMEM((D, d_exp), BF16),
                pltpu.VMEM((d_exp, D), BF16),
                pltpu.SemaphoreType.DMA((2,)),
            ],
        ),
        out_shape=jax.ShapeDtypeStruct((P,) + ROW_TILE, F32),
        compiler_params=pltpu.CompilerParams(
            dimension_semantics=("arbitrary",), vmem_limit_bytes=VMEM_LIMIT),
        name="experts",
    )(block_expert, n_used, n_valid, first, next_expert, slot, x_buf, w1, w3, w2)


def _final_kernel(dest_ref, h_ref, rf_ref, p_ref, wup_ref, gple_ref, ggate_ref, wgate_ref,
                  gfin_ref, ybuf_ref, out_ref, *scratch, tm):
    row_bufs, sems = scratch[:-1], scratch[-1]
    s = pl.program_id(0)
    D = h_ref.shape[1]
    assert ROUTE_TILE % tm == 0

    def gather(tile, bufs, sem, straight=False):
        slot0 = _slot_base(tile * tm)

        def issue(g, c):
            for j in range(SUBLANES):
                for k in range(2):
                    t = g * SUBLANES + j
                    d = dest_ref[slot0 + k * ROUTE_TILE + t]
                    pltpu.make_async_copy(ybuf_ref.at[d], bufs[k].at[t], sem).start(priority=k)
            return c

        if straight:
            for g in range(tm // SUBLANES):
                issue(g, 0)
        else:
            lax.fori_loop(0, tm // SUBLANES, issue, 0)

    def wait(sem):
        for k in range(2):
            pltpu.make_async_copy(ybuf_ref.at[pl.ds(0, tm)], ybuf_ref.at[pl.ds(tm, tm)], sem).wait()

    def compute(rows, bufs):
        e = _rms(_dot(p_ref[rows, :].astype(BF16), wup_ref[...]), gple_ref[...])
        rf = rf_ref[rows, :]
        h = (h_ref[rows, :] + rf[:, 0:1] * bufs[0][...].reshape(tm, D)
             + rf[:, 1:2] * bufs[1][...].reshape(tm, D))
        gate = jax.nn.sigmoid(_dot(_rms(h, ggate_ref[...]).astype(BF16), wgate_ref[...]))
        h = h + gate * e
        out_ref[rows, :] = _rms(h, gfin_ref[...])

    n_tiles = pl.num_programs(0) * FINAL_TILES
    bufs = [(row_bufs[2 * i], row_bufs[2 * i + 1]) for i in range(FINAL_TILES)]

    @pl.when(s == 0)
    def _():
        for i in range(FINAL_AHEAD):
            gather(i, bufs[i], sems.at[i])

    for i in range(FINAL_TILES):
        wait(sems.at[i])
        nxt = (i + FINAL_AHEAD) % FINAL_TILES
        gather(jnp.minimum(s * FINAL_TILES + i + FINAL_AHEAD, n_tiles - 1), bufs[nxt],
               sems.at[nxt], straight=True)
        compute(slice(i * tm, (i + 1) * tm), bufs[i])

    @pl.when(s == pl.num_programs(0) - 1)
    def _():
        for i in range(FINAL_AHEAD):
            wait(sems.at[i])


def _final(dest, h, rf, p2, w_up, g_ple, g_gate, w_gate, g_fin, y_buf, tm=256):
    T, D = h.shape
    const = lambda i, *_: (0, 0)
    tile = lambda i, *_: (i, 0)
    assert y_buf.shape[1:] == ROW_TILE and D == SUBLANES * LANES
    row_buf = pltpu.VMEM((tm,) + ROW_TILE, F32)
    return pl.pallas_call(
        functools.partial(_final_kernel, tm=tm),
        grid_spec=pltpu.PrefetchScalarGridSpec(
            num_scalar_prefetch=1,
            grid=(T // (FINAL_TILES * tm),),
            in_specs=[
                pl.BlockSpec((FINAL_TILES * tm, D), tile),
                pl.BlockSpec((FINAL_TILES * tm, LANES), tile),
                pl.BlockSpec((FINAL_TILES * tm, p2.shape[1]), tile),
                pl.BlockSpec(w_up.shape, const),
                pl.BlockSpec((1, D), const),
                pl.BlockSpec((1, D), const),
                pl.BlockSpec(w_gate.shape, const),
                pl.BlockSpec((1, D), const),
                pl.BlockSpec(memory_space=pl.ANY),
            ],
            out_specs=pl.BlockSpec((FINAL_TILES * tm, D), tile),
            scratch_shapes=[row_buf] * (2 * FINAL_TILES) + [pltpu.SemaphoreType.DMA((FINAL_TILES,))],
        ),
        out_shape=jax.ShapeDtypeStruct((T, D), F32),
        compiler_params=pltpu.CompilerParams(
            dimension_semantics=("arbitrary",), vmem_limit_bytes=VMEM_LIMIT),
        name="final",
    )(dest, h, rf, p2, w_up, g_ple, g_gate, w_gate, g_fin, y_buf)


def _layer(h2, p2, positions, B, S, attn_norm, w_in, conv_w, conv_b, b_igate, b_fgate, ret_gn,
           ml_gn, w_out, moe_norm, w_group, b_group, w_router, b_router, w1, w3, w2, w_ple_up,
           ple_norm, ple_gate_norm, w_ple_gate, out_norm):
    T, D = h2.shape
    ret_w = ret_gn.shape[0]
    ml_w = ml_gn.shape[0]
    n_main = 4 * ret_w + 4 * ml_w
    row = lambda v: v.reshape(1, -1).astype(F32)
    pad_lanes = lambda a: jnp.pad(a, ((0, 0), (0, LANES - a.shape[1])))

    ml_heads = ml_w // LANES
    w_main = w_in
    gate_rows = lambda a: jnp.pad(a, ((0, SUBLANES - ml_heads), (0, 0)))
    w_i, w_f = w_in[:, n_main:n_main + ml_heads].T, w_in[:, n_main + ml_heads:].T
    w_gate_t = jnp.concatenate([gate_rows(w_i), gate_rows(w_f)]).astype(BF16)
    gate_bias = jnp.concatenate([gate_rows(b_igate[:, None]), gate_rows(b_fgate[:, None])])
    cos_t, sin_t = _rope_tables(positions, LANES // 2)
    decay, ws, wq, cd = _ret_tables(CHUNK)
    tm_in = 2 * CHUNK
    ops = _inproj(h2.reshape(B, S, D), row(attn_norm), w_main, w_gate_t, gate_bias.astype(F32),
                  cos_t.reshape(B, S, LANES), sin_t.reshape(B, S, LANES),
                  jnp.tile(ws, (tm_in // CHUNK, 1)), conv_w, conv_b.reshape(1, -1), ret_w, ml_w,
                  tm=tm_in)

    y = _mixer(ops, decay, wq, cd, ret_gn, ml_gn, B, S, ret_w, ml_w)

    pad_rows = lambda a: jnp.pad(a, ((0, ROUTER_ROWS - a.shape[0]), (0, 0)))
    wr = pad_rows(jnp.concatenate([w_router, w_group], axis=1).T)
    wr_hi = wr.astype(BF16)
    wr_lo = (wr - wr_hi.astype(F32)).astype(BF16)
    br = pad_rows(jnp.concatenate([b_router, b_group])[:, None].astype(F32))
    h_mid, xn, ri, rf, counts = _router(y, h2, w_out.astype(BF16), row(moe_norm), wr_hi, wr_lo, br)

    blk = EXPERT_BLOCK
    counts = counts[:, 0]
    padded = (counts + blk - 1) // blk * blk
    pad_end = jnp.cumsum(padded)
    pad_start = pad_end - padded
    n_blocks = (2 * T) // blk + N_EXPERTS
    assert n_blocks <= 256
    block_start = jnp.arange(n_blocks, dtype=jnp.int32) * blk
    block_expert = jnp.minimum(jnp.sum(pad_end[None, :] <= block_start[:, None], axis=1),
                               N_EXPERTS - 1).astype(jnp.int32)
    n_used = (pad_end[-1:] // blk).astype(jnp.int32)
    real_end = jnp.sum(jnp.where(block_expert[:, None] == jnp.arange(N_EXPERTS)[None, :],
                                 (pad_start + counts)[None, :], 0), axis=1)
    n_valid = jnp.clip(real_end - block_start, 0, blk).astype(jnp.int32)
    blk_id = jnp.arange(n_blocks, dtype=jnp.int32)
    is_used = blk_id < n_used[0]
    first = is_used & jnp.concatenate([jnp.ones((1,), bool), block_expert[1:] != block_expert[:-1]])
    seg_end = jnp.sum(jnp.where(block_expert[:, None] == jnp.arange(N_EXPERTS)[None, :],
                                (pad_end // blk)[None, :], 0), axis=1)
    next_blk = jnp.minimum(seg_end, n_blocks - 1)
    next_expert = jnp.sum(jnp.where(next_blk[:, None] == blk_id[None, :], block_expert[None, :], 0),
                          axis=1)
    next_expert = jnp.where(seg_end < n_used[0], next_expert, -1).astype(jnp.int32)
    slot = ((jnp.cumsum(first.astype(jnp.int32)) - 1) % 2).astype(jnp.int32)
    codes = ri[:, 0:2, :].reshape(-1)
    start_blk = jnp.dot(jax.nn.one_hot(codes % N_EXPERTS, N_EXPERTS, dtype=BF16),
                        (pad_start // blk).astype(BF16), preferred_element_type=F32)
    dest = (codes // N_EXPERTS + start_blk.astype(jnp.int32) * blk).astype(jnp.int32)

    x_buf = _dispatch(dest, pad_end.astype(jnp.int32), n_used, xn, n_blocks * blk,
                      tm=min(T, DISPATCH_TILE))
    y_buf = _experts(block_expert, n_used, n_valid, first.astype(jnp.int32), next_expert, slot,
                     x_buf, w1, w3, w2)

    return _final(dest, h_mid, rf, p2, w_ple_up.astype(BF16), row(ple_norm), row(ple_gate_norm),
                  w_ple_gate.astype(BF16), row(out_norm), y_buf)


def kernel(x, p, positions, attn_norm, w_in, conv_w, conv_b, b_igate, b_fgate, ret_gn, ml_gn,
           w_out, moe_norm, w_group, b_group, w_router, b_router, w1, w3, w2, w_ple_up, ple_norm,
           ple_gate_norm, w_ple_gate, final_norm):
    B, S, D = x.shape
    depth = p.shape[0]
    assert depth == 1, "the final RMSNorm is fused into the layer's last kernel"
    out = _layer(x.reshape(B * S, D), p[0].reshape(B * S, -1), positions, B, S,
                 attn_norm[0], w_in[0], conv_w[0], conv_b[0], b_igate[0], b_fgate[0], ret_gn[0],
                 ml_gn[0], w_out[0], moe_norm[0], w_group[0], b_group[0], w_router[0],
                 b_router[0], w1[0], w3[0], w2[0], w_ple_up[0], ple_norm[0], ple_gate_norm[0],
                 w_ple_gate[0], final_norm)
    return out.reshape(B, S, D)
```

```python
import functools

import jax
import jax.numpy as jnp
from jax import lax
from jax.experimental import pallas as pl
from jax.experimental.pallas import tpu as pltpu

F32 = jnp.float32
BF16 = jnp.bfloat16

RET_HEADS = 8
ML_HEADS = 4
CHUNK = 128
CONV_W = 4
ROPE_BASE = 10000.0
N_GROUPS = 4
EXPERTS_PER_GROUP = 8
N_EXPERTS = N_GROUPS * EXPERTS_PER_GROUP
EPS = 1e-6

LANES = 128
SUBLANES = 8
MXU_COLS = 256
MIXER_SEQS = 2
FINAL_TILES = 4
FINAL_AHEAD = 3
VMEM_LIMIT = 56 * 1024 * 1024
EXPERT_BLOCK = 512
GROUP_ROW0 = N_EXPERTS
ROUTER_ROWS = 64
ROUTE_TILE = 1024
ROUTE_ROWS = 8
DISPATCH_TILE = 4 * ROUTE_TILE
GATE_ROWS = 2 * SUBLANES
ROW_TILE = (SUBLANES, LANES)


def _rms(x, g):
    return x * lax.rsqrt(jnp.mean(x * x, axis=-1, keepdims=True) + EPS) * g


def _dot(a, b):
    return jnp.dot(a, b, preferred_element_type=F32)


def _dot_nt(a, b):
    return lax.dot_general(a, b, (((1,), (1,)), ((), ())), preferred_element_type=F32)


def _split3(x):
    hi = x.astype(BF16)
    r1 = x - hi.astype(F32)
    mid = r1.astype(BF16)
    lo = (r1 - mid.astype(F32)).astype(BF16)
    return hi, mid, lo


def _rope_kernel(pos_ref, freq_ref, sign_ref, cos_ref, sin_ref, *, half):
    ang = pos_ref[...].astype(F32) * freq_ref[...]
    cos_c = jnp.cos(ang)
    sin_c = jnp.sin(ang)
    group = lax.broadcasted_iota(jnp.int32, ang.shape, 1) // half
    per_row = LANES // half
    for s in range(per_row):
        def spread(t):
            out = t
            for g in range(per_row):
                if g != s:
                    out = jnp.where(group == g, pltpu.roll(t, (half * (g - s)) % LANES, 1), out)
            return out
        cos_ref[s] = spread(cos_c)
        sin_ref[s] = spread(sin_c) * sign_ref[...]


def _rope_tables(positions, dh):
    half = dh // 2
    per_row = LANES // half
    T = positions.size
    rows = T // per_row
    tr = min(rows, 1024)
    freqs = ROPE_BASE ** (-jnp.arange(half, dtype=F32) / half)
    pos_c = jnp.repeat(positions.reshape(per_row, rows).T, half, axis=1)
    sign = jnp.tile(jnp.concatenate([-jnp.ones((half,), F32), jnp.ones((half,), F32)]),
                    LANES // dh)[None, :]
    const = pl.BlockSpec((1, LANES), lambda i: (0, 0))
    out = pl.BlockSpec((per_row, tr, LANES), lambda i: (0, i, 0))
    cos_t, sin_t = pl.pallas_call(
        functools.partial(_rope_kernel, half=half),
        grid=(rows // tr,),
        in_specs=[pl.BlockSpec((tr, LANES), lambda i: (i, 0)), const, const],
        out_specs=[out, out],
        out_shape=[jax.ShapeDtypeStruct((per_row, rows, LANES), F32)] * 2,
        name="rope",
    )(pos_c, jnp.tile(freqs, per_row)[None, :], sign)
    return cos_t.reshape(T, LANES), sin_t.reshape(T, LANES)


def _inproj_kernel(x_ref, g_ref, wm_ref, wgt_ref, gb_ref, cos_ref, sin_ref, ws_ref, convw_ref,
                   convb_ref, rq_ref, rk_ref, rkwt_ref, rv_ref, rg_ref, mq_ref, mk_ref, mvt_ref,
                   mo_ref, gt_ref, carry, *, ret_w, ml_w):
    tm = x_ref.shape[0]
    ret_dh = LANES // 2

    @pl.when(pl.program_id(1) == 0)
    def _():
        carry[0:SUBLANES] = jnp.zeros((SUBLANES, carry.shape[1]), F32)

    xn = _rms(x_ref[...], g_ref[...]).astype(BF16)
    proj = lambda o, w: _dot(xn, wm_ref[:, o:o + w])
    tiles = lambda w: [slice(t, t + LANES) for t in range(0, w, LANES)]

    lane = lax.broadcasted_iota(jnp.int32, (tm, LANES), 1)
    first_half = (lane % ret_dh) < (ret_dh // 2)
    cos_t = cos_ref[...]
    sin_t = sin_ref[...]

    def rot(t):
        swapped = jnp.where(first_half, pltpu.roll(t, LANES - ret_dh // 2, 1),
                            pltpu.roll(t, ret_dh // 2, 1))
        return t * cos_t + swapped * sin_t

    for c0 in range(0, ret_w, MXU_COLS):
        rq = proj(c0, MXU_COLS)
        for ps in tiles(MXU_COLS):
            rq_ref[:, c0 + ps.start:c0 + ps.stop] = rot(rq[:, ps]).astype(BF16)
    for c0 in range(0, ret_w, MXU_COLS):
        rk = proj(ret_w + c0, MXU_COLS)
        for ps in tiles(MXU_COLS):
            cs = slice(c0 + ps.start, c0 + ps.stop)
            k = rot(rk[:, ps]) * (ret_dh ** -0.5)
            rk_ref[:, cs] = k.astype(BF16)
            rkwt_ref[cs, :] = (k * ws_ref[:, cs]).T.astype(BF16)
    for c0 in range(0, ret_w, MXU_COLS):
        cs = slice(c0, c0 + MXU_COLS)
        rv_ref[:, cs] = proj(2 * ret_w + c0, MXU_COLS).astype(BF16)
    for c0 in range(0, ret_w, MXU_COLS):
        g = proj(3 * ret_w + c0, MXU_COLS)
        rg_ref[:, c0:c0 + MXU_COLS] = g * jax.nn.sigmoid(g)

    o_mq = 4 * ret_w
    for c0 in range(0, 2 * ml_w, MXU_COLS):
        cs = slice(c0, c0 + MXU_COLS)
        xq = proj(o_mq + c0, MXU_COLS)
        carry[SUBLANES:SUBLANES + tm, cs] = xq
        acc = xq * convw_ref[CONV_W - 1:CONV_W, cs] + convb_ref[:, cs]
        for s in range(1, CONV_W):
            shifted = carry[SUBLANES - s:SUBLANES - s + tm, cs]
            acc = acc + shifted * convw_ref[CONV_W - 1 - s:CONV_W - s, cs]
        carry[0:SUBLANES, cs] = xq[tm - SUBLANES:tm]
        act = acc * jax.nn.sigmoid(acc)
        if c0 < ml_w:
            mq_ref[:, cs] = act.astype(BF16)
        else:
            mk_ref[:, c0 - ml_w:c0 - ml_w + MXU_COLS] = (act * (LANES ** -0.5)).astype(BF16)
    for c0 in range(0, ml_w, MXU_COLS):
        mv = proj(o_mq + 2 * ml_w + c0, MXU_COLS)
        for hs in tiles(MXU_COLS):
            mvt_ref[c0 + hs.start:c0 + hs.stop, :] = mv[:, hs].T
    for c0 in range(0, ml_w, MXU_COLS):
        mo_ref[:, c0:c0 + MXU_COLS] = jax.nn.sigmoid(proj(o_mq + 3 * ml_w + c0, MXU_COLS))
    gt_ref[...] = _dot_nt(wgt_ref[...], xn) + gb_ref[...]


def _inproj(x3, g, w_main, w_gate_t, gate_bias, cos_t, sin_t, ws, conv_w, conv_b, ret_w, ml_w,
            tm=256):
    B, S, D = x3.shape
    n_main = 4 * (ret_w + ml_w)
    const = lambda b, i: (0, 0)
    tok = lambda w: pl.BlockSpec((None, tm, w), lambda b, i: (b, i, 0))
    tok_t = lambda r: pl.BlockSpec((None, r, tm), lambda b, i: (b, 0, i))
    act = lambda w, dt: jax.ShapeDtypeStruct((B, S, w), dt)
    act_t = lambda r, dt: jax.ShapeDtypeStruct((B, r, S), dt)
    return pl.pallas_call(
        functools.partial(_inproj_kernel, ret_w=ret_w, ml_w=ml_w),
        grid=(B, S // tm),
        in_specs=[
            tok(D),
            pl.BlockSpec((1, D), const),
            pl.BlockSpec((D, n_main), const),
            pl.BlockSpec((GATE_ROWS, D), const),
            pl.BlockSpec((GATE_ROWS, 1), const),
            tok(LANES),
            tok(LANES),
            pl.BlockSpec((tm, ret_w), const),
            pl.BlockSpec((CONV_W, 2 * ml_w), const),
            pl.BlockSpec((1, 2 * ml_w), const),
        ],
        out_specs=[tok(ret_w), tok(ret_w), tok_t(ret_w), tok(ret_w), tok(ret_w),
                   tok(ml_w), tok(ml_w), tok_t(ml_w), tok(ml_w), tok_t(GATE_ROWS)],
        out_shape=[
            act(ret_w, BF16),
            act(ret_w, BF16),
            act_t(ret_w, BF16),
            act(ret_w, BF16),
            act(ret_w, F32),
            act(ml_w, BF16),
            act(ml_w, BF16),
            act_t(ml_w, F32),
            act(ml_w, F32),
            act_t(GATE_ROWS, F32),
        ],
        scratch_shapes=[pltpu.VMEM((SUBLANES + tm, 2 * ml_w), F32)],
        compiler_params=pltpu.CompilerParams(
            dimension_semantics=("arbitrary", "arbitrary"), vmem_limit_bytes=VMEM_LIMIT),
        name="inproj",
    )(x3, g, w_main, w_gate_t, gate_bias, cos_t, sin_t, ws, conv_w, conv_b)


def _mixer_kernel(rq_ref, rk_ref, rkwt_ref, rv_ref, rg_ref, mq_ref, mk_ref, mvt_ref, mo_ref,
                  gt_ref, decay_ref, wq_ref, cd_ref, retgn_ref, mlgn_ref,
                  y_ref, r_state, s_state, n_state, m_state, *, ret_w, ml_w):
    BB, C = rq_ref.shape[0], rq_ref.shape[1]
    n_pairs = ret_w // LANES
    ml_heads = ml_w // LANES
    ret_dh = LANES // 2

    @pl.when(pl.program_id(1) == 0)
    def _():
        r_state[...] = jnp.zeros_like(r_state)
        s_state[...] = jnp.zeros_like(s_state)
        n_state[...] = jnp.zeros_like(n_state)
        m_state[...] = jnp.zeros_like(m_state)

    lane = lax.broadcasted_iota(jnp.int32, (C, LANES), 1)
    row = lax.broadcasted_iota(jnp.int32, (C, LANES), 0)
    assert C == LANES
    lo = lane < ret_dh
    blockdiag = (row < ret_dh) == lo
    lo_b = jnp.where(lo, 1.0, 0.0).astype(BF16)
    hi_b = jnp.where(lo, 0.0, 1.0).astype(BF16)
    seqs = range(BB)
    pair_units = [(s, p, slice(p * LANES, (p + 1) * LANES)) for s in seqs for p in range(n_pairs)]
    head_units = [(s, h, slice(h * LANES, (h + 1) * LANES)) for s in seqs for h in range(ml_heads)]


    row8 = lax.broadcasted_iota(jnp.int32, (SUBLANES, C), 0)
    lane8 = lax.broadcasted_iota(jnp.int32, (SUBLANES, C), 1)
    live = row8 < ml_heads
    triu = (row <= lane).astype(BF16)
    before = row <= lane
    mx, w_inter, e_negm, w_state, w_state_b, dec, beta_t = [], [], [], [], [], [], []
    for s in seqs:
        ig = jnp.where(live, gt_ref[s, 0:SUBLANES, :], 0.0)
        f_pre = jnp.where(live, gt_ref[s, SUBLANES:2 * SUBLANES, :], 30.0)
        l_hi, l_mid, l_lo = _split3(jax.nn.log_sigmoid(f_pre))
        b = _dot(l_hi, triu) + _dot(l_mid, triu) + _dot(l_lo, triu)
        beta = ig - b
        cm = beta
        shift = 1
        while shift < C:
            cm = jnp.maximum(cm, jnp.where(lane8 >= shift, pltpu.roll(cm, shift, 1), -jnp.inf))
            shift *= 2
        m_prev = m_state[s]
        mx.append(jnp.maximum(cm, m_prev))
        mx_last = jnp.broadcast_to(mx[s][:, C - 1:C], (SUBLANES, C))
        w_inter.append(jnp.exp(m_prev - mx[s]))
        e_negm.append(jnp.exp(-(b + mx[s])))
        w_state.append(jnp.exp(beta - mx_last))
        w_state_b.append(w_state[s].astype(BF16))
        dec.append(jnp.exp(m_prev - mx_last))
        m_state[s] = jnp.where(live, jnp.broadcast_to(b[:, C - 1:C], (SUBLANES, C)) + mx_last, 0.0)
        beta_t.append(jnp.concatenate([beta, jnp.zeros((LANES - SUBLANES, C), F32)], axis=0).T)

    hrow = lambda t, h: t[h:h + 1, :]

    rq = [rq_ref[s, :, ps] for s, p, ps in pair_units]
    rk = [rk_ref[s, :, ps] for s, p, ps in pair_units]
    rv = [rv_ref[s, :, ps] for s, p, ps in pair_units]
    r_prev = [r_state[s, p] for s, p, ps in pair_units]
    pu = range(len(pair_units))
    s_ab = [_dot_nt(jnp.concatenate([rq[u] * lo_b, rq[u] * hi_b], axis=0), rk[u]) for u in pu]
    r_read = [_dot(rq[u], r_prev[u].astype(BF16)) for u in pu]
    r_new = [_dot(rkwt_ref[s, ps, :], rv[u]) for u, (s, p, ps) in enumerate(pair_units)]
    hu = range(len(head_units))
    mq = [mq_ref[s, :, hs] for s, h, hs in head_units]
    mk = [mk_ref[s, :, hs] for s, h, hs in head_units]
    mv_t = [mvt_ref[s, hs, :] for s, h, hs in head_units]
    st_prev = [s_state[s, h] for s, h, hs in head_units]
    n_prev = [n_state[s, h] for s, h, hs in head_units]
    a_t = [_dot_nt(mk[u], mq[u]) for u in hu]
    s_read = [_dot_nt(st_prev[u].astype(BF16), mq[u]) for u in hu]
    qn = [_dot_nt(n_prev[u].astype(BF16), mq[u])[0:1, :] for u in hu]
    s_new = [_dot((mv_t[u] * hrow(w_state[s], h)).astype(BF16), mk[u])
             for u, (s, h, hs) in enumerate(head_units)]
    n_new = [_dot(w_state_b[s], mk[u])[h:h + 1, :] for u, (s, h, hs) in enumerate(head_units)]
    for u, (s, p, ps) in enumerate(pair_units):
        r_state[s, p] = cd_ref[p] * r_prev[u] + jnp.where(blockdiag, r_new[u], 0.0)
    for u, (s, h, hs) in enumerate(head_units):
        s_state[s, h] = hrow(dec[s], h) * st_prev[u] + s_new[u]
        n_state[s, h] = hrow(dec[s], h) * n_prev[u] + n_new[u]

    s_ab = [jnp.concatenate([(s_ab[u][0:C] * decay_ref[2 * p]).astype(BF16),
                             (s_ab[u][C:2 * C] * decay_ref[2 * p + 1]).astype(BF16)], axis=1)
            for u, (s, p, ps) in enumerate(pair_units)]
    p_t = [jnp.exp(jnp.where(before, beta_t[s][:, h:h + 1] - hrow(mx[s], h), -jnp.inf)) * a_t[u]
           for u, (s, h, hs) in enumerate(head_units)]

    o = [_dot(s_ab[u], jnp.concatenate([rv[u] * lo_b, rv[u] * hi_b], axis=0)) + r_read[u] * wq_ref[p]
         for u, (s, p, ps) in enumerate(pair_units)]
    num_t = [_dot(mv_t[u].astype(BF16), p_t[u].astype(BF16)) + hrow(w_inter[s], h) * s_read[u]
             for u, (s, h, hs) in enumerate(head_units)]

    for u, (s, p, ps) in enumerate(pair_units):
        sq = o[u] * o[u]
        ms_a = jnp.sum(jnp.where(lo, sq, 0.0), axis=-1, keepdims=True)
        ms_b = jnp.sum(jnp.where(lo, 0.0, sq), axis=-1, keepdims=True)
        ms = jnp.where(lo, ms_a, ms_b) * (1.0 / ret_dh)
        r = o[u] * lax.rsqrt(ms + EPS) * retgn_ref[:, ps]
        y_ref[s, :, ps] = (rg_ref[s, :, ps] * r).astype(y_ref.dtype)
    for u, (s, h, hs) in enumerate(head_units):
        den = jnp.sum(p_t[u], axis=0, keepdims=True) + hrow(w_inter[s], h) * qn[u]
        hh = (num_t[u] * (1.0 / jnp.maximum(jnp.abs(den), hrow(e_negm[s], h)))).T
        hm = _rms(mo_ref[s, :, hs] * hh, mlgn_ref[:, hs])
        y_ref[s, :, ret_w + h * LANES:ret_w + (h + 1) * LANES] = hm.astype(y_ref.dtype)


def _ret_tables(C):
    H = RET_HEADS
    dh = LANES // 2
    log_gamma = jnp.log1p(-(2.0 ** (-5.0 - jnp.arange(H, dtype=F32))))
    idx = jnp.arange(C, dtype=F32)
    rel = idx[:, None] - idx[None, :]
    causal = rel >= 0
    decay = jnp.where(causal, jnp.exp(log_gamma[:, None, None] * jnp.where(causal, rel, 0.0)), 0.0)
    w_state = jnp.exp(log_gamma[:, None] * (C - 1 - idx))
    w_query = jnp.exp(log_gamma[:, None] * (idx + 1.0))
    chunk_decay = jnp.exp(log_gamma * C)
    pair = lambda t: jnp.repeat(t.reshape(H // 2, 2, C).transpose(0, 2, 1), dh, axis=2)
    cd = jnp.repeat(chunk_decay.reshape(H // 2, 2), dh, axis=1)
    cd = jnp.broadcast_to(cd[:, :, None], (H // 2, LANES, LANES))
    ws = pair(w_state).transpose(1, 0, 2).reshape(C, (H // 2) * LANES)
    return decay, ws, pair(w_query), cd


def _mixer(ops, decay, wq, cd, ret_gn, ml_gn, B, S, ret_w, ml_w):
    C = CHUNK
    N = S // C
    n_pairs = ret_w // LANES
    ml_heads = ml_w // LANES
    c2 = lambda b, n: (0, 0)
    c3 = lambda b, n: (0, 0, 0)
    BB = MIXER_SEQS if B % MIXER_SEQS == 0 else 1
    tok = lambda w: pl.BlockSpec((BB, C, w), lambda b, n: (b, n, 0))
    tok_t = lambda r: pl.BlockSpec((BB, r, C), lambda b, n: (b, 0, n))
    y = pl.pallas_call(
        functools.partial(_mixer_kernel, ret_w=ret_w, ml_w=ml_w),
        grid=(B // BB, N),
        in_specs=[
            tok(ret_w), tok(ret_w), tok_t(ret_w), tok(ret_w), tok(ret_w),
            tok(ml_w), tok(ml_w), tok_t(ml_w), tok(ml_w), tok_t(GATE_ROWS),
            pl.BlockSpec((RET_HEADS, C, C), c3),
            pl.BlockSpec((n_pairs, C, LANES), c3),
            pl.BlockSpec((n_pairs, LANES, LANES), c3),
            pl.BlockSpec((1, ret_w), c2),
            pl.BlockSpec((1, ml_w), c2),
        ],
        out_specs=pl.BlockSpec((BB, C, ret_w + ml_w), lambda b, n: (b, n, 0)),
        out_shape=jax.ShapeDtypeStruct((B, S, ret_w + ml_w), BF16),
        scratch_shapes=[
            pltpu.VMEM((BB, n_pairs, LANES, LANES), F32),
            pltpu.VMEM((BB, ml_heads, LANES, LANES), F32),
            pltpu.VMEM((BB, ml_heads, SUBLANES, LANES), F32),
            pltpu.VMEM((BB, SUBLANES, LANES), F32),
        ],
        compiler_params=pltpu.CompilerParams(
            dimension_semantics=("arbitrary", "arbitrary"), vmem_limit_bytes=VMEM_LIMIT),
        name="mixer",
    )(*ops, decay, wq, cd, ret_gn.reshape(1, -1), ml_gn.reshape(1, -1))
    return y.reshape(B * S, ret_w + ml_w)


def _router_kernel(y_ref, x_ref, wo_ref, g_ref, wr_hi_ref, wr_lo_ref, br_ref,
                   h_ref, xn_ref, ri_ref, rf_ref, cnt_ref, run_cnt):
    tm = y_ref.shape[0]

    @pl.when(pl.program_id(0) == 0)
    def _():
        run_cnt[...] = jnp.zeros_like(run_cnt)

    h = x_ref[...] + _dot(y_ref[...], wo_ref[...])
    h_ref[...] = h
    xn = _rms(h, g_ref[...])
    x_hi = xn.astype(BF16)
    xn_ref[...] = x_hi.reshape(xn_ref.shape)
    x_lo = (xn - x_hi.astype(F32)).astype(BF16)
    logits = (_dot_nt(wr_hi_ref[...], x_hi) + _dot_nt(wr_hi_ref[...], x_lo)
              + _dot_nt(wr_lo_ref[...], x_hi) + br_ref[...])
    big = jnp.int32(LANES)
    neg = -jnp.inf
    gl = logits[GROUP_ROW0:GROUP_ROW0 + 8]
    grow = lax.broadcasted_iota(jnp.int32, gl.shape, 0)
    is_g = grow < N_GROUPS
    gl = jnp.where(is_g, gl, neg)
    gmax = jnp.max(gl, axis=0, keepdims=True)
    gsum = jnp.sum(jnp.where(is_g, jnp.exp(gl - gmax), 0.0), axis=0, keepdims=True)
    p_g = 1.0 / gsum
    g_sel = jnp.min(jnp.where(is_g & (gl == gmax), grow, big), axis=0, keepdims=True)
    el = logits[0:N_EXPERTS]
    erow = lax.broadcasted_iota(jnp.int32, el.shape, 0)
    in_grp = (erow // EXPERTS_PER_GROUP) == g_sel
    el = jnp.where(in_grp, el, neg)
    emax = jnp.max(el, axis=0, keepdims=True)
    eexp = jnp.where(in_grp, jnp.exp(el - emax), 0.0)
    prob = eexp / jnp.sum(eexp, axis=0, keepdims=True)
    pm1 = jnp.where(in_grp, prob, -1.0)
    p1 = jnp.max(pm1, axis=0, keepdims=True)
    i1 = jnp.min(jnp.where(pm1 == p1, erow, big), axis=0, keepdims=True)
    pm2 = jnp.where(erow == i1, -1.0, pm1)
    p2 = jnp.max(pm2, axis=0, keepdims=True)
    i2 = jnp.min(jnp.where(pm2 == p2, erow, big), axis=0, keepdims=True)
    denom = p1 + p2
    g1 = p_g * p1 / denom
    g2 = p_g * p2 / denom

    sel1 = erow == i1
    sel2 = erow == i2
    onehot = (sel1 | sel2).astype(BF16)
    r_i = lax.broadcasted_iota(jnp.int32, (tm, tm), 0)
    c_i = lax.broadcasted_iota(jnp.int32, (tm, tm), 1)
    tri = (r_i < c_i).astype(BF16)
    prefix = _dot(onehot, tri) + run_cnt[:, 0:1]
    rank1 = jnp.sum(jnp.where(sel1, prefix, 0.0), axis=0, keepdims=True).astype(jnp.int32)
    rank2 = jnp.sum(jnp.where(sel2, prefix, 0.0), axis=0, keepdims=True).astype(jnp.int32)
    new_cnt = run_cnt[...] + jnp.sum(onehot.astype(F32), axis=1, keepdims=True)
    run_cnt[...] = new_cnt
    cnt_ref[...] = new_cnt.astype(jnp.int32)

    rrow = lax.broadcasted_iota(jnp.int32, (ROUTE_ROWS, tm), 0)
    ri_ref[...] = jnp.where(rrow == 0, rank1 * N_EXPERTS + i1,
                            jnp.where(rrow == 1, rank2 * N_EXPERTS + i2, 0))
    lrow = lax.broadcasted_iota(jnp.int32, (LANES, tm), 0)
    rf_ref[...] = jnp.where(lrow == 0, g1, jnp.where(lrow == 1, g2, 0.0)).T


def _router(y, x2, w_out, g, wr_hi, wr_lo, br, tm=ROUTE_TILE):
    T, D = x2.shape
    const = lambda i: (0, 0)
    tile = lambda i: (i, 0)
    return pl.pallas_call(
        _router_kernel,
        grid=(T // tm,),
        in_specs=[
            pl.BlockSpec((tm, y.shape[1]), tile),
            pl.BlockSpec((tm, D), tile),
            pl.BlockSpec(w_out.shape, const),
            pl.BlockSpec((1, D), const),
            pl.BlockSpec((ROUTER_ROWS, D), const),
            pl.BlockSpec((ROUTER_ROWS, D), const),
            pl.BlockSpec((ROUTER_ROWS, 1), const),
        ],
        out_specs=[
            pl.BlockSpec((tm, D), tile),
            pl.BlockSpec((tm,) + ROW_TILE, lambda i: (i, 0, 0)),
            pl.BlockSpec((None, ROUTE_ROWS, tm), lambda i: (i, 0, 0)),
            pl.BlockSpec((tm, LANES), tile),
            pl.BlockSpec((N_EXPERTS, LANES), const),
        ],
        out_shape=[
            jax.ShapeDtypeStruct((T, D), F32),
            jax.ShapeDtypeStruct((T,) + ROW_TILE, BF16),
            jax.ShapeDtypeStruct((T // tm, ROUTE_ROWS, tm), jnp.int32),
            jax.ShapeDtypeStruct((T, LANES), F32),
            jax.ShapeDtypeStruct((N_EXPERTS, LANES), jnp.int32),
        ],
        scratch_shapes=[pltpu.VMEM((N_EXPERTS, LANES), F32)],
        compiler_params=pltpu.CompilerParams(
            dimension_semantics=("arbitrary",), vmem_limit_bytes=VMEM_LIMIT),
        name="router",
    )(y, x2, w_out, g, wr_hi, wr_lo, br)


def _slot_base(tok0):
    return (tok0 // ROUTE_TILE) * (2 * ROUTE_TILE) + tok0 % ROUTE_TILE


def _route_spans(tm):
    assert tm % ROUTE_TILE == 0 or ROUTE_TILE % tm == 0
    span = min(tm, ROUTE_TILE)
    return [(t0, span) for t0 in range(0, tm, span)]


def _for_each_pad_block(pend_ref, nu_ref, n_blocks, fn):
    blk = EXPERT_BLOCK
    for e in range(N_EXPERTS):
        prev_end = 0 if e == 0 else pend_ref[e - 1]

        @pl.when(pend_ref[e] > prev_end)
        def _():
            fn(pl.multiple_of(pend_ref[e] - blk, blk))

    def tail(j, c):
        fn(pl.multiple_of(j * blk, blk))
        return c

    lax.fori_loop(nu_ref[0], n_blocks, tail, 0)


def _dispatch_kernel(dest_ref, pend_ref, nu_ref, xn_ref, buf_ref, zeros, sem, zsem):
    tm = xn_ref.shape[0]
    blk = EXPERT_BLOCK
    i = pl.program_id(0)

    @pl.when(i == 0)
    def _():
        zeros[...] = jnp.zeros_like(zeros)
        zcopy = lambda row: pltpu.make_async_copy(zeros, buf_ref.at[pl.ds(row, blk)], zsem)
        n_blocks = buf_ref.shape[0] // blk
        _for_each_pad_block(pend_ref, nu_ref, n_blocks, lambda row: zcopy(row).start())
        _for_each_pad_block(pend_ref, nu_ref, n_blocks, lambda row: zcopy(row).wait())

    for t0, span in _route_spans(tm):
        slot0 = _slot_base(i * tm + t0)

        def issue(g, c, t0=t0, slot0=slot0):
            for j in range(SUBLANES):
                for k in range(2):
                    t = g * SUBLANES + j
                    d = dest_ref[slot0 + k * ROUTE_TILE + t]
                    pltpu.make_async_copy(xn_ref.at[t0 + t], buf_ref.at[d], sem).start(priority=k)
            return c

        lax.fori_loop(0, span // SUBLANES, issue, 0)
    for k in range(2):
        pltpu.make_async_copy(buf_ref.at[pl.ds(tm, tm)], buf_ref.at[pl.ds(0, tm)], sem).wait()


def _dispatch(dest, pad_end, n_used, xn, n_rows, tm=ROUTE_TILE):
    T = xn.shape[0]
    return pl.pallas_call(
        _dispatch_kernel,
        grid_spec=pltpu.PrefetchScalarGridSpec(
            num_scalar_prefetch=3,
            grid=(T // tm,),
            in_specs=[pl.BlockSpec((tm,) + ROW_TILE, lambda i, *_: (i, 0, 0))],
            out_specs=pl.BlockSpec(memory_space=pl.ANY),
            scratch_shapes=[
                pltpu.VMEM((EXPERT_BLOCK,) + ROW_TILE, xn.dtype),
                pltpu.SemaphoreType.DMA,
                pltpu.SemaphoreType.DMA,
            ],
        ),
        out_shape=jax.ShapeDtypeStruct((n_rows,) + ROW_TILE, xn.dtype),
        compiler_params=pltpu.CompilerParams(
            dimension_semantics=("arbitrary",), vmem_limit_bytes=VMEM_LIMIT),
        name="dispatch",
    )(dest, pad_end, n_used, xn)


def _expert_kernel(be_ref, nu_ref, nv_ref, first_ref, next_ref, slot_ref, x_ref, w1_hbm, w3_hbm,
                   w2_hbm, y_ref, s1, s3, s2, w1b, w3b, w2b, sems):
    j = pl.program_id(0)
    used = j < nu_ref[0]
    blk = x_ref.shape[0]
    half = blk // 2

    def weight_copies(e, slot):
        return [pltpu.make_async_copy(w.at[e], s.at[slot], sems.at[slot])
                for w, s in ((w1_hbm, s1), (w3_hbm, s3), (w2_hbm, s2))]

    @pl.when(j == 0)
    def _():
        for c in weight_copies(be_ref[0], 0):
            c.start()

    @pl.when(used & (first_ref[j] == 1))
    def _():
        slot = slot_ref[j]
        for c in weight_copies(be_ref[j], slot):
            c.wait()

        @pl.when(next_ref[j] >= 0)
        def _():
            for c in weight_copies(next_ref[j], 1 - slot):
                c.start()

        w1b[...] = s1[slot].astype(BF16)
        w3b[...] = s3[slot].astype(BF16)
        w2b[...] = s2[slot].astype(BF16)

    def swiglu(rows):
        n = rows.stop - rows.start
        x = x_ref[rows].reshape(n, -1)
        a = _dot(x, w1b[...])
        hmid = a * jax.nn.sigmoid(a) * _dot(x, w3b[...])
        y_ref[rows] = _dot(hmid.astype(BF16), w2b[...]).reshape((n,) + y_ref.shape[1:])

    @pl.when(used & (nv_ref[j] > half))
    def _():
        swiglu(slice(0, blk))

    @pl.when(used & (nv_ref[j] <= half))
    def _():
        swiglu(slice(0, half))
        y_ref[half:blk] = jnp.zeros((blk - half,) + y_ref.shape[1:], y_ref.dtype)

    @pl.when(jnp.logical_not(used))
    def _():
        y_ref[...] = jnp.zeros_like(y_ref)


def _experts(block_expert, n_used, n_valid, first, next_expert, slot, x_buf, w1, w3, w2):
    P = x_buf.shape[0]
    D = w1.shape[1]
    assert x_buf.shape[1:] == ROW_TILE and D == SUBLANES * LANES
    blk = EXPERT_BLOCK
    d_exp = w2.shape[1]
    hbm = pl.BlockSpec(memory_space=pl.ANY)
    return pl.pallas_call(
        _expert_kernel,
        grid_spec=pltpu.PrefetchScalarGridSpec(
            num_scalar_prefetch=6,
            grid=(P // blk,),
            in_specs=[
                pl.BlockSpec((blk,) + ROW_TILE,
                             lambda j, be, nu, *_: (jnp.minimum(j, nu[0] - 1), 0, 0)),
                hbm, hbm, hbm,
            ],
            out_specs=pl.BlockSpec((blk,) + ROW_TILE, lambda j, *_: (j, 0, 0)),
            scratch_shapes=[
                pltpu.VMEM((2, D, d_exp), F32),
                pltpu.VMEM((2, D, d_exp), F32),
                pltpu.VMEM((2, d_exp, D), F32),
                pltpu.VMEM((D, d_exp), BF16),
                pltpu.VMEM((D, d_exp), BF16),
                pltpu.VMEM((d_exp, D), BF16),
                pltpu.SemaphoreType.DMA((2,)),
            ],
        ),
        out_shape=jax.ShapeDtypeStruct((P,) + ROW_TILE, F32),
        compiler_params=pltpu.CompilerParams(
            dimension_semantics=("arbitrary",), vmem_limit_bytes=VMEM_LIMIT),
        name="experts",
    )(block_expert, n_used, n_valid, first, next_expert, slot, x_buf, w1, w3, w2)


def _final_kernel(dest_ref, h_ref, rf_ref, p_ref, wup_ref, gple_ref, ggate_ref, wgate_ref,
                  gfin_ref, ybuf_ref, out_ref, *scratch, tm):
    row_bufs, sems = scratch[:-1], scratch[-1]
    s = pl.program_id(0)
    D = h_ref.shape[1]
    assert ROUTE_TILE % tm == 0

    def gather(tile, bufs, sem, straight=False):
        slot0 = _slot_base(tile * tm)

        def issue(g, c):
            for j in range(SUBLANES):
                for k in range(2):
                    t = g * SUBLANES + j
                    d = dest_ref[slot0 + k * ROUTE_TILE + t]
                    pltpu.make_async_copy(ybuf_ref.at[d], bufs[k].at[t], sem).start(priority=k)
            return c

        if straight:
            for g in range(tm // SUBLANES):
                issue(g, 0)
        else:
            lax.fori_loop(0, tm // SUBLANES, issue, 0)

    def wait(sem):
        for k in range(2):
            pltpu.make_async_copy(ybuf_ref.at[pl.ds(0, tm)], ybuf_ref.at[pl.ds(tm, tm)], sem).wait()

    def compute(rows, bufs):
        e = _rms(_dot(p_ref[rows, :].astype(BF16), wup_ref[...]), gple_ref[...])
        rf = rf_ref[rows, :]
        h = (h_ref[rows, :] + rf[:, 0:1] * bufs[0][...].reshape(tm, D)
             + rf[:, 1:2] * bufs[1][...].reshape(tm, D))
        gate = jax.nn.sigmoid(_dot(_rms(h, ggate_ref[...]).astype(BF16), wgate_ref[...]))
        h = h + gate * e
        out_ref[rows, :] = _rms(h, gfin_ref[...])

    n_tiles = pl.num_programs(0) * FINAL_TILES
    bufs = [(row_bufs[2 * i], row_bufs[2 * i + 1]) for i in range(FINAL_TILES)]

    @pl.when(s == 0)
    def _():
        for i in range(FINAL_AHEAD):
            gather(i, bufs[i], sems.at[i])

    for i in range(FINAL_TILES):
        wait(sems.at[i])
        nxt = (i + FINAL_AHEAD) % FINAL_TILES
        gather(jnp.minimum(s * FINAL_TILES + i + FINAL_AHEAD, n_tiles - 1), bufs[nxt],
               sems.at[nxt], straight=True)
        compute(slice(i * tm, (i + 1) * tm), bufs[i])

    @pl.when(s == pl.num_programs(0) - 1)
    def _():
        for i in range(FINAL_AHEAD):
            wait(sems.at[i])


def _final(dest, h, rf, p2, w_up, g_ple, g_gate, w_gate, g_fin, y_buf, tm=256):
    T, D = h.shape
    const = lambda i, *_: (0, 0)
    tile = lambda i, *_: (i, 0)
    assert y_buf.shape[1:] == ROW_TILE and D == SUBLANES * LANES
    row_buf = pltpu.VMEM((tm,) + ROW_TILE, F32)
    return pl.pallas_call(
        functools.partial(_final_kernel, tm=tm),
        grid_spec=pltpu.PrefetchScalarGridSpec(
            num_scalar_prefetch=1,
            grid=(T // (FINAL_TILES * tm),),
            in_specs=[
                pl.BlockSpec((FINAL_TILES * tm, D), tile),
                pl.BlockSpec((FINAL_TILES * tm, LANES), tile),
                pl.BlockSpec((FINAL_TILES * tm, p2.shape[1]), tile),
                pl.BlockSpec(w_up.shape, const),
                pl.BlockSpec((1, D), const),
                pl.BlockSpec((1, D), const),
                pl.BlockSpec(w_gate.shape, const),
                pl.BlockSpec((1, D), const),
                pl.BlockSpec(memory_space=pl.ANY),
            ],
            out_specs=pl.BlockSpec((FINAL_TILES * tm, D), tile),
            scratch_shapes=[row_buf] * (2 * FINAL_TILES) + [pltpu.SemaphoreType.DMA((FINAL_TILES,))],
        ),
        out_shape=jax.ShapeDtypeStruct((T, D), F32),
        compiler_params=pltpu.CompilerParams(
            dimension_semantics=("arbitrary",), vmem_limit_bytes=VMEM_LIMIT),
        name="final",
    )(dest, h, rf, p2, w_up, g_ple, g_gate, w_gate, g_fin, y_buf)


def _layer(h2, p2, positions, B, S, attn_norm, w_in, conv_w, conv_b, b_igate, b_fgate, ret_gn,
           ml_gn, w_out, moe_norm, w_group, b_group, w_router, b_router, w1, w3, w2, w_ple_up,
           ple_norm, ple_gate_norm, w_ple_gate, out_norm):
    T, D = h2.shape
    ret_w = ret_gn.shape[0]
    ml_w = ml_gn.shape[0]
    n_main = 4 * ret_w + 4 * ml_w
    row = lambda v: v.reshape(1, -1).astype(F32)
    pad_lanes = lambda a: jnp.pad(a, ((0, 0), (0, LANES - a.shape[1])))

    ml_heads = ml_w // LANES
    w_main = w_in.astype(BF16)
    gate_rows = lambda a: jnp.pad(a, ((0, SUBLANES - ml_heads), (0, 0)))
    w_i, w_f = w_in[:, n_main:n_main + ml_heads].T, w_in[:, n_main + ml_heads:].T
    w_gate_t = jnp.concatenate([gate_rows(w_i), gate_rows(w_f)]).astype(BF16)
    gate_bias = jnp.concatenate([gate_rows(b_igate[:, None]), gate_rows(b_fgate[:, None])])
    cos_t, sin_t = _rope_tables(positions, LANES // 2)
    decay, ws, wq, cd = _ret_tables(CHUNK)
    tm_in = 2 * CHUNK
    ops = _inproj(h2.reshape(B, S, D), row(attn_norm), w_main, w_gate_t, gate_bias.astype(F32),
                  cos_t.reshape(B, S, LANES), sin_t.reshape(B, S, LANES),
                  jnp.tile(ws, (tm_in // CHUNK, 1)), conv_w, conv_b.reshape(1, -1), ret_w, ml_w,
                  tm=tm_in)

    y = _mixer(ops, decay, wq, cd, ret_gn, ml_gn, B, S, ret_w, ml_w)

    pad_rows = lambda a: jnp.pad(a, ((0, ROUTER_ROWS - a.shape[0]), (0, 0)))
    wr = pad_rows(jnp.concatenate([w_router, w_group], axis=1).T)
    wr_hi = wr.astype(BF16)
    wr_lo = (wr - wr_hi.astype(F32)).astype(BF16)
    br = pad_rows(jnp.concatenate([b_router, b_group])[:, None].astype(F32))
    h_mid, xn, ri, rf, counts = _router(y, h2, w_out.astype(BF16), row(moe_norm), wr_hi, wr_lo, br)

    blk = EXPERT_BLOCK
    counts = counts[:, 0]
    padded = (counts + blk - 1) // blk * blk
    pad_end = jnp.cumsum(padded)
    pad_start = pad_end - padded
    n_blocks = (2 * T) // blk + N_EXPERTS
    assert n_blocks <= 256
    block_start = jnp.arange(n_blocks, dtype=jnp.int32) * blk
    block_expert = jnp.minimum(jnp.sum(pad_end[None, :] <= block_start[:, None], axis=1),
                               N_EXPERTS - 1).astype(jnp.int32)
    n_used = (pad_end[-1:] // blk).astype(jnp.int32)
    real_end = jnp.sum(jnp.where(block_expert[:, None] == jnp.arange(N_EXPERTS)[None, :],
                                 (pad_start + counts)[None, :], 0), axis=1)
    n_valid = jnp.clip(real_end - block_start, 0, blk).astype(jnp.int32)
    blk_id = jnp.arange(n_blocks, dtype=jnp.int32)
    is_used = blk_id < n_used[0]
    first = is_used & jnp.concatenate([jnp.ones((1,), bool), block_expert[1:] != block_expert[:-1]])
    seg_end = jnp.sum(jnp.where(block_expert[:, None] == jnp.arange(N_EXPERTS)[None, :],
                                (pad_end // blk)[None, :], 0), axis=1)
    next_blk = jnp.minimum(seg_end, n_blocks - 1)
    next_expert = jnp.sum(jnp.where(next_blk[:, None] == blk_id[None, :], block_expert[None, :], 0),
                          axis=1)
    next_expert = jnp.where(seg_end < n_used[0], next_expert, -1).astype(jnp.int32)
    slot = ((jnp.cumsum(first.astype(jnp.int32)) - 1) % 2).astype(jnp.int32)
    codes = ri[:, 0:2, :].reshape(-1)
    start_blk = jnp.dot(jax.nn.one_hot(codes % N_EXPERTS, N_EXPERTS, dtype=BF16),
                        (pad_start // blk).astype(BF16), preferred_element_type=F32)
    dest = (codes // N_EXPERTS + start_blk.astype(jnp.int32) * blk).astype(jnp.int32)

    x_buf = _dispatch(dest, pad_end.astype(jnp.int32), n_used, xn, n_blocks * blk,
                      tm=min(T, DISPATCH_TILE))
    y_buf = _experts(block_expert, n_used, n_valid, first.astype(jnp.int32), next_expert, slot,
                     x_buf, w1, w3, w2)

    return _final(dest, h_mid, rf, p2, w_ple_up.astype(BF16), row(ple_norm), row(ple_gate_norm),
                  w_ple_gate.astype(BF16), row(out_norm), y_buf)


def kernel(x, p, positions, attn_norm, w_in, conv_w, conv_b, b_igate, b_fgate, ret_gn, ml_gn,
           w_out, moe_norm, w_group, b_group, w_router, b_router, w1, w3, w2, w_ple_up, ple_norm,
           ple_gate_norm, w_ple_gate, final_norm):
    B, S, D = x.shape
    depth = p.shape[0]
    assert depth == 1, "the final RMSNorm is fused into the layer's last kernel"
    out = _layer(x.reshape(B * S, D), p[0].reshape(B * S, -1), positions, B, S,
                 attn_norm[0], w_in[0], conv_w[0], conv_b[0], b_igate[0], b_fgate[0], ret_gn[0],
                 ml_gn[0], w_out[0], moe_norm[0], w_group[0], b_group[0], w_router[0],
                 b_router[0], w1[0], w3[0], w2[0], w_ple_up[0], ple_norm[0], ple_gate_norm[0],
                 w_ple_gate[0], final_norm)
    return out.reshape(B, S, D)
```

```python
import functools

import jax
import jax.numpy as jnp
from jax import lax
from jax.experimental import pallas as pl
from jax.experimental.pallas import tpu as pltpu

F32 = jnp.float32
BF16 = jnp.bfloat16

RET_HEADS = 8
ML_HEADS = 4
CHUNK = 128
CONV_W = 4
ROPE_BASE = 10000.0
N_GROUPS = 4
EXPERTS_PER_GROUP = 8
N_EXPERTS = N_GROUPS * EXPERTS_PER_GROUP
EPS = 1e-6

LANES = 128
SUBLANES = 8
MXU_COLS = 256
MIXER_SEQS = 4
FINAL_TILES = 4
FINAL_AHEAD = 3
VMEM_LIMIT = 56 * 1024 * 1024
EXPERT_BLOCK = 512
GROUP_ROW0 = N_EXPERTS
ROUTER_ROWS = 64
ROUTE_TILE = 1024
ROUTE_ROWS = 8
DISPATCH_TILE = 4 * ROUTE_TILE
GATE_ROWS = 2 * SUBLANES
ROW_TILE = (SUBLANES, LANES)


def _rms(x, g):
    return x * lax.rsqrt(jnp.mean(x * x, axis=-1, keepdims=True) + EPS) * g


def _dot(a, b):
    return jnp.dot(a, b, preferred_element_type=F32)


def _dot_nt(a, b):
    return lax.dot_general(a, b, (((1,), (1,)), ((), ())), preferred_element_type=F32)


def _split3(x):
    hi = x.astype(BF16)
    r1 = x - hi.astype(F32)
    mid = r1.astype(BF16)
    lo = (r1 - mid.astype(F32)).astype(BF16)
    return hi, mid, lo


def _rope_kernel(pos_ref, freq_ref, sign_ref, cos_ref, sin_ref, *, half):
    ang = pos_ref[...].astype(F32) * freq_ref[...]
    cos_c = jnp.cos(ang)
    sin_c = jnp.sin(ang)
    group = lax.broadcasted_iota(jnp.int32, ang.shape, 1) // half
    per_row = LANES // half
    for s in range(per_row):
        def spread(t):
            out = t
            for g in range(per_row):
                if g != s:
                    out = jnp.where(group == g, pltpu.roll(t, (half * (g - s)) % LANES, 1), out)
            return out
        cos_ref[s] = spread(cos_c)
        sin_ref[s] = spread(sin_c) * sign_ref[...]


def _rope_tables(positions, dh):
    half = dh // 2
    per_row = LANES // half
    T = positions.size
    rows = T // per_row
    tr = min(rows, 1024)
    freqs = ROPE_BASE ** (-jnp.arange(half, dtype=F32) / half)
    pos_c = jnp.repeat(positions.reshape(per_row, rows).T, half, axis=1)
    sign = jnp.tile(jnp.concatenate([-jnp.ones((half,), F32), jnp.ones((half,), F32)]),
                    LANES // dh)[None, :]
    const = pl.BlockSpec((1, LANES), lambda i: (0, 0))
    out = pl.BlockSpec((per_row, tr, LANES), lambda i: (0, i, 0))
    cos_t, sin_t = pl.pallas_call(
        functools.partial(_rope_kernel, half=half),
        grid=(rows // tr,),
        in_specs=[pl.BlockSpec((tr, LANES), lambda i: (i, 0)), const, const],
        out_specs=[out, out],
        out_shape=[jax.ShapeDtypeStruct((per_row, rows, LANES), F32)] * 2,
        name="rope",
    )(pos_c, jnp.tile(freqs, per_row)[None, :], sign)
    return cos_t.reshape(T, LANES), sin_t.reshape(T, LANES)


def _inproj_kernel(x_ref, g_ref, wm_ref, wgt_ref, gb_ref, cos_ref, sin_ref, ws_ref, convw_ref,
                   convb_ref, rq_ref, rk_ref, rkwt_ref, rv_ref, rg_ref, mq_ref, mk_ref, mvt_ref,
                   mo_ref, gt_ref, carry, *, ret_w, ml_w):
    tm = x_ref.shape[0]
    ret_dh = LANES // 2

    @pl.when(pl.program_id(1) == 0)
    def _():
        carry[0:SUBLANES] = jnp.zeros((SUBLANES, carry.shape[1]), F32)

    xn = _rms(x_ref[...], g_ref[...]).astype(BF16)
    proj = lambda o, w: _dot(xn, wm_ref[:, o:o + w])
    tiles = lambda w: [slice(t, t + LANES) for t in range(0, w, LANES)]

    lane = lax.broadcasted_iota(jnp.int32, (tm, LANES), 1)
    first_half = (lane % ret_dh) < (ret_dh // 2)
    cos_t = cos_ref[...]
    sin_t = sin_ref[...]

    def rot(t):
        swapped = jnp.where(first_half, pltpu.roll(t, LANES - ret_dh // 2, 1),
                            pltpu.roll(t, ret_dh // 2, 1))
        return t * cos_t + swapped * sin_t

    for c0 in range(0, ret_w, MXU_COLS):
        rq = proj(c0, MXU_COLS)
        for ps in tiles(MXU_COLS):
            rq_ref[:, c0 + ps.start:c0 + ps.stop] = rot(rq[:, ps]).astype(BF16)
    for c0 in range(0, ret_w, MXU_COLS):
        rk = proj(ret_w + c0, MXU_COLS)
        for ps in tiles(MXU_COLS):
            cs = slice(c0 + ps.start, c0 + ps.stop)
            k = rot(rk[:, ps]) * (ret_dh ** -0.5)
            rk_ref[:, cs] = k.astype(BF16)
            rkwt_ref[cs, :] = (k * ws_ref[:, cs]).T.astype(BF16)
    for c0 in range(0, ret_w, MXU_COLS):
        cs = slice(c0, c0 + MXU_COLS)
        rv_ref[:, cs] = proj(2 * ret_w + c0, MXU_COLS).astype(BF16)
    for c0 in range(0, ret_w, MXU_COLS):
        g = proj(3 * ret_w + c0, MXU_COLS)
        rg_ref[:, c0:c0 + MXU_COLS] = g * jax.nn.sigmoid(g)

    o_mq = 4 * ret_w
    for c0 in range(0, 2 * ml_w, MXU_COLS):
        cs = slice(c0, c0 + MXU_COLS)
        xq = proj(o_mq + c0, MXU_COLS)
        carry[SUBLANES:SUBLANES + tm, cs] = xq
        acc = xq * convw_ref[CONV_W - 1:CONV_W, cs] + convb_ref[:, cs]
        for s in range(1, CONV_W):
            shifted = carry[SUBLANES - s:SUBLANES - s + tm, cs]
            acc = acc + shifted * convw_ref[CONV_W - 1 - s:CONV_W - s, cs]
        carry[0:SUBLANES, cs] = xq[tm - SUBLANES:tm]
        act = acc * jax.nn.sigmoid(acc)
        if c0 < ml_w:
            mq_ref[:, cs] = act.astype(BF16)
        else:
            mk_ref[:, c0 - ml_w:c0 - ml_w + MXU_COLS] = (act * (LANES ** -0.5)).astype(BF16)
    for c0 in range(0, ml_w, MXU_COLS):
        mv = proj(o_mq + 2 * ml_w + c0, MXU_COLS)
        for hs in tiles(MXU_COLS):
            mvt_ref[c0 + hs.start:c0 + hs.stop, :] = mv[:, hs].T
    for c0 in range(0, ml_w, MXU_COLS):
        mo_ref[:, c0:c0 + MXU_COLS] = jax.nn.sigmoid(proj(o_mq + 3 * ml_w + c0, MXU_COLS))
    gt_ref[...] = _dot_nt(wgt_ref[...], xn) + gb_ref[...]


def _inproj(x3, g, w_main, w_gate_t, gate_bias, cos_t, sin_t, ws, conv_w, conv_b, ret_w, ml_w,
            tm=256):
    B, S, D = x3.shape
    n_main = 4 * (ret_w + ml_w)
    const = lambda b, i: (0, 0)
    tok = lambda w: pl.BlockSpec((None, tm, w), lambda b, i: (b, i, 0))
    tok_t = lambda r: pl.BlockSpec((None, r, tm), lambda b, i: (b, 0, i))
    act = lambda w, dt: jax.ShapeDtypeStruct((B, S, w), dt)
    act_t = lambda r, dt: jax.ShapeDtypeStruct((B, r, S), dt)
    return pl.pallas_call(
        functools.partial(_inproj_kernel, ret_w=ret_w, ml_w=ml_w),
        grid=(B, S // tm),
        in_specs=[
            tok(D),
            pl.BlockSpec((1, D), const),
            pl.BlockSpec((D, n_main), const),
            pl.BlockSpec((GATE_ROWS, D), const),
            pl.BlockSpec((GATE_ROWS, 1), const),
            tok(LANES),
            tok(LANES),
            pl.BlockSpec((tm, ret_w), const),
            pl.BlockSpec((CONV_W, 2 * ml_w), const),
            pl.BlockSpec((1, 2 * ml_w), const),
        ],
        out_specs=[tok(ret_w), tok(ret_w), tok_t(ret_w), tok(ret_w), tok(ret_w),
                   tok(ml_w), tok(ml_w), tok_t(ml_w), tok(ml_w), tok_t(GATE_ROWS)],
        out_shape=[
            act(ret_w, BF16),
            act(ret_w, BF16),
            act_t(ret_w, BF16),
            act(ret_w, BF16),
            act(ret_w, F32),
            act(ml_w, BF16),
            act(ml_w, BF16),
            act_t(ml_w, F32),
            act(ml_w, F32),
            act_t(GATE_ROWS, F32),
        ],
        scratch_shapes=[pltpu.VMEM((SUBLANES + tm, 2 * ml_w), F32)],
        compiler_params=pltpu.CompilerParams(
            dimension_semantics=("arbitrary", "arbitrary"), vmem_limit_bytes=VMEM_LIMIT),
        name="inproj",
    )(x3, g, w_main, w_gate_t, gate_bias, cos_t, sin_t, ws, conv_w, conv_b)


def _mixer_kernel(rq_ref, rk_ref, rkwt_ref, rv_ref, rg_ref, mq_ref, mk_ref, mvt_ref, mo_ref,
                  gt_ref, decay_ref, wq_ref, cd_ref, retgn_ref, mlgn_ref,
                  y_ref, r_state, s_state, n_state, m_state, *, ret_w, ml_w):
    BB, C = rq_ref.shape[0], rq_ref.shape[1]
    n_pairs = ret_w // LANES
    ml_heads = ml_w // LANES
    ret_dh = LANES // 2

    @pl.when(pl.program_id(1) == 0)
    def _():
        r_state[...] = jnp.zeros_like(r_state)
        s_state[...] = jnp.zeros_like(s_state)
        n_state[...] = jnp.zeros_like(n_state)
        m_state[...] = jnp.zeros_like(m_state)

    lane = lax.broadcasted_iota(jnp.int32, (C, LANES), 1)
    row = lax.broadcasted_iota(jnp.int32, (C, LANES), 0)
    assert C == LANES
    lo = lane < ret_dh
    blockdiag = (row < ret_dh) == lo
    lo_b = jnp.where(lo, 1.0, 0.0).astype(BF16)
    hi_b = jnp.where(lo, 0.0, 1.0).astype(BF16)
    seqs = range(BB)
    pair_units = [(s, p, slice(p * LANES, (p + 1) * LANES)) for s in seqs for p in range(n_pairs)]
    head_units = [(s, h, slice(h * LANES, (h + 1) * LANES)) for s in seqs for h in range(ml_heads)]


    row8 = lax.broadcasted_iota(jnp.int32, (SUBLANES, C), 0)
    lane8 = lax.broadcasted_iota(jnp.int32, (SUBLANES, C), 1)
    live = row8 < ml_heads
    triu = (row <= lane).astype(BF16)
    before = row <= lane
    mx, w_inter, e_negm, w_state, w_state_b, dec, beta_t = [], [], [], [], [], [], []
    for s in seqs:
        ig = jnp.where(live, gt_ref[s, 0:SUBLANES, :], 0.0)
        f_pre = jnp.where(live, gt_ref[s, SUBLANES:2 * SUBLANES, :], 30.0)
        l_hi, l_mid, l_lo = _split3(jax.nn.log_sigmoid(f_pre))
        b = _dot(l_hi, triu) + _dot(l_mid, triu) + _dot(l_lo, triu)
        beta = ig - b
        cm = beta
        shift = 1
        while shift < C:
            cm = jnp.maximum(cm, jnp.where(lane8 >= shift, pltpu.roll(cm, shift, 1), -jnp.inf))
            shift *= 2
        m_prev = m_state[s]
        mx.append(jnp.maximum(cm, m_prev))
        mx_last = jnp.broadcast_to(mx[s][:, C - 1:C], (SUBLANES, C))
        w_inter.append(jnp.exp(m_prev - mx[s]))
        e_negm.append(jnp.exp(-(b + mx[s])))
        w_state.append(jnp.exp(beta - mx_last))
        w_state_b.append(w_state[s].astype(BF16))
        dec.append(jnp.exp(m_prev - mx_last))
        m_state[s] = jnp.where(live, jnp.broadcast_to(b[:, C - 1:C], (SUBLANES, C)) + mx_last, 0.0)
        beta_t.append(jnp.concatenate([beta, jnp.zeros((LANES - SUBLANES, C), F32)], axis=0).T)

    hrow = lambda t, h: t[h:h + 1, :]

    rq = [rq_ref[s, :, ps] for s, p, ps in pair_units]
    rk = [rk_ref[s, :, ps] for s, p, ps in pair_units]
    rv = [rv_ref[s, :, ps] for s, p, ps in pair_units]
    r_prev = [r_state[s, p] for s, p, ps in pair_units]
    pu = range(len(pair_units))
    s_ab = [_dot_nt(jnp.concatenate([rq[u] * lo_b, rq[u] * hi_b], axis=0), rk[u]) for u in pu]
    r_read = [_dot(rq[u], r_prev[u].astype(BF16)) for u in pu]
    r_new = [_dot(rkwt_ref[s, ps, :], rv[u]) for u, (s, p, ps) in enumerate(pair_units)]
    hu = range(len(head_units))
    mq = [mq_ref[s, :, hs] for s, h, hs in head_units]
    mk = [mk_ref[s, :, hs] for s, h, hs in head_units]
    mv_t = [mvt_ref[s, hs, :] for s, h, hs in head_units]
    st_prev = [s_state[s, h] for s, h, hs in head_units]
    n_prev = [n_state[s, h] for s, h, hs in head_units]
    a_t = [_dot_nt(mk[u], mq[u]) for u in hu]
    s_read = [_dot_nt(st_prev[u].astype(BF16), mq[u]) for u in hu]
    qn = [_dot_nt(n_prev[u].astype(BF16), mq[u])[0:1, :] for u in hu]
    s_new = [_dot((mv_t[u] * hrow(w_state[s], h)).astype(BF16), mk[u])
             for u, (s, h, hs) in enumerate(head_units)]
    n_new = [_dot(w_state_b[s], mk[u])[h:h + 1, :] for u, (s, h, hs) in enumerate(head_units)]
    for u, (s, p, ps) in enumerate(pair_units):
        r_state[s, p] = cd_ref[p] * r_prev[u] + jnp.where(blockdiag, r_new[u], 0.0)
    for u, (s, h, hs) in enumerate(head_units):
        s_state[s, h] = hrow(dec[s], h) * st_prev[u] + s_new[u]
        n_state[s, h] = hrow(dec[s], h) * n_prev[u] + n_new[u]

    s_ab = [jnp.concatenate([(s_ab[u][0:C] * decay_ref[2 * p]).astype(BF16),
                             (s_ab[u][C:2 * C] * decay_ref[2 * p + 1]).astype(BF16)], axis=1)
            for u, (s, p, ps) in enumerate(pair_units)]
    p_t = [jnp.exp(jnp.where(before, beta_t[s][:, h:h + 1] - hrow(mx[s], h), -jnp.inf)) * a_t[u]
           for u, (s, h, hs) in enumerate(head_units)]

    o = [_dot(s_ab[u], jnp.concatenate([rv[u] * lo_b, rv[u] * hi_b], axis=0)) + r_read[u] * wq_ref[p]
         for u, (s, p, ps) in enumerate(pair_units)]
    num_t = [_dot(mv_t[u].astype(BF16), p_t[u].astype(BF16)) + hrow(w_inter[s], h) * s_read[u]
             for u, (s, h, hs) in enumerate(head_units)]

    for u, (s, p, ps) in enumerate(pair_units):
        sq = o[u] * o[u]
        ms_a = jnp.sum(jnp.where(lo, sq, 0.0), axis=-1, keepdims=True)
        ms_b = jnp.sum(jnp.where(lo, 0.0, sq), axis=-1, keepdims=True)
        ms = jnp.where(lo, ms_a, ms_b) * (1.0 / ret_dh)
        r = o[u] * lax.rsqrt(ms + EPS) * retgn_ref[:, ps]
        y_ref[s, :, ps] = (rg_ref[s, :, ps] * r).astype(y_ref.dtype)
    for u, (s, h, hs) in enumerate(head_units):
        den = jnp.sum(p_t[u], axis=0, keepdims=True) + hrow(w_inter[s], h) * qn[u]
        hh = (num_t[u] * (1.0 / jnp.maximum(jnp.abs(den), hrow(e_negm[s], h)))).T
        hm = _rms(mo_ref[s, :, hs] * hh, mlgn_ref[:, hs])
        y_ref[s, :, ret_w + h * LANES:ret_w + (h + 1) * LANES] = hm.astype(y_ref.dtype)


def _ret_tables(C):
    H = RET_HEADS
    dh = LANES // 2
    log_gamma = jnp.log1p(-(2.0 ** (-5.0 - jnp.arange(H, dtype=F32))))
    idx = jnp.arange(C, dtype=F32)
    rel = idx[:, None] - idx[None, :]
    causal = rel >= 0
    decay = jnp.where(causal, jnp.exp(log_gamma[:, None, None] * jnp.where(causal, rel, 0.0)), 0.0)
    w_state = jnp.exp(log_gamma[:, None] * (C - 1 - idx))
    w_query = jnp.exp(log_gamma[:, None] * (idx + 1.0))
    chunk_decay = jnp.exp(log_gamma * C)
    pair = lambda t: jnp.repeat(t.reshape(H // 2, 2, C).transpose(0, 2, 1), dh, axis=2)
    cd = jnp.repeat(chunk_decay.reshape(H // 2, 2), dh, axis=1)
    cd = jnp.broadcast_to(cd[:, :, None], (H // 2, LANES, LANES))
    ws = pair(w_state).transpose(1, 0, 2).reshape(C, (H // 2) * LANES)
    return decay, ws, pair(w_query), cd


def _mixer(ops, decay, wq, cd, ret_gn, ml_gn, B, S, ret_w, ml_w):
    C = CHUNK
    N = S // C
    n_pairs = ret_w // LANES
    ml_heads = ml_w // LANES
    c2 = lambda b, n: (0, 0)
    c3 = lambda b, n: (0, 0, 0)
    BB = MIXER_SEQS if B % MIXER_SEQS == 0 else 1
    tok = lambda w: pl.BlockSpec((BB, C, w), lambda b, n: (b, n, 0))
    tok_t = lambda r: pl.BlockSpec((BB, r, C), lambda b, n: (b, 0, n))
    y = pl.pallas_call(
        functools.partial(_mixer_kernel, ret_w=ret_w, ml_w=ml_w),
        grid=(B // BB, N),
        in_specs=[
            tok(ret_w), tok(ret_w), tok_t(ret_w), tok(ret_w), tok(ret_w),
            tok(ml_w), tok(ml_w), tok_t(ml_w), tok(ml_w), tok_t(GATE_ROWS),
            pl.BlockSpec((RET_HEADS, C, C), c3),
            pl.BlockSpec((n_pairs, C, LANES), c3),
            pl.BlockSpec((n_pairs, LANES, LANES), c3),
            pl.BlockSpec((1, ret_w), c2),
            pl.BlockSpec((1, ml_w), c2),
        ],
        out_specs=pl.BlockSpec((BB, C, ret_w + ml_w), lambda b, n: (b, n, 0)),
        out_shape=jax.ShapeDtypeStruct((B, S, ret_w + ml_w), BF16),
        scratch_shapes=[
            pltpu.VMEM((BB, n_pairs, LANES, LANES), F32),
            pltpu.VMEM((BB, ml_heads, LANES, LANES), F32),
            pltpu.VMEM((BB, ml_heads, SUBLANES, LANES), F32),
            pltpu.VMEM((BB, SUBLANES, LANES), F32),
        ],
        compiler_params=pltpu.CompilerParams(
            dimension_semantics=("arbitrary", "arbitrary"), vmem_limit_bytes=VMEM_LIMIT),
        name="mixer",
    )(*ops, decay, wq, cd, ret_gn.reshape(1, -1), ml_gn.reshape(1, -1))
    return y.reshape(B * S, ret_w + ml_w)


def _router_kernel(y_ref, x_ref, wo_ref, g_ref, wr_hi_ref, wr_lo_ref, br_ref,
                   h_ref, xn_ref, ri_ref, rf_ref, cnt_ref, run_cnt):
    tm = y_ref.shape[0]

    @pl.when(pl.program_id(0) == 0)
    def _():
        run_cnt[...] = jnp.zeros_like(run_cnt)

    h = x_ref[...] + _dot(y_ref[...], wo_ref[...])
    h_ref[...] = h
    xn = _rms(h, g_ref[...])
    x_hi = xn.astype(BF16)
    xn_ref[...] = x_hi.reshape(xn_ref.shape)
    x_lo = (xn - x_hi.astype(F32)).astype(BF16)
    logits = (_dot_nt(wr_hi_ref[...], x_hi) + _dot_nt(wr_hi_ref[...], x_lo)
              + _dot_nt(wr_lo_ref[...], x_hi) + br_ref[...])
    big = jnp.int32(LANES)
    neg = -jnp.inf
    gl = logits[GROUP_ROW0:GROUP_ROW0 + 8]
    grow = lax.broadcasted_iota(jnp.int32, gl.shape, 0)
    is_g = grow < N_GROUPS
    gl = jnp.where(is_g, gl, neg)
    gmax = jnp.max(gl, axis=0, keepdims=True)
    gsum = jnp.sum(jnp.where(is_g, jnp.exp(gl - gmax), 0.0), axis=0, keepdims=True)
    p_g = 1.0 / gsum
    g_sel = jnp.min(jnp.where(is_g & (gl == gmax), grow, big), axis=0, keepdims=True)
    el = logits[0:N_EXPERTS]
    erow = lax.broadcasted_iota(jnp.int32, el.shape, 0)
    in_grp = (erow // EXPERTS_PER_GROUP) == g_sel
    el = jnp.where(in_grp, el, neg)
    emax = jnp.max(el, axis=0, keepdims=True)
    eexp = jnp.where(in_grp, jnp.exp(el - emax), 0.0)
    prob = eexp / jnp.sum(eexp, axis=0, keepdims=True)
    pm1 = jnp.where(in_grp, prob, -1.0)
    p1 = jnp.max(pm1, axis=0, keepdims=True)
    i1 = jnp.min(jnp.where(pm1 == p1, erow, big), axis=0, keepdims=True)
    pm2 = jnp.where(erow == i1, -1.0, pm1)
    p2 = jnp.max(pm2, axis=0, keepdims=True)
    i2 = jnp.min(jnp.where(pm2 == p2, erow, big), axis=0, keepdims=True)
    denom = p1 + p2
    g1 = p_g * p1 / denom
    g2 = p_g * p2 / denom

    sel1 = erow == i1
    sel2 = erow == i2
    onehot = (sel1 | sel2).astype(BF16)
    r_i = lax.broadcasted_iota(jnp.int32, (tm, tm), 0)
    c_i = lax.broadcasted_iota(jnp.int32, (tm, tm), 1)
    tri = (r_i < c_i).astype(BF16)
    prefix = _dot(onehot, tri) + run_cnt[:, 0:1]
    rank1 = jnp.sum(jnp.where(sel1, prefix, 0.0), axis=0, keepdims=True).astype(jnp.int32)
    rank2 = jnp.sum(jnp.where(sel2, prefix, 0.0), axis=0, keepdims=True).astype(jnp.int32)
    new_cnt = run_cnt[...] + jnp.sum(onehot.astype(F32), axis=1, keepdims=True)
    run_cnt[...] = new_cnt
    cnt_ref[...] = new_cnt.astype(jnp.int32)

    rrow = lax.broadcasted_iota(jnp.int32, (ROUTE_ROWS, tm), 0)
    ri_ref[...] = jnp.where(rrow == 0, rank1 * N_EXPERTS + i1,
                            jnp.where(rrow == 1, rank2 * N_EXPERTS + i2, 0))
    lrow = lax.broadcasted_iota(jnp.int32, (LANES, tm), 0)
    rf_ref[...] = jnp.where(lrow == 0, g1, jnp.where(lrow == 1, g2, 0.0)).T


def _router(y, x2, w_out, g, wr_hi, wr_lo, br, tm=ROUTE_TILE):
    T, D = x2.shape
    const = lambda i: (0, 0)
    tile = lambda i: (i, 0)
    return pl.pallas_call(
        _router_kernel,
        grid=(T // tm,),
        in_specs=[
            pl.BlockSpec((tm, y.shape[1]), tile),
            pl.BlockSpec((tm, D), tile),
            pl.BlockSpec(w_out.shape, const),
            pl.BlockSpec((1, D), const),
            pl.BlockSpec((ROUTER_ROWS, D), const),
            pl.BlockSpec((ROUTER_ROWS, D), const),
            pl.BlockSpec((ROUTER_ROWS, 1), const),
        ],
        out_specs=[
            pl.BlockSpec((tm, D), tile),
            pl.BlockSpec((tm,) + ROW_TILE, lambda i: (i, 0, 0)),
            pl.BlockSpec((None, ROUTE_ROWS, tm), lambda i: (i, 0, 0)),
            pl.BlockSpec((tm, LANES), tile),
            pl.BlockSpec((N_EXPERTS, LANES), const),
        ],
        out_shape=[
            jax.ShapeDtypeStruct((T, D), F32),
            jax.ShapeDtypeStruct((T,) + ROW_TILE, BF16),
            jax.ShapeDtypeStruct((T // tm, ROUTE_ROWS, tm), jnp.int32),
            jax.ShapeDtypeStruct((T, LANES), F32),
            jax.ShapeDtypeStruct((N_EXPERTS, LANES), jnp.int32),
        ],
        scratch_shapes=[pltpu.VMEM((N_EXPERTS, LANES), F32)],
        compiler_params=pltpu.CompilerParams(
            dimension_semantics=("arbitrary",), vmem_limit_bytes=VMEM_LIMIT),
        name="router",
    )(y, x2, w_out, g, wr_hi, wr_lo, br)


def _slot_base(tok0):
    return (tok0 // ROUTE_TILE) * (2 * ROUTE_TILE) + tok0 % ROUTE_TILE


def _route_spans(tm):
    assert tm % ROUTE_TILE == 0 or ROUTE_TILE % tm == 0
    span = min(tm, ROUTE_TILE)
    return [(t0, span) for t0 in range(0, tm, span)]


def _for_each_pad_block(pend_ref, nu_ref, n_blocks, fn):
    blk = EXPERT_BLOCK
    for e in range(N_EXPERTS):
        prev_end = 0 if e == 0 else pend_ref[e - 1]

        @pl.when(pend_ref[e] > prev_end)
        def _():
            fn(pl.multiple_of(pend_ref[e] - blk, blk))

    def tail(j, c):
        fn(pl.multiple_of(j * blk, blk))
        return c

    lax.fori_loop(nu_ref[0], n_blocks, tail, 0)


def _dispatch_kernel(dest_ref, pend_ref, nu_ref, xn_ref, buf_ref, zeros, sem, zsem):
    tm = xn_ref.shape[0]
    blk = EXPERT_BLOCK
    i = pl.program_id(0)

    @pl.when(i == 0)
    def _():
        zeros[...] = jnp.zeros_like(zeros)
        zcopy = lambda row: pltpu.make_async_copy(zeros, buf_ref.at[pl.ds(row, blk)], zsem)
        n_blocks = buf_ref.shape[0] // blk
        _for_each_pad_block(pend_ref, nu_ref, n_blocks, lambda row: zcopy(row).start())
        _for_each_pad_block(pend_ref, nu_ref, n_blocks, lambda row: zcopy(row).wait())

    for t0, span in _route_spans(tm):
        slot0 = _slot_base(i * tm + t0)

        def issue(g, c, t0=t0, slot0=slot0):
            for j in range(SUBLANES):
                for k in range(2):
                    t = g * SUBLANES + j
                    d = dest_ref[slot0 + k * ROUTE_TILE + t]
                    pltpu.make_async_copy(xn_ref.at[t0 + t], buf_ref.at[d], sem).start(priority=k)
            return c

        lax.fori_loop(0, span // SUBLANES, issue, 0)
    for k in range(2):
        pltpu.make_async_copy(buf_ref.at[pl.ds(tm, tm)], buf_ref.at[pl.ds(0, tm)], sem).wait()


def _dispatch(dest, pad_end, n_used, xn, n_rows, tm=ROUTE_TILE):
    T = xn.shape[0]
    return pl.pallas_call(
        _dispatch_kernel,
        grid_spec=pltpu.PrefetchScalarGridSpec(
            num_scalar_prefetch=3,
            grid=(T // tm,),
            in_specs=[pl.BlockSpec((tm,) + ROW_TILE, lambda i, *_: (i, 0, 0))],
            out_specs=pl.BlockSpec(memory_space=pl.ANY),
            scratch_shapes=[
                pltpu.VMEM((EXPERT_BLOCK,) + ROW_TILE, xn.dtype),
                pltpu.SemaphoreType.DMA,
                pltpu.SemaphoreType.DMA,
            ],
        ),
        out_shape=jax.ShapeDtypeStruct((n_rows,) + ROW_TILE, xn.dtype),
        compiler_params=pltpu.CompilerParams(
            dimension_semantics=("arbitrary",), vmem_limit_bytes=VMEM_LIMIT),
        name="dispatch",
    )(dest, pad_end, n_used, xn)


def _expert_kernel(be_ref, nu_ref, nv_ref, first_ref, next_ref, slot_ref, x_ref, w1_hbm, w3_hbm,
                   w2_hbm, y_ref, s1, s3, s2, w1b, w3b, w2b, sems):
    j = pl.program_id(0)
    used = j < nu_ref[0]
    blk = x_ref.shape[0]

    def weight_copies(e, slot):
        return [pltpu.make_async_copy(w.at[e], s.at[slot], sems.at[slot])
                for w, s in ((w1_hbm, s1), (w3_hbm, s3), (w2_hbm, s2))]

    @pl.when(j == 0)
    def _():
        for c in weight_copies(be_ref[0], 0):
            c.start()

    @pl.when(used & (first_ref[j] == 1))
    def _():
        slot = slot_ref[j]
        for c in weight_copies(be_ref[j], slot):
            c.wait()

        @pl.when(next_ref[j] >= 0)
        def _():
            for c in weight_copies(next_ref[j], 1 - slot):
                c.start()

        w1b[...] = s1[slot].astype(BF16)
        w3b[...] = s3[slot].astype(BF16)
        w2b[...] = s2[slot].astype(BF16)

    def swiglu(rows):
        n = rows.stop - rows.start
        x = x_ref[rows].reshape(n, -1)
        a = _dot(x, w1b[...])
        hmid = a * jax.nn.sigmoid(a) * _dot(x, w3b[...])
        y_ref[rows] = _dot(hmid.astype(BF16), w2b[...]).reshape((n,) + y_ref.shape[1:])

    quarter = blk // 4
    for n in range(quarter, blk + 1, quarter):
        lower = n - quarter if n > quarter else -1
        @pl.when(used & (nv_ref[j] > lower) & (nv_ref[j] <= n))
        def _(n=n):
            swiglu(slice(0, n))
            if n < blk:
                y_ref[n:blk] = jnp.zeros((blk - n,) + y_ref.shape[1:], y_ref.dtype)

    @pl.when(jnp.logical_not(used))
    def _():
        y_ref[...] = jnp.zeros_like(y_ref)


def _experts(block_expert, n_used, n_valid, first, next_expert, slot, x_buf, w1, w3, w2):
    P = x_buf.shape[0]
    D = w1.shape[1]
    assert x_buf.shape[1:] == ROW_TILE and D == SUBLANES * LANES
    blk = EXPERT_BLOCK
    d_exp = w2.shape[1]
    hbm = pl.BlockSpec(memory_space=pl.ANY)
    return pl.pallas_call(
        _expert_kernel,
        grid_spec=pltpu.PrefetchScalarGridSpec(
            num_scalar_prefetch=6,
            grid=(P // blk,),
            in_specs=[
                pl.BlockSpec((blk,) + ROW_TILE,
                             lambda j, be, nu, *_: (jnp.minimum(j, nu[0] - 1), 0, 0)),
                hbm, hbm, hbm,
            ],
            out_specs=pl.BlockSpec((blk,) + ROW_TILE, lambda j, *_: (j, 0, 0)),
            scratch_shapes=[
                pltpu.VMEM((2, D, d_exp), F32),
                pltpu.VMEM((2, D, d_exp), F32),
                pltpu.VMEM((2, d_exp, D), F32),
                pltpu.VMEM((D, d_exp), BF16),
                pltpu.VMEM((D, d_exp), BF16),
                pltpu.VMEM((d_exp, D), BF16),
                pltpu.SemaphoreType.DMA((2,)),
            ],
        ),
        out_shape=jax.ShapeDtypeStruct((P,) + ROW_TILE, F32),
        compiler_params=pltpu.CompilerParams(
            dimension_semantics=("arbitrary",), vmem_limit_bytes=VMEM_LIMIT),
        name="experts",
    )(block_expert, n_used, n_valid, first, next_expert, slot, x_buf, w1, w3, w2)


def _final_kernel(dest_ref, h_ref, rf_ref, p_ref, wup_ref, gple_ref, ggate_ref, wgate_ref,
                  gfin_ref, ybuf_ref, out_ref, *scratch, tm):
    row_bufs, sems = scratch[:-1], scratch[-1]
    s = pl.program_id(0)
    D = h_ref.shape[1]
    assert ROUTE_TILE % tm == 0

    def gather(tile, bufs, sem, straight=False):
        slot0 = _slot_base(tile * tm)

        def issue(g, c):
            for j in range(SUBLANES):
                for k in range(2):
                    t = g * SUBLANES + j
                    d = dest_ref[slot0 + k * ROUTE_TILE + t]
                    pltpu.make_async_copy(ybuf_ref.at[d], bufs[k].at[t], sem).start(priority=k)
            return c

        if straight:
            for g in range(tm // SUBLANES):
                issue(g, 0)
        else:
            lax.fori_loop(0, tm // SUBLANES, issue, 0)

    def wait(sem):
        for k in range(2):
            pltpu.make_async_copy(ybuf_ref.at[pl.ds(0, tm)], ybuf_ref.at[pl.ds(tm, tm)], sem).wait()

    def compute(rows, bufs):
        e = _rms(_dot(p_ref[rows, :].astype(BF16), wup_ref[...]), gple_ref[...])
        rf = rf_ref[rows, :]
        h = (h_ref[rows, :] + rf[:, 0:1] * bufs[0][...].reshape(tm, D)
             + rf[:, 1:2] * bufs[1][...].reshape(tm, D))
        gate = jax.nn.sigmoid(_dot(_rms(h, ggate_ref[...]).astype(BF16), wgate_ref[...]))
        h = h + gate * e
        out_ref[rows, :] = _rms(h, gfin_ref[...])

    n_tiles = pl.num_programs(0) * FINAL_TILES
    bufs = [(row_bufs[2 * i], row_bufs[2 * i + 1]) for i in range(FINAL_TILES)]

    @pl.when(s == 0)
    def _():
        for i in range(FINAL_AHEAD):
            gather(i, bufs[i], sems.at[i])

    for i in range(FINAL_TILES):
        wait(sems.at[i])
        nxt = (i + FINAL_AHEAD) % FINAL_TILES
        gather(jnp.minimum(s * FINAL_TILES + i + FINAL_AHEAD, n_tiles - 1), bufs[nxt],
               sems.at[nxt], straight=True)
        compute(slice(i * tm, (i + 1) * tm), bufs[i])

    @pl.when(s == pl.num_programs(0) - 1)
    def _():
        for i in range(FINAL_AHEAD):
            wait(sems.at[i])


def _final(dest, h, rf, p2, w_up, g_ple, g_gate, w_gate, g_fin, y_buf, tm=256):
    T, D = h.shape
    const = lambda i, *_: (0, 0)
    tile = lambda i, *_: (i, 0)
    assert y_buf.shape[1:] == ROW_TILE and D == SUBLANES * LANES
    row_buf = pltpu.VMEM((tm,) + ROW_TILE, F32)
    return pl.pallas_call(
        functools.partial(_final_kernel, tm=tm),
        grid_spec=pltpu.PrefetchScalarGridSpec(
            num_scalar_prefetch=1,
            grid=(T // (FINAL_TILES * tm),),
            in_specs=[
                pl.BlockSpec((FINAL_TILES * tm, D), tile),
                pl.BlockSpec((FINAL_TILES * tm, LANES), tile),
                pl.BlockSpec((FINAL_TILES * tm, p2.shape[1]), tile),
                pl.BlockSpec(w_up.shape, const),
                pl.BlockSpec((1, D), const),
                pl.BlockSpec((1, D), const),
                pl.BlockSpec(w_gate.shape, const),
                pl.BlockSpec((1, D), const),
                pl.BlockSpec(memory_space=pl.ANY),
            ],
            out_specs=pl.BlockSpec((FINAL_TILES * tm, D), tile),
            scratch_shapes=[row_buf] * (2 * FINAL_TILES) + [pltpu.SemaphoreType.DMA((FINAL_TILES,))],
        ),
        out_shape=jax.ShapeDtypeStruct((T, D), F32),
        compiler_params=pltpu.CompilerParams(
            dimension_semantics=("arbitrary",), vmem_limit_bytes=VMEM_LIMIT),
        name="final",
    )(dest, h, rf, p2, w_up, g_ple, g_gate, w_gate, g_fin, y_buf)


def _layer(h2, p2, positions, B, S, attn_norm, w_in, conv_w, conv_b, b_igate, b_fgate, ret_gn,
           ml_gn, w_out, moe_norm, w_group, b_group, w_router, b_router, w1, w3, w2, w_ple_up,
           ple_norm, ple_gate_norm, w_ple_gate, out_norm):
    T, D = h2.shape
    ret_w = ret_gn.shape[0]
    ml_w = ml_gn.shape[0]
    n_main = 4 * ret_w + 4 * ml_w
    row = lambda v: v.reshape(1, -1).astype(F32)
    pad_lanes = lambda a: jnp.pad(a, ((0, 0), (0, LANES - a.shape[1])))

    ml_heads = ml_w // LANES
    w_main = w_in.astype(BF16)
    gate_rows = lambda a: jnp.pad(a, ((0, SUBLANES - ml_heads), (0, 0)))
    w_i, w_f = w_in[:, n_main:n_main + ml_heads].T, w_in[:, n_main + ml_heads:].T
    w_gate_t = jnp.concatenate([gate_rows(w_i), gate_rows(w_f)]).astype(BF16)
    gate_bias = jnp.concatenate([gate_rows(b_igate[:, None]), gate_rows(b_fgate[:, None])])
    cos_t, sin_t = _rope_tables(positions, LANES // 2)
    decay, ws, wq, cd = _ret_tables(CHUNK)
    tm_in = 2 * CHUNK
    ops = _inproj(h2.reshape(B, S, D), row(attn_norm), w_main, w_gate_t, gate_bias.astype(F32),
                  cos_t.reshape(B, S, LANES), sin_t.reshape(B, S, LANES),
                  jnp.tile(ws, (tm_in // CHUNK, 1)), conv_w, conv_b.reshape(1, -1), ret_w, ml_w,
                  tm=tm_in)

    y = _mixer(ops, decay, wq, cd, ret_gn, ml_gn, B, S, ret_w, ml_w)

    pad_rows = lambda a: jnp.pad(a, ((0, ROUTER_ROWS - a.shape[0]), (0, 0)))
    wr = pad_rows(jnp.concatenate([w_router, w_group], axis=1).T)
    wr_hi = wr.astype(BF16)
    wr_lo = (wr - wr_hi.astype(F32)).astype(BF16)
    br = pad_rows(jnp.concatenate([b_router, b_group])[:, None].astype(F32))
    h_mid, xn, ri, rf, counts = _router(y, h2, w_out.astype(BF16), row(moe_norm), wr_hi, wr_lo, br)

    blk = EXPERT_BLOCK
    counts = counts[:, 0]
    padded = (counts + blk - 1) // blk * blk
    pad_end = jnp.cumsum(padded)
    pad_start = pad_end - padded
    n_blocks = (2 * T) // blk + N_EXPERTS
    assert n_blocks <= 256
    block_start = jnp.arange(n_blocks, dtype=jnp.int32) * blk
    block_expert = jnp.minimum(jnp.sum(pad_end[None, :] <= block_start[:, None], axis=1),
                               N_EXPERTS - 1).astype(jnp.int32)
    n_used = (pad_end[-1:] // blk).astype(jnp.int32)
    real_end = jnp.sum(jnp.where(block_expert[:, None] == jnp.arange(N_EXPERTS)[None, :],
                                 (pad_start + counts)[None, :], 0), axis=1)
    n_valid = jnp.clip(real_end - block_start, 0, blk).astype(jnp.int32)
    blk_id = jnp.arange(n_blocks, dtype=jnp.int32)
    is_used = blk_id < n_used[0]
    first = is_used & jnp.concatenate([jnp.ones((1,), bool), block_expert[1:] != block_expert[:-1]])
    seg_end = jnp.sum(jnp.where(block_expert[:, None] == jnp.arange(N_EXPERTS)[None, :],
                                (pad_end // blk)[None, :], 0), axis=1)
    next_blk = jnp.minimum(seg_end, n_blocks - 1)
    next_expert = jnp.sum(jnp.where(next_blk[:, None] == blk_id[None, :], block_expert[None, :], 0),
                          axis=1)
    next_expert = jnp.where(seg_end < n_used[0], next_expert, -1).astype(jnp.int32)
    slot = ((jnp.cumsum(first.astype(jnp.int32)) - 1) % 2).astype(jnp.int32)
    codes = ri[:, 0:2, :].reshape(-1)
    start_blk = jnp.dot(jax.nn.one_hot(codes % N_EXPERTS, N_EXPERTS, dtype=BF16),
                        (pad_start // blk).astype(BF16), preferred_element_type=F32)
    dest = (codes // N_EXPERTS + start_blk.astype(jnp.int32) * blk).astype(jnp.int32)

    x_buf = _dispatch(dest, pad_end.astype(jnp.int32), n_used, xn, n_blocks * blk,
                      tm=min(T, DISPATCH_TILE))
    y_buf = _experts(block_expert, n_used, n_valid, first.astype(jnp.int32), next_expert, slot,
                     x_buf, w1, w3, w2)

    return _final(dest, h_mid, rf, p2, w_ple_up.astype(BF16), row(ple_norm), row(ple_gate_norm),
                  w_ple_gate.astype(BF16), row(out_norm), y_buf)


def kernel(x, p, positions, attn_norm, w_in, conv_w, conv_b, b_igate, b_fgate, ret_gn, ml_gn,
           w_out, moe_norm, w_group, b_group, w_router, b_router, w1, w3, w2, w_ple_up, ple_norm,
           ple_gate_norm, w_ple_gate, final_norm):
    B, S, D = x.shape
    depth = p.shape[0]
    assert depth == 1, "the final RMSNorm is fused into the layer's last kernel"
    out = _layer(x.reshape(B * S, D), p[0].reshape(B * S, -1), positions, B, S,
                 attn_norm[0], w_in[0], conv_w[0], conv_b[0], b_igate[0], b_fgate[0], ret_gn[0],
                 ml_gn[0], w_out[0], moe_norm[0], w_group[0], b_group[0], w_router[0],
                 b_router[0], w1[0], w3[0], w2[0], w_ple_up[0], ple_norm[0], ple_gate_norm[0],
                 w_ple_gate[0], final_norm)
    return out.reshape(B, S, D)
```

```python
import functools

import jax
import jax.numpy as jnp
from jax import lax
from jax.experimental import pallas as pl
from jax.experimental.pallas import tpu as pltpu

F32 = jnp.float32
BF16 = jnp.bfloat16

RET_HEADS = 8
ML_HEADS = 4
CHUNK = 128
CONV_W = 4
ROPE_BASE = 10000.0
N_GROUPS = 4
EXPERTS_PER_GROUP = 8
N_EXPERTS = N_GROUPS * EXPERTS_PER_GROUP
EPS = 1e-6

LANES = 128
SUBLANES = 8
MXU_COLS = 256
MIXER_SEQS = 4
FINAL_TILES = 4
FINAL_AHEAD = 3
VMEM_LIMIT = 56 * 1024 * 1024
EXPERT_BLOCK = 512
GROUP_ROW0 = N_EXPERTS
ROUTER_ROWS = 64
ROUTE_TILE = 1024
ROUTE_ROWS = 8
DISPATCH_TILE = 4 * ROUTE_TILE
GATE_ROWS = 2 * SUBLANES
ROW_TILE = (SUBLANES, LANES)


def _rms(x, g):
    return x * lax.rsqrt(jnp.mean(x * x, axis=-1, keepdims=True) + EPS) * g


def _dot(a, b):
    return jnp.dot(a, b, preferred_element_type=F32)


def _dot_nt(a, b):
    return lax.dot_general(a, b, (((1,), (1,)), ((), ())), preferred_element_type=F32)


def _split3(x):
    hi = x.astype(BF16)
    r1 = x - hi.astype(F32)
    mid = r1.astype(BF16)
    lo = (r1 - mid.astype(F32)).astype(BF16)
    return hi, mid, lo


def _rope_kernel(pos_ref, freq_ref, sign_ref, cos_ref, sin_ref, *, half):
    ang = pos_ref[...].astype(F32) * freq_ref[...]
    cos_c = jnp.cos(ang)
    sin_c = jnp.sin(ang)
    group = lax.broadcasted_iota(jnp.int32, ang.shape, 1) // half
    per_row = LANES // half
    for s in range(per_row):
        def spread(t):
            out = t
            for g in range(per_row):
                if g != s:
                    out = jnp.where(group == g, pltpu.roll(t, (half * (g - s)) % LANES, 1), out)
            return out
        cos_ref[s] = spread(cos_c)
        sin_ref[s] = spread(sin_c) * sign_ref[...]


def _rope_tables(positions, dh):
    half = dh // 2
    per_row = LANES // half
    T = positions.size
    rows = T // per_row
    tr = min(rows, 1024)
    freqs = ROPE_BASE ** (-jnp.arange(half, dtype=F32) / half)
    pos_c = jnp.repeat(positions.reshape(per_row, rows).T, half, axis=1)
    sign = jnp.tile(jnp.concatenate([-jnp.ones((half,), F32), jnp.ones((half,), F32)]),
                    LANES // dh)[None, :]
    const = pl.BlockSpec((1, LANES), lambda i: (0, 0))
    out = pl.BlockSpec((per_row, tr, LANES), lambda i: (0, i, 0))
    cos_t, sin_t = pl.pallas_call(
        functools.partial(_rope_kernel, half=half),
        grid=(rows // tr,),
        in_specs=[pl.BlockSpec((tr, LANES), lambda i: (i, 0)), const, const],
        out_specs=[out, out],
        out_shape=[jax.ShapeDtypeStruct((per_row, rows, LANES), F32)] * 2,
        name="rope",
    )(pos_c, jnp.tile(freqs, per_row)[None, :], sign)
    return cos_t.reshape(T, LANES), sin_t.reshape(T, LANES)


def _inproj_kernel(x_ref, g_ref, wm_ref, wgt_ref, gb_ref, cos_ref, sin_ref, ws_ref, convw_ref,
                   convb_ref, rq_ref, rk_ref, rkwt_ref, rv_ref, rg_ref, mq_ref, mk_ref, mvt_ref,
                   mo_ref, gt_ref, carry, *, ret_w, ml_w):
    tm = x_ref.shape[0]
    ret_dh = LANES // 2

    @pl.when(pl.program_id(1) == 0)
    def _():
        carry[0:SUBLANES] = jnp.zeros((SUBLANES, carry.shape[1]), F32)

    xn = _rms(x_ref[...], g_ref[...]).astype(BF16)
    proj = lambda o, w: _dot(xn, wm_ref[:, o:o + w])
    tiles = lambda w: [slice(t, t + LANES) for t in range(0, w, LANES)]

    lane = lax.broadcasted_iota(jnp.int32, (tm, LANES), 1)
    first_half = (lane % ret_dh) < (ret_dh // 2)
    cos_t = cos_ref[...]
    sin_t = sin_ref[...]

    def rot(t):
        swapped = jnp.where(first_half, pltpu.roll(t, LANES - ret_dh // 2, 1),
                            pltpu.roll(t, ret_dh // 2, 1))
        return t * cos_t + swapped * sin_t

    for c0 in range(0, ret_w, MXU_COLS):
        rq = proj(c0, MXU_COLS)
        for ps in tiles(MXU_COLS):
            rq_ref[:, c0 + ps.start:c0 + ps.stop] = rot(rq[:, ps]).astype(BF16)
    for c0 in range(0, ret_w, MXU_COLS):
        rk = proj(ret_w + c0, MXU_COLS)
        for ps in tiles(MXU_COLS):
            cs = slice(c0 + ps.start, c0 + ps.stop)
            k = rot(rk[:, ps]) * (ret_dh ** -0.5)
            rk_ref[:, cs] = k.astype(BF16)
            rkwt_ref[cs, :] = (k * ws_ref[:, cs]).T.astype(BF16)
    for c0 in range(0, ret_w, MXU_COLS):
        cs = slice(c0, c0 + MXU_COLS)
        rv_ref[:, cs] = proj(2 * ret_w + c0, MXU_COLS).astype(BF16)
    for c0 in range(0, ret_w, MXU_COLS):
        g = proj(3 * ret_w + c0, MXU_COLS)
        rg_ref[:, c0:c0 + MXU_COLS] = g * jax.nn.sigmoid(g)

    o_mq = 4 * ret_w
    for c0 in range(0, 2 * ml_w, MXU_COLS):
        cs = slice(c0, c0 + MXU_COLS)
        xq = proj(o_mq + c0, MXU_COLS)
        carry[SUBLANES:SUBLANES + tm, cs] = xq
        acc = xq * convw_ref[CONV_W - 1:CONV_W, cs] + convb_ref[:, cs]
        for s in range(1, CONV_W):
            shifted = carry[SUBLANES - s:SUBLANES - s + tm, cs]
            acc = acc + shifted * convw_ref[CONV_W - 1 - s:CONV_W - s, cs]
        carry[0:SUBLANES, cs] = xq[tm - SUBLANES:tm]
        act = acc * jax.nn.sigmoid(acc)
        if c0 < ml_w:
            mq_ref[:, cs] = act.astype(BF16)
        else:
            mk_ref[:, c0 - ml_w:c0 - ml_w + MXU_COLS] = (act * (LANES ** -0.5)).astype(BF16)
    for c0 in range(0, ml_w, MXU_COLS):
        mv = proj(o_mq + 2 * ml_w + c0, MXU_COLS)
        for hs in tiles(MXU_COLS):
            mvt_ref[c0 + hs.start:c0 + hs.stop, :] = mv[:, hs].T
    for c0 in range(0, ml_w, MXU_COLS):
        mo_ref[:, c0:c0 + MXU_COLS] = jax.nn.sigmoid(proj(o_mq + 3 * ml_w + c0, MXU_COLS))
    gt_ref[...] = _dot_nt(wgt_ref[...], xn) + gb_ref[...]


def _inproj(x3, g, w_main, w_gate_t, gate_bias, cos_t, sin_t, ws, conv_w, conv_b, ret_w, ml_w,
            tm=256):
    B, S, D = x3.shape
    n_main = 4 * (ret_w + ml_w)
    const = lambda b, i: (0, 0)
    tok = lambda w: pl.BlockSpec((None, tm, w), lambda b, i: (b, i, 0))
    tok_t = lambda r: pl.BlockSpec((None, r, tm), lambda b, i: (b, 0, i))
    act = lambda w, dt: jax.ShapeDtypeStruct((B, S, w), dt)
    act_t = lambda r, dt: jax.ShapeDtypeStruct((B, r, S), dt)
    return pl.pallas_call(
        functools.partial(_inproj_kernel, ret_w=ret_w, ml_w=ml_w),
        grid=(B, S // tm),
        in_specs=[
            tok(D),
            pl.BlockSpec((1, D), const),
            pl.BlockSpec((D, n_main), const),
            pl.BlockSpec((GATE_ROWS, D), const),
            pl.BlockSpec((GATE_ROWS, 1), const),
            tok(LANES),
            tok(LANES),
            pl.BlockSpec((tm, ret_w), const),
            pl.BlockSpec((CONV_W, 2 * ml_w), const),
            pl.BlockSpec((1, 2 * ml_w), const),
        ],
        out_specs=[tok(ret_w), tok(ret_w), tok_t(ret_w), tok(ret_w), tok(ret_w),
                   tok(ml_w), tok(ml_w), tok_t(ml_w), tok(ml_w), tok_t(GATE_ROWS)],
        out_shape=[
            act(ret_w, BF16),
            act(ret_w, BF16),
            act_t(ret_w, BF16),
            act(ret_w, BF16),
            act(ret_w, F32),
            act(ml_w, BF16),
            act(ml_w, BF16),
            act_t(ml_w, F32),
            act(ml_w, F32),
            act_t(GATE_ROWS, F32),
        ],
        scratch_shapes=[pltpu.VMEM((SUBLANES + tm, 2 * ml_w), F32)],
        compiler_params=pltpu.CompilerParams(
            dimension_semantics=("arbitrary", "arbitrary"), vmem_limit_bytes=VMEM_LIMIT),
        name="inproj",
    )(x3, g, w_main, w_gate_t, gate_bias, cos_t, sin_t, ws, conv_w, conv_b)


def _mixer_kernel(rq_ref, rk_ref, rkwt_ref, rv_ref, rg_ref, mq_ref, mk_ref, mvt_ref, mo_ref,
                  gt_ref, decay_ref, wq_ref, cd_ref, retgn_ref, mlgn_ref,
                  y_ref, r_state, s_state, n_state, m_state, *, ret_w, ml_w):
    BB, C = rq_ref.shape[0], rq_ref.shape[1]
    n_pairs = ret_w // LANES
    ml_heads = ml_w // LANES
    ret_dh = LANES // 2

    @pl.when(pl.program_id(1) == 0)
    def _():
        r_state[...] = jnp.zeros_like(r_state)
        s_state[...] = jnp.zeros_like(s_state)
        n_state[...] = jnp.zeros_like(n_state)
        m_state[...] = jnp.zeros_like(m_state)

    lane = lax.broadcasted_iota(jnp.int32, (C, LANES), 1)
    row = lax.broadcasted_iota(jnp.int32, (C, LANES), 0)
    assert C == LANES
    lo = lane < ret_dh
    blockdiag = (row < ret_dh) == lo
    lo_b = jnp.where(lo, 1.0, 0.0).astype(BF16)
    hi_b = jnp.where(lo, 0.0, 1.0).astype(BF16)
    seqs = range(BB)
    pair_units = [(s, p, slice(p * LANES, (p + 1) * LANES)) for s in seqs for p in range(n_pairs)]
    head_units = [(s, h, slice(h * LANES, (h + 1) * LANES)) for s in seqs for h in range(ml_heads)]


    row8 = lax.broadcasted_iota(jnp.int32, (SUBLANES, C), 0)
    lane8 = lax.broadcasted_iota(jnp.int32, (SUBLANES, C), 1)
    live = row8 < ml_heads
    triu = (row <= lane).astype(BF16)
    before = row <= lane
    mx, w_inter, e_negm, w_state, w_state_b, dec, beta_t = [], [], [], [], [], [], []
    for s in seqs:
        ig = jnp.where(live, gt_ref[s, 0:SUBLANES, :], 0.0)
        f_pre = jnp.where(live, gt_ref[s, SUBLANES:2 * SUBLANES, :], 30.0)
        l_hi, l_mid, l_lo = _split3(jax.nn.log_sigmoid(f_pre))
        b = _dot(l_hi, triu) + _dot(l_mid, triu) + _dot(l_lo, triu)
        beta = ig - b
        cm = beta
        shift = 1
        while shift < C:
            cm = jnp.maximum(cm, jnp.where(lane8 >= shift, pltpu.roll(cm, shift, 1), -jnp.inf))
            shift *= 2
        m_prev = m_state[s]
        mx.append(jnp.maximum(cm, m_prev))
        mx_last = jnp.broadcast_to(mx[s][:, C - 1:C], (SUBLANES, C))
        w_inter.append(jnp.exp(m_prev - mx[s]))
        e_negm.append(jnp.exp(-(b + mx[s])))
        w_state.append(jnp.exp(beta - mx_last))
        w_state_b.append(w_state[s].astype(BF16))
        dec.append(jnp.exp(m_prev - mx_last))
        m_state[s] = jnp.where(live, jnp.broadcast_to(b[:, C - 1:C], (SUBLANES, C)) + mx_last, 0.0)
        beta_t.append(jnp.concatenate([beta, jnp.zeros((LANES - SUBLANES, C), F32)], axis=0).T)

    hrow = lambda t, h: t[h:h + 1, :]

    rq = [rq_ref[s, :, ps] for s, p, ps in pair_units]
    rk = [rk_ref[s, :, ps] for s, p, ps in pair_units]
    rv = [rv_ref[s, :, ps] for s, p, ps in pair_units]
    r_prev = [r_state[s, p] for s, p, ps in pair_units]
    pu = range(len(pair_units))
    s_ab = [_dot_nt(jnp.concatenate([rq[u] * lo_b, rq[u] * hi_b], axis=0), rk[u]) for u in pu]
    r_read = [_dot(rq[u], r_prev[u].astype(BF16)) for u in pu]
    r_new = [_dot(rkwt_ref[s, ps, :], rv[u]) for u, (s, p, ps) in enumerate(pair_units)]
    hu = range(len(head_units))
    mq = [mq_ref[s, :, hs] for s, h, hs in head_units]
    mk = [mk_ref[s, :, hs] for s, h, hs in head_units]
    mv_t = [mvt_ref[s, hs, :] for s, h, hs in head_units]
    st_prev = [s_state[s, h] for s, h, hs in head_units]
    n_prev = [n_state[s, h] for s, h, hs in head_units]
    a_t = [_dot_nt(mk[u], mq[u]) for u in hu]
    s_read = [_dot_nt(st_prev[u].astype(BF16), mq[u]) for u in hu]
    qn = [_dot_nt(n_prev[u].astype(BF16), mq[u])[0:1, :] for u in hu]
    s_new = [_dot((mv_t[u] * hrow(w_state[s], h)).astype(BF16), mk[u])
             for u, (s, h, hs) in enumerate(head_units)]
    n_new = [_dot(w_state_b[s], mk[u])[h:h + 1, :] for u, (s, h, hs) in enumerate(head_units)]
    for u, (s, p, ps) in enumerate(pair_units):
        r_state[s, p] = cd_ref[p] * r_prev[u] + jnp.where(blockdiag, r_new[u], 0.0)
    for u, (s, h, hs) in enumerate(head_units):
        s_state[s, h] = hrow(dec[s], h) * st_prev[u] + s_new[u]
        n_state[s, h] = hrow(dec[s], h) * n_prev[u] + n_new[u]

    s_ab = [jnp.concatenate([(s_ab[u][0:C] * decay_ref[2 * p]).astype(BF16),
                             (s_ab[u][C:2 * C] * decay_ref[2 * p + 1]).astype(BF16)], axis=1)
            for u, (s, p, ps) in enumerate(pair_units)]
    p_t = [jnp.exp(jnp.where(before, beta_t[s][:, h:h + 1] - hrow(mx[s], h), -jnp.inf)) * a_t[u]
           for u, (s, h, hs) in enumerate(head_units)]

    o = [_dot(s_ab[u], jnp.concatenate([rv[u] * lo_b, rv[u] * hi_b], axis=0)) + r_read[u] * wq_ref[p]
         for u, (s, p, ps) in enumerate(pair_units)]
    num_t = [_dot(mv_t[u].astype(BF16), p_t[u].astype(BF16)) + hrow(w_inter[s], h) * s_read[u]
             for u, (s, h, hs) in enumerate(head_units)]

    for u, (s, p, ps) in enumerate(pair_units):
        sq = o[u] * o[u]
        ms_a = jnp.sum(jnp.where(lo, sq, 0.0), axis=-1, keepdims=True)
        ms_b = jnp.sum(jnp.where(lo, 0.0, sq), axis=-1, keepdims=True)
        ms = jnp.where(lo, ms_a, ms_b) * (1.0 / ret_dh)
        r = o[u] * lax.rsqrt(ms + EPS) * retgn_ref[:, ps]
        y_ref[s, :, ps] = (rg_ref[s, :, ps] * r).astype(y_ref.dtype)
    for u, (s, h, hs) in enumerate(head_units):
        den = jnp.sum(p_t[u], axis=0, keepdims=True) + hrow(w_inter[s], h) * qn[u]
        hh = (num_t[u] * (1.0 / jnp.maximum(jnp.abs(den), hrow(e_negm[s], h)))).T
        hm = _rms(mo_ref[s, :, hs] * hh, mlgn_ref[:, hs])
        y_ref[s, :, ret_w + h * LANES:ret_w + (h + 1) * LANES] = hm.astype(y_ref.dtype)


def _ret_tables(C):
    H = RET_HEADS
    dh = LANES // 2
    log_gamma = jnp.log1p(-(2.0 ** (-5.0 - jnp.arange(H, dtype=F32))))
    idx = jnp.arange(C, dtype=F32)
    rel = idx[:, None] - idx[None, :]
    causal = rel >= 0
    decay = jnp.where(causal, jnp.exp(log_gamma[:, None, None] * jnp.where(causal, rel, 0.0)), 0.0)
    w_state = jnp.exp(log_gamma[:, None] * (C - 1 - idx))
    w_query = jnp.exp(log_gamma[:, None] * (idx + 1.0))
    chunk_decay = jnp.exp(log_gamma * C)
    pair = lambda t: jnp.repeat(t.reshape(H // 2, 2, C).transpose(0, 2, 1), dh, axis=2)
    cd = jnp.repeat(chunk_decay.reshape(H // 2, 2), dh, axis=1)
    cd = jnp.broadcast_to(cd[:, :, None], (H // 2, LANES, LANES))
    ws = pair(w_state).transpose(1, 0, 2).reshape(C, (H // 2) * LANES)
    return decay, ws, pair(w_query), cd


def _mixer(ops, decay, wq, cd, ret_gn, ml_gn, B, S, ret_w, ml_w):
    C = CHUNK
    N = S // C
    n_pairs = ret_w // LANES
    ml_heads = ml_w // LANES
    c2 = lambda b, n: (0, 0)
    c3 = lambda b, n: (0, 0, 0)
    BB = MIXER_SEQS if B % MIXER_SEQS == 0 else 1
    tok = lambda w: pl.BlockSpec((BB, C, w), lambda b, n: (b, n, 0))
    tok_t = lambda r: pl.BlockSpec((BB, r, C), lambda b, n: (b, 0, n))
    y = pl.pallas_call(
        functools.partial(_mixer_kernel, ret_w=ret_w, ml_w=ml_w),
        grid=(B // BB, N),
        in_specs=[
            tok(ret_w), tok(ret_w), tok_t(ret_w), tok(ret_w), tok(ret_w),
            tok(ml_w), tok(ml_w), tok_t(ml_w), tok(ml_w), tok_t(GATE_ROWS),
            pl.BlockSpec((RET_HEADS, C, C), c3),
            pl.BlockSpec((n_pairs, C, LANES), c3),
            pl.BlockSpec((n_pairs, LANES, LANES), c3),
            pl.BlockSpec((1, ret_w), c2),
            pl.BlockSpec((1, ml_w), c2),
        ],
        out_specs=pl.BlockSpec((BB, C, ret_w + ml_w), lambda b, n: (b, n, 0)),
        out_shape=jax.ShapeDtypeStruct((B, S, ret_w + ml_w), BF16),
        scratch_shapes=[
            pltpu.VMEM((BB, n_pairs, LANES, LANES), F32),
            pltpu.VMEM((BB, ml_heads, LANES, LANES), F32),
            pltpu.VMEM((BB, ml_heads, SUBLANES, LANES), F32),
            pltpu.VMEM((BB, SUBLANES, LANES), F32),
        ],
        compiler_params=pltpu.CompilerParams(
            dimension_semantics=("arbitrary", "arbitrary"), vmem_limit_bytes=VMEM_LIMIT),
        name="mixer",
    )(*ops, decay, wq, cd, ret_gn.reshape(1, -1), ml_gn.reshape(1, -1))
    return y.reshape(B * S, ret_w + ml_w)


def _router_kernel(y_ref, x_ref, wo_ref, g_ref, wr_hi_ref, wr_lo_ref, br_ref,
                   h_ref, xn_ref, ri_ref, rf_ref, cnt_ref, run_cnt):
    tm = y_ref.shape[0]

    @pl.when(pl.program_id(0) == 0)
    def _():
        run_cnt[...] = jnp.zeros_like(run_cnt)

    h = x_ref[...] + _dot(y_ref[...], wo_ref[...])
    h_ref[...] = h
    xn = _rms(h, g_ref[...])
    x_hi = xn.astype(BF16)
    xn_ref[...] = x_hi.reshape(xn_ref.shape)
    x_lo = (xn - x_hi.astype(F32)).astype(BF16)
    logits = (_dot_nt(wr_hi_ref[...], x_hi) + _dot_nt(wr_hi_ref[...], x_lo)
              + _dot_nt(wr_lo_ref[...], x_hi) + br_ref[...])
    big = jnp.int32(LANES)
    neg = -jnp.inf
    gl = logits[GROUP_ROW0:GROUP_ROW0 + 8]
    grow = lax.broadcasted_iota(jnp.int32, gl.shape, 0)
    is_g = grow < N_GROUPS
    gl = jnp.where(is_g, gl, neg)
    gmax = jnp.max(gl, axis=0, keepdims=True)
    gsum = jnp.sum(jnp.where(is_g, jnp.exp(gl - gmax), 0.0), axis=0, keepdims=True)
    p_g = 1.0 / gsum
    g_sel = jnp.min(jnp.where(is_g & (gl == gmax), grow, big), axis=0, keepdims=True)
    el = logits[0:N_EXPERTS]
    erow = lax.broadcasted_iota(jnp.int32, el.shape, 0)
    in_grp = (erow // EXPERTS_PER_GROUP) == g_sel
    el = jnp.where(in_grp, el, neg)
    emax = jnp.max(el, axis=0, keepdims=True)
    eexp = jnp.where(in_grp, jnp.exp(el - emax), 0.0)
    prob = eexp / jnp.sum(eexp, axis=0, keepdims=True)
    pm1 = jnp.where(in_grp, prob, -1.0)
    p1 = jnp.max(pm1, axis=0, keepdims=True)
    i1 = jnp.min(jnp.where(pm1 == p1, erow, big), axis=0, keepdims=True)
    pm2 = jnp.where(erow == i1, -1.0, pm1)
    p2 = jnp.max(pm2, axis=0, keepdims=True)
    i2 = jnp.min(jnp.where(pm2 == p2, erow, big), axis=0, keepdims=True)
    denom = p1 + p2
    g1 = p_g * p1 / denom
    g2 = p_g * p2 / denom

    sel1 = erow == i1
    sel2 = erow == i2
    onehot = (sel1 | sel2).astype(BF16)
    r_i = lax.broadcasted_iota(jnp.int32, (tm, tm), 0)
    c_i = lax.broadcasted_iota(jnp.int32, (tm, tm), 1)
    tri = (r_i < c_i).astype(BF16)
    prefix = _dot(onehot, tri) + run_cnt[:, 0:1]
    rank1 = jnp.sum(jnp.where(sel1, prefix, 0.0), axis=0, keepdims=True).astype(jnp.int32)
    rank2 = jnp.sum(jnp.where(sel2, prefix, 0.0), axis=0, keepdims=True).astype(jnp.int32)
    new_cnt = run_cnt[...] + jnp.sum(onehot.astype(F32), axis=1, keepdims=True)
    run_cnt[...] = new_cnt
    cnt_ref[...] = new_cnt.astype(jnp.int32)

    rrow = lax.broadcasted_iota(jnp.int32, (ROUTE_ROWS, tm), 0)
    ri_ref[...] = jnp.where(rrow == 0, rank1 * N_EXPERTS + i1,
                            jnp.where(rrow == 1, rank2 * N_EXPERTS + i2, 0))
    lrow = lax.broadcasted_iota(jnp.int32, (LANES, tm), 0)
    rf_ref[...] = jnp.where(lrow == 0, g1, jnp.where(lrow == 1, g2, 0.0)).T


def _router(y, x2, w_out, g, wr_hi, wr_lo, br, tm=ROUTE_TILE):
    T, D = x2.shape
    const = lambda i: (0, 0)
    tile = lambda i: (i, 0)
    return pl.pallas_call(
        _router_kernel,
        grid=(T // tm,),
        in_specs=[
            pl.BlockSpec((tm, y.shape[1]), tile),
            pl.BlockSpec((tm, D), tile),
            pl.BlockSpec(w_out.shape, const),
            pl.BlockSpec((1, D), const),
            pl.BlockSpec((ROUTER_ROWS, D), const),
            pl.BlockSpec((ROUTER_ROWS, D), const),
            pl.BlockSpec((ROUTER_ROWS, 1), const),
        ],
        out_specs=[
            pl.BlockSpec((tm, D), tile),
            pl.BlockSpec((tm,) + ROW_TILE, lambda i: (i, 0, 0)),
            pl.BlockSpec((None, ROUTE_ROWS, tm), lambda i: (i, 0, 0)),
            pl.BlockSpec((tm, LANES), tile),
            pl.BlockSpec((N_EXPERTS, LANES), const),
        ],
        out_shape=[
            jax.ShapeDtypeStruct((T, D), F32),
            jax.ShapeDtypeStruct((T,) + ROW_TILE, BF16),
            jax.ShapeDtypeStruct((T // tm, ROUTE_ROWS, tm), jnp.int32),
            jax.ShapeDtypeStruct((T, LANES), F32),
            jax.ShapeDtypeStruct((N_EXPERTS, LANES), jnp.int32),
        ],
        scratch_shapes=[pltpu.VMEM((N_EXPERTS, LANES), F32)],
        compiler_params=pltpu.CompilerParams(
            dimension_semantics=("arbitrary",), vmem_limit_bytes=VMEM_LIMIT),
        name="router",
    )(y, x2, w_out, g, wr_hi, wr_lo, br)


def _slot_base(tok0):
    return (tok0 // ROUTE_TILE) * (2 * ROUTE_TILE) + tok0 % ROUTE_TILE


def _route_spans(tm):
    assert tm % ROUTE_TILE == 0 or ROUTE_TILE % tm == 0
    span = min(tm, ROUTE_TILE)
    return [(t0, span) for t0 in range(0, tm, span)]


def _for_each_pad_block(pend_ref, nu_ref, n_blocks, fn):
    blk = EXPERT_BLOCK
    for e in range(N_EXPERTS):
        prev_end = 0 if e == 0 else pend_ref[e - 1]

        @pl.when(pend_ref[e] > prev_end)
        def _():
            fn(pl.multiple_of(pend_ref[e] - blk, blk))

    def tail(j, c):
        fn(pl.multiple_of(j * blk, blk))
        return c

    lax.fori_loop(nu_ref[0], n_blocks, tail, 0)


def _dispatch_kernel(dest_ref, pend_ref, nu_ref, xn_ref, buf_ref, zeros, sem, zsem):
    tm = xn_ref.shape[0]
    blk = EXPERT_BLOCK
    i = pl.program_id(0)

    @pl.when(i == 0)
    def _():
        zeros[...] = jnp.zeros_like(zeros)
        zcopy = lambda row: pltpu.make_async_copy(zeros, buf_ref.at[pl.ds(row, blk)], zsem)
        n_blocks = buf_ref.shape[0] // blk
        _for_each_pad_block(pend_ref, nu_ref, n_blocks, lambda row: zcopy(row).start())
        _for_each_pad_block(pend_ref, nu_ref, n_blocks, lambda row: zcopy(row).wait())

    for t0, span in _route_spans(tm):
        slot0 = _slot_base(i * tm + t0)

        def issue(g, c, t0=t0, slot0=slot0):
            for j in range(SUBLANES):
                for k in range(2):
                    t = g * SUBLANES + j
                    d = dest_ref[slot0 + k * ROUTE_TILE + t]
                    pltpu.make_async_copy(xn_ref.at[t0 + t], buf_ref.at[d], sem).start(priority=k)
            return c

        lax.fori_loop(0, span // SUBLANES, issue, 0)
    for k in range(2):
        pltpu.make_async_copy(buf_ref.at[pl.ds(tm, tm)], buf_ref.at[pl.ds(0, tm)], sem).wait()


def _dispatch(dest, pad_end, n_used, xn, n_rows, tm=ROUTE_TILE):
    T = xn.shape[0]
    return pl.pallas_call(
        _dispatch_kernel,
        grid_spec=pltpu.PrefetchScalarGridSpec(
            num_scalar_prefetch=3,
            grid=(T // tm,),
            in_specs=[pl.BlockSpec((tm,) + ROW_TILE, lambda i, *_: (i, 0, 0))],
            out_specs=pl.BlockSpec(memory_space=pl.ANY),
            scratch_shapes=[
                pltpu.VMEM((EXPERT_BLOCK,) + ROW_TILE, xn.dtype),
                pltpu.SemaphoreType.DMA,
                pltpu.SemaphoreType.DMA,
            ],
        ),
        out_shape=jax.ShapeDtypeStruct((n_rows,) + ROW_TILE, xn.dtype),
        compiler_params=pltpu.CompilerParams(
            dimension_semantics=("arbitrary",), vmem_limit_bytes=VMEM_LIMIT),
        name="dispatch",
    )(dest, pad_end, n_used, xn)


def _expert_kernel(be_ref, nu_ref, first_ref, next_ref, slot_ref, x_ref, w1_hbm, w3_hbm,
                   w2_hbm, y_ref, s1, s3, s2, w1b, w3b, w2b, sems):
    j = pl.program_id(0)
    used = j < nu_ref[0]
    blk = x_ref.shape[0]

    def weight_copies(e, slot):
        return [pltpu.make_async_copy(w.at[e], s.at[slot], sems.at[slot])
                for w, s in ((w1_hbm, s1), (w3_hbm, s3), (w2_hbm, s2))]

    @pl.when(j == 0)
    def _():
        for c in weight_copies(be_ref[0], 0):
            c.start()

    @pl.when(used & (first_ref[j] == 1))
    def _():
        slot = slot_ref[j]
        for c in weight_copies(be_ref[j], slot):
            c.wait()

        @pl.when(next_ref[j] >= 0)
        def _():
            for c in weight_copies(next_ref[j], 1 - slot):
                c.start()

        w1b[...] = s1[slot].astype(BF16)
        w3b[...] = s3[slot].astype(BF16)
        w2b[...] = s2[slot].astype(BF16)

    @pl.when(used)
    def _():
        x = x_ref[...].reshape(blk, -1)
        a = _dot(x, w1b[...])
        hmid = a * jax.nn.sigmoid(a) * _dot(x, w3b[...])
        y_ref[...] = _dot(hmid.astype(BF16), w2b[...]).reshape(y_ref.shape)

    @pl.when(jnp.logical_not(used))
    def _():
        y_ref[...] = jnp.zeros_like(y_ref)


def _experts(block_expert, n_used, first, next_expert, slot, x_buf, w1, w3, w2):
    P = x_buf.shape[0]
    D = w1.shape[1]
    assert x_buf.shape[1:] == ROW_TILE and D == SUBLANES * LANES
    blk = EXPERT_BLOCK
    d_exp = w2.shape[1]
    hbm = pl.BlockSpec(memory_space=pl.ANY)
    return pl.pallas_call(
        _expert_kernel,
        grid_spec=pltpu.PrefetchScalarGridSpec(
            num_scalar_prefetch=5,
            grid=(P // blk,),
            in_specs=[
                pl.BlockSpec((blk,) + ROW_TILE,
                             lambda j, be, nu, *_: (jnp.minimum(j, nu[0] - 1), 0, 0)),
                hbm, hbm, hbm,
            ],
            out_specs=pl.BlockSpec((blk,) + ROW_TILE, lambda j, *_: (j, 0, 0)),
            scratch_shapes=[
                pltpu.VMEM((2, D, d_exp), F32),
                pltpu.VMEM((2, D, d_exp), F32),
                pltpu.VMEM((2, d_exp, D), F32),
                pltpu.VMEM((D, d_exp), BF16),
                pltpu.VMEM((D, d_exp), BF16),
                pltpu.VMEM((d_exp, D), BF16),
                pltpu.SemaphoreType.DMA((2,)),
            ],
        ),
        out_shape=jax.ShapeDtypeStruct((P,) + ROW_TILE, F32),
        compiler_params=pltpu.CompilerParams(
            dimension_semantics=("arbitrary",), vmem_limit_bytes=VMEM_LIMIT),
        name="experts",
    )(block_expert, n_used, first, next_expert, slot, x_buf, w1, w3, w2)


def _final_kernel(dest_ref, h_ref, rf_ref, p_ref, wup_ref, gple_ref, ggate_ref, wgate_ref,
                  gfin_ref, ybuf_ref, out_ref, *scratch, tm):
    row_bufs, sems = scratch[:-1], scratch[-1]
    s = pl.program_id(0)
    D = h_ref.shape[1]
    assert ROUTE_TILE % tm == 0

    def gather(tile, bufs, sem, straight=False):
        slot0 = _slot_base(tile * tm)

        def issue(g, c):
            for j in range(SUBLANES):
                for k in range(2):
                    t = g * SUBLANES + j
                    d = dest_ref[slot0 + k * ROUTE_TILE + t]
                    pltpu.make_async_copy(ybuf_ref.at[d], bufs[k].at[t], sem).start(priority=k)
            return c

        if straight:
            for g in range(tm // SUBLANES):
                issue(g, 0)
        else:
            lax.fori_loop(0, tm // SUBLANES, issue, 0)

    def wait(sem):
        for k in range(2):
            pltpu.make_async_copy(ybuf_ref.at[pl.ds(0, tm)], ybuf_ref.at[pl.ds(tm, tm)], sem).wait()

    def compute(rows, bufs):
        e = _rms(_dot(p_ref[rows, :].astype(BF16), wup_ref[...]), gple_ref[...])
        rf = rf_ref[rows, :]
        h = (h_ref[rows, :] + rf[:, 0:1] * bufs[0][...].reshape(tm, D)
             + rf[:, 1:2] * bufs[1][...].reshape(tm, D))
        gate = jax.nn.sigmoid(_dot(_rms(h, ggate_ref[...]).astype(BF16), wgate_ref[...]))
        h = h + gate * e
        out_ref[rows, :] = _rms(h, gfin_ref[...])

    n_tiles = pl.num_programs(0) * FINAL_TILES
    bufs = [(row_bufs[2 * i], row_bufs[2 * i + 1]) for i in range(FINAL_TILES)]

    @pl.when(s == 0)
    def _():
        for i in range(FINAL_AHEAD):
            gather(i, bufs[i], sems.at[i])

    for i in range(FINAL_TILES):
        wait(sems.at[i])
        nxt = (i + FINAL_AHEAD) % FINAL_TILES
        gather(jnp.minimum(s * FINAL_TILES + i + FINAL_AHEAD, n_tiles - 1), bufs[nxt],
               sems.at[nxt], straight=True)
        compute(slice(i * tm, (i + 1) * tm), bufs[i])

    @pl.when(s == pl.num_programs(0) - 1)
    def _():
        for i in range(FINAL_AHEAD):
            wait(sems.at[i])


def _final(dest, h, rf, p2, w_up, g_ple, g_gate, w_gate, g_fin, y_buf, tm=256):
    T, D = h.shape
    const = lambda i, *_: (0, 0)
    tile = lambda i, *_: (i, 0)
    assert y_buf.shape[1:] == ROW_TILE and D == SUBLANES * LANES
    row_buf = pltpu.VMEM((tm,) + ROW_TILE, F32)
    return pl.pallas_call(
        functools.partial(_final_kernel, tm=tm),
        grid_spec=pltpu.PrefetchScalarGridSpec(
            num_scalar_prefetch=1,
            grid=(T // (FINAL_TILES * tm),),
            in_specs=[
                pl.BlockSpec((FINAL_TILES * tm, D), tile),
                pl.BlockSpec((FINAL_TILES * tm, LANES), tile),
                pl.BlockSpec((FINAL_TILES * tm, p2.shape[1]), tile),
                pl.BlockSpec(w_up.shape, const),
                pl.BlockSpec((1, D), const),
                pl.BlockSpec((1, D), const),
                pl.BlockSpec(w_gate.shape, const),
                pl.BlockSpec((1, D), const),
                pl.BlockSpec(memory_space=pl.ANY),
            ],
            out_specs=pl.BlockSpec((FINAL_TILES * tm, D), tile),
            scratch_shapes=[row_buf] * (2 * FINAL_TILES) + [pltpu.SemaphoreType.DMA((FINAL_TILES,))],
        ),
        out_shape=jax.ShapeDtypeStruct((T, D), F32),
        compiler_params=pltpu.CompilerParams(
            dimension_semantics=("arbitrary",), vmem_limit_bytes=VMEM_LIMIT),
        name="final",
    )(dest, h, rf, p2, w_up, g_ple, g_gate, w_gate, g_fin, y_buf)


def _layer(h2, p2, positions, B, S, attn_norm, w_in, conv_w, conv_b, b_igate, b_fgate, ret_gn,
           ml_gn, w_out, moe_norm, w_group, b_group, w_router, b_router, w1, w3, w2, w_ple_up,
           ple_norm, ple_gate_norm, w_ple_gate, out_norm):
    T, D = h2.shape
    ret_w = ret_gn.shape[0]
    ml_w = ml_gn.shape[0]
    n_main = 4 * ret_w + 4 * ml_w
    row = lambda v: v.reshape(1, -1).astype(F32)

    ml_heads = ml_w // LANES
    w_main = w_in.astype(BF16)
    gate_rows = lambda a: jnp.pad(a, ((0, SUBLANES - ml_heads), (0, 0)))
    w_i, w_f = w_in[:, n_main:n_main + ml_heads].T, w_in[:, n_main + ml_heads:].T
    w_gate_t = jnp.concatenate([gate_rows(w_i), gate_rows(w_f)]).astype(BF16)
    gate_bias = jnp.concatenate([gate_rows(b_igate[:, None]), gate_rows(b_fgate[:, None])])
    cos_t, sin_t = _rope_tables(positions, LANES // 2)
    decay, ws, wq, cd = _ret_tables(CHUNK)
    tm_in = 2 * CHUNK
    ops = _inproj(h2.reshape(B, S, D), row(attn_norm), w_main, w_gate_t, gate_bias.astype(F32),
                  cos_t.reshape(B, S, LANES), sin_t.reshape(B, S, LANES),
                  jnp.tile(ws, (tm_in // CHUNK, 1)), conv_w, conv_b.reshape(1, -1), ret_w, ml_w,
                  tm=tm_in)

    y = _mixer(ops, decay, wq, cd, ret_gn, ml_gn, B, S, ret_w, ml_w)

    pad_rows = lambda a: jnp.pad(a, ((0, ROUTER_ROWS - a.shape[0]), (0, 0)))
    wr = pad_rows(jnp.concatenate([w_router, w_group], axis=1).T)
    wr_hi = wr.astype(BF16)
    wr_lo = (wr - wr_hi.astype(F32)).astype(BF16)
    br = pad_rows(jnp.concatenate([b_router, b_group])[:, None].astype(F32))
    h_mid, xn, ri, rf, counts = _router(y, h2, w_out.astype(BF16), row(moe_norm), wr_hi, wr_lo, br)

    blk = EXPERT_BLOCK
    counts = counts[:, 0]
    padded = (counts + blk - 1) // blk * blk
    pad_end = jnp.cumsum(padded)
    pad_start = pad_end - padded
    n_blocks = (2 * T) // blk + N_EXPERTS
    assert n_blocks <= 256
    block_start = jnp.arange(n_blocks, dtype=jnp.int32) * blk
    block_expert = jnp.minimum(jnp.sum(pad_end[None, :] <= block_start[:, None], axis=1),
                               N_EXPERTS - 1).astype(jnp.int32)
    n_used = (pad_end[-1:] // blk).astype(jnp.int32)
    blk_id = jnp.arange(n_blocks, dtype=jnp.int32)
    is_used = blk_id < n_used[0]
    first = is_used & jnp.concatenate([jnp.ones((1,), bool), block_expert[1:] != block_expert[:-1]])
    seg_end = jnp.sum(jnp.where(block_expert[:, None] == jnp.arange(N_EXPERTS)[None, :],
                                (pad_end // blk)[None, :], 0), axis=1)
    next_blk = jnp.minimum(seg_end, n_blocks - 1)
    next_expert = jnp.sum(jnp.where(next_blk[:, None] == blk_id[None, :], block_expert[None, :], 0),
                          axis=1)
    next_expert = jnp.where(seg_end < n_used[0], next_expert, -1).astype(jnp.int32)
    slot = ((jnp.cumsum(first.astype(jnp.int32)) - 1) % 2).astype(jnp.int32)
    codes = ri[:, 0:2, :].reshape(-1)
    start_blk = jnp.dot(jax.nn.one_hot(codes % N_EXPERTS, N_EXPERTS, dtype=BF16),
                        (pad_start // blk).astype(BF16), preferred_element_type=F32)
    dest = (codes // N_EXPERTS + start_blk.astype(jnp.int32) * blk).astype(jnp.int32)

    x_buf = _dispatch(dest, pad_end.astype(jnp.int32), n_used, xn, n_blocks * blk,
                      tm=min(T, DISPATCH_TILE))
    y_buf = _experts(block_expert, n_used, first.astype(jnp.int32), next_expert, slot,
                     x_buf, w1, w3, w2)

    return _final(dest, h_mid, rf, p2, w_ple_up.astype(BF16), row(ple_norm), row(ple_gate_norm),
                  w_ple_gate.astype(BF16), row(out_norm), y_buf)


def kernel(x, p, positions, attn_norm, w_in, conv_w, conv_b, b_igate, b_fgate, ret_gn, ml_gn,
           w_out, moe_norm, w_group, b_group, w_router, b_router, w1, w3, w2, w_ple_up, ple_norm,
           ple_gate_norm, w_ple_gate, final_norm):
    B, S, D = x.shape
    depth = p.shape[0]
    assert depth == 1, "the final RMSNorm is fused into the layer's last kernel"
    out = _layer(x.reshape(B * S, D), p[0].reshape(B * S, -1), positions, B, S,
                 attn_norm[0], w_in[0], conv_w[0], conv_b[0], b_igate[0], b_fgate[0], ret_gn[0],
                 ml_gn[0], w_out[0], moe_norm[0], w_group[0], b_group[0], w_router[0],
                 b_router[0], w1[0], w3[0], w2[0], w_ple_up[0], ple_norm[0], ple_gate_norm[0],
                 w_ple_gate[0], final_norm)
    return out.reshape(B, S, D)
```

```python
import functools

import jax
import jax.numpy as jnp
from jax import lax
from jax.experimental import pallas as pl
from jax.experimental.pallas import tpu as pltpu

F32 = jnp.float32
BF16 = jnp.bfloat16

RET_HEADS = 8
ML_HEADS = 4
CHUNK = 128
CONV_W = 4
ROPE_BASE = 10000.0
N_GROUPS = 4
EXPERTS_PER_GROUP = 8
N_EXPERTS = N_GROUPS * EXPERTS_PER_GROUP
EPS = 1e-6

LANES = 128
SUBLANES = 8
MXU_COLS = 256
MIXER_SEQS = 4
FINAL_TILES = 4
FINAL_AHEAD = 3
VMEM_LIMIT = 56 * 1024 * 1024
EXPERT_BLOCK = 512
GROUP_ROW0 = N_EXPERTS
ROUTER_ROWS = 64
ROUTE_TILE = 1024
ROUTE_ROWS = 8
DISPATCH_TILE = 4 * ROUTE_TILE
GATE_ROWS = 2 * SUBLANES
ROW_TILE = (SUBLANES, LANES)


def _rms(x, g):
    return x * lax.rsqrt(jnp.mean(x * x, axis=-1, keepdims=True) + EPS) * g


def _dot(a, b):
    return jnp.dot(a, b, preferred_element_type=F32)


def _dot_nt(a, b):
    return lax.dot_general(a, b, (((1,), (1,)), ((), ())), preferred_element_type=F32)


def _split3(x):
    hi = x.astype(BF16)
    r1 = x - hi.astype(F32)
    mid = r1.astype(BF16)
    lo = (r1 - mid.astype(F32)).astype(BF16)
    return hi, mid, lo


def _rope_kernel(pos_ref, freq_ref, sign_ref, cos_ref, sin_ref, *, half):
    ang = pos_ref[...].astype(F32) * freq_ref[...]
    cos_c = jnp.cos(ang)
    sin_c = jnp.sin(ang)
    group = lax.broadcasted_iota(jnp.int32, ang.shape, 1) // half
    per_row = LANES // half
    for s in range(per_row):
        def spread(t):
            out = t
            for g in range(per_row):
                if g != s:
                    out = jnp.where(group == g, pltpu.roll(t, (half * (g - s)) % LANES, 1), out)
            return out
        cos_ref[s] = spread(cos_c)
        sin_ref[s] = spread(sin_c) * sign_ref[...]


def _rope_tables(positions, dh):
    half = dh // 2
    per_row = LANES // half
    T = positions.size
    rows = T // per_row
    tr = min(rows, 1024)
    freqs = ROPE_BASE ** (-jnp.arange(half, dtype=F32) / half)
    pos_c = jnp.repeat(positions.reshape(per_row, rows).T, half, axis=1)
    sign = jnp.tile(jnp.concatenate([-jnp.ones((half,), F32), jnp.ones((half,), F32)]),
                    LANES // dh)[None, :]
    const = pl.BlockSpec((1, LANES), lambda i: (0, 0))
    out = pl.BlockSpec((per_row, tr, LANES), lambda i: (0, i, 0))
    cos_t, sin_t = pl.pallas_call(
        functools.partial(_rope_kernel, half=half),
        grid=(rows // tr,),
        in_specs=[pl.BlockSpec((tr, LANES), lambda i: (i, 0)), const, const],
        out_specs=[out, out],
        out_shape=[jax.ShapeDtypeStruct((per_row, rows, LANES), F32)] * 2,
        name="rope",
    )(pos_c, jnp.tile(freqs, per_row)[None, :], sign)
    return cos_t.reshape(T, LANES), sin_t.reshape(T, LANES)


def _inproj_kernel(x_ref, g_ref, wm_ref, wgt_ref, gb_ref, cos_ref, sin_ref, ws_ref, convw_ref,
                   convb_ref, rq_ref, rk_ref, rkwt_ref, rv_ref, rg_ref, mq_ref, mk_ref, mvt_ref,
                   mo_ref, gt_ref, carry, *, ret_w, ml_w):
    tm = x_ref.shape[0]
    ret_dh = LANES // 2

    @pl.when(pl.program_id(1) == 0)
    def _():
        carry[0:SUBLANES] = jnp.zeros((SUBLANES, carry.shape[1]), F32)

    xn = _rms(x_ref[...], g_ref[...]).astype(BF16)
    proj = lambda o, w: _dot(xn, wm_ref[:, o:o + w])
    tiles = lambda w: [slice(t, t + LANES) for t in range(0, w, LANES)]

    lane = lax.broadcasted_iota(jnp.int32, (tm, LANES), 1)
    first_half = (lane % ret_dh) < (ret_dh // 2)
    cos_t = cos_ref[...]
    sin_t = sin_ref[...]

    def rot(t):
        swapped = jnp.where(first_half, pltpu.roll(t, LANES - ret_dh // 2, 1),
                            pltpu.roll(t, ret_dh // 2, 1))
        return t * cos_t + swapped * sin_t

    for c0 in range(0, ret_w, MXU_COLS):
        rq = proj(c0, MXU_COLS)
        for ps in tiles(MXU_COLS):
            rq_ref[:, c0 + ps.start:c0 + ps.stop] = rot(rq[:, ps]).astype(BF16)
    for c0 in range(0, ret_w, MXU_COLS):
        rk = proj(ret_w + c0, MXU_COLS)
        for ps in tiles(MXU_COLS):
            cs = slice(c0 + ps.start, c0 + ps.stop)
            k = rot(rk[:, ps]) * (ret_dh ** -0.5)
            rk_ref[:, cs] = k.astype(BF16)
            rkwt_ref[cs, :] = (k * ws_ref[:, cs]).T.astype(BF16)
    for c0 in range(0, ret_w, MXU_COLS):
        cs = slice(c0, c0 + MXU_COLS)
        rv_ref[:, cs] = proj(2 * ret_w + c0, MXU_COLS).astype(BF16)
    for c0 in range(0, ret_w, MXU_COLS):
        g = proj(3 * ret_w + c0, MXU_COLS)
        rg_ref[:, c0:c0 + MXU_COLS] = g * jax.nn.sigmoid(g)

    o_mq = 4 * ret_w
    for c0 in range(0, 2 * ml_w, MXU_COLS):
        cs = slice(c0, c0 + MXU_COLS)
        xq = proj(o_mq + c0, MXU_COLS)
        carry[SUBLANES:SUBLANES + tm, cs] = xq
        acc = xq * convw_ref[CONV_W - 1:CONV_W, cs] + convb_ref[:, cs]
        for s in range(1, CONV_W):
            shifted = carry[SUBLANES - s:SUBLANES - s + tm, cs]
            acc = acc + shifted * convw_ref[CONV_W - 1 - s:CONV_W - s, cs]
        carry[0:SUBLANES, cs] = xq[tm - SUBLANES:tm]
        act = acc * jax.nn.sigmoid(acc)
        if c0 < ml_w:
            mq_ref[:, cs] = act.astype(BF16)
        else:
            mk_ref[:, c0 - ml_w:c0 - ml_w + MXU_COLS] = (act * (LANES ** -0.5)).astype(BF16)
    for c0 in range(0, ml_w, MXU_COLS):
        mv = proj(o_mq + 2 * ml_w + c0, MXU_COLS)
        for hs in tiles(MXU_COLS):
            mvt_ref[c0 + hs.start:c0 + hs.stop, :] = mv[:, hs].T
    for c0 in range(0, ml_w, MXU_COLS):
        mo_ref[:, c0:c0 + MXU_COLS] = jax.nn.sigmoid(proj(o_mq + 3 * ml_w + c0, MXU_COLS))
    gt_ref[...] = _dot_nt(wgt_ref[...], xn) + gb_ref[...]


def _inproj(x3, g, w_main, w_gate_t, gate_bias, cos_t, sin_t, ws, conv_w, conv_b, ret_w, ml_w,
            tm=256):
    B, S, D = x3.shape
    n_main = 4 * (ret_w + ml_w)
    const = lambda b, i: (0, 0)
    tok = lambda w: pl.BlockSpec((None, tm, w), lambda b, i: (b, i, 0))
    tok_t = lambda r: pl.BlockSpec((None, r, tm), lambda b, i: (b, 0, i))
    act = lambda w, dt: jax.ShapeDtypeStruct((B, S, w), dt)
    act_t = lambda r, dt: jax.ShapeDtypeStruct((B, r, S), dt)
    return pl.pallas_call(
        functools.partial(_inproj_kernel, ret_w=ret_w, ml_w=ml_w),
        grid=(B, S // tm),
        in_specs=[
            tok(D),
            pl.BlockSpec((1, D), const),
            pl.BlockSpec((D, n_main), const),
            pl.BlockSpec((GATE_ROWS, D), const),
            pl.BlockSpec((GATE_ROWS, 1), const),
            tok(LANES),
            tok(LANES),
            pl.BlockSpec((tm, ret_w), const),
            pl.BlockSpec((CONV_W, 2 * ml_w), const),
            pl.BlockSpec((1, 2 * ml_w), const),
        ],
        out_specs=[tok(ret_w), tok(ret_w), tok_t(ret_w), tok(ret_w), tok(ret_w),
                   tok(ml_w), tok(ml_w), tok_t(ml_w), tok(ml_w), tok_t(GATE_ROWS)],
        out_shape=[
            act(ret_w, BF16),
            act(ret_w, BF16),
            act_t(ret_w, BF16),
            act(ret_w, BF16),
            act(ret_w, F32),
            act(ml_w, BF16),
            act(ml_w, BF16),
            act_t(ml_w, F32),
            act(ml_w, F32),
            act_t(GATE_ROWS, F32),
        ],
        scratch_shapes=[pltpu.VMEM((SUBLANES + tm, 2 * ml_w), F32)],
        compiler_params=pltpu.CompilerParams(
            dimension_semantics=("arbitrary", "arbitrary"), vmem_limit_bytes=VMEM_LIMIT),
        name="inproj",
    )(x3, g, w_main, w_gate_t, gate_bias, cos_t, sin_t, ws, conv_w, conv_b)


def _mixer_kernel(rq_ref, rk_ref, rkwt_ref, rv_ref, rg_ref, mq_ref, mk_ref, mvt_ref, mo_ref,
                  gt_ref, decay_ref, wq_ref, cd_ref, retgn_ref, mlgn_ref,
                  y_ref, r_state, s_state, n_state, m_state, *, ret_w, ml_w):
    BB, C = rq_ref.shape[0], rq_ref.shape[1]
    n_pairs = ret_w // LANES
    ml_heads = ml_w // LANES
    ret_dh = LANES // 2

    @pl.when(pl.program_id(1) == 0)
    def _():
        r_state[...] = jnp.zeros_like(r_state)
        s_state[...] = jnp.zeros_like(s_state)
        n_state[...] = jnp.zeros_like(n_state)
        m_state[...] = jnp.zeros_like(m_state)

    lane = lax.broadcasted_iota(jnp.int32, (C, LANES), 1)
    row = lax.broadcasted_iota(jnp.int32, (C, LANES), 0)
    assert C == LANES
    lo = lane < ret_dh
    blockdiag = (row < ret_dh) == lo
    lo_b = jnp.where(lo, 1.0, 0.0).astype(BF16)
    hi_b = jnp.where(lo, 0.0, 1.0).astype(BF16)
    seqs = range(BB)
    pair_units = [(s, p, slice(p * LANES, (p + 1) * LANES)) for s in seqs for p in range(n_pairs)]
    head_units = [(s, h, slice(h * LANES, (h + 1) * LANES)) for s in seqs for h in range(ml_heads)]


    row8 = lax.broadcasted_iota(jnp.int32, (SUBLANES, C), 0)
    lane8 = lax.broadcasted_iota(jnp.int32, (SUBLANES, C), 1)
    live = row8 < ml_heads
    triu = (row <= lane).astype(BF16)
    before = row <= lane
    mx, w_inter, e_negm, w_state, w_state_b, dec, beta_t = [], [], [], [], [], [], []
    for s in seqs:
        ig = jnp.where(live, gt_ref[s, 0:SUBLANES, :], 0.0)
        f_pre = jnp.where(live, gt_ref[s, SUBLANES:2 * SUBLANES, :], 30.0)
        l_hi, l_mid, l_lo = _split3(jax.nn.log_sigmoid(f_pre))
        b = _dot(l_hi, triu) + _dot(l_mid, triu) + _dot(l_lo, triu)
        beta = ig - b
        cm = beta
        shift = 1
        while shift < C:
            cm = jnp.maximum(cm, jnp.where(lane8 >= shift, pltpu.roll(cm, shift, 1), -jnp.inf))
            shift *= 2
        m_prev = m_state[s]
        mx.append(jnp.maximum(cm, m_prev))
        mx_last = jnp.broadcast_to(mx[s][:, C - 1:C], (SUBLANES, C))
        w_inter.append(jnp.exp(m_prev - mx[s]))
        e_negm.append(jnp.exp(-(b + mx[s])))
        w_state.append(jnp.exp(beta - mx_last))
        w_state_b.append(w_state[s].astype(BF16))
        dec.append(jnp.exp(m_prev - mx_last))
        m_state[s] = jnp.where(live, jnp.broadcast_to(b[:, C - 1:C], (SUBLANES, C)) + mx_last, 0.0)
        beta_t.append(jnp.concatenate([beta, jnp.zeros((LANES - SUBLANES, C), F32)], axis=0).T)

    hrow = lambda t, h: t[h:h + 1, :]

    rq = [rq_ref[s, :, ps] for s, p, ps in pair_units]
    rk = [rk_ref[s, :, ps] for s, p, ps in pair_units]
    rv = [rv_ref[s, :, ps] for s, p, ps in pair_units]
    r_prev = [r_state[s, p] for s, p, ps in pair_units]
    pu = range(len(pair_units))
    s_ab = [_dot_nt(jnp.concatenate([rq[u] * lo_b, rq[u] * hi_b], axis=0), rk[u]) for u in pu]
    r_read = [_dot(rq[u], r_prev[u].astype(BF16)) for u in pu]
    r_new = [_dot(rkwt_ref[s, ps, :], rv[u]) for u, (s, p, ps) in enumerate(pair_units)]
    hu = range(len(head_units))
    mq = [mq_ref[s, :, hs] for s, h, hs in head_units]
    mk = [mk_ref[s, :, hs] for s, h, hs in head_units]
    mv_t = [mvt_ref[s, hs, :] for s, h, hs in head_units]
    st_prev = [s_state[s, h] for s, h, hs in head_units]
    n_prev = [n_state[s, h] for s, h, hs in head_units]
    a_t = [_dot_nt(mk[u], mq[u]) for u in hu]
    s_read = [_dot_nt(st_prev[u].astype(BF16), mq[u]) for u in hu]
    qn = [_dot_nt(n_prev[u].astype(BF16), mq[u])[0:1, :] for u in hu]
    s_new = [_dot((mv_t[u] * hrow(w_state[s], h)).astype(BF16), mk[u])
             for u, (s, h, hs) in enumerate(head_units)]
    n_new = [_dot(w_state_b[s], mk[u])[h:h + 1, :] for u, (s, h, hs) in enumerate(head_units)]
    for u, (s, p, ps) in enumerate(pair_units):
        r_state[s, p] = cd_ref[p] * r_prev[u] + jnp.where(blockdiag, r_new[u], 0.0)
    for u, (s, h, hs) in enumerate(head_units):
        s_state[s, h] = hrow(dec[s], h) * st_prev[u] + s_new[u]
        n_state[s, h] = hrow(dec[s], h) * n_prev[u] + n_new[u]

    s_ab = [jnp.concatenate([(s_ab[u][0:C] * decay_ref[2 * p]).astype(BF16),
                             (s_ab[u][C:2 * C] * decay_ref[2 * p + 1]).astype(BF16)], axis=1)
            for u, (s, p, ps) in enumerate(pair_units)]
    p_t = [jnp.exp(jnp.where(before, beta_t[s][:, h:h + 1] - hrow(mx[s], h), -jnp.inf)) * a_t[u]
           for u, (s, h, hs) in enumerate(head_units)]

    o = [_dot(s_ab[u], jnp.concatenate([rv[u] * lo_b, rv[u] * hi_b], axis=0)) + r_read[u] * wq_ref[p]
         for u, (s, p, ps) in enumerate(pair_units)]
    num_t = [_dot(mv_t[u].astype(BF16), p_t[u].astype(BF16)) + hrow(w_inter[s], h) * s_read[u]
             for u, (s, h, hs) in enumerate(head_units)]

    for u, (s, p, ps) in enumerate(pair_units):
        sq = o[u] * o[u]
        ms_a = jnp.sum(jnp.where(lo, sq, 0.0), axis=-1, keepdims=True)
        ms_b = jnp.sum(jnp.where(lo, 0.0, sq), axis=-1, keepdims=True)
        ms = jnp.where(lo, ms_a, ms_b) * (1.0 / ret_dh)
        r = o[u] * lax.rsqrt(ms + EPS) * retgn_ref[:, ps]
        y_ref[s, :, ps] = (rg_ref[s, :, ps] * r).astype(y_ref.dtype)
    for u, (s, h, hs) in enumerate(head_units):
        den = jnp.sum(p_t[u], axis=0, keepdims=True) + hrow(w_inter[s], h) * qn[u]
        hh = (num_t[u] * (1.0 / jnp.maximum(jnp.abs(den), hrow(e_negm[s], h)))).T
        hm = _rms(mo_ref[s, :, hs] * hh, mlgn_ref[:, hs])
        y_ref[s, :, ret_w + h * LANES:ret_w + (h + 1) * LANES] = hm.astype(y_ref.dtype)


def _ret_tables(C):
    H = RET_HEADS
    dh = LANES // 2
    log_gamma = jnp.log1p(-(2.0 ** (-5.0 - jnp.arange(H, dtype=F32))))
    idx = jnp.arange(C, dtype=F32)
    rel = idx[:, None] - idx[None, :]
    causal = rel >= 0
    decay = jnp.where(causal, jnp.exp(log_gamma[:, None, None] * jnp.where(causal, rel, 0.0)), 0.0)
    w_state = jnp.exp(log_gamma[:, None] * (C - 1 - idx))
    w_query = jnp.exp(log_gamma[:, None] * (idx + 1.0))
    chunk_decay = jnp.exp(log_gamma * C)
    pair = lambda t: jnp.repeat(t.reshape(H // 2, 2, C).transpose(0, 2, 1), dh, axis=2)
    cd = jnp.repeat(chunk_decay.reshape(H // 2, 2), dh, axis=1)
    cd = jnp.broadcast_to(cd[:, :, None], (H // 2, LANES, LANES))
    ws = pair(w_state).transpose(1, 0, 2).reshape(C, (H // 2) * LANES)
    return decay, ws, pair(w_query), cd


def _mixer(ops, decay, wq, cd, ret_gn, ml_gn, B, S, ret_w, ml_w):
    C = CHUNK
    N = S // C
    n_pairs = ret_w // LANES
    ml_heads = ml_w // LANES
    c2 = lambda b, n: (0, 0)
    c3 = lambda b, n: (0, 0, 0)
    BB = MIXER_SEQS if B % MIXER_SEQS == 0 else 1
    tok = lambda w: pl.BlockSpec((BB, C, w), lambda b, n: (b, n, 0))
    tok_t = lambda r: pl.BlockSpec((BB, r, C), lambda b, n: (b, 0, n))
    y = pl.pallas_call(
        functools.partial(_mixer_kernel, ret_w=ret_w, ml_w=ml_w),
        grid=(B // BB, N),
        in_specs=[
            tok(ret_w), tok(ret_w), tok_t(ret_w), tok(ret_w), tok(ret_w),
            tok(ml_w), tok(ml_w), tok_t(ml_w), tok(ml_w), tok_t(GATE_ROWS),
            pl.BlockSpec((RET_HEADS, C, C), c3),
            pl.BlockSpec((n_pairs, C, LANES), c3),
            pl.BlockSpec((n_pairs, LANES, LANES), c3),
            pl.BlockSpec((1, ret_w), c2),
            pl.BlockSpec((1, ml_w), c2),
        ],
        out_specs=pl.BlockSpec((BB, C, ret_w + ml_w), lambda b, n: (b, n, 0)),
        out_shape=jax.ShapeDtypeStruct((B, S, ret_w + ml_w), BF16),
        scratch_shapes=[
            pltpu.VMEM((BB, n_pairs, LANES, LANES), F32),
            pltpu.VMEM((BB, ml_heads, LANES, LANES), F32),
            pltpu.VMEM((BB, ml_heads, SUBLANES, LANES), F32),
            pltpu.VMEM((BB, SUBLANES, LANES), F32),
        ],
        compiler_params=pltpu.CompilerParams(
            dimension_semantics=("arbitrary", "arbitrary"), vmem_limit_bytes=VMEM_LIMIT),
        name="mixer",
    )(*ops, decay, wq, cd, ret_gn.reshape(1, -1), ml_gn.reshape(1, -1))
    return y.reshape(B * S, ret_w + ml_w)


def _router_kernel(y_ref, x_ref, wo_ref, g_ref, wr_hi_ref, wr_lo_ref, br_ref,
                   h_ref, xn_ref, ri_ref, rf_ref, cnt_ref, run_cnt):
    tm = y_ref.shape[0]

    @pl.when(pl.program_id(0) == 0)
    def _():
        run_cnt[...] = jnp.zeros_like(run_cnt)

    h = x_ref[...] + _dot(y_ref[...], wo_ref[...])
    h_ref[...] = h
    xn = _rms(h, g_ref[...])
    x_hi = xn.astype(BF16)
    xn_ref[...] = x_hi.reshape(xn_ref.shape)
    x_lo = (xn - x_hi.astype(F32)).astype(BF16)
    logits = (_dot_nt(wr_hi_ref[...], x_hi) + _dot_nt(wr_hi_ref[...], x_lo)
              + _dot_nt(wr_lo_ref[...], x_hi) + br_ref[...])
    big = jnp.int32(LANES)
    neg = -jnp.inf
    gl = logits[GROUP_ROW0:GROUP_ROW0 + 8]
    grow = lax.broadcasted_iota(jnp.int32, gl.shape, 0)
    is_g = grow < N_GROUPS
    gl = jnp.where(is_g, gl, neg)
    gmax = jnp.max(gl, axis=0, keepdims=True)
    gsum = jnp.sum(jnp.where(is_g, jnp.exp(gl - gmax), 0.0), axis=0, keepdims=True)
    p_g = 1.0 / gsum
    g_sel = jnp.min(jnp.where(is_g & (gl == gmax), grow, big), axis=0, keepdims=True)
    el = logits[0:N_EXPERTS]
    erow = lax.broadcasted_iota(jnp.int32, el.shape, 0)
    in_grp = (erow // EXPERTS_PER_GROUP) == g_sel
    el = jnp.where(in_grp, el, neg)
    emax = jnp.max(el, axis=0, keepdims=True)
    eexp = jnp.where(in_grp, jnp.exp(el - emax), 0.0)
    prob = eexp / jnp.sum(eexp, axis=0, keepdims=True)
    pm1 = jnp.where(in_grp, prob, -1.0)
    p1 = jnp.max(pm1, axis=0, keepdims=True)
    i1 = jnp.min(jnp.where(pm1 == p1, erow, big), axis=0, keepdims=True)
    pm2 = jnp.where(erow == i1, -1.0, pm1)
    p2 = jnp.max(pm2, axis=0, keepdims=True)
    i2 = jnp.min(jnp.where(pm2 == p2, erow, big), axis=0, keepdims=True)
    denom = p1 + p2
    g1 = p_g * p1 / denom
    g2 = p_g * p2 / denom

    sel1 = erow == i1
    sel2 = erow == i2
    onehot = (sel1 | sel2).astype(BF16)
    r_i = lax.broadcasted_iota(jnp.int32, (tm, tm), 0)
    c_i = lax.broadcasted_iota(jnp.int32, (tm, tm), 1)
    tri = (r_i < c_i).astype(BF16)
    prefix = _dot(onehot, tri) + run_cnt[:, 0:1]
    rank1 = jnp.sum(jnp.where(sel1, prefix, 0.0), axis=0, keepdims=True).astype(jnp.int32)
    rank2 = jnp.sum(jnp.where(sel2, prefix, 0.0), axis=0, keepdims=True).astype(jnp.int32)
    new_cnt = run_cnt[...] + jnp.sum(onehot.astype(F32), axis=1, keepdims=True)
    run_cnt[...] = new_cnt
    cnt_ref[...] = new_cnt.astype(jnp.int32)

    rrow = lax.broadcasted_iota(jnp.int32, (ROUTE_ROWS, tm), 0)
    ri_ref[...] = jnp.where(rrow == 0, rank1 * N_EXPERTS + i1,
                            jnp.where(rrow == 1, rank2 * N_EXPERTS + i2, 0))
    lrow = lax.broadcasted_iota(jnp.int32, (LANES, tm), 0)
    rf_ref[...] = jnp.where(lrow == 0, g1, jnp.where(lrow == 1, g2, 0.0)).T


def _router(y, x2, w_out, g, wr_hi, wr_lo, br, tm=ROUTE_TILE):
    T, D = x2.shape
    const = lambda i: (0, 0)
    tile = lambda i: (i, 0)
    return pl.pallas_call(
        _router_kernel,
        grid=(T // tm,),
        in_specs=[
            pl.BlockSpec((tm, y.shape[1]), tile),
            pl.BlockSpec((tm, D), tile),
            pl.BlockSpec(w_out.shape, const),
            pl.BlockSpec((1, D), const),
            pl.BlockSpec((ROUTER_ROWS, D), const),
            pl.BlockSpec((ROUTER_ROWS, D), const),
            pl.BlockSpec((ROUTER_ROWS, 1), const),
        ],
        out_specs=[
            pl.BlockSpec((tm, D), tile),
            pl.BlockSpec((tm,) + ROW_TILE, lambda i: (i, 0, 0)),
            pl.BlockSpec((None, ROUTE_ROWS, tm), lambda i: (i, 0, 0)),
            pl.BlockSpec((tm, LANES), tile),
            pl.BlockSpec((N_EXPERTS, LANES), const),
        ],
        out_shape=[
            jax.ShapeDtypeStruct((T, D), F32),
            jax.ShapeDtypeStruct((T,) + ROW_TILE, BF16),
            jax.ShapeDtypeStruct((T // tm, ROUTE_ROWS, tm), jnp.int32),
            jax.ShapeDtypeStruct((T, LANES), F32),
            jax.ShapeDtypeStruct((N_EXPERTS, LANES), jnp.int32),
        ],
        scratch_shapes=[pltpu.VMEM((N_EXPERTS, LANES), F32)],
        compiler_params=pltpu.CompilerParams(
            dimension_semantics=("arbitrary",), vmem_limit_bytes=VMEM_LIMIT),
        name="router",
    )(y, x2, w_out, g, wr_hi, wr_lo, br)


def _slot_base(tok0):
    return (tok0 // ROUTE_TILE) * (2 * ROUTE_TILE) + tok0 % ROUTE_TILE


def _route_spans(tm):
    assert tm % ROUTE_TILE == 0 or ROUTE_TILE % tm == 0
    span = min(tm, ROUTE_TILE)
    return [(t0, span) for t0 in range(0, tm, span)]


def _for_each_pad_block(pend_ref, nu_ref, n_blocks, fn):
    blk = EXPERT_BLOCK
    for e in range(N_EXPERTS):
        prev_end = 0 if e == 0 else pend_ref[e - 1]

        @pl.when(pend_ref[e] > prev_end)
        def _():
            fn(pl.multiple_of(pend_ref[e] - blk, blk))

    def tail(j, c):
        fn(pl.multiple_of(j * blk, blk))
        return c

    lax.fori_loop(nu_ref[0], n_blocks, tail, 0)


def _dispatch_kernel(dest_ref, pend_ref, nu_ref, xn_ref, buf_ref, zeros, sem, zsem):
    tm = xn_ref.shape[0]
    blk = EXPERT_BLOCK
    i = pl.program_id(0)

    @pl.when(i == 0)
    def _():
        zeros[...] = jnp.zeros_like(zeros)
        zcopy = lambda row: pltpu.make_async_copy(zeros, buf_ref.at[pl.ds(row, blk)], zsem)
        n_blocks = buf_ref.shape[0] // blk
        _for_each_pad_block(pend_ref, nu_ref, n_blocks, lambda row: zcopy(row).start())
        _for_each_pad_block(pend_ref, nu_ref, n_blocks, lambda row: zcopy(row).wait())

    for t0, span in _route_spans(tm):
        slot0 = _slot_base(i * tm + t0)

        def issue(g, c, t0=t0, slot0=slot0):
            for j in range(SUBLANES):
                for k in range(2):
                    t = g * SUBLANES + j
                    d = dest_ref[slot0 + k * ROUTE_TILE + t]
                    pltpu.make_async_copy(xn_ref.at[t0 + t], buf_ref.at[d], sem).start(priority=k)
            return c

        lax.fori_loop(0, span // SUBLANES, issue, 0)
    for k in range(2):
        pltpu.make_async_copy(buf_ref.at[pl.ds(tm, tm)], buf_ref.at[pl.ds(0, tm)], sem).wait()


def _dispatch(dest, pad_end, n_used, xn, n_rows, tm=ROUTE_TILE):
    T = xn.shape[0]
    return pl.pallas_call(
        _dispatch_kernel,
        grid_spec=pltpu.PrefetchScalarGridSpec(
            num_scalar_prefetch=3,
            grid=(T // tm,),
            in_specs=[pl.BlockSpec((tm,) + ROW_TILE, lambda i, *_: (i, 0, 0))],
            out_specs=pl.BlockSpec(memory_space=pl.ANY),
            scratch_shapes=[
                pltpu.VMEM((EXPERT_BLOCK,) + ROW_TILE, xn.dtype),
                pltpu.SemaphoreType.DMA,
                pltpu.SemaphoreType.DMA,
            ],
        ),
        out_shape=jax.ShapeDtypeStruct((n_rows,) + ROW_TILE, xn.dtype),
        compiler_params=pltpu.CompilerParams(
            dimension_semantics=("arbitrary",), vmem_limit_bytes=VMEM_LIMIT),
        name="dispatch",
    )(dest, pad_end, n_used, xn)


def _expert_kernel(be_ref, nu_ref, first_ref, next_ref, slot_ref, x_ref, w1_hbm, w3_hbm,
                   w2_hbm, y_ref, s1, s3, s2, w1b, w3b, w2b, sems):
    j = pl.program_id(0)
    used = j < nu_ref[0]
    blk = x_ref.shape[0]

    def weight_copies(e, slot):
        return [pltpu.make_async_copy(w.at[e], s.at[slot], sems.at[slot])
                for w, s in ((w1_hbm, s1), (w3_hbm, s3), (w2_hbm, s2))]

    @pl.when(j == 0)
    def _():
        for c in weight_copies(be_ref[0], 0):
            c.start()

    @pl.when(used & (first_ref[j] == 1))
    def _():
        slot = slot_ref[j]
        for c in weight_copies(be_ref[j], slot):
            c.wait()

        @pl.when(next_ref[j] >= 0)
        def _():
            for c in weight_copies(next_ref[j], 1 - slot):
                c.start()

        w1b[...] = s1[slot].astype(BF16)
        w3b[...] = s3[slot].astype(BF16)
        w2b[...] = s2[slot].astype(BF16)

    @pl.when(used)
    def _():
        x = x_ref[...].reshape(blk, -1)
        a = _dot(x, w1b[...])
        hmid = a * jax.nn.sigmoid(a) * _dot(x, w3b[...])
        y_ref[...] = _dot(hmid.astype(BF16), w2b[...]).astype(y_ref.dtype).reshape(y_ref.shape)

    @pl.when(jnp.logical_not(used))
    def _():
        y_ref[...] = jnp.zeros_like(y_ref)


def _experts(block_expert, n_used, first, next_expert, slot, x_buf, w1, w3, w2):
    P = x_buf.shape[0]
    D = w1.shape[1]
    assert x_buf.shape[1:] == ROW_TILE and D == SUBLANES * LANES
    blk = EXPERT_BLOCK
    d_exp = w2.shape[1]
    hbm = pl.BlockSpec(memory_space=pl.ANY)
    return pl.pallas_call(
        _expert_kernel,
        grid_spec=pltpu.PrefetchScalarGridSpec(
            num_scalar_prefetch=5,
            grid=(P // blk,),
            in_specs=[
                pl.BlockSpec((blk,) + ROW_TILE,
                             lambda j, be, nu, *_: (jnp.minimum(j, nu[0] - 1), 0, 0)),
                hbm, hbm, hbm,
            ],
            out_specs=pl.BlockSpec((blk,) + ROW_TILE, lambda j, *_: (j, 0, 0)),
            scratch_shapes=[
                pltpu.VMEM((2, D, d_exp), F32),
                pltpu.VMEM((2, D, d_exp), F32),
                pltpu.VMEM((2, d_exp, D), F32),
                pltpu.VMEM((D, d_exp), BF16),
                pltpu.VMEM((D, d_exp), BF16),
                pltpu.VMEM((d_exp, D), BF16),
                pltpu.SemaphoreType.DMA((2,)),
            ],
        ),
        out_shape=jax.ShapeDtypeStruct((P,) + ROW_TILE, BF16),
        compiler_params=pltpu.CompilerParams(
            dimension_semantics=("arbitrary",), vmem_limit_bytes=VMEM_LIMIT),
        name="experts",
    )(block_expert, n_used, first, next_expert, slot, x_buf, w1, w3, w2)


def _final_kernel(dest_ref, h_ref, rf_ref, p_ref, wup_ref, gple_ref, ggate_ref, wgate_ref,
                  gfin_ref, ybuf_ref, out_ref, *scratch, tm):
    row_bufs, sems = scratch[:-1], scratch[-1]
    s = pl.program_id(0)
    D = h_ref.shape[1]
    assert ROUTE_TILE % tm == 0

    def gather(tile, bufs, sem, straight=False):
        slot0 = _slot_base(tile * tm)

        def issue(g, c):
            for j in range(SUBLANES):
                for k in range(2):
                    t = g * SUBLANES + j
                    d = dest_ref[slot0 + k * ROUTE_TILE + t]
                    pltpu.make_async_copy(ybuf_ref.at[d], bufs[k].at[t], sem).start(priority=k)
            return c

        if straight:
            for g in range(tm // SUBLANES):
                issue(g, 0)
        else:
            lax.fori_loop(0, tm // SUBLANES, issue, 0)

    def wait(sem):
        for k in range(2):
            pltpu.make_async_copy(ybuf_ref.at[pl.ds(0, tm)], ybuf_ref.at[pl.ds(tm, tm)], sem).wait()

    def compute(rows, bufs):
        e = _rms(_dot(p_ref[rows, :].astype(BF16), wup_ref[...]), gple_ref[...])
        rf = rf_ref[rows, :]
        h = (h_ref[rows, :] + rf[:, 0:1] * bufs[0][...].reshape(tm, D).astype(F32)
             + rf[:, 1:2] * bufs[1][...].reshape(tm, D).astype(F32))
        gate = jax.nn.sigmoid(_dot(_rms(h, ggate_ref[...]).astype(BF16), wgate_ref[...]))
        h = h + gate * e
        out_ref[rows, :] = _rms(h, gfin_ref[...])

    n_tiles = pl.num_programs(0) * FINAL_TILES
    bufs = [(row_bufs[2 * i], row_bufs[2 * i + 1]) for i in range(FINAL_TILES)]

    @pl.when(s == 0)
    def _():
        for i in range(FINAL_AHEAD):
            gather(i, bufs[i], sems.at[i])

    for i in range(FINAL_TILES):
        wait(sems.at[i])
        nxt = (i + FINAL_AHEAD) % FINAL_TILES
        gather(jnp.minimum(s * FINAL_TILES + i + FINAL_AHEAD, n_tiles - 1), bufs[nxt],
               sems.at[nxt], straight=True)
        compute(slice(i * tm, (i + 1) * tm), bufs[i])

    @pl.when(s == pl.num_programs(0) - 1)
    def _():
        for i in range(FINAL_AHEAD):
            wait(sems.at[i])


def _final(dest, h, rf, p2, w_up, g_ple, g_gate, w_gate, g_fin, y_buf, tm=256):
    T, D = h.shape
    const = lambda i, *_: (0, 0)
    tile = lambda i, *_: (i, 0)
    assert y_buf.shape[1:] == ROW_TILE and D == SUBLANES * LANES
    row_buf = pltpu.VMEM((tm,) + ROW_TILE, y_buf.dtype)
    return pl.pallas_call(
        functools.partial(_final_kernel, tm=tm),
        grid_spec=pltpu.PrefetchScalarGridSpec(
            num_scalar_prefetch=1,
            grid=(T // (FINAL_TILES * tm),),
            in_specs=[
                pl.BlockSpec((FINAL_TILES * tm, D), tile),
                pl.BlockSpec((FINAL_TILES * tm, LANES), tile),
                pl.BlockSpec((FINAL_TILES * tm, p2.shape[1]), tile),
                pl.BlockSpec(w_up.shape, const),
                pl.BlockSpec((1, D), const),
                pl.BlockSpec((1, D), const),
                pl.BlockSpec(w_gate.shape, const),
                pl.BlockSpec((1, D), const),
                pl.BlockSpec(memory_space=pl.ANY),
            ],
            out_specs=pl.BlockSpec((FINAL_TILES * tm, D), tile),
            scratch_shapes=[row_buf] * (2 * FINAL_TILES) + [pltpu.SemaphoreType.DMA((FINAL_TILES,))],
        ),
        out_shape=jax.ShapeDtypeStruct((T, D), F32),
        compiler_params=pltpu.CompilerParams(
            dimension_semantics=("arbitrary",), vmem_limit_bytes=VMEM_LIMIT),
        name="final",
    )(dest, h, rf, p2, w_up, g_ple, g_gate, w_gate, g_fin, y_buf)


def _layer(h2, p2, positions, B, S, attn_norm, w_in, conv_w, conv_b, b_igate, b_fgate, ret_gn,
           ml_gn, w_out, moe_norm, w_group, b_group, w_router, b_router, w1, w3, w2, w_ple_up,
           ple_norm, ple_gate_norm, w_ple_gate, out_norm):
    T, D = h2.shape
    ret_w = ret_gn.shape[0]
    ml_w = ml_gn.shape[0]
    n_main = 4 * ret_w + 4 * ml_w
    row = lambda v: v.reshape(1, -1).astype(F32)

    ml_heads = ml_w // LANES
    w_main = w_in.astype(BF16)
    gate_rows = lambda a: jnp.pad(a, ((0, SUBLANES - ml_heads), (0, 0)))
    w_i, w_f = w_in[:, n_main:n_main + ml_heads].T, w_in[:, n_main + ml_heads:].T
    w_gate_t = jnp.concatenate([gate_rows(w_i), gate_rows(w_f)]).astype(BF16)
    gate_bias = jnp.concatenate([gate_rows(b_igate[:, None]), gate_rows(b_fgate[:, None])])
    cos_t, sin_t = _rope_tables(positions, LANES // 2)
    decay, ws, wq, cd = _ret_tables(CHUNK)
    tm_in = 2 * CHUNK
    ops = _inproj(h2.reshape(B, S, D), row(attn_norm), w_main, w_gate_t, gate_bias.astype(F32),
                  cos_t.reshape(B, S, LANES), sin_t.reshape(B, S, LANES),
                  jnp.tile(ws, (tm_in // CHUNK, 1)), conv_w, conv_b.reshape(1, -1), ret_w, ml_w,
                  tm=tm_in)

    y = _mixer(ops, decay, wq, cd, ret_gn, ml_gn, B, S, ret_w, ml_w)

    pad_rows = lambda a: jnp.pad(a, ((0, ROUTER_ROWS - a.shape[0]), (0, 0)))
    wr = pad_rows(jnp.concatenate([w_router, w_group], axis=1).T)
    wr_hi = wr.astype(BF16)
    wr_lo = (wr - wr_hi.astype(F32)).astype(BF16)
    br = pad_rows(jnp.concatenate([b_router, b_group])[:, None].astype(F32))
    h_mid, xn, ri, rf, counts = _router(y, h2, w_out.astype(BF16), row(moe_norm), wr_hi, wr_lo, br)

    blk = EXPERT_BLOCK
    counts = counts[:, 0]
    padded = (counts + blk - 1) // blk * blk
    pad_end = jnp.cumsum(padded)
    pad_start = pad_end - padded
    n_blocks = (2 * T) // blk + N_EXPERTS
    assert n_blocks <= 256
    block_start = jnp.arange(n_blocks, dtype=jnp.int32) * blk
    block_expert = jnp.minimum(jnp.sum(pad_end[None, :] <= block_start[:, None], axis=1),
                               N_EXPERTS - 1).astype(jnp.int32)
    n_used = (pad_end[-1:] // blk).astype(jnp.int32)
    blk_id = jnp.arange(n_blocks, dtype=jnp.int32)
    is_used = blk_id < n_used[0]
    first = is_used & jnp.concatenate([jnp.ones((1,), bool), block_expert[1:] != block_expert[:-1]])
    seg_end = jnp.sum(jnp.where(block_expert[:, None] == jnp.arange(N_EXPERTS)[None, :],
                                (pad_end // blk)[None, :], 0), axis=1)
    next_blk = jnp.minimum(seg_end, n_blocks - 1)
    next_expert = jnp.sum(jnp.where(next_blk[:, None] == blk_id[None, :], block_expert[None, :], 0),
                          axis=1)
    next_expert = jnp.where(seg_end < n_used[0], next_expert, -1).astype(jnp.int32)
    slot = ((jnp.cumsum(first.astype(jnp.int32)) - 1) % 2).astype(jnp.int32)
    codes = ri[:, 0:2, :].reshape(-1)
    start_blk = jnp.dot(jax.nn.one_hot(codes % N_EXPERTS, N_EXPERTS, dtype=BF16),
                        (pad_start // blk).astype(BF16), preferred_element_type=F32)
    dest = (codes // N_EXPERTS + start_blk.astype(jnp.int32) * blk).astype(jnp.int32)

    x_buf = _dispatch(dest, pad_end.astype(jnp.int32), n_used, xn, n_blocks * blk,
                      tm=min(T, DISPATCH_TILE))
    y_buf = _experts(block_expert, n_used, first.astype(jnp.int32), next_expert, slot,
                     x_buf, w1, w3, w2)

    return _final(dest, h_mid, rf, p2, w_ple_up.astype(BF16), row(ple_norm), row(ple_gate_norm),
                  w_ple_gate.astype(BF16), row(out_norm), y_buf)


def kernel(x, p, positions, attn_norm, w_in, conv_w, conv_b, b_igate, b_fgate, ret_gn, ml_gn,
           w_out, moe_norm, w_group, b_group, w_router, b_router, w1, w3, w2, w_ple_up, ple_norm,
           ple_gate_norm, w_ple_gate, final_norm):
    B, S, D = x.shape
    depth = p.shape[0]
    assert depth == 1, "the final RMSNorm is fused into the layer's last kernel"
    out = _layer(x.reshape(B * S, D), p[0].reshape(B * S, -1), positions, B, S,
                 attn_norm[0], w_in[0], conv_w[0], conv_b[0], b_igate[0], b_fgate[0], ret_gn[0],
                 ml_gn[0], w_out[0], moe_norm[0], w_group[0], b_group[0], w_router[0],
                 b_router[0], w1[0], w3[0], w2[0], w_ple_up[0], ple_norm[0], ple_gate_norm[0],
                 w_ple_gate[0], final_norm)
    return out.reshape(B, S, D)
```

```python
import functools

import jax
import jax.numpy as jnp
from jax import lax
from jax.experimental import pallas as pl
from jax.experimental.pallas import tpu as pltpu

F32 = jnp.float32
BF16 = jnp.bfloat16

RET_HEADS = 8
ML_HEADS = 4
CHUNK = 128
CONV_W = 4
ROPE_BASE = 10000.0
N_GROUPS = 4
EXPERTS_PER_GROUP = 8
N_EXPERTS = N_GROUPS * EXPERTS_PER_GROUP
EPS = 1e-6

LANES = 128
SUBLANES = 8
MXU_COLS = 256
MIXER_SEQS = 4
FINAL_TILES = 4
FINAL_AHEAD = 3
VMEM_LIMIT = 56 * 1024 * 1024
EXPERT_BLOCK = 512
GROUP_ROW0 = N_EXPERTS
ROUTER_ROWS = 64
ROUTE_TILE = 1024
ROUTE_ROWS = 8
DISPATCH_TILE = 4 * ROUTE_TILE
GATE_ROWS = 2 * SUBLANES
ROW_TILE = (SUBLANES, LANES)


def _rms(x, g):
    return x * lax.rsqrt(jnp.mean(x * x, axis=-1, keepdims=True) + EPS) * g


def _dot(a, b):
    return jnp.dot(a, b, preferred_element_type=F32)


def _dot_nt(a, b):
    return lax.dot_general(a, b, (((1,), (1,)), ((), ())), preferred_element_type=F32)


def _split3(x):
    hi = x.astype(BF16)
    r1 = x - hi.astype(F32)
    mid = r1.astype(BF16)
    lo = (r1 - mid.astype(F32)).astype(BF16)
    return hi, mid, lo


def _rope_kernel(pos_ref, freq_ref, sign_ref, cos_ref, sin_ref, *, half):
    ang = pos_ref[...].astype(F32) * freq_ref[...]
    cos_c = jnp.cos(ang)
    sin_c = jnp.sin(ang)
    group = lax.broadcasted_iota(jnp.int32, ang.shape, 1) // half
    per_row = LANES // half
    for s in range(per_row):
        def spread(t):
            out = t
            for g in range(per_row):
                if g != s:
                    out = jnp.where(group == g, pltpu.roll(t, (half * (g - s)) % LANES, 1), out)
            return out
        cos_ref[s] = spread(cos_c)
        sin_ref[s] = spread(sin_c) * sign_ref[...]


def _rope_tables(positions, dh):
    half = dh // 2
    per_row = LANES // half
    T = positions.size
    rows = T // per_row
    tr = min(rows, 1024)
    freqs = ROPE_BASE ** (-jnp.arange(half, dtype=F32) / half)
    pos_c = jnp.repeat(positions.reshape(per_row, rows).T, half, axis=1)
    sign = jnp.tile(jnp.concatenate([-jnp.ones((half,), F32), jnp.ones((half,), F32)]),
                    LANES // dh)[None, :]
    const = pl.BlockSpec((1, LANES), lambda i: (0, 0))
    out = pl.BlockSpec((per_row, tr, LANES), lambda i: (0, i, 0))
    cos_t, sin_t = pl.pallas_call(
        functools.partial(_rope_kernel, half=half),
        grid=(rows // tr,),
        in_specs=[pl.BlockSpec((tr, LANES), lambda i: (i, 0)), const, const],
        out_specs=[out, out],
        out_shape=[jax.ShapeDtypeStruct((per_row, rows, LANES), F32)] * 2,
        name="rope",
    )(pos_c, jnp.tile(freqs, per_row)[None, :], sign)
    return cos_t.reshape(T, LANES), sin_t.reshape(T, LANES)


def _inproj_kernel(x_ref, g_ref, wm_ref, wgt_ref, gb_ref, cos_ref, sin_ref, ws_ref, convw_ref,
                   convb_ref, rq_ref, rk_ref, rkwt_ref, rv_ref, rg_ref, mq_ref, mk_ref, mvt_ref,
                   mo_ref, gt_ref, carry, *, ret_w, ml_w):
    tm = x_ref.shape[0]
    ret_dh = LANES // 2

    @pl.when(pl.program_id(1) == 0)
    def _():
        carry[0:SUBLANES] = jnp.zeros((SUBLANES, carry.shape[1]), F32)

    xn = _rms(x_ref[...], g_ref[...]).astype(BF16)
    proj = lambda o, w: _dot(xn, wm_ref[:, o:o + w])
    tiles = lambda w: [slice(t, t + LANES) for t in range(0, w, LANES)]

    lane = lax.broadcasted_iota(jnp.int32, (tm, LANES), 1)
    first_half = (lane % ret_dh) < (ret_dh // 2)
    cos_t = cos_ref[...]
    sin_t = sin_ref[...]

    def rot(t):
        swapped = jnp.where(first_half, pltpu.roll(t, LANES - ret_dh // 2, 1),
                            pltpu.roll(t, ret_dh // 2, 1))
        return t * cos_t + swapped * sin_t

    for c0 in range(0, ret_w, MXU_COLS):
        rq = proj(c0, MXU_COLS)
        for ps in tiles(MXU_COLS):
            rq_ref[:, c0 + ps.start:c0 + ps.stop] = rot(rq[:, ps]).astype(BF16)
    for c0 in range(0, ret_w, MXU_COLS):
        rk = proj(ret_w + c0, MXU_COLS)
        for ps in tiles(MXU_COLS):
            cs = slice(c0 + ps.start, c0 + ps.stop)
            k = rot(rk[:, ps]) * (ret_dh ** -0.5)
            rk_ref[:, cs] = k.astype(BF16)
            rkwt_ref[cs, :] = (k * ws_ref[:, cs]).T.astype(BF16)
    for c0 in range(0, ret_w, MXU_COLS):
        cs = slice(c0, c0 + MXU_COLS)
        rv_ref[:, cs] = proj(2 * ret_w + c0, MXU_COLS).astype(BF16)
    for c0 in range(0, ret_w, MXU_COLS):
        g = proj(3 * ret_w + c0, MXU_COLS)
        rg_ref[:, c0:c0 + MXU_COLS] = g * jax.nn.sigmoid(g)

    o_mq = 4 * ret_w
    for c0 in range(0, 2 * ml_w, MXU_COLS):
        cs = slice(c0, c0 + MXU_COLS)
        xq = proj(o_mq + c0, MXU_COLS)
        carry[SUBLANES:SUBLANES + tm, cs] = xq
        acc = xq * convw_ref[CONV_W - 1:CONV_W, cs] + convb_ref[:, cs]
        for s in range(1, CONV_W):
            shifted = carry[SUBLANES - s:SUBLANES - s + tm, cs]
            acc = acc + shifted * convw_ref[CONV_W - 1 - s:CONV_W - s, cs]
        carry[0:SUBLANES, cs] = xq[tm - SUBLANES:tm]
        act = acc * jax.nn.sigmoid(acc)
        if c0 < ml_w:
            mq_ref[:, cs] = act.astype(BF16)
        else:
            mk_ref[:, c0 - ml_w:c0 - ml_w + MXU_COLS] = (act * (LANES ** -0.5)).astype(BF16)
    for c0 in range(0, ml_w, MXU_COLS):
        mv = proj(o_mq + 2 * ml_w + c0, MXU_COLS)
        for hs in tiles(MXU_COLS):
            mvt_ref[c0 + hs.start:c0 + hs.stop, :] = mv[:, hs].T
    for c0 in range(0, ml_w, MXU_COLS):
        mo_ref[:, c0:c0 + MXU_COLS] = jax.nn.sigmoid(proj(o_mq + 3 * ml_w + c0, MXU_COLS))
    gt_ref[...] = _dot_nt(wgt_ref[...], xn) + gb_ref[...]


def _inproj(x3, g, w_main, w_gate_t, gate_bias, cos_t, sin_t, ws, conv_w, conv_b, ret_w, ml_w,
            tm=256):
    B, S, D = x3.shape
    n_main = 4 * (ret_w + ml_w)
    const = lambda b, i: (0, 0)
    tok = lambda w: pl.BlockSpec((None, tm, w), lambda b, i: (b, i, 0))
    tok_t = lambda r: pl.BlockSpec((None, r, tm), lambda b, i: (b, 0, i))
    act = lambda w, dt: jax.ShapeDtypeStruct((B, S, w), dt)
    act_t = lambda r, dt: jax.ShapeDtypeStruct((B, r, S), dt)
    return pl.pallas_call(
        functools.partial(_inproj_kernel, ret_w=ret_w, ml_w=ml_w),
        grid=(B, S // tm),
        in_specs=[
            tok(D),
            pl.BlockSpec((1, D), const),
            pl.BlockSpec((D, n_main), const),
            pl.BlockSpec((GATE_ROWS, D), const),
            pl.BlockSpec((GATE_ROWS, 1), const),
            tok(LANES),
            tok(LANES),
            pl.BlockSpec((tm, ret_w), const),
            pl.BlockSpec((CONV_W, 2 * ml_w), const),
            pl.BlockSpec((1, 2 * ml_w), const),
        ],
        out_specs=[tok(ret_w), tok(ret_w), tok_t(ret_w), tok(ret_w), tok(ret_w),
                   tok(ml_w), tok(ml_w), tok_t(ml_w), tok(ml_w), tok_t(GATE_ROWS)],
        out_shape=[
            act(ret_w, BF16),
            act(ret_w, BF16),
            act_t(ret_w, BF16),
            act(ret_w, BF16),
            act(ret_w, F32),
            act(ml_w, BF16),
            act(ml_w, BF16),
            act_t(ml_w, F32),
            act(ml_w, F32),
            act_t(GATE_ROWS, F32),
        ],
        scratch_shapes=[pltpu.VMEM((SUBLANES + tm, 2 * ml_w), F32)],
        compiler_params=pltpu.CompilerParams(
            dimension_semantics=("arbitrary", "arbitrary"), vmem_limit_bytes=VMEM_LIMIT),
        name="inproj",
    )(x3, g, w_main, w_gate_t, gate_bias, cos_t, sin_t, ws, conv_w, conv_b)


def _mixer_kernel(rq_ref, rk_ref, rkwt_ref, rv_ref, rg_ref, mq_ref, mk_ref, mvt_ref, mo_ref,
                  gt_ref, decay_ref, wq_ref, cd_ref, retgn_ref, mlgn_ref,
                  y_ref, r_state, s_state, n_state, m_state, *, ret_w, ml_w):
    BB, C = rq_ref.shape[0], rq_ref.shape[1]
    n_pairs = ret_w // LANES
    ml_heads = ml_w // LANES
    ret_dh = LANES // 2

    @pl.when(pl.program_id(1) == 0)
    def _():
        r_state[...] = jnp.zeros_like(r_state)
        s_state[...] = jnp.zeros_like(s_state)
        n_state[...] = jnp.zeros_like(n_state)
        m_state[...] = jnp.zeros_like(m_state)

    lane = lax.broadcasted_iota(jnp.int32, (C, LANES), 1)
    row = lax.broadcasted_iota(jnp.int32, (C, LANES), 0)
    assert C == LANES
    lo = lane < ret_dh
    blockdiag = (row < ret_dh) == lo
    lo_b = jnp.where(lo, 1.0, 0.0).astype(BF16)
    hi_b = jnp.where(lo, 0.0, 1.0).astype(BF16)
    seqs = range(BB)
    pair_units = [(s, p, slice(p * LANES, (p + 1) * LANES)) for s in seqs for p in range(n_pairs)]
    head_units = [(s, h, slice(h * LANES, (h + 1) * LANES)) for s in seqs for h in range(ml_heads)]


    row8 = lax.broadcasted_iota(jnp.int32, (SUBLANES, C), 0)
    lane8 = lax.broadcasted_iota(jnp.int32, (SUBLANES, C), 1)
    live = row8 < ml_heads
    triu = (row <= lane).astype(BF16)
    before = row <= lane
    mx, w_inter, e_negm, w_state, w_state_b, dec, beta_t = [], [], [], [], [], [], []
    for s in seqs:
        ig = jnp.where(live, gt_ref[s, 0:SUBLANES, :], 0.0)
        f_pre = jnp.where(live, gt_ref[s, SUBLANES:2 * SUBLANES, :], 30.0)
        l_hi, l_mid, l_lo = _split3(jax.nn.log_sigmoid(f_pre))
        b = _dot(l_hi, triu) + _dot(l_mid, triu) + _dot(l_lo, triu)
        beta = ig - b
        cm = beta
        shift = 1
        while shift < C:
            cm = jnp.maximum(cm, jnp.where(lane8 >= shift, pltpu.roll(cm, shift, 1), -jnp.inf))
            shift *= 2
        m_prev = m_state[s]
        mx.append(jnp.maximum(cm, m_prev))
        mx_last = jnp.broadcast_to(mx[s][:, C - 1:C], (SUBLANES, C))
        w_inter.append(jnp.exp(m_prev - mx[s]))
        e_negm.append(jnp.exp(-(b + mx[s])))
        w_state.append(jnp.exp(beta - mx_last))
        w_state_b.append(w_state[s].astype(BF16))
        dec.append(jnp.exp(m_prev - mx_last))
        m_state[s] = jnp.where(live, jnp.broadcast_to(b[:, C - 1:C], (SUBLANES, C)) + mx_last, 0.0)
        beta_t.append(jnp.concatenate([beta, jnp.zeros((LANES - SUBLANES, C), F32)], axis=0).T)

    hrow = lambda t, h: t[h:h + 1, :]

    rq = [rq_ref[s, :, ps] for s, p, ps in pair_units]
    rk = [rk_ref[s, :, ps] for s, p, ps in pair_units]
    rv = [rv_ref[s, :, ps] for s, p, ps in pair_units]
    r_prev = [r_state[s, p] for s, p, ps in pair_units]
    pu = range(len(pair_units))
    s_ab = [_dot_nt(jnp.concatenate([rq[u] * lo_b, rq[u] * hi_b], axis=0), rk[u]) for u in pu]
    r_read = [_dot(rq[u], r_prev[u].astype(BF16)) for u in pu]
    r_new = [_dot(rkwt_ref[s, ps, :], rv[u]) for u, (s, p, ps) in enumerate(pair_units)]
    hu = range(len(head_units))
    mq = [mq_ref[s, :, hs] for s, h, hs in head_units]
    mk = [mk_ref[s, :, hs] for s, h, hs in head_units]
    mv_t = [mvt_ref[s, hs, :] for s, h, hs in head_units]
    st_prev = [s_state[s, h] for s, h, hs in head_units]
    n_prev = [n_state[s, h] for s, h, hs in head_units]
    a_t = [_dot_nt(mk[u], mq[u]) for u in hu]
    s_read = [_dot_nt(st_prev[u].astype(BF16), mq[u]) for u in hu]
    qn = [_dot_nt(n_prev[u].astype(BF16), mq[u])[0:1, :] for u in hu]
    s_new = [_dot((mv_t[u] * hrow(w_state[s], h)).astype(BF16), mk[u])
             for u, (s, h, hs) in enumerate(head_units)]
    n_new = [_dot(w_state_b[s], mk[u])[h:h + 1, :] for u, (s, h, hs) in enumerate(head_units)]
    for u, (s, p, ps) in enumerate(pair_units):
        r_state[s, p] = cd_ref[p] * r_prev[u] + jnp.where(blockdiag, r_new[u], 0.0)
    for u, (s, h, hs) in enumerate(head_units):
        s_state[s, h] = hrow(dec[s], h) * st_prev[u] + s_new[u]
        n_state[s, h] = hrow(dec[s], h) * n_prev[u] + n_new[u]

    s_ab = [jnp.concatenate([(s_ab[u][0:C] * decay_ref[2 * p]).astype(BF16),
                             (s_ab[u][C:2 * C] * decay_ref[2 * p + 1]).astype(BF16)], axis=1)
            for u, (s, p, ps) in enumerate(pair_units)]
    p_t = [jnp.exp(jnp.where(before, beta_t[s][:, h:h + 1] - hrow(mx[s], h), -jnp.inf)) * a_t[u]
           for u, (s, h, hs) in enumerate(head_units)]

    o = [_dot(s_ab[u], jnp.concatenate([rv[u] * lo_b, rv[u] * hi_b], axis=0)) + r_read[u] * wq_ref[p]
         for u, (s, p, ps) in enumerate(pair_units)]
    num_t = [_dot(mv_t[u].astype(BF16), p_t[u].astype(BF16)) + hrow(w_inter[s], h) * s_read[u]
             for u, (s, h, hs) in enumerate(head_units)]

    for u, (s, p, ps) in enumerate(pair_units):
        sq = o[u] * o[u]
        ms_a = jnp.sum(jnp.where(lo, sq, 0.0), axis=-1, keepdims=True)
        ms_b = jnp.sum(jnp.where(lo, 0.0, sq), axis=-1, keepdims=True)
        ms = jnp.where(lo, ms_a, ms_b) * (1.0 / ret_dh)
        r = o[u] * lax.rsqrt(ms + EPS) * retgn_ref[:, ps]
        y_ref[s, :, ps] = (rg_ref[s, :, ps] * r).astype(y_ref.dtype)
    for u, (s, h, hs) in enumerate(head_units):
        den = jnp.sum(p_t[u], axis=0, keepdims=True) + hrow(w_inter[s], h) * qn[u]
        hh = (num_t[u] * (1.0 / jnp.maximum(jnp.abs(den), hrow(e_negm[s], h)))).T
        hm = _rms(mo_ref[s, :, hs] * hh, mlgn_ref[:, hs])
        y_ref[s, :, ret_w + h * LANES:ret_w + (h + 1) * LANES] = hm.astype(y_ref.dtype)


def _ret_tables(C):
    H = RET_HEADS
    dh = LANES // 2
    log_gamma = jnp.log1p(-(2.0 ** (-5.0 - jnp.arange(H, dtype=F32))))
    idx = jnp.arange(C, dtype=F32)
    rel = idx[:, None] - idx[None, :]
    causal = rel >= 0
    decay = jnp.where(causal, jnp.exp(log_gamma[:, None, None] * jnp.where(causal, rel, 0.0)), 0.0)
    w_state = jnp.exp(log_gamma[:, None] * (C - 1 - idx))
    w_query = jnp.exp(log_gamma[:, None] * (idx + 1.0))
    chunk_decay = jnp.exp(log_gamma * C)
    pair = lambda t: jnp.repeat(t.reshape(H // 2, 2, C).transpose(0, 2, 1), dh, axis=2)
    cd = jnp.repeat(chunk_decay.reshape(H // 2, 2), dh, axis=1)
    cd = jnp.broadcast_to(cd[:, :, None], (H // 2, LANES, LANES))
    ws = pair(w_state).transpose(1, 0, 2).reshape(C, (H // 2) * LANES)
    return decay, ws, pair(w_query), cd


def _mixer(ops, decay, wq, cd, ret_gn, ml_gn, B, S, ret_w, ml_w):
    C = CHUNK
    N = S // C
    n_pairs = ret_w // LANES
    ml_heads = ml_w // LANES
    c2 = lambda b, n: (0, 0)
    c3 = lambda b, n: (0, 0, 0)
    BB = MIXER_SEQS if B % MIXER_SEQS == 0 else 1
    tok = lambda w: pl.BlockSpec((BB, C, w), lambda b, n: (b, n, 0))
    tok_t = lambda r: pl.BlockSpec((BB, r, C), lambda b, n: (b, 0, n))
    y = pl.pallas_call(
        functools.partial(_mixer_kernel, ret_w=ret_w, ml_w=ml_w),
        grid=(B // BB, N),
        in_specs=[
            tok(ret_w), tok(ret_w), tok_t(ret_w), tok(ret_w), tok(ret_w),
            tok(ml_w), tok(ml_w), tok_t(ml_w), tok(ml_w), tok_t(GATE_ROWS),
            pl.BlockSpec((RET_HEADS, C, C), c3),
            pl.BlockSpec((n_pairs, C, LANES), c3),
            pl.BlockSpec((n_pairs, LANES, LANES), c3),
            pl.BlockSpec((1, ret_w), c2),
            pl.BlockSpec((1, ml_w), c2),
        ],
        out_specs=pl.BlockSpec((BB, C, ret_w + ml_w), lambda b, n: (b, n, 0)),
        out_shape=jax.ShapeDtypeStruct((B, S, ret_w + ml_w), BF16),
        scratch_shapes=[
            pltpu.VMEM((BB, n_pairs, LANES, LANES), F32),
            pltpu.VMEM((BB, ml_heads, LANES, LANES), F32),
            pltpu.VMEM((BB, ml_heads, SUBLANES, LANES), F32),
            pltpu.VMEM((BB, SUBLANES, LANES), F32),
        ],
        compiler_params=pltpu.CompilerParams(
            dimension_semantics=("arbitrary", "arbitrary"), vmem_limit_bytes=VMEM_LIMIT),
        name="mixer",
    )(*ops, decay, wq, cd, ret_gn.reshape(1, -1), ml_gn.reshape(1, -1))
    return y.reshape(B * S, ret_w + ml_w)


def _router_kernel(y_ref, x_ref, wo_ref, g_ref, wr_hi_ref, wr_lo_ref, br_ref,
                   h_ref, xn_ref, ri_ref, rf_ref, cnt_ref, run_cnt):
    tm = y_ref.shape[0]

    @pl.when(pl.program_id(0) == 0)
    def _():
        run_cnt[...] = jnp.zeros_like(run_cnt)

    h = x_ref[...] + _dot(y_ref[...], wo_ref[...])
    h_ref[...] = h
    xn = _rms(h, g_ref[...])
    x_hi = xn.astype(BF16)
    xn_ref[...] = x_hi.reshape(xn_ref.shape)
    x_lo = (xn - x_hi.astype(F32)).astype(BF16)
    logits = (_dot_nt(wr_hi_ref[...], x_hi) + _dot_nt(wr_hi_ref[...], x_lo)
              + _dot_nt(wr_lo_ref[...], x_hi) + br_ref[...])
    big = jnp.int32(LANES)
    neg = -jnp.inf
    gl = logits[GROUP_ROW0:GROUP_ROW0 + 8]
    grow = lax.broadcasted_iota(jnp.int32, gl.shape, 0)
    is_g = grow < N_GROUPS
    gl = jnp.where(is_g, gl, neg)
    gmax = jnp.max(gl, axis=0, keepdims=True)
    gsum = jnp.sum(jnp.where(is_g, jnp.exp(gl - gmax), 0.0), axis=0, keepdims=True)
    p_g = 1.0 / gsum
    g_sel = jnp.min(jnp.where(is_g & (gl == gmax), grow, big), axis=0, keepdims=True)
    el = logits[0:N_EXPERTS]
    erow = lax.broadcasted_iota(jnp.int32, el.shape, 0)
    in_grp = (erow // EXPERTS_PER_GROUP) == g_sel
    el = jnp.where(in_grp, el, neg)
    emax = jnp.max(el, axis=0, keepdims=True)
    eexp = jnp.where(in_grp, jnp.exp(el - emax), 0.0)
    prob = eexp / jnp.sum(eexp, axis=0, keepdims=True)
    pm1 = jnp.where(in_grp, prob, -1.0)
    p1 = jnp.max(pm1, axis=0, keepdims=True)
    i1 = jnp.min(jnp.where(pm1 == p1, erow, big), axis=0, keepdims=True)
    pm2 = jnp.where(erow == i1, -1.0, pm1)
    p2 = jnp.max(pm2, axis=0, keepdims=True)
    i2 = jnp.min(jnp.where(pm2 == p2, erow, big), axis=0, keepdims=True)
    denom = p1 + p2
    g1 = p_g * p1 / denom
    g2 = p_g * p2 / denom

    sel1 = erow == i1
    sel2 = erow == i2
    onehot = (sel1 | sel2).astype(BF16)
    r_i = lax.broadcasted_iota(jnp.int32, (tm, tm), 0)
    c_i = lax.broadcasted_iota(jnp.int32, (tm, tm), 1)
    tri = (r_i < c_i).astype(BF16)
    prefix = _dot(onehot, tri) + run_cnt[:, 0:1]
    rank1 = jnp.sum(jnp.where(sel1, prefix, 0.0), axis=0, keepdims=True).astype(jnp.int32)
    rank2 = jnp.sum(jnp.where(sel2, prefix, 0.0), axis=0, keepdims=True).astype(jnp.int32)
    new_cnt = run_cnt[...] + jnp.sum(onehot.astype(F32), axis=1, keepdims=True)
    run_cnt[...] = new_cnt
    cnt_ref[...] = new_cnt.astype(jnp.int32)

    rrow = lax.broadcasted_iota(jnp.int32, (ROUTE_ROWS, tm), 0)
    ri_ref[...] = jnp.where(rrow == 0, rank1 * N_EXPERTS + i1,
                            jnp.where(rrow == 1, rank2 * N_EXPERTS + i2, 0))
    lrow = lax.broadcasted_iota(jnp.int32, (LANES, tm), 0)
    rf_ref[...] = jnp.where(lrow == 0, g1, jnp.where(lrow == 1, g2, 0.0)).T


def _router(y, x2, w_out, g, wr_hi, wr_lo, br, tm=ROUTE_TILE):
    T, D = x2.shape
    const = lambda i: (0, 0)
    tile = lambda i: (i, 0)
    return pl.pallas_call(
        _router_kernel,
        grid=(T // tm,),
        in_specs=[
            pl.BlockSpec((tm, y.shape[1]), tile),
            pl.BlockSpec((tm, D), tile),
            pl.BlockSpec(w_out.shape, const),
            pl.BlockSpec((1, D), const),
            pl.BlockSpec((ROUTER_ROWS, D), const),
            pl.BlockSpec((ROUTER_ROWS, D), const),
            pl.BlockSpec((ROUTER_ROWS, 1), const),
        ],
        out_specs=[
            pl.BlockSpec((tm, D), tile),
            pl.BlockSpec((tm,) + ROW_TILE, lambda i: (i, 0, 0)),
            pl.BlockSpec((None, ROUTE_ROWS, tm), lambda i: (i, 0, 0)),
            pl.BlockSpec((tm, LANES), tile),
            pl.BlockSpec((N_EXPERTS, LANES), const),
        ],
        out_shape=[
            jax.ShapeDtypeStruct((T, D), F32),
            jax.ShapeDtypeStruct((T,) + ROW_TILE, BF16),
            jax.ShapeDtypeStruct((T // tm, ROUTE_ROWS, tm), jnp.int32),
            jax.ShapeDtypeStruct((T, LANES), F32),
            jax.ShapeDtypeStruct((N_EXPERTS, LANES), jnp.int32),
        ],
        scratch_shapes=[pltpu.VMEM((N_EXPERTS, LANES), F32)],
        compiler_params=pltpu.CompilerParams(
            dimension_semantics=("arbitrary",), vmem_limit_bytes=VMEM_LIMIT),
        name="router",
    )(y, x2, w_out, g, wr_hi, wr_lo, br)


def _slot_base(tok0):
    return (tok0 // ROUTE_TILE) * (2 * ROUTE_TILE) + tok0 % ROUTE_TILE


def _route_spans(tm):
    assert tm % ROUTE_TILE == 0 or ROUTE_TILE % tm == 0
    span = min(tm, ROUTE_TILE)
    return [(t0, span) for t0 in range(0, tm, span)]


def _for_each_pad_block(pend_ref, nu_ref, n_blocks, fn):
    blk = EXPERT_BLOCK
    for e in range(N_EXPERTS):
        prev_end = 0 if e == 0 else pend_ref[e - 1]

        @pl.when(pend_ref[e] > prev_end)
        def _():
            fn(pl.multiple_of(pend_ref[e] - blk, blk))

    def tail(j, c):
        fn(pl.multiple_of(j * blk, blk))
        return c

    lax.fori_loop(nu_ref[0], n_blocks, tail, 0)


def _dispatch_kernel(dest_ref, pend_ref, nu_ref, xn_ref, buf_ref, zeros, sem, zsem, *, tm):
    blk = EXPERT_BLOCK
    i = pl.program_id(0)

    @pl.when(i == 0)
    def _():
        zeros[...] = jnp.zeros_like(zeros)
        zcopy = lambda row: pltpu.make_async_copy(zeros, buf_ref.at[pl.ds(row, blk)], zsem)
        n_blocks = buf_ref.shape[0] // blk
        _for_each_pad_block(pend_ref, nu_ref, n_blocks, lambda row: zcopy(row).start())
        _for_each_pad_block(pend_ref, nu_ref, n_blocks, lambda row: zcopy(row).wait())

    for t0, span in _route_spans(tm):
        slot0 = _slot_base(i * tm + t0)

        def issue(g, c, t0=t0, slot0=slot0):
            for j in range(SUBLANES):
                for k in range(2):
                    t = g * SUBLANES + j
                    d = dest_ref[slot0 + k * ROUTE_TILE + t]
                    pltpu.make_async_copy(xn_ref.at[i * tm + t0 + t], buf_ref.at[d],
                                          sem).start(priority=k)
            return c

        lax.fori_loop(0, span // SUBLANES, issue, 0)

    def drain():
        for k in range(2):
            pltpu.make_async_copy(buf_ref.at[pl.ds(tm, tm)], buf_ref.at[pl.ds(0, tm)], sem).wait()

    @pl.when(i > 0)
    def _():
        drain()

    @pl.when(i == pl.num_programs(0) - 1)
    def _():
        drain()


def _dispatch(dest, pad_end, n_used, xn, n_rows, tm=ROUTE_TILE):
    T = xn.shape[0]
    return pl.pallas_call(
        functools.partial(_dispatch_kernel, tm=tm),
        grid_spec=pltpu.PrefetchScalarGridSpec(
            num_scalar_prefetch=3,
            grid=(T // tm,),
            in_specs=[pl.BlockSpec(memory_space=pl.ANY)],
            out_specs=pl.BlockSpec(memory_space=pl.ANY),
            scratch_shapes=[
                pltpu.VMEM((EXPERT_BLOCK,) + ROW_TILE, xn.dtype),
                pltpu.SemaphoreType.DMA,
                pltpu.SemaphoreType.DMA,
            ],
        ),
        out_shape=jax.ShapeDtypeStruct((n_rows,) + ROW_TILE, xn.dtype),
        compiler_params=pltpu.CompilerParams(
            dimension_semantics=("arbitrary",), vmem_limit_bytes=VMEM_LIMIT),
        name="dispatch",
    )(dest, pad_end, n_used, xn)


def _expert_kernel(be_ref, nu_ref, first_ref, next_ref, slot_ref, x_ref, w1_hbm, w3_hbm,
                   w2_hbm, y_ref, s1, s3, s2, w1b, w3b, w2b, sems):
    j = pl.program_id(0)
    used = j < nu_ref[0]
    blk = x_ref.shape[0]

    def weight_copies(e, slot):
        return [pltpu.make_async_copy(w.at[e], s.at[slot], sems.at[slot])
                for w, s in ((w1_hbm, s1), (w3_hbm, s3), (w2_hbm, s2))]

    @pl.when(j == 0)
    def _():
        for c in weight_copies(be_ref[0], 0):
            c.start()

    @pl.when(used & (first_ref[j] == 1))
    def _():
        slot = slot_ref[j]
        for c in weight_copies(be_ref[j], slot):
            c.wait()

        @pl.when(next_ref[j] >= 0)
        def _():
            for c in weight_copies(next_ref[j], 1 - slot):
                c.start()

        w1b[...] = s1[slot].astype(BF16)
        w3b[...] = s3[slot].astype(BF16)
        w2b[...] = s2[slot].astype(BF16)

    @pl.when(used)
    def _():
        x = x_ref[...].reshape(blk, -1)
        a = _dot(x, w1b[...])
        hmid = a * jax.nn.sigmoid(a) * _dot(x, w3b[...])
        y_ref[...] = _dot(hmid.astype(BF16), w2b[...]).astype(y_ref.dtype).reshape(y_ref.shape)

    @pl.when(jnp.logical_not(used))
    def _():
        y_ref[...] = jnp.zeros_like(y_ref)


def _experts(block_expert, n_used, first, next_expert, slot, x_buf, w1, w3, w2):
    P = x_buf.shape[0]
    D = w1.shape[1]
    assert x_buf.shape[1:] == ROW_TILE and D == SUBLANES * LANES
    blk = EXPERT_BLOCK
    d_exp = w2.shape[1]
    hbm = pl.BlockSpec(memory_space=pl.ANY)
    return pl.pallas_call(
        _expert_kernel,
        grid_spec=pltpu.PrefetchScalarGridSpec(
            num_scalar_prefetch=5,
            grid=(P // blk,),
            in_specs=[
                pl.BlockSpec((blk,) + ROW_TILE,
                             lambda j, be, nu, *_: (jnp.minimum(j, nu[0] - 1), 0, 0)),
                hbm, hbm, hbm,
            ],
            out_specs=pl.BlockSpec((blk,) + ROW_TILE, lambda j, *_: (j, 0, 0)),
            scratch_shapes=[
                pltpu.VMEM((2, D, d_exp), F32),
                pltpu.VMEM((2, D, d_exp), F32),
                pltpu.VMEM((2, d_exp, D), F32),
                pltpu.VMEM((D, d_exp), BF16),
                pltpu.VMEM((D, d_exp), BF16),
                pltpu.VMEM((d_exp, D), BF16),
                pltpu.SemaphoreType.DMA((2,)),
            ],
        ),
        out_shape=jax.ShapeDtypeStruct((P,) + ROW_TILE, BF16),
        compiler_params=pltpu.CompilerParams(
            dimension_semantics=("arbitrary",), vmem_limit_bytes=VMEM_LIMIT),
        name="experts",
    )(block_expert, n_used, first, next_expert, slot, x_buf, w1, w3, w2)


def _final_kernel(dest_ref, h_ref, rf_ref, p_ref, wup_ref, gple_ref, ggate_ref, wgate_ref,
                  gfin_ref, ybuf_ref, out_ref, *scratch, tm):
    row_bufs, sems = scratch[:-1], scratch[-1]
    s = pl.program_id(0)
    D = h_ref.shape[1]
    assert ROUTE_TILE % tm == 0

    def gather(tile, bufs, sem, straight=False):
        slot0 = _slot_base(tile * tm)

        def issue(g, c):
            for j in range(SUBLANES):
                for k in range(2):
                    t = g * SUBLANES + j
                    d = dest_ref[slot0 + k * ROUTE_TILE + t]
                    pltpu.make_async_copy(ybuf_ref.at[d], bufs[k].at[t], sem).start(priority=k)
            return c

        if straight:
            for g in range(tm // SUBLANES):
                issue(g, 0)
        else:
            lax.fori_loop(0, tm // SUBLANES, issue, 0)

    def wait(sem):
        for k in range(2):
            pltpu.make_async_copy(ybuf_ref.at[pl.ds(0, tm)], ybuf_ref.at[pl.ds(tm, tm)], sem).wait()

    def compute(rows, bufs):
        e = _rms(_dot(p_ref[rows, :].astype(BF16), wup_ref[...]), gple_ref[...])
        rf = rf_ref[rows, :]
        h = (h_ref[rows, :] + rf[:, 0:1] * bufs[0][...].reshape(tm, D).astype(F32)
             + rf[:, 1:2] * bufs[1][...].reshape(tm, D).astype(F32))
        gate = jax.nn.sigmoid(_dot(_rms(h, ggate_ref[...]).astype(BF16), wgate_ref[...]))
        h = h + gate * e
        out_ref[rows, :] = _rms(h, gfin_ref[...])

    n_tiles = pl.num_programs(0) * FINAL_TILES
    bufs = [(row_bufs[2 * i], row_bufs[2 * i + 1]) for i in range(FINAL_TILES)]

    @pl.when(s == 0)
    def _():
        for i in range(FINAL_AHEAD):
            gather(i, bufs[i], sems.at[i])

    for i in range(FINAL_TILES):
        wait(sems.at[i])
        nxt = (i + FINAL_AHEAD) % FINAL_TILES
        gather(jnp.minimum(s * FINAL_TILES + i + FINAL_AHEAD, n_tiles - 1), bufs[nxt],
               sems.at[nxt], straight=True)
        compute(slice(i * tm, (i + 1) * tm), bufs[i])

    @pl.when(s == pl.num_programs(0) - 1)
    def _():
        for i in range(FINAL_AHEAD):
            wait(sems.at[i])


def _final(dest, h, rf, p2, w_up, g_ple, g_gate, w_gate, g_fin, y_buf, tm=256):
    T, D = h.shape
    const = lambda i, *_: (0, 0)
    tile = lambda i, *_: (i, 0)
    assert y_buf.shape[1:] == ROW_TILE and D == SUBLANES * LANES
    row_buf = pltpu.VMEM((tm,) + ROW_TILE, y_buf.dtype)
    return pl.pallas_call(
        functools.partial(_final_kernel, tm=tm),
        grid_spec=pltpu.PrefetchScalarGridSpec(
            num_scalar_prefetch=1,
            grid=(T // (FINAL_TILES * tm),),
            in_specs=[
                pl.BlockSpec((FINAL_TILES * tm, D), tile),
                pl.BlockSpec((FINAL_TILES * tm, LANES), tile),
                pl.BlockSpec((FINAL_TILES * tm, p2.shape[1]), tile),
                pl.BlockSpec(w_up.shape, const),
                pl.BlockSpec((1, D), const),
                pl.BlockSpec((1, D), const),
                pl.BlockSpec(w_gate.shape, const),
                pl.BlockSpec((1, D), const),
                pl.BlockSpec(memory_space=pl.ANY),
            ],
            out_specs=pl.BlockSpec((FINAL_TILES * tm, D), tile),
            scratch_shapes=[row_buf] * (2 * FINAL_TILES) + [pltpu.SemaphoreType.DMA((FINAL_TILES,))],
        ),
        out_shape=jax.ShapeDtypeStruct((T, D), F32),
        compiler_params=pltpu.CompilerParams(
            dimension_semantics=("arbitrary",), vmem_limit_bytes=VMEM_LIMIT),
        name="final",
    )(dest, h, rf, p2, w_up, g_ple, g_gate, w_gate, g_fin, y_buf)


def _layer(h2, p2, positions, B, S, attn_norm, w_in, conv_w, conv_b, b_igate, b_fgate, ret_gn,
           ml_gn, w_out, moe_norm, w_group, b_group, w_router, b_router, w1, w3, w2, w_ple_up,
           ple_norm, ple_gate_norm, w_ple_gate, out_norm):
    T, D = h2.shape
    ret_w = ret_gn.shape[0]
    ml_w = ml_gn.shape[0]
    n_main = 4 * ret_w + 4 * ml_w
    row = lambda v: v.reshape(1, -1).astype(F32)

    ml_heads = ml_w // LANES
    w_main = w_in.astype(BF16)
    gate_rows = lambda a: jnp.pad(a, ((0, SUBLANES - ml_heads), (0, 0)))
    w_i, w_f = w_in[:, n_main:n_main + ml_heads].T, w_in[:, n_main + ml_heads:].T
    w_gate_t = jnp.concatenate([gate_rows(w_i), gate_rows(w_f)]).astype(BF16)
    gate_bias = jnp.concatenate([gate_rows(b_igate[:, None]), gate_rows(b_fgate[:, None])])
    cos_t, sin_t = _rope_tables(positions, LANES // 2)
    decay, ws, wq, cd = _ret_tables(CHUNK)
    tm_in = 2 * CHUNK
    ops = _inproj(h2.reshape(B, S, D), row(attn_norm), w_main, w_gate_t, gate_bias.astype(F32),
                  cos_t.reshape(B, S, LANES), sin_t.reshape(B, S, LANES),
                  jnp.tile(ws, (tm_in // CHUNK, 1)), conv_w, conv_b.reshape(1, -1), ret_w, ml_w,
                  tm=tm_in)

    y = _mixer(ops, decay, wq, cd, ret_gn, ml_gn, B, S, ret_w, ml_w)

    pad_rows = lambda a: jnp.pad(a, ((0, ROUTER_ROWS - a.shape[0]), (0, 0)))
    wr = pad_rows(jnp.concatenate([w_router, w_group], axis=1).T)
    wr_hi = wr.astype(BF16)
    wr_lo = (wr - wr_hi.astype(F32)).astype(BF16)
    br = pad_rows(jnp.concatenate([b_router, b_group])[:, None].astype(F32))
    h_mid, xn, ri, rf, counts = _router(y, h2, w_out.astype(BF16), row(moe_norm), wr_hi, wr_lo, br)

    blk = EXPERT_BLOCK
    counts = counts[:, 0]
    padded = (counts + blk - 1) // blk * blk
    pad_end = jnp.cumsum(padded)
    pad_start = pad_end - padded
    n_blocks = (2 * T) // blk + N_EXPERTS
    assert n_blocks <= 256
    block_start = jnp.arange(n_blocks, dtype=jnp.int32) * blk
    block_expert = jnp.minimum(jnp.sum(pad_end[None, :] <= block_start[:, None], axis=1),
                               N_EXPERTS - 1).astype(jnp.int32)
    n_used = (pad_end[-1:] // blk).astype(jnp.int32)
    blk_id = jnp.arange(n_blocks, dtype=jnp.int32)
    is_used = blk_id < n_used[0]
    first = is_used & jnp.concatenate([jnp.ones((1,), bool), block_expert[1:] != block_expert[:-1]])
    seg_end = jnp.sum(jnp.where(block_expert[:, None] == jnp.arange(N_EXPERTS)[None, :],
                                (pad_end // blk)[None, :], 0), axis=1)
    next_blk = jnp.minimum(seg_end, n_blocks - 1)
    next_expert = jnp.sum(jnp.where(next_blk[:, None] == blk_id[None, :], block_expert[None, :], 0),
                          axis=1)
    next_expert = jnp.where(seg_end < n_used[0], next_expert, -1).astype(jnp.int32)
    slot = ((jnp.cumsum(first.astype(jnp.int32)) - 1) % 2).astype(jnp.int32)
    codes = ri[:, 0:2, :].reshape(-1)
    start_blk = jnp.dot(jax.nn.one_hot(codes % N_EXPERTS, N_EXPERTS, dtype=BF16),
                        (pad_start // blk).astype(BF16), preferred_element_type=F32)
    dest = (codes // N_EXPERTS + start_blk.astype(jnp.int32) * blk).astype(jnp.int32)

    x_buf = _dispatch(dest, pad_end.astype(jnp.int32), n_used, xn, n_blocks * blk,
                      tm=min(T, DISPATCH_TILE))
    y_buf = _experts(block_expert, n_used, first.astype(jnp.int32), next_expert, slot,
                     x_buf, w1, w3, w2)

    return _final(dest, h_mid, rf, p2, w_ple_up.astype(BF16), row(ple_norm), row(ple_gate_norm),
                  w_ple_gate.astype(BF16), row(out_norm), y_buf)


def kernel(x, p, positions, attn_norm, w_in, conv_w, conv_b, b_igate, b_fgate, ret_gn, ml_gn,
           w_out, moe_norm, w_group, b_group, w_router, b_router, w1, w3, w2, w_ple_up, ple_norm,
           ple_gate_norm, w_ple_gate, final_norm):
    B, S, D = x.shape
    depth = p.shape[0]
    assert depth == 1, "the final RMSNorm is fused into the layer's last kernel"
    out = _layer(x.reshape(B * S, D), p[0].reshape(B * S, -1), positions, B, S,
                 attn_norm[0], w_in[0], conv_w[0], conv_b[0], b_igate[0], b_fgate[0], ret_gn[0],
                 ml_gn[0], w_out[0], moe_norm[0], w_group[0], b_group[0], w_router[0],
                 b_router[0], w1[0], w3[0], w2[0], w_ple_up[0], ple_norm[0], ple_gate_norm[0],
                 w_ple_gate[0], final_norm)
    return out.reshape(B, S, D)
```
